```python
import math
import jax, jax.numpy as jnp
from jax import lax
import numpy as np

D_MODEL = 1024
BATCH = 8
SEQ = 4096
DEPTH = 4

N_EVEN = (DEPTH + 1) // 2
N_ODD = DEPTH // 2
CONV_WIDTH = 4
NORM_EPS = 1e-6
LRU_WIDTH = D_MODEL
LRU_BLOCKS = 16
LRU_BLOCK = LRU_WIDTH // LRU_BLOCKS
LRU_C = 8.0
SSD_WIDTH = D_MODEL
SSD_HEAD_DIM = 64
SSD_HEADS = SSD_WIDTH // SSD_HEAD_DIM
SSD_GROUPS = 2
SSD_STATE = 128
SSD_CHUNK = 128
SSD_CONV_CH = SSD_WIDTH + 2 * SSD_GROUPS * SSD_STATE
REC_IN = 2 * LRU_WIDTH + SSD_WIDTH + SSD_CONV_CH + SSD_HEADS
REC_OUT = LRU_WIDTH + SSD_WIDTH
ATT_HEADS = 16
ATT_HEAD_DIM = D_MODEL // ATT_HEADS
ATT_WIDTH = ATT_HEADS * ATT_HEAD_DIM
ROPE_DIM = ATT_HEAD_DIM // 4
ROPE_THETA = 500000.0
DILATED_PATTERNS = ((128, 1), (512, 4), (2048, 16))
FFN_HIDDEN = -(-8 * D_MODEL // (3 * 256)) * 256

kernel_name = 'hybrid_rglru_ssd_dilated_attn_trunk'


def rmsnorm(x, g):
    x32 = x.astype(jnp.float32)
    y = x32 * lax.rsqrt(jnp.mean(x32 * x32, axis=-1, keepdims=True) + NORM_EPS)
    return (y * g.astype(jnp.float32)).astype(x.dtype)


def causal_dwconv(x, w, b):
    k = w.shape[0]
    l = x.shape[1]
    xp = jnp.pad(x, ((0, 0), (k - 1, 0), (0, 0)))
    y = b
    for j in range(k):
        y = y + xp[:, j:j + l] * w[j]
    return y


def rg_lru(xc, w_r, b_r, w_i, b_i, lam):
    b, l, w = xc.shape
    xb = xc.reshape(b, l, LRU_BLOCKS, LRU_BLOCK)
    r = jax.nn.sigmoid(jnp.einsum('blhi,hij->blhj', xb, w_r).reshape(b, l, w) + b_r)
    i = jax.nn.sigmoid(jnp.einsum('blhi,hij->blhj', xb, w_i).reshape(b, l, w) + b_i)
    log_a = -LRU_C * r * jax.nn.softplus(-lam)
    a = jnp.exp(log_a)
    u = jnp.sqrt(-jnp.expm1(2.0 * log_a)) * (i * xc)

    def combine(e1, e2):
        a1, h1 = e1
        a2, h2 = e2
        return a1 * a2, a2 * h1 + h2

    _, h = lax.associative_scan(combine, (a, u), axis=1)
    return h


def ssd_chunked(xs, dt, a, bm, cm):
    b, l, nh, p = xs.shape
    g, n = bm.shape[2], bm.shape[3]
    r = nh // g
    c = l // SSD_CHUNK
    t = SSD_CHUNK
    xdt = (xs * dt[..., None]).reshape(b, c, t, g, r, p)
    adt = (dt * a).reshape(b, c, t, g, r)
    bc = bm.reshape(b, c, t, g, n)
    cc = cm.reshape(b, c, t, g, n)
    cs = jnp.cumsum(adt, axis=2)
    seg = cs[:, :, :, None] - cs[:, :, None, :]
    causal = (jnp.arange(t)[:, None] >= jnp.arange(t)[None, :])[:, :, None, None]
    decay = jnp.exp(jnp.where(causal, seg, -jnp.inf))
    cb = jnp.einsum('bclgn,bcsgn->bclsg', cc, bc)
    y_diag = jnp.einsum('bclsgr,bcsgrp->bclgrp', cb[..., None] * decay, xdt)
    decay_to_end = jnp.exp(cs[:, :, -1:] - cs)
    states = jnp.einsum('bcsgn,bcsgrp->bcgrpn', bc, xdt * decay_to_end[..., None])
    chunk_decay = jnp.exp(cs[:, :, -1])

    def step(carry, inp):
        s_c, d_c = inp
        return carry * d_c[..., None, None] + s_c, carry

    init = jnp.zeros((b, g, r, p, n), jnp.float32)
    _, prev = lax.scan(step, init, (jnp.moveaxis(states, 1, 0), jnp.moveaxis(chunk_decay, 1, 0)))
    prev = jnp.moveaxis(prev, 0, 1)
    y_off = jnp.einsum('bclgn,bcgrpn->bclgrp', cc, prev) * jnp.exp(cs)[..., None]
    return (y_diag + y_off).reshape(b, l, nh, p)


def recurrent_layer(x, norm_g, w_in, lru_conv_w, lru_conv_b, lru_w_r, lru_b_r, lru_w_i, lru_b_i,
                    lru_lambda, ssd_conv_w, ssd_conv_b, ssd_dt_bias, ssd_a_log, ssd_d, ssd_norm, w_out):
    f32 = jnp.float32
    b, l, _ = x.shape
    h = rmsnorm(x, norm_g)
    proj = h @ w_in
    s1 = LRU_WIDTH
    s2 = 2 * LRU_WIDTH
    s3 = s2 + SSD_WIDTH
    s4 = s3 + SSD_CONV_CH
    lru_x, lru_gate, z, xbc, dt_raw = jnp.split(proj, [s1, s2, s3, s4], axis=-1)
    xc = causal_dwconv(lru_x, lru_conv_w, lru_conv_b).astype(f32)
    h_lru = rg_lru(xc, lru_w_r.astype(f32), lru_b_r.astype(f32), lru_w_i.astype(f32),
                   lru_b_i.astype(f32), lru_lambda.astype(f32))
    out_a = h_lru * jax.nn.gelu(lru_gate.astype(f32))
    xbc = jax.nn.silu(causal_dwconv(xbc, ssd_conv_w, ssd_conv_b).astype(f32))
    xs, bm, cm = jnp.split(xbc, [SSD_WIDTH, SSD_WIDTH + SSD_GROUPS * SSD_STATE], axis=-1)
    xs = xs.reshape(b, l, SSD_HEADS, SSD_HEAD_DIM)
    bm = bm.reshape(b, l, SSD_GROUPS, SSD_STATE)
    cm = cm.reshape(b, l, SSD_GROUPS, SSD_STATE)
    dt = jax.nn.softplus(dt_raw.astype(f32) + ssd_dt_bias.astype(f32))
    a = -jnp.exp(ssd_a_log.astype(f32))
    y = ssd_chunked(xs, dt, a, bm, cm) + ssd_d.astype(f32)[:, None] * xs
    y = y.reshape(b, l, SSD_WIDTH) * jax.nn.silu(z.astype(f32))
    y = y.reshape(b, l, SSD_GROUPS, SSD_WIDTH // SSD_GROUPS)
    y = y * lax.rsqrt(jnp.mean(y * y, axis=-1, keepdims=True) + NORM_EPS)
    out_b = y.reshape(b, l, SSD_WIDTH) * ssd_norm.astype(f32)
    mixed = jnp.concatenate([out_a, out_b], axis=-1).astype(x.dtype)
    return x + mixed @ w_out


def partial_rope(x, pos):
    half = ROPE_DIM // 2
    inv = ROPE_THETA ** (-2.0 * jnp.arange(half, dtype=jnp.float32) / ROPE_DIM)
    ang = pos[:, None] * inv[None, :]
    cos = jnp.cos(ang)[None, :, None, :]
    sin = jnp.sin(ang)[None, :, None, :]
    x1 = x[..., :half].astype(jnp.float32)
    x2 = x[..., half:ROPE_DIM].astype(jnp.float32)
    rot = jnp.concatenate([x1 * cos - x2 * sin, x2 * cos + x1 * sin], axis=-1)
    return jnp.concatenate([rot.astype(x.dtype), x[..., ROPE_DIM:]], axis=-1)


def dilated_window_attention(q, k, v, window, dilation):
    b, l, h, e = q.shape
    span = window // dilation
    u = l // dilation
    nblk = -(-u // span)
    up = nblk * span

    def to_strided(t):
        t = t.reshape(b, u, dilation, h, e).transpose(0, 2, 1, 3, 4).reshape(b * dilation, u, h, e)
        t = jnp.pad(t, ((0, 0), (0, up - u), (0, 0), (0, 0)))
        return t.reshape(b * dilation, nblk, span, h, e)

    def with_prev(t):
        prev = jnp.pad(t[:, :-1], ((0, 0), (1, 0), (0, 0), (0, 0), (0, 0)))
        return jnp.concatenate([prev, t], axis=2)

    def from_strided(t):
        tail = t.shape[3:]
        t = t.reshape((b, dilation, up) + tail)[:, :, :u]
        return jnp.swapaxes(t, 1, 2).reshape((b, l) + tail)

    qb = to_strided(q)
    kc = with_prev(to_strided(k))
    vc = with_prev(to_strided(v))
    s = jnp.einsum('bnqhe,bnkhe->bnhqk', qb, kc).astype(jnp.float32)
    qi = jnp.arange(span)[:, None]
    kj = jnp.arange(2 * span)[None, :]
    band = (kj >= qi) & (kj <= qi + span)
    valid = band[None] & ((jnp.arange(nblk)[:, None, None] > 0) | (kj[None] >= span))
    s = jnp.where(valid[None, :, None], s, -jnp.inf)
    m = jnp.max(s, axis=-1)
    p = jnp.exp(s - m[..., None])
    den = jnp.sum(p, axis=-1)
    o = jnp.einsum('bnhqk,bnkhe->bnqhe', p, vc.astype(jnp.float32))
    den_q = jnp.swapaxes(den, 2, 3)
    o = o / den_q[..., None]
    return from_strided(o), from_strided(jnp.swapaxes(m, 2, 3)), from_strided(den_q)


def attention_layer(x, norm_g, w_qkv, q_norm, k_norm, w_out):
    b, l, _ = x.shape
    h = rmsnorm(x, norm_g)
    qkv = (h @ w_qkv).reshape(b, l, 3, ATT_HEADS, ATT_HEAD_DIM)
    pos = jnp.arange(l, dtype=jnp.float32)
    q = partial_rope(rmsnorm(qkv[:, :, 0], q_norm), pos) * (ATT_HEAD_DIM ** -0.5)
    k = partial_rope(rmsnorm(qkv[:, :, 1], k_norm), pos)
    v = qkv[:, :, 2]
    outs, maxes, dens = [], [], []
    for window, dilation in DILATED_PATTERNS:
        o_i, m_i, d_i = dilated_window_attention(q, k, v, window, dilation)
        outs.append(o_i)
        maxes.append(m_i)
        dens.append(d_i)
    m_all = jnp.stack(maxes)
    wts = jnp.stack(dens) * jnp.exp(m_all - jnp.max(m_all, axis=0, keepdims=True))
    o = jnp.sum(wts[..., None] * jnp.stack(outs), axis=0) / jnp.sum(wts, axis=0)[..., None]
    o = o.astype(x.dtype).reshape(b, l, ATT_WIDTH)
    return x + o @ w_out


def swiglu_layer(x, norm_g, w_gate_up, w_down):
    h = rmsnorm(x, norm_g)
    g, u = jnp.split(h @ w_gate_up, 2, axis=-1)
    return x + (jax.nn.silu(g) * u) @ w_down


def setup_inputs(seed: int = 0) -> dict:
    key = jax.random.key(seed)
    ks = iter(jax.random.split(key, 32))
    f32 = jnp.float32
    ne, no = N_EVEN, N_ODD

    def nrm(shape, scale):
        return jax.random.normal(next(ks), shape, f32) * scale

    def gain(shape):
        return 1.0 + 0.02 * jax.random.normal(next(ks), shape, f32)

    lam_base = jax.random.uniform(next(ks), (ne, LRU_WIDTH), f32, 0.9, 0.999)
    sig = lam_base ** (1.0 / LRU_C)
    lru_lambda = jnp.log(sig) - jnp.log1p(-sig)
    dt0 = jnp.exp(jax.random.uniform(next(ks), (ne, SSD_HEADS), f32, math.log(1e-3), math.log(1e-1)))
    ssd_dt_bias = dt0 + jnp.log(-jnp.expm1(-dt0))
    ssd_a_log = jnp.log(jax.random.uniform(next(ks), (ne, SSD_HEADS), f32, 1.0, 16.0))
    inputs = {}
    inputs['x'] = nrm((BATCH, SEQ, D_MODEL), 1.0)
    inputs['rec_norm'] = gain((ne, D_MODEL))
    inputs['rec_w_in'] = nrm((ne, D_MODEL, REC_IN), D_MODEL ** -0.5)
    inputs['lru_conv_w'] = nrm((ne, CONV_WIDTH, LRU_WIDTH), CONV_WIDTH ** -0.5)
    inputs['lru_conv_b'] = nrm((ne, LRU_WIDTH), 0.02)
    inputs['lru_w_r'] = nrm((ne, LRU_BLOCKS, LRU_BLOCK, LRU_BLOCK), LRU_BLOCK ** -0.5)
    inputs['lru_b_r'] = nrm((ne, LRU_WIDTH), 0.02)
    inputs['lru_w_i'] = nrm((ne, LRU_BLOCKS, LRU_BLOCK, LRU_BLOCK), LRU_BLOCK ** -0.5)
    inputs['lru_b_i'] = nrm((ne, LRU_WIDTH), 0.02)
    inputs['lru_lambda'] = lru_lambda
    inputs['ssd_conv_w'] = nrm((ne, CONV_WIDTH, SSD_CONV_CH), CONV_WIDTH ** -0.5)
    inputs['ssd_conv_b'] = nrm((ne, SSD_CONV_CH), 0.02)
    inputs['ssd_dt_bias'] = ssd_dt_bias
    inputs['ssd_a_log'] = ssd_a_log
    inputs['ssd_d'] = 1.0 + 0.1 * jax.random.normal(next(ks), (ne, SSD_HEADS), f32)
    inputs['ssd_norm'] = gain((ne, SSD_WIDTH))
    inputs['rec_w_out'] = nrm((ne, REC_OUT, D_MODEL), REC_OUT ** -0.5)
    inputs['att_norm'] = gain((no, D_MODEL))
    inputs['att_w_qkv'] = nrm((no, D_MODEL, 3 * ATT_WIDTH), D_MODEL ** -0.5)
    inputs['att_q_norm'] = gain((no, ATT_HEAD_DIM))
    inputs['att_k_norm'] = gain((no, ATT_HEAD_DIM))
    inputs['att_w_out'] = nrm((no, ATT_WIDTH, D_MODEL), ATT_WIDTH ** -0.5)
    inputs['ffn_norm'] = gain((DEPTH, D_MODEL))
    inputs['ffn_w_gate_up'] = nrm((DEPTH, D_MODEL, 2 * FFN_HIDDEN), D_MODEL ** -0.5)
    inputs['ffn_w_down'] = nrm((DEPTH, FFN_HIDDEN, D_MODEL), FFN_HIDDEN ** -0.5)
    return inputs


def reference(x, rec_norm, rec_w_in, lru_conv_w, lru_conv_b, lru_w_r, lru_b_r, lru_w_i, lru_b_i,
              lru_lambda, ssd_conv_w, ssd_conv_b, ssd_dt_bias, ssd_a_log, ssd_d, ssd_norm, rec_w_out,
              att_norm, att_w_qkv, att_q_norm, att_k_norm, att_w_out,
              ffn_norm, ffn_w_gate_up, ffn_w_down):
    for layer in range(DEPTH):
        i = layer // 2
        if layer % 2 == 0:
            x = recurrent_layer(x, rec_norm[i], rec_w_in[i], lru_conv_w[i], lru_conv_b[i],
                                lru_w_r[i], lru_b_r[i], lru_w_i[i], lru_b_i[i], lru_lambda[i],
                                ssd_conv_w[i], ssd_conv_b[i], ssd_dt_bias[i], ssd_a_log[i],
                                ssd_d[i], ssd_norm[i], rec_w_out[i])
        else:
            x = attention_layer(x, att_norm[i], att_w_qkv[i], att_q_norm[i], att_k_norm[i],
                                att_w_out[i])
        x = swiglu_layer(x, ffn_norm[layer], ffn_w_gate_up[layer], ffn_w_down[layer])
    return x
```

```python
import functools
import math

import jax
import jax.numpy as jnp
import numpy as np
from jax import lax
from jax.experimental import pallas as pl
from jax.experimental.pallas import tpu as pltpu

F32 = jnp.float32
BF16 = jnp.bfloat16

NORM_EPS = 1e-6
CONV_WIDTH = 4
LRU_BLOCK = 64
LRU_C = 8.0
GATE_SLAB = 256
SSD_HEAD_DIM = 64
SSD_GROUPS = 2
SSD_STATE = 128
SSD_CHUNK = 128
ATT_HEAD_DIM = 64
ROPE_DIM = 16
ROPE_THETA = 500000.0
SPAN = 128
DILATIONS = (1, 4, 16)
SUPER = SPAN * DILATIONS[-1]
ATT_LANES = 256
LANES = 128
SUBLANES = 8
VMEM_LIMIT_BYTES = 56 * 1024 * 1024


def _params(*semantics):
    return pltpu.CompilerParams(dimension_semantics=semantics, vmem_limit_bytes=VMEM_LIMIT_BYTES)


def _rmsnorm(x, g):
    ms = jnp.mean(x * x, axis=-1, keepdims=True)
    return x * lax.rsqrt(ms + NORM_EPS) * g


def _sigmoid(x):
    return 1.0 / (1.0 + jnp.exp(-x))


def _silu(x):
    return x * _sigmoid(x)


def _softplus(x):
    return jnp.maximum(x, 0.0) + jnp.log1p(jnp.exp(-jnp.abs(x)))


def _gelu_tanh(x):
    c = math.sqrt(2.0 / math.pi)
    return x * (0.5 * (1.0 + jnp.tanh(c * (x + 0.044715 * (x * x * x)))))


FFN_TM = 512
FFN_TH = 256


def _ffn_body(x_ref, g_ref, wgu_ref, wd_ref, o_ref, h_ref, acc_ref):
    j = pl.program_id(1)

    @pl.when(j == 0)
    def _():
        h_ref[...] = _rmsnorm(x_ref[...], g_ref[...]).astype(BF16)
        acc_ref[...] = jnp.zeros_like(acc_ref)

    gu = jnp.dot(h_ref[...], wgu_ref[...], preferred_element_type=F32)
    act = (_silu(gu[:, :FFN_TH]) * gu[:, FFN_TH:]).astype(BF16)
    acc_ref[...] += jnp.dot(act, wd_ref[...], preferred_element_type=F32)

    @pl.when(j == pl.num_programs(1) - 1)
    def _():
        o_ref[...] = x_ref[...] + acc_ref[...]


def _ffn(x, g, wgu, wd):
    t, d = x.shape
    nh = wd.shape[0] // FFN_TH
    return pl.pallas_call(
        _ffn_body,
        grid=(t // FFN_TM, nh),
        in_specs=[
            pl.BlockSpec((FFN_TM, d), lambda i, j: (i, 0)),
            pl.BlockSpec((1, d), lambda i, j: (0, 0)),
            pl.BlockSpec((d, 2 * FFN_TH), lambda i, j: (0, j)),
            pl.BlockSpec((FFN_TH, d), lambda i, j: (j, 0)),
        ],
        out_specs=pl.BlockSpec((FFN_TM, d), lambda i, j: (i, 0)),
        out_shape=jax.ShapeDtypeStruct((t, d), F32),
        scratch_shapes=[pltpu.VMEM((FFN_TM, d), BF16), pltpu.VMEM((FFN_TM, d), F32)],
        compiler_params=_params("parallel", "arbitrary"),
        name="ffn",
    )(x, g, wgu, wd)


def _prep_ffn_weights(w_gate_up, w_down):
    d, two_h = w_gate_up.shape
    hid = two_h // 2
    nh = hid // FFN_TH
    wg = w_gate_up[:, :hid].reshape(d, nh, FFN_TH)
    wu = w_gate_up[:, hid:].reshape(d, nh, FFN_TH)
    wgu = jnp.concatenate([wg, wu], axis=2).reshape(d, 2 * hid).astype(BF16)
    return wgu, w_down.astype(BF16)


PROJ_TM = 512


def _proj_body(n_in, *refs):
    x_ref = refs[0]
    a_refs = refs[1:1 + n_in]
    w_refs = refs[1 + n_in:1 + 2 * n_in]
    o_ref = refs[1 + 2 * n_in]
    acc = x_ref[...]
    for a_ref, w_ref in zip(a_refs, w_refs):
        acc = acc + jnp.dot(a_ref[...], w_ref[...], preferred_element_type=F32)
    o_ref[...] = acc


def _proj_residual(x, acts, weights):
    t, d = x.shape
    n_in = len(acts)
    in_specs = [pl.BlockSpec((PROJ_TM, d), lambda i: (i, 0))]
    in_specs += [pl.BlockSpec((PROJ_TM, a.shape[1]), lambda i: (i, 0)) for a in acts]
    in_specs += [pl.BlockSpec(w.shape, lambda i: (0, 0)) for w in weights]
    return pl.pallas_call(
        functools.partial(_proj_body, n_in),
        grid=(t // PROJ_TM,),
        in_specs=in_specs,
        out_specs=pl.BlockSpec((PROJ_TM, d), lambda i: (i, 0)),
        out_shape=jax.ShapeDtypeStruct((t, d), F32),
        compiler_params=_params("parallel"),
        name="proj_residual",
    )(x, *acts, *weights)


RECIN_TM = 512
RECIN_TN = 512


def _recin_body(x_ref, g_ref, w_ref, wdt_ref, o_ref, dt_ref, h_ref):
    j = pl.program_id(1)

    @pl.when(j == 0)
    def _():
        h = _rmsnorm(x_ref[...], g_ref[...]).astype(BF16)
        h_ref[...] = h
        dt_ref[...] = jnp.dot(h, wdt_ref[...], preferred_element_type=F32)

    o_ref[...] = jnp.dot(h_ref[...], w_ref[...], preferred_element_type=F32)


def _rec_in(x, g, w_main, w_dt):
    t, d = x.shape
    n = w_main.shape[1]
    return pl.pallas_call(
        _recin_body,
        grid=(t // RECIN_TM, n // RECIN_TN),
        in_specs=[
            pl.BlockSpec((RECIN_TM, d), lambda i, j: (i, 0)),
            pl.BlockSpec((1, d), lambda i, j: (0, 0)),
            pl.BlockSpec((d, RECIN_TN), lambda i, j: (0, j)),
            pl.BlockSpec((d, LANES), lambda i, j: (0, 0)),
        ],
        out_specs=[
            pl.BlockSpec((RECIN_TM, RECIN_TN), lambda i, j: (i, j)),
            pl.BlockSpec((RECIN_TM, LANES), lambda i, j: (i, 0)),
        ],
        out_shape=[jax.ShapeDtypeStruct((t, n), F32), jax.ShapeDtypeStruct((t, LANES), F32)],
        scratch_shapes=[pltpu.VMEM((RECIN_TM, d), BF16)],
        compiler_params=_params("parallel", "arbitrary"),
        name="rec_in",
    )(x, g, w_main, w_dt)


def _causal_conv(pad_ref, x, w_ref, b_ref, rows, first):
    @pl.when(first)
    def _():
        pad_ref[0:SUBLANES, :] = jnp.zeros((SUBLANES, pad_ref.shape[1]), F32)

    pad_ref[SUBLANES:SUBLANES + rows, :] = x
    y = b_ref[...] + w_ref[CONV_WIDTH - 1:CONV_WIDTH, :] * x
    for j in range(CONV_WIDTH - 1):
        off = SUBLANES - (CONV_WIDTH - 1) + j
        y = y + w_ref[j:j + 1, :] * pad_ref[off:off + rows, :]
    pad_ref[0:SUBLANES, :] = pad_ref[rows:rows + SUBLANES, :]
    return y


LRU_TL = 256


def _lru_body(x_ref, gate_ref, cw_ref, cb_ref, wr_ref, wi_ref, br_ref, bi_ref, lam_ref,
              o_ref, pad_ref, a_ref, u_ref, carry_ref):
    first = pl.program_id(1) == 0
    tl, w = x_ref.shape

    @pl.when(first)
    def _():
        carry_ref[...] = jnp.zeros_like(carry_ref)

    xc = _causal_conv(pad_ref, x_ref[...], cw_ref, cb_ref, tl, first)
    xcb = xc.astype(BF16)
    pre_r, pre_i = [], []
    for s in range(w // GATE_SLAB):
        slab = xcb[:, s * GATE_SLAB:(s + 1) * GATE_SLAB]
        pre_r.append(jnp.dot(slab, wr_ref[s], preferred_element_type=F32))
        pre_i.append(jnp.dot(slab, wi_ref[s], preferred_element_type=F32))
    r = _sigmoid(jnp.concatenate(pre_r, axis=1) + br_ref[...])
    i = _sigmoid(jnp.concatenate(pre_i, axis=1) + bi_ref[...])
    log_a = (-LRU_C * r) * _softplus(-lam_ref[...])
    a = jnp.exp(log_a)
    u = jnp.sqrt(1.0 - a * a) * (i * xc)

    sub = lax.broadcasted_iota(jnp.int32, (tl, w), 0) & (SUBLANES - 1)
    for s in (1, 2, 4):
        keep = sub >= s
        a_prev = jnp.where(keep, pltpu.roll(a, s, 0), 1.0)
        u_prev = jnp.where(keep, pltpu.roll(u, s, 0), 0.0)
        u = u + a * u_prev
        a = a * a_prev
    a_ref[...] = a
    u_ref[...] = u

    def step(gi, h_prev):
        rows = pl.ds(pl.multiple_of(gi * SUBLANES, SUBLANES), SUBLANES)
        h = a_ref[rows, :] * h_prev + u_ref[rows, :]
        u_ref[rows, :] = h
        return jnp.broadcast_to(h[SUBLANES - 1:SUBLANES, :], (SUBLANES, w))

    carry_ref[...] = lax.fori_loop(0, tl // SUBLANES, step, carry_ref[...], unroll=4)
    o_ref[...] = (u_ref[...] * _gelu_tanh(gate_ref[...])).astype(BF16)


def _lru(proj3, cw, cb, wr_bd, wi_bd, br, bi, lam, width):
    b, l, _ = proj3.shape
    vec = lambda: pl.BlockSpec((1, width), lambda bi_, li: (0, 0))
    return pl.pallas_call(
        _lru_body,
        grid=(b, l // LRU_TL),
        in_specs=[
            pl.BlockSpec((None, LRU_TL, width), lambda bi_, li: (bi_, li, 0)),
            pl.BlockSpec((None, LRU_TL, width), lambda bi_, li: (bi_, li, 1)),
            pl.BlockSpec((CONV_WIDTH, width), lambda bi_, li: (0, 0)),
            vec(),
            pl.BlockSpec(wr_bd.shape, lambda bi_, li: (0, 0, 0)),
            pl.BlockSpec(wi_bd.shape, lambda bi_, li: (0, 0, 0)),
            vec(), vec(), vec(),
        ],
        out_specs=pl.BlockSpec((None, LRU_TL, width), lambda bi_, li: (bi_, li, 0)),
        out_shape=jax.ShapeDtypeStruct((b, l, width), BF16),
        scratch_shapes=[
            pltpu.VMEM((LRU_TL + SUBLANES, width), F32),
            pltpu.VMEM((LRU_TL, width), F32),
            pltpu.VMEM((LRU_TL, width), F32),
            pltpu.VMEM((SUBLANES, width), F32),
        ],
        compiler_params=_params("parallel", "arbitrary"),
        name="lru",
    )(proj3, proj3, cw, cb, wr_bd, wi_bd, br, bi, lam)


def _block_diag_slabs(w):
    nb, bs, _ = w.shape
    per = GATE_SLAB // bs
    w = w.reshape(nb // per, per, bs, bs)
    eye = jnp.eye(per, dtype=w.dtype)
    bd = jnp.einsum("spij,pq->spiqj", w, eye).reshape(nb // per, GATE_SLAB, GATE_SLAB)
    return bd.astype(BF16)


def _ssd_body(n_heads, xbc_ref, z_ref, dt_ref, cw_ref, cb_ref, dtb_ref, alog_ref, dvec_ref, nrm_ref,
              o_ref, pad_ref, state_ref):
    first = pl.program_id(1) == 0
    t = SSD_CHUNK
    width = n_heads * SSD_HEAD_DIM
    gw = SSD_STATE
    heads_per_group = n_heads // SSD_GROUPS

    @pl.when(first)
    def _():
        state_ref[...] = jnp.zeros_like(state_ref)

    xbc = _silu(_causal_conv(pad_ref, xbc_ref[...], cw_ref, cb_ref, t, first))
    xs = xbc[:, :width]
    dt = _softplus(dt_ref[...] + dtb_ref[...])
    adt = dt * (-jnp.exp(alog_ref[...]))
    row = lax.broadcasted_iota(jnp.int32, (t, t), 0)
    col = lax.broadcasted_iota(jnp.int32, (t, t), 1)
    causal = row >= col
    tri = jnp.where(causal, 1.0, 0.0).astype(F32)
    cs = jnp.dot(tri, adt, preferred_element_type=F32, precision=lax.Precision.HIGHEST)
    cs_t = cs.T
    cs_last = cs[t - 1:t, :]
    ecs = jnp.exp(cs)
    dte = jnp.exp(cs_last - cs)
    cdec = jnp.exp(cs_last)

    lane = lax.broadcasted_iota(jnp.int32, (t, LANES), 1)
    lo = lane < SSD_HEAD_DIM
    rlo = lax.broadcasted_iota(jnp.int32, (LANES, gw), 0) < SSD_HEAD_DIM

    cbs = []
    for g in range(SSD_GROUPS):
        bm = xbc[:, width + g * gw: width + (g + 1) * gw].astype(BF16)
        cm = xbc[:, width + (SSD_GROUPS + g) * gw: width + (SSD_GROUPS + g + 1) * gw].astype(BF16)
        cb = lax.dot_general(cm, bm, (((1,), (1,)), ((), ())), preferred_element_type=F32)
        cbs.append((bm, cm, cb))

    ys = []
    for p in range(n_heads // 2):
        h0, h1 = 2 * p, 2 * p + 1
        bm, cm, cb = cbs[h0 // heads_per_group]
        sl = slice(p * LANES, (p + 1) * LANES)
        xs_p = xs[:, sl]
        pick = lambda v: jnp.where(lo, v[:, h0:h0 + 1], v[:, h1:h1 + 1])
        xdt = xs_p * pick(dt)
        ms = []
        for h in (h0, h1):
            seg = cs[:, h:h + 1] - cs_t[h:h + 1, :]
            ms.append((cb * jnp.where(causal, jnp.exp(seg), 0.0)).astype(BF16))
        m_cat = jnp.concatenate(ms, axis=1)
        xdt_bd = jnp.concatenate([jnp.where(lo, xdt, 0.0), jnp.where(lo, 0.0, xdt)], axis=0).astype(BF16)
        y = jnp.dot(m_cat, xdt_bd, preferred_element_type=F32)
        prev = state_ref[sl, :]
        y_off = lax.dot_general(cm, prev.astype(BF16), (((1,), (1,)), ((), ())), preferred_element_type=F32)
        y = y + y_off * pick(ecs)
        xw = (xdt * pick(dte)).astype(BF16)
        st = lax.dot_general(xw, bm, (((0,), (0,)), ((), ())), preferred_element_type=F32)
        dec = jnp.where(rlo, cdec[:, h0:h0 + 1], cdec[:, h1:h1 + 1])
        state_ref[sl, :] = prev * dec + st
        ys.append(y + dvec_ref[:, sl] * xs_p)

    y = jnp.concatenate(ys, axis=1) * _silu(z_ref[...])
    gsz = width // SSD_GROUPS
    outs = []
    for g in range(SSD_GROUPS):
        yg = y[:, g * gsz:(g + 1) * gsz]
        outs.append(yg * lax.rsqrt(jnp.mean(yg * yg, axis=-1, keepdims=True) + NORM_EPS))
    o_ref[...] = (jnp.concatenate(outs, axis=1) * nrm_ref[...]).astype(BF16)


def _ssd(proj3, dt3, cw, cb, dt_bias, a_log, d_vec, nrm, n_heads):
    b, l, _ = proj3.shape
    width = n_heads * SSD_HEAD_DIM
    conv_ch = width + 2 * SSD_GROUPS * SSD_STATE
    xbc_block = (3 * width) // conv_ch
    assert xbc_block * conv_ch == 3 * width
    const = lambda shape: pl.BlockSpec(shape, lambda bi_, ci: (0, 0))
    return pl.pallas_call(
        functools.partial(_ssd_body, n_heads),
        grid=(b, l // SSD_CHUNK),
        in_specs=[
            pl.BlockSpec((None, SSD_CHUNK, conv_ch), lambda bi_, ci: (bi_, ci, xbc_block)),
            pl.BlockSpec((None, SSD_CHUNK, width), lambda bi_, ci: (bi_, ci, 2)),
            pl.BlockSpec((None, SSD_CHUNK, LANES), lambda bi_, ci: (bi_, ci, 0)),
            const((CONV_WIDTH, conv_ch)), const((1, conv_ch)),
            const((1, LANES)), const((1, LANES)), const((1, width)), const((1, width)),
        ],
        out_specs=pl.BlockSpec((None, SSD_CHUNK, width), lambda bi_, ci: (bi_, ci, 0)),
        out_shape=jax.ShapeDtypeStruct((b, l, width), BF16),
        scratch_shapes=[
            pltpu.VMEM((SSD_CHUNK + SUBLANES, conv_ch), F32),
            pltpu.VMEM((width, SSD_STATE), F32),
        ],
        compiler_params=_params("parallel", "arbitrary"),
        name="ssd",
    )(proj3, proj3, dt3, cw, cb, dt_bias, a_log, d_vec, nrm)


QKV_TM = 512


def _head_norm_rope(y, gain, cos, sin_dn, sin_up):
    tm, w = y.shape
    lane = lax.broadcasted_iota(jnp.int32, (tm, w), 1)
    ss = y * y
    k = 1
    while k < ATT_HEAD_DIM:
        up = pltpu.roll(ss, k, 1)
        dn = pltpu.roll(ss, w - k, 1)
        ss = ss + jnp.where((lane & k) == 0, dn, up)
        k *= 2
    yn = y * lax.rsqrt(ss * (1.0 / ATT_HEAD_DIM) + NORM_EPS) * gain
    half = ROPE_DIM // 2
    reps = w // LANES
    tile = lambda tbl: jnp.concatenate([tbl] * reps, axis=1)
    return (yn * tile(cos) + pltpu.roll(yn, w - half, 1) * tile(sin_dn)
            + pltpu.roll(yn, half, 1) * tile(sin_up))


def _qkv_body(x_ref, g_ref, w_ref, qg_ref, kg_ref, cos_ref, sdn_ref, sup_ref, *refs):
    outs, (h_ref, s_ref) = refs[:9], refs[9:]
    j = pl.program_id(2)
    tm = x_ref.shape[0]

    @pl.when(j == 0)
    def _():
        h_ref[...] = _rmsnorm(x_ref[...], g_ref[...]).astype(BF16)

    acc = jnp.dot(h_ref[...], w_ref[...], preferred_element_type=F32)

    def emit(y, o1, o4, o16):
        o1[...] = y.astype(BF16)
        for c in range(y.shape[1] // LANES):
            sl = slice(c * LANES, (c + 1) * LANES)
            s_ref[c] = y[:, sl]
            for d, o in ((4, o4), (16, o16)):
                for r in range(d):
                    o[r, :, sl] = s_ref[c, pl.ds(r, tm // d, stride=d), :].astype(BF16)

    rope = (cos_ref[...], sdn_ref[...], sup_ref[...])

    @pl.when(j == 0)
    def _():
        emit(_head_norm_rope(acc, qg_ref[...], *rope) * (ATT_HEAD_DIM ** -0.5), *outs[0:3])

    @pl.when(j == 1)
    def _():
        emit(_head_norm_rope(acc, kg_ref[...], *rope), *outs[3:6])

    @pl.when(j == 2)
    def _():
        emit(acc, *outs[6:9])


def _qkv(x3, g, w_qkv, qg, kg, cos, sdn, sup):
    b, l, d = x3.shape
    tm = QKV_TM
    out_specs, out_shape = [], []
    for _ in range(3):
        out_specs.append(pl.BlockSpec((None, tm, d), lambda bi_, i, j: (bi_, i, 0)))
        out_shape.append(jax.ShapeDtypeStruct((b, l, d), BF16))
        for dil in DILATIONS[1:]:
            out_specs.append(pl.BlockSpec((None, dil, tm // dil, d), lambda bi_, i, j: (bi_, 0, i, 0)))
            out_shape.append(jax.ShapeDtypeStruct((b, dil, l // dil, d), BF16))
    vec = lambda: pl.BlockSpec((1, d), lambda bi_, i, j: (0, 0))
    tbl = lambda: pl.BlockSpec((tm, LANES), lambda bi_, i, j: (i, 0))
    return pl.pallas_call(
        _qkv_body,
        grid=(b, l // tm, 3),
        in_specs=[
            pl.BlockSpec((None, tm, d), lambda bi_, i, j: (bi_, i, 0)),
            vec(),
            pl.BlockSpec((d, d), lambda bi_, i, j: (0, j)),
            vec(), vec(), tbl(), tbl(), tbl(),
        ],
        out_specs=out_specs,
        out_shape=out_shape,
        scratch_shapes=[pltpu.VMEM((tm, d), BF16), pltpu.VMEM((d // LANES, tm, LANES), F32)],
        compiler_params=_params("parallel", "parallel", "arbitrary"),
        name="qkv",
    )(x3, g, w_qkv, qg, kg, cos, sdn, sup)


def _rope_tables(l):
    half = ROPE_DIM // 2
    pos = jnp.arange(l, dtype=F32)
    inv = ROPE_THETA ** (-2.0 * jnp.arange(half, dtype=F32) / ROPE_DIM)
    ang = pos[:, None] * inv[None, :]
    cos, sin = jnp.cos(ang), jnp.sin(ang)
    pad = ATT_HEAD_DIM - ROPE_DIM
    ones = jnp.ones((l, pad), F32)
    zeros = jnp.zeros((l, pad), F32)
    zh = jnp.zeros((l, half), F32)
    cos_h = jnp.concatenate([cos, cos, ones], axis=1)
    sdn_h = jnp.concatenate([-sin, zh, zeros], axis=1)
    sup_h = jnp.concatenate([zh, sin, zeros], axis=1)
    rep = LANES // ATT_HEAD_DIM
    return tuple(jnp.concatenate([tb] * rep, axis=1) for tb in (cos_h, sdn_h, sup_h))


def _band_block(q, k_slab, v_slab, first):
    nq, nk = q.shape[0], k_slab.shape[0]
    qi = lax.broadcasted_iota(jnp.int32, (nq, nk), 0)
    kj = lax.broadcasted_iota(jnp.int32, (nq, nk), 1)
    valid = (kj >= qi) & (kj <= qi + SPAN) & (kj >= jnp.where(first, SPAN, 0))
    lo = lax.broadcasted_iota(jnp.int32, (nq, LANES), 1) < ATT_HEAD_DIM
    ms, ls, pvs = [], [], []
    for p in range(q.shape[1] // LANES):
        sl = slice(p * LANES, (p + 1) * LANES)
        qp, kp, vp = q[:, sl], k_slab[:, sl], v_slab[:, sl]
        res = []
        for keep in (lo, jnp.logical_not(lo)):
            qm = jnp.where(keep, qp, jnp.zeros_like(qp))
            s = lax.dot_general(qm, kp, (((1,), (1,)), ((), ())), preferred_element_type=F32)
            s = jnp.where(valid, s, -jnp.inf)
            mx = jnp.max(s, axis=1, keepdims=True)
            pr = jnp.exp(s - mx)
            den = jnp.sum(pr, axis=1, keepdims=True)
            pv = jnp.dot(pr.astype(BF16), vp, preferred_element_type=F32)
            res.append((mx, den, pv))
        ms.append(jnp.where(lo, res[0][0], res[1][0]))
        ls.append(jnp.where(lo, res[0][1], res[1][1]))
        pvs.append(jnp.where(lo, res[0][2], res[1][2]))
    return ms, ls, pvs


def _attn_body(q1, k1c, k1p, v1c, v1p, q4, k4c, k4p, v4c, v4p, q16, k16c, k16p, v16c, v16p,
               o_ref, acc_ref, m_ref, l_ref):
    first = pl.program_id(2) == 0
    not_first = False

    def slabs(kc, kp, vc, vp, nb):
        if isinstance(nb, int) and nb == 0:
            return (jnp.concatenate([kp[...], kc[0:SPAN, :]], axis=0),
                    jnp.concatenate([vp[...], vc[0:SPAN, :]], axis=0))
        rows = pl.ds(pl.multiple_of((nb - 1) * SPAN, SPAN), 2 * SPAN)
        return kc[rows, :], vc[rows, :]

    def assign(rows, res):
        for c, (m_new, l_new, pv_new) in enumerate(zip(*res)):
            m_ref[c, rows, :], l_ref[c, rows, :], acc_ref[c, rows, :] = m_new, l_new, pv_new

    def merge(rows, res):
        for c, (m_new, l_new, pv_new) in enumerate(zip(*res)):
            m_old = m_ref[c, rows, :]
            m = jnp.maximum(m_old, m_new)
            e_old = jnp.exp(m_old - m)
            e_new = jnp.exp(m_new - m)
            l_ref[c, rows, :] = e_old * l_ref[c, rows, :] + e_new * l_new
            acc_ref[c, rows, :] = e_old * acc_ref[c, rows, :] + e_new * pv_new
            m_ref[c, rows, :] = m

    n1 = SUPER // SPAN
    assign(pl.ds(0, SPAN), _band_block(q1[0:SPAN, :], *slabs(k1c, k1p, v1c, v1p, 0), first))

    def body1(nb, c):
        rows = pl.ds(pl.multiple_of(nb * SPAN, SPAN), SPAN)
        assign(rows, _band_block(q1[rows, :], *slabs(k1c, k1p, v1c, v1p, nb), not_first))
        return c

    lax.fori_loop(1, n1, body1, 0)

    d = DILATIONS[1]
    n4 = SUPER // (SPAN * d)
    for r in range(d):
        kc, kp, vc, vp = k4c.at[r], k4p.at[r], v4c.at[r], v4p.at[r]
        merge(pl.ds(r, SPAN, stride=d), _band_block(q4[r, 0:SPAN, :], *slabs(kc, kp, vc, vp, 0), first))

        def body4(nb, c, r=r, kc=kc, kp=kp, vc=vc, vp=vp):
            qrows = pl.ds(pl.multiple_of(nb * SPAN, SPAN), SPAN)
            res = _band_block(q4[r, qrows, :], *slabs(kc, kp, vc, vp, nb), not_first)
            merge(pl.ds(nb * (SPAN * d) + r, SPAN, stride=d), res)
            return c

        lax.fori_loop(1, n4, body4, 0)

    d16 = DILATIONS[2]

    def body16(r, c):
        k_slab = jnp.concatenate([k16p[r], k16c[r]], axis=0)
        v_slab = jnp.concatenate([v16p[r], v16c[r]], axis=0)
        merge(pl.ds(r, SPAN, stride=d16), _band_block(q16[r], k_slab, v_slab, first))
        return c

    lax.fori_loop(0, d16, body16, 0)

    for c in range(acc_ref.shape[0]):
        o_ref[:, c * LANES:(c + 1) * LANES] = (acc_ref[c] / l_ref[c]).astype(BF16)


def _attention(qkv_outs):
    q1, q4, q16, k1, k4, k16, v1, v4, v16 = qkv_outs
    b, l, d = q1.shape
    w = ATT_LANES
    d4, d16 = DILATIONS[1], DILATIONS[2]
    cur1 = pl.BlockSpec((None, SUPER, w), lambda bi_, g, s: (bi_, s, g))
    prev1 = pl.BlockSpec((None, SPAN, w), lambda bi_, g, s: (bi_, jnp.maximum(s * (SUPER // SPAN) - 1, 0), g))
    cur4 = pl.BlockSpec((None, d4, SUPER // d4, w), lambda bi_, g, s: (bi_, 0, s, g))
    prev4 = pl.BlockSpec((None, d4, SPAN, w),
                         lambda bi_, g, s: (bi_, 0, jnp.maximum(s * (SUPER // (d4 * SPAN)) - 1, 0), g))
    cur16 = pl.BlockSpec((None, d16, SPAN, w), lambda bi_, g, s: (bi_, 0, s, g))
    prev16 = pl.BlockSpec((None, d16, SPAN, w), lambda bi_, g, s: (bi_, 0, jnp.maximum(s - 1, 0), g))
    return pl.pallas_call(
        _attn_body,
        grid=(b, d // w, l // SUPER),
        in_specs=[cur1, cur1, prev1, cur1, prev1,
                  cur4, cur4, prev4, cur4, prev4,
                  cur16, cur16, prev16, cur16, prev16],
        out_specs=pl.BlockSpec((None, SUPER, w), lambda bi_, g, s: (bi_, s, g)),
        out_shape=jax.ShapeDtypeStruct((b, l, d), BF16),
        scratch_shapes=[pltpu.VMEM((w // LANES, SUPER, LANES), F32)] * 3,
        compiler_params=_params("parallel", "parallel", "arbitrary"),
        name="dilated_attention",
    )(q1, k1, k1, v1, v1, q4, k4, k4, v4, v4, q16, k16, k16, v16, v16)


def _recurrent_layer(x, b, l, norm_g, w_in, lru_conv_w, lru_conv_b, lru_w_r, lru_b_r, lru_w_i, lru_b_i,
                     lru_lambda, ssd_conv_w, ssd_conv_b, ssd_dt_bias, ssd_a_log, ssd_d, ssd_norm, w_out):
    d = x.shape[1]
    width = lru_lambda.shape[0]
    n_heads = ssd_a_log.shape[0]
    n_main = w_in.shape[1] - n_heads
    row = lambda v: v.reshape(1, -1)
    pad_lanes = lambda v: jnp.pad(v.reshape(1, -1), ((0, 0), (0, LANES - v.shape[0])))
    w_main = w_in[:, :n_main].astype(BF16)
    w_dt = jnp.pad(w_in[:, n_main:], ((0, 0), (0, LANES - n_heads))).astype(BF16)
    proj, dt = _rec_in(x, row(norm_g), w_main, w_dt)
    proj3 = proj.reshape(b, l, n_main)
    out_a = _lru(proj3, lru_conv_w, row(lru_conv_b), _block_diag_slabs(lru_w_r), _block_diag_slabs(lru_w_i),
                 row(lru_b_r), row(lru_b_i), row(lru_lambda), width)
    out_b = _ssd(proj3, dt.reshape(b, l, LANES), ssd_conv_w, row(ssd_conv_b), pad_lanes(ssd_dt_bias),
                 pad_lanes(ssd_a_log), row(jnp.repeat(ssd_d, SSD_HEAD_DIM)), row(ssd_norm), n_heads)
    w_out = w_out.astype(BF16)
    return _proj_residual(x, [out_a.reshape(b * l, width), out_b.reshape(b * l, -1)],
                          [w_out[:width], w_out[width:]])


def _attention_layer(x, b, l, norm_g, w_qkv, q_norm, k_norm, w_out):
    d = x.shape[1]
    heads = d // ATT_HEAD_DIM
    row = lambda v: v.reshape(1, -1)
    tables = _rope_tables(l)
    outs = _qkv(x.reshape(b, l, d), row(norm_g), w_qkv.astype(BF16),
                row(jnp.tile(q_norm, heads)), row(jnp.tile(k_norm, heads)), *tables)
    o = _attention(outs)
    return _proj_residual(x, [o.reshape(b * l, d)], [w_out.astype(BF16)])


def kernel(x, rec_norm, rec_w_in, lru_conv_w, lru_conv_b, lru_w_r, lru_b_r, lru_w_i, lru_b_i, lru_lambda,
           ssd_conv_w, ssd_conv_b, ssd_dt_bias, ssd_a_log, ssd_d, ssd_norm, rec_w_out, att_norm, att_w_qkv,
           att_q_norm, att_k_norm, att_w_out, ffn_norm, ffn_w_gate_up, ffn_w_down):
    b, l, d = x.shape
    depth = ffn_norm.shape[0]
    h = x.reshape(b * l, d)
    for layer in range(depth):
        i = layer // 2
        if layer % 2 == 0:
            h = _recurrent_layer(h, b, l, rec_norm[i], rec_w_in[i], lru_conv_w[i], lru_conv_b[i], lru_w_r[i],
                                 lru_b_r[i], lru_w_i[i], lru_b_i[i], lru_lambda[i], ssd_conv_w[i],
                                 ssd_conv_b[i], ssd_dt_bias[i], ssd_a_log[i], ssd_d[i], ssd_norm[i],
                                 rec_w_out[i])
        else:
            h = _attention_layer(h, b, l, att_norm[i], att_w_qkv[i], att_q_norm[i], att_k_norm[i],
                                 att_w_out[i])
        wgu, wd = _prep_ffn_weights(ffn_w_gate_up[layer], ffn_w_down[layer])
        h = _ffn(h, ffn_norm[layer].reshape(1, d), wgu, wd)
    return h.reshape(b, l, d)
```

```python
import functools
import math

import jax
import jax.numpy as jnp
import numpy as np
from jax import lax
from jax.experimental import pallas as pl
from jax.experimental.pallas import tpu as pltpu

F32 = jnp.float32
BF16 = jnp.bfloat16

NORM_EPS = 1e-6
CONV_WIDTH = 4
LRU_BLOCK = 64
LRU_C = 8.0
GATE_SLAB = 256
SSD_HEAD_DIM = 64
SSD_GROUPS = 2
SSD_STATE = 128
SSD_CHUNK = 128
ATT_HEAD_DIM = 64
ROPE_DIM = 16
ROPE_THETA = 500000.0
SPAN = 128
DILATIONS = (1, 4, 16)
SUPER = SPAN * DILATIONS[-1]
ATT_LANES = 256
LANES = 128
SUBLANES = 8
VMEM_LIMIT_BYTES = 56 * 1024 * 1024


def _params(*semantics):
    return pltpu.CompilerParams(dimension_semantics=semantics, vmem_limit_bytes=VMEM_LIMIT_BYTES)


def _rmsnorm(x, g):
    ms = jnp.mean(x * x, axis=-1, keepdims=True)
    return x * lax.rsqrt(ms + NORM_EPS) * g


def _sigmoid(x):
    return 1.0 / (1.0 + jnp.exp(-x))


def _silu(x):
    return x * _sigmoid(x)


def _softplus(x):
    return jnp.maximum(x, 0.0) + jnp.log1p(jnp.exp(-jnp.abs(x)))


def _gelu_tanh(x):
    c = math.sqrt(2.0 / math.pi)
    return x * (0.5 * (1.0 + jnp.tanh(c * (x + 0.044715 * (x * x * x)))))


FFN_TM = 512
FFN_TH = 256


def _ffn_body(x_ref, g_ref, wgu_ref, wd_ref, o_ref, h_ref, act_ref):
    x = x_ref[...]
    h_ref[...] = _rmsnorm(x, g_ref[...]).astype(BF16)
    for c in range(act_ref.shape[1] // FFN_TH):
        gu = jnp.dot(h_ref[...], wgu_ref[:, 2 * c * FFN_TH:2 * (c + 1) * FFN_TH], preferred_element_type=F32)
        act_ref[:, c * FFN_TH:(c + 1) * FFN_TH] = (_silu(gu[:, :FFN_TH]) * gu[:, FFN_TH:]).astype(BF16)
    o_ref[...] = x + jnp.dot(act_ref[...], wd_ref[...], preferred_element_type=F32)


def _resident(shape):
    return pl.BlockSpec(shape, lambda *_: (0,) * len(shape), pipeline_mode=pl.Buffered(1))


def _ffn(x, g, wgu, wd):
    t, d = x.shape
    hid = wd.shape[0]
    return pl.pallas_call(
        _ffn_body,
        grid=(t // FFN_TM,),
        in_specs=[
            pl.BlockSpec((FFN_TM, d), lambda i: (i, 0)),
            _resident((1, d)),
            _resident(wgu.shape),
            _resident(wd.shape),
        ],
        out_specs=pl.BlockSpec((FFN_TM, d), lambda i: (i, 0)),
        out_shape=jax.ShapeDtypeStruct((t, d), F32),
        scratch_shapes=[pltpu.VMEM((FFN_TM, d), BF16), pltpu.VMEM((FFN_TM, hid), BF16)],
        compiler_params=_params("parallel"),
        name="ffn",
    )(x, g, wgu, wd)


def _prep_ffn_weights(w_gate_up, w_down):
    d, two_h = w_gate_up.shape
    hid = two_h // 2
    nh = hid // FFN_TH
    wg = w_gate_up[:, :hid].reshape(d, nh, FFN_TH)
    wu = w_gate_up[:, hid:].reshape(d, nh, FFN_TH)
    wgu = jnp.concatenate([wg, wu], axis=2).reshape(d, 2 * hid).astype(BF16)
    return wgu, w_down.astype(BF16)


PROJ_TM = 512


def _proj_body(n_in, *refs):
    x_ref = refs[0]
    a_refs = refs[1:1 + n_in]
    w_refs = refs[1 + n_in:1 + 2 * n_in]
    o_ref = refs[1 + 2 * n_in]
    acc = x_ref[...]
    for a_ref, w_ref in zip(a_refs, w_refs):
        acc = acc + jnp.dot(a_ref[...], w_ref[...], preferred_element_type=F32)
    o_ref[...] = acc


def _proj_residual(x, acts, weights):
    t, d = x.shape
    n_in = len(acts)
    in_specs = [pl.BlockSpec((PROJ_TM, d), lambda i: (i, 0))]
    in_specs += [pl.BlockSpec((PROJ_TM, a.shape[1]), lambda i: (i, 0)) for a in acts]
    in_specs += [_resident(w.shape) for w in weights]
    return pl.pallas_call(
        functools.partial(_proj_body, n_in),
        grid=(t // PROJ_TM,),
        in_specs=in_specs,
        out_specs=pl.BlockSpec((PROJ_TM, d), lambda i: (i, 0)),
        out_shape=jax.ShapeDtypeStruct((t, d), F32),
        compiler_params=_params("parallel"),
        name="proj_residual",
    )(x, *acts, *weights)


RECIN_TM = 512
RECIN_TN = 512


def _recin_body(x_ref, g_ref, w_ref, wdt_ref, o_ref, dt_ref, h_ref):
    h_ref[...] = _rmsnorm(x_ref[...], g_ref[...]).astype(BF16)
    dt_ref[...] = jnp.dot(h_ref[...], wdt_ref[...], preferred_element_type=F32)
    for c in range(w_ref.shape[1] // RECIN_TN):
        cols = slice(c * RECIN_TN, (c + 1) * RECIN_TN)
        o_ref[:, cols] = jnp.dot(h_ref[...], w_ref[:, cols], preferred_element_type=F32)


def _rec_in(x, g, w_main, w_dt):
    t, d = x.shape
    n = w_main.shape[1]
    return pl.pallas_call(
        _recin_body,
        grid=(t // RECIN_TM,),
        in_specs=[
            pl.BlockSpec((RECIN_TM, d), lambda i: (i, 0)),
            _resident((1, d)),
            _resident(w_main.shape),
            _resident(w_dt.shape),
        ],
        out_specs=[
            pl.BlockSpec((RECIN_TM, n), lambda i: (i, 0)),
            pl.BlockSpec((RECIN_TM, LANES), lambda i: (i, 0)),
        ],
        out_shape=[jax.ShapeDtypeStruct((t, n), F32), jax.ShapeDtypeStruct((t, LANES), F32)],
        scratch_shapes=[pltpu.VMEM((RECIN_TM, d), BF16)],
        compiler_params=_params("parallel"),
        name="rec_in",
    )(x, g, w_main, w_dt)


def _causal_conv(pad_ref, x, w_ref, b_ref, rows, first):
    @pl.when(first)
    def _():
        pad_ref[0:SUBLANES, :] = jnp.zeros((SUBLANES, pad_ref.shape[1]), F32)

    pad_ref[SUBLANES:SUBLANES + rows, :] = x
    y = b_ref[...] + w_ref[CONV_WIDTH - 1:CONV_WIDTH, :] * x
    for j in range(CONV_WIDTH - 1):
        off = SUBLANES - (CONV_WIDTH - 1) + j
        y = y + w_ref[j:j + 1, :] * pad_ref[off:off + rows, :]
    pad_ref[0:SUBLANES, :] = pad_ref[rows:rows + SUBLANES, :]
    return y


LRU_TL = 256


def _lru_body(x_ref, gate_ref, cw_ref, cb_ref, wr_ref, wi_ref, br_ref, bi_ref, lam_ref,
              o_ref, pad_ref, a_ref, u_ref, carry_ref):
    first = pl.program_id(1) == 0
    tl, w = x_ref.shape

    @pl.when(first)
    def _():
        carry_ref[...] = jnp.zeros_like(carry_ref)

    xc = _causal_conv(pad_ref, x_ref[...], cw_ref, cb_ref, tl, first)
    xcb = xc.astype(BF16)
    pre_r, pre_i = [], []
    for s in range(w // GATE_SLAB):
        slab = xcb[:, s * GATE_SLAB:(s + 1) * GATE_SLAB]
        pre_r.append(jnp.dot(slab, wr_ref[s], preferred_element_type=F32))
        pre_i.append(jnp.dot(slab, wi_ref[s], preferred_element_type=F32))
    r = _sigmoid(jnp.concatenate(pre_r, axis=1) + br_ref[...])
    i = _sigmoid(jnp.concatenate(pre_i, axis=1) + bi_ref[...])
    log_a = (-LRU_C * r) * _softplus(-lam_ref[...])
    a = jnp.exp(log_a)
    u = jnp.sqrt(1.0 - a * a) * (i * xc)

    sub = lax.broadcasted_iota(jnp.int32, (tl, w), 0) & (SUBLANES - 1)
    for s in (1, 2, 4):
        keep = sub >= s
        a_prev = jnp.where(keep, pltpu.roll(a, s, 0), 1.0)
        u_prev = jnp.where(keep, pltpu.roll(u, s, 0), 0.0)
        u = u + a * u_prev
        a = a * a_prev
    a_ref[...] = a
    u_ref[...] = u

    def step(gi, h_prev):
        rows = pl.ds(pl.multiple_of(gi * SUBLANES, SUBLANES), SUBLANES)
        h = a_ref[rows, :] * h_prev + u_ref[rows, :]
        u_ref[rows, :] = h
        return jnp.broadcast_to(h[SUBLANES - 1:SUBLANES, :], (SUBLANES, w))

    carry_ref[...] = lax.fori_loop(0, tl // SUBLANES, step, carry_ref[...], unroll=4)
    o_ref[...] = (u_ref[...] * _gelu_tanh(gate_ref[...])).astype(BF16)


def _lru(proj3, cw, cb, wr_bd, wi_bd, br, bi, lam, width):
    b, l, _ = proj3.shape
    vec = lambda: pl.BlockSpec((1, width), lambda bi_, li: (0, 0))
    return pl.pallas_call(
        _lru_body,
        grid=(b, l // LRU_TL),
        in_specs=[
            pl.BlockSpec((None, LRU_TL, width), lambda bi_, li: (bi_, li, 0)),
            pl.BlockSpec((None, LRU_TL, width), lambda bi_, li: (bi_, li, 1)),
            pl.BlockSpec((CONV_WIDTH, width), lambda bi_, li: (0, 0)),
            vec(),
            pl.BlockSpec(wr_bd.shape, lambda bi_, li: (0, 0, 0)),
            pl.BlockSpec(wi_bd.shape, lambda bi_, li: (0, 0, 0)),
            vec(), vec(), vec(),
        ],
        out_specs=pl.BlockSpec((None, LRU_TL, width), lambda bi_, li: (bi_, li, 0)),
        out_shape=jax.ShapeDtypeStruct((b, l, width), BF16),
        scratch_shapes=[
            pltpu.VMEM((LRU_TL + SUBLANES, width), F32),
            pltpu.VMEM((LRU_TL, width), F32),
            pltpu.VMEM((LRU_TL, width), F32),
            pltpu.VMEM((SUBLANES, width), F32),
        ],
        compiler_params=_params("parallel", "arbitrary"),
        name="lru",
    )(proj3, proj3, cw, cb, wr_bd, wi_bd, br, bi, lam)


def _block_diag_slabs(w):
    nb, bs, _ = w.shape
    per = GATE_SLAB // bs
    w = w.reshape(nb // per, per, bs, bs)
    eye = jnp.eye(per, dtype=w.dtype)
    bd = jnp.einsum("spij,pq->spiqj", w, eye).reshape(nb // per, GATE_SLAB, GATE_SLAB)
    return bd.astype(BF16)


def _ssd_body(n_heads, xbc_ref, z_ref, dt_ref, cw_ref, cb_ref, dtb_ref, alog_ref, dvec_ref, nrm_ref,
              o_ref, pad_ref, state_ref):
    first = pl.program_id(1) == 0
    t = SSD_CHUNK
    width = n_heads * SSD_HEAD_DIM
    gw = SSD_STATE
    heads_per_group = n_heads // SSD_GROUPS

    @pl.when(first)
    def _():
        state_ref[...] = jnp.zeros_like(state_ref)

    xbc = _silu(_causal_conv(pad_ref, xbc_ref[...], cw_ref, cb_ref, t, first))
    xs = xbc[:, :width]
    dt = _softplus(dt_ref[...] + dtb_ref[...])
    adt = dt * (-jnp.exp(alog_ref[...]))
    row = lax.broadcasted_iota(jnp.int32, (t, t), 0)
    col = lax.broadcasted_iota(jnp.int32, (t, t), 1)
    causal = row >= col
    tri = jnp.where(causal, 1.0, 0.0).astype(F32)
    cs = jnp.dot(tri, adt, preferred_element_type=F32, precision=lax.Precision.HIGHEST)
    cs_t = cs.T
    cs_last = cs[t - 1:t, :]
    ecs = jnp.exp(cs)
    dte = jnp.exp(cs_last - cs)
    cdec = jnp.exp(cs_last)

    lane = lax.broadcasted_iota(jnp.int32, (t, LANES), 1)
    lo = lane < SSD_HEAD_DIM
    rlo = lax.broadcasted_iota(jnp.int32, (LANES, gw), 0) < SSD_HEAD_DIM

    cbs = []
    for g in range(SSD_GROUPS):
        bm = xbc[:, width + g * gw: width + (g + 1) * gw].astype(BF16)
        cm = xbc[:, width + (SSD_GROUPS + g) * gw: width + (SSD_GROUPS + g + 1) * gw].astype(BF16)
        cb = lax.dot_general(cm, bm, (((1,), (1,)), ((), ())), preferred_element_type=F32)
        cbs.append((bm, cm, cb))

    ys = []
    for p in range(n_heads // 2):
        h0, h1 = 2 * p, 2 * p + 1
        bm, cm, cb = cbs[h0 // heads_per_group]
        sl = slice(p * LANES, (p + 1) * LANES)
        xs_p = xs[:, sl]
        pick = lambda v: jnp.where(lo, v[:, h0:h0 + 1], v[:, h1:h1 + 1])
        xdt = xs_p * pick(dt)
        ms = []
        for h in (h0, h1):
            seg = cs[:, h:h + 1] - cs_t[h:h + 1, :]
            ms.append((cb * jnp.where(causal, jnp.exp(seg), 0.0)).astype(BF16))
        m_cat = jnp.concatenate(ms, axis=1)
        xdt_bd = jnp.concatenate([jnp.where(lo, xdt, 0.0), jnp.where(lo, 0.0, xdt)], axis=0).astype(BF16)
        y = jnp.dot(m_cat, xdt_bd, preferred_element_type=F32)
        prev = state_ref[sl, :]
        y_off = lax.dot_general(cm, prev.astype(BF16), (((1,), (1,)), ((), ())), preferred_element_type=F32)
        y = y + y_off * pick(ecs)
        xw = (xdt * pick(dte)).astype(BF16)
        st = lax.dot_general(xw, bm, (((0,), (0,)), ((), ())), preferred_element_type=F32)
        dec = jnp.where(rlo, cdec[:, h0:h0 + 1], cdec[:, h1:h1 + 1])
        state_ref[sl, :] = prev * dec + st
        ys.append(y + dvec_ref[:, sl] * xs_p)

    y = jnp.concatenate(ys, axis=1) * _silu(z_ref[...])
    gsz = width // SSD_GROUPS
    outs = []
    for g in range(SSD_GROUPS):
        yg = y[:, g * gsz:(g + 1) * gsz]
        outs.append(yg * lax.rsqrt(jnp.mean(yg * yg, axis=-1, keepdims=True) + NORM_EPS))
    o_ref[...] = (jnp.concatenate(outs, axis=1) * nrm_ref[...]).astype(BF16)


def _ssd(proj3, dt3, cw, cb, dt_bias, a_log, d_vec, nrm, n_heads):
    b, l, _ = proj3.shape
    width = n_heads * SSD_HEAD_DIM
    conv_ch = width + 2 * SSD_GROUPS * SSD_STATE
    xbc_block = (3 * width) // conv_ch
    assert xbc_block * conv_ch == 3 * width
    const = lambda shape: pl.BlockSpec(shape, lambda bi_, ci: (0, 0))
    return pl.pallas_call(
        functools.partial(_ssd_body, n_heads),
        grid=(b, l // SSD_CHUNK),
        in_specs=[
            pl.BlockSpec((None, SSD_CHUNK, conv_ch), lambda bi_, ci: (bi_, ci, xbc_block)),
            pl.BlockSpec((None, SSD_CHUNK, width), lambda bi_, ci: (bi_, ci, 2)),
            pl.BlockSpec((None, SSD_CHUNK, LANES), lambda bi_, ci: (bi_, ci, 0)),
            const((CONV_WIDTH, conv_ch)), const((1, conv_ch)),
            const((1, LANES)), const((1, LANES)), const((1, width)), const((1, width)),
        ],
        out_specs=pl.BlockSpec((None, SSD_CHUNK, width), lambda bi_, ci: (bi_, ci, 0)),
        out_shape=jax.ShapeDtypeStruct((b, l, width), BF16),
        scratch_shapes=[
            pltpu.VMEM((SSD_CHUNK + SUBLANES, conv_ch), F32),
            pltpu.VMEM((width, SSD_STATE), F32),
        ],
        compiler_params=_params("parallel", "arbitrary"),
        name="ssd",
    )(proj3, proj3, dt3, cw, cb, dt_bias, a_log, d_vec, nrm)


QKV_TM = 512


def _split_dot(x, m):
    hi = x.astype(BF16)
    lo = (x - hi.astype(F32)).astype(BF16)
    return jnp.dot(hi, m, preferred_element_type=F32) + jnp.dot(lo, m, preferred_element_type=F32)


def _slab_map(fn, x):
    return jnp.concatenate([fn(x[:, c * GATE_SLAB:(c + 1) * GATE_SLAB]) for c in range(x.shape[1] // GATE_SLAB)],
                           axis=1)


def _head_norm_rope(y, gain, cos, sin, gsum, rot):
    ss = _slab_map(lambda v: _split_dot(v, gsum), y * y)
    yn = y * lax.rsqrt(ss * (1.0 / ATT_HEAD_DIM) + NORM_EPS) * gain
    partner = _slab_map(lambda v: _split_dot(v, rot), yn)
    tile = lambda tbl: jnp.concatenate([tbl] * (y.shape[1] // LANES), axis=1)
    return yn * tile(cos) + partner * tile(sin)


def _qkv_body(x_ref, g_ref, w_ref, qg_ref, kg_ref, cos_ref, sin_ref, gsum_ref, rot_ref, p4_ref, p16_ref,
              *refs):
    outs, h_ref = refs[:9], refs[9]
    j = pl.program_id(2)

    @pl.when(j == 0)
    def _():
        h_ref[...] = _rmsnorm(x_ref[...], g_ref[...]).astype(BF16)

    acc = jnp.dot(h_ref[...], w_ref[...], preferred_element_type=F32)

    def emit(y, o1, o4, o16):
        yb = y.astype(BF16)
        o1[...] = yb
        for perm_ref, o in ((p4_ref, o4), (p16_ref, o16)):
            rows = o.shape[1]
            yp = jnp.dot(perm_ref[...], yb, preferred_element_type=F32).astype(BF16)
            for r in range(o.shape[0]):
                o[r] = yp[r * rows:(r + 1) * rows, :]

    rope = (cos_ref[...], sin_ref[...], gsum_ref[...], rot_ref[...])

    @pl.when(j == 0)
    def _():
        emit(_head_norm_rope(acc, qg_ref[...], *rope) * (ATT_HEAD_DIM ** -0.5), *outs[0:3])

    @pl.when(j == 1)
    def _():
        emit(_head_norm_rope(acc, kg_ref[...], *rope), *outs[3:6])

    @pl.when(j == 2)
    def _():
        emit(acc, *outs[6:9])


def _qkv(x3, g, w_qkv, qg, kg, cos, sin):
    b, l, d = x3.shape
    tm = QKV_TM
    out_specs, out_shape = [], []
    for _ in range(3):
        out_specs.append(pl.BlockSpec((None, tm, d), lambda bi_, i, j: (bi_, i, 0)))
        out_shape.append(jax.ShapeDtypeStruct((b, l, d), BF16))
        for dil in DILATIONS[1:]:
            out_specs.append(pl.BlockSpec((None, dil, tm // dil, d), lambda bi_, i, j: (bi_, 0, i, 0)))
            out_shape.append(jax.ShapeDtypeStruct((b, dil, l // dil, d), BF16))
    consts = _qkv_constants(tm)
    tbl = lambda: pl.BlockSpec((tm, LANES), lambda bi_, i, j: (i, 0))
    return pl.pallas_call(
        _qkv_body,
        grid=(b, l // tm, 3),
        in_specs=[
            pl.BlockSpec((None, tm, d), lambda bi_, i, j: (bi_, i, 0)),
            _resident((1, d)),
            pl.BlockSpec((d, d), lambda bi_, i, j: (0, j)),
            _resident((1, d)), _resident((1, d)), tbl(), tbl(),
        ] + [_resident(c.shape) for c in consts],
        out_specs=out_specs,
        out_shape=out_shape,
        scratch_shapes=[pltpu.VMEM((tm, d), BF16)],
        compiler_params=_params("parallel", "parallel", "arbitrary"),
        name="qkv",
    )(x3, g, w_qkv, qg, kg, cos, sin, *consts)


def _qkv_constants(tm):
    lane = np.arange(GATE_SLAB)
    gsum = (lane[:, None] // ATT_HEAD_DIM == lane[None, :] // ATT_HEAD_DIM)
    half = ROPE_DIM // 2
    pos = lane % ATT_HEAD_DIM
    partner = np.where(pos < half, lane + half, np.where(pos < ROPE_DIM, lane - half, -1))
    rot = lane[:, None] == partner[None, :]
    perms = []
    for dil in DILATIONS[1:]:
        out_row = np.arange(tm)
        src = (out_row % (tm // dil)) * dil + out_row // (tm // dil)
        perms.append(src[:, None] == np.arange(tm)[None, :])
    return tuple(jnp.asarray(m, dtype=BF16) for m in (gsum, rot, *perms))


def _rope_tables(l):
    half = ROPE_DIM // 2
    pos = jnp.arange(l, dtype=F32)
    inv = ROPE_THETA ** (-2.0 * jnp.arange(half, dtype=F32) / ROPE_DIM)
    ang = pos[:, None] * inv[None, :]
    cos, sin = jnp.cos(ang), jnp.sin(ang)
    pad = ATT_HEAD_DIM - ROPE_DIM
    cos_h = jnp.concatenate([cos, cos, jnp.ones((l, pad), F32)], axis=1)
    sin_h = jnp.concatenate([-sin, sin, jnp.zeros((l, pad), F32)], axis=1)
    rep = LANES // ATT_HEAD_DIM
    return tuple(jnp.concatenate([tb] * rep, axis=1) for tb in (cos_h, sin_h))


def _attn_body(q1, k1c, k1p, v1c, v1p, q4, k4c, k4p, v4c, v4p, q16, k16c, k16p, v16c, v16p,
               o_ref, kb1, vb1, kb4, vb4, kb16, vb16, s_scr, p_scr, mx_scr, bias_scr, acc_ref, m_ref, l_ref):
    first = pl.program_id(2) == 0
    n_pairs = acc_ref.shape[0]
    n_blocks = SUPER // SPAN
    trans_b = (((1,), (1,)), ((), ()))

    for buf, prev, cur in ((kb1, k1p, k1c), (vb1, v1p, v1c)):
        buf[0, 0:SPAN, :] = prev[...]
        buf[0, SPAN:, :] = cur[...]
    for buf, prev, cur in ((kb4, k4p, k4c), (vb4, v4p, v4c), (kb16, k16p, k16c), (vb16, v16p, v16c)):
        buf[:, 0:SPAN, :] = prev[...]
        buf[:, SPAN:, :] = cur[...]

    qi = lax.broadcasted_iota(jnp.int32, (SPAN, 2 * SPAN), 0)
    kj = lax.broadcasted_iota(jnp.int32, (SPAN, 2 * SPAN), 1)
    band = (kj >= qi) & (kj <= qi + SPAN)
    bias_scr[0] = jnp.where(band, 0.0, -jnp.inf)
    bias_scr[1] = jnp.where(band & (kj >= SPAN), 0.0, -jnp.inf)

    m_ref[...] = jnp.full(m_ref.shape, -jnp.inf, F32)
    l_ref[...] = jnp.zeros(l_ref.shape, F32)
    acc_ref[...] = jnp.zeros(acc_ref.shape, F32)

    lo = lax.broadcasted_iota(jnp.int32, (SPAN, LANES), 1) < ATT_HEAD_DIM
    ones_v = jnp.ones((2 * SPAN, LANES), BF16)
    lanes = lambda p: slice(p * LANES, (p + 1) * LANES)

    def run_pattern(dil, q_get, kb, vb):
        per_res = SUPER // (SPAN * dil)

        def split(f):
            if per_res == 1:
                return f, 0
            if dil == 1:
                return 0, f
            if isinstance(f, int):
                return f // per_res, f % per_res
            return lax.shift_right_logical(f, per_res.bit_length() - 1), f & (per_res - 1)

        def slab_rows(nb):
            return pl.ds(nb * SPAN if isinstance(nb, int) else pl.multiple_of(nb * SPAN, SPAN), 2 * SPAN)

        def stage_scores(f, slot):
            r, nb = split(f)
            if isinstance(nb, int):
                flag = jnp.where(first, 1, 0) if nb == 0 else 0
            else:
                flag = jnp.where(first & (nb == 0), 1, 0)
            bias = bias_scr[flag]
            bias2 = jnp.concatenate([bias, bias], axis=0)
            q = q_get(r, nb)
            ks = kb[r, slab_rows(nb), :]
            for p in range(n_pairs):
                qp = q[:, lanes(p)]
                zero = jnp.zeros_like(qp)
                q2 = jnp.concatenate([jnp.where(lo, qp, zero), jnp.where(lo, zero, qp)], axis=0)
                s_scr[slot, p] = lax.dot_general(q2, ks[:, lanes(p)], trans_b, preferred_element_type=F32) + bias2

        def stage_softmax(slot):
            for p in range(n_pairs):
                s = s_scr[slot, p]
                mx = jnp.max(s, axis=1, keepdims=True)
                p_scr[slot, p] = jnp.exp(s - mx).astype(BF16)
                mx_scr[slot, p] = jnp.where(lo, mx[:SPAN], mx[SPAN:])

        def stage_values(f, slot):
            r, nb = split(f)
            vs = vb[r, slab_rows(nb), :]
            start = nb * (SPAN * dil) + r
            rows = pl.ds(start, SPAN) if dil == 1 else pl.ds(start, SPAN, stride=dil)
            for p in range(n_pairs):
                v_aug = jnp.concatenate([vs[:, lanes(p)], ones_v], axis=1)
                pv = jnp.dot(p_scr[slot, p], v_aug, preferred_element_type=F32)
                o_new = jnp.where(lo, pv[:SPAN, :LANES], pv[SPAN:, :LANES])
                l_new = jnp.where(lo, pv[:SPAN, LANES:], pv[SPAN:, LANES:])
                m_new = mx_scr[slot, p]
                m_old = m_ref[p, rows, :]
                m = jnp.maximum(m_old, m_new)
                e_old = jnp.exp(m_old - m)
                e_new = jnp.exp(m_new - m)
                l_ref[p, rows, :] = e_old * l_ref[p, rows, :] + e_new * l_new
                acc_ref[p, rows, :] = e_old * acc_ref[p, rows, :] + e_new * o_new
                m_ref[p, rows, :] = m

        stage_scores(0, 0)
        stage_scores(1, 1)
        stage_softmax(0)

        def body(i, c):
            f = 2 * i + 2
            stage_scores(f, 0)
            stage_softmax(1)
            stage_values(f - 2, 0)
            stage_scores(f + 1, 1)
            stage_softmax(0)
            stage_values(f - 1, 1)
            return c

        lax.fori_loop(0, (n_blocks - 2) // 2, body, 0)
        stage_softmax(1)
        stage_values(n_blocks - 2, 0)
        stage_values(n_blocks - 1, 1)

    def q_rows(nb):
        return pl.ds(nb * SPAN if isinstance(nb, int) else pl.multiple_of(nb * SPAN, SPAN), SPAN)

    run_pattern(DILATIONS[0], lambda r, nb: q1[q_rows(nb), :], kb1, vb1)
    run_pattern(DILATIONS[1], lambda r, nb: q4[r, q_rows(nb), :], kb4, vb4)
    run_pattern(DILATIONS[2], lambda r, nb: q16[r], kb16, vb16)

    for c in range(n_pairs):
        o_ref[:, lanes(c)] = (acc_ref[c] / l_ref[c]).astype(BF16)


def _attention(qkv_outs):
    q1, q4, q16, k1, k4, k16, v1, v4, v16 = qkv_outs
    b, l, d = q1.shape
    w = ATT_LANES
    n_pairs = w // LANES
    d4, d16 = DILATIONS[1], DILATIONS[2]
    cur1 = pl.BlockSpec((None, SUPER, w), lambda bi_, g, s: (bi_, s, g))
    prev1 = pl.BlockSpec((None, SPAN, w), lambda bi_, g, s: (bi_, jnp.maximum(s * (SUPER // SPAN) - 1, 0), g))
    cur4 = pl.BlockSpec((None, d4, SUPER // d4, w), lambda bi_, g, s: (bi_, 0, s, g))
    prev4 = pl.BlockSpec((None, d4, SPAN, w),
                         lambda bi_, g, s: (bi_, 0, jnp.maximum(s * (SUPER // (d4 * SPAN)) - 1, 0), g))
    cur16 = pl.BlockSpec((None, d16, SPAN, w), lambda bi_, g, s: (bi_, 0, s, g))
    prev16 = pl.BlockSpec((None, d16, SPAN, w), lambda bi_, g, s: (bi_, 0, jnp.maximum(s - 1, 0), g))
    return pl.pallas_call(
        _attn_body,
        grid=(b, d // w, l // SUPER),
        in_specs=[cur1, cur1, prev1, cur1, prev1,
                  cur4, cur4, prev4, cur4, prev4,
                  cur16, cur16, prev16, cur16, prev16],
        out_specs=pl.BlockSpec((None, SUPER, w), lambda bi_, g, s: (bi_, s, g)),
        out_shape=jax.ShapeDtypeStruct((b, l, d), BF16),
        scratch_shapes=[
            pltpu.VMEM((1, SPAN + SUPER, w), BF16), pltpu.VMEM((1, SPAN + SUPER, w), BF16),
            pltpu.VMEM((d4, SPAN + SUPER // d4, w), BF16), pltpu.VMEM((d4, SPAN + SUPER // d4, w), BF16),
            pltpu.VMEM((d16, 2 * SPAN, w), BF16), pltpu.VMEM((d16, 2 * SPAN, w), BF16),
            pltpu.VMEM((2, n_pairs, 2 * SPAN, 2 * SPAN), F32),
            pltpu.VMEM((2, n_pairs, 2 * SPAN, 2 * SPAN), BF16),
            pltpu.VMEM((2, n_pairs, SPAN, LANES), F32),
            pltpu.VMEM((2, SPAN, 2 * SPAN), F32),
        ] + [pltpu.VMEM((n_pairs, SUPER, LANES), F32)] * 3,
        compiler_params=_params("parallel", "parallel", "arbitrary"),
        name="dilated_attention",
    )(q1, k1, k1, v1, v1, q4, k4, k4, v4, v4, q16, k16, k16, v16, v16)


def _recurrent_layer(x, b, l, norm_g, w_in, lru_conv_w, lru_conv_b, lru_w_r, lru_b_r, lru_w_i, lru_b_i,
                     lru_lambda, ssd_conv_w, ssd_conv_b, ssd_dt_bias, ssd_a_log, ssd_d, ssd_norm, w_out):
    d = x.shape[1]
    width = lru_lambda.shape[0]
    n_heads = ssd_a_log.shape[0]
    n_main = w_in.shape[1] - n_heads
    row = lambda v: v.reshape(1, -1)
    pad_lanes = lambda v: jnp.pad(v.reshape(1, -1), ((0, 0), (0, LANES - v.shape[0])))
    w_main = w_in[:, :n_main].astype(BF16)
    w_dt = jnp.pad(w_in[:, n_main:], ((0, 0), (0, LANES - n_heads))).astype(BF16)
    proj, dt = _rec_in(x, row(norm_g), w_main, w_dt)
    proj3 = proj.reshape(b, l, n_main)
    out_a = _lru(proj3, lru_conv_w, row(lru_conv_b), _block_diag_slabs(lru_w_r), _block_diag_slabs(lru_w_i),
                 row(lru_b_r), row(lru_b_i), row(lru_lambda), width)
    out_b = _ssd(proj3, dt.reshape(b, l, LANES), ssd_conv_w, row(ssd_conv_b), pad_lanes(ssd_dt_bias),
                 pad_lanes(ssd_a_log), row(jnp.repeat(ssd_d, SSD_HEAD_DIM)), row(ssd_norm), n_heads)
    w_out = w_out.astype(BF16)
    return _proj_residual(x, [out_a.reshape(b * l, width), out_b.reshape(b * l, -1)],
                          [w_out[:width], w_out[width:]])


def _attention_layer(x, b, l, norm_g, w_qkv, q_norm, k_norm, w_out):
    d = x.shape[1]
    heads = d // ATT_HEAD_DIM
    row = lambda v: v.reshape(1, -1)
    outs = _qkv(x.reshape(b, l, d), row(norm_g), w_qkv.astype(BF16),
                row(jnp.tile(q_norm, heads)), row(jnp.tile(k_norm, heads)), *_rope_tables(l))
    o = _attention(outs)
    return _proj_residual(x, [o.reshape(b * l, d)], [w_out.astype(BF16)])


def kernel(x, rec_norm, rec_w_in, lru_conv_w, lru_conv_b, lru_w_r, lru_b_r, lru_w_i, lru_b_i, lru_lambda,
           ssd_conv_w, ssd_conv_b, ssd_dt_bias, ssd_a_log, ssd_d, ssd_norm, rec_w_out, att_norm, att_w_qkv,
           att_q_norm, att_k_norm, att_w_out, ffn_norm, ffn_w_gate_up, ffn_w_down):
    b, l, d = x.shape
    depth = ffn_norm.shape[0]
    h = x.reshape(b * l, d)
    for layer in range(depth):
        i = layer // 2
        if layer % 2 == 0:
            h = _recurrent_layer(h, b, l, rec_norm[i], rec_w_in[i], lru_conv_w[i], lru_conv_b[i], lru_w_r[i],
                                 lru_b_r[i], lru_w_i[i], lru_b_i[i], lru_lambda[i], ssd_conv_w[i],
                                 ssd_conv_b[i], ssd_dt_bias[i], ssd_a_log[i], ssd_d[i], ssd_norm[i],
                                 rec_w_out[i])
        else:
            h = _attention_layer(h, b, l, att_norm[i], att_w_qkv[i], att_q_norm[i], att_k_norm[i],
                                 att_w_out[i])
        wgu, wd = _prep_ffn_weights(ffn_w_gate_up[layer], ffn_w_down[layer])
        h = _ffn(h, ffn_norm[layer].reshape(1, d), wgu, wd)
    return h.reshape(b, l, d)
```

```python
import functools
import math

import jax
import jax.numpy as jnp
import numpy as np
from jax import lax
from jax.experimental import pallas as pl
from jax.experimental.pallas import tpu as pltpu

F32 = jnp.float32
BF16 = jnp.bfloat16

NORM_EPS = 1e-6
CONV_WIDTH = 4
LRU_BLOCK = 64
LRU_C = 8.0
GATE_SLAB = 256
SSD_HEAD_DIM = 64
SSD_GROUPS = 2
SSD_STATE = 128
SSD_CHUNK = 128
ATT_HEAD_DIM = 64
ROPE_DIM = 16
ROPE_THETA = 500000.0
SPAN = 128
DILATIONS = (1, 4, 16)
SUPER = SPAN * DILATIONS[-1]
ATT_LANES = 256
LANES = 128
SUBLANES = 8
BF16_ROWS = 16
VMEM_LIMIT_BYTES = 56 * 1024 * 1024


def _params(*semantics):
    return pltpu.CompilerParams(dimension_semantics=semantics, vmem_limit_bytes=VMEM_LIMIT_BYTES)


def _rmsnorm(x, g):
    ms = jnp.mean(x * x, axis=-1, keepdims=True)
    return x * lax.rsqrt(ms + NORM_EPS) * g


def _sigmoid(x):
    return 1.0 / (1.0 + jnp.exp(-x))


def _silu(x):
    return x * _sigmoid(x)


def _softplus(x):
    return jnp.maximum(x, 0.0) + jnp.log1p(jnp.exp(-jnp.abs(x)))


def _gelu_tanh(x):
    c = math.sqrt(2.0 / math.pi)
    return x * (0.5 * (1.0 + jnp.tanh(c * (x + 0.044715 * (x * x * x)))))


FFN_TM = 512
FFN_TH = 256


def _ffn_body(x_ref, g_ref, wgu_ref, wd_ref, o_ref, h_ref, act_ref):
    x = x_ref[...]
    h_ref[...] = _rmsnorm(x, g_ref[...]).astype(BF16)
    for c in range(act_ref.shape[1] // FFN_TH):
        gu = jnp.dot(h_ref[...], wgu_ref[:, 2 * c * FFN_TH:2 * (c + 1) * FFN_TH], preferred_element_type=F32)
        act_ref[:, c * FFN_TH:(c + 1) * FFN_TH] = (_silu(gu[:, :FFN_TH]) * gu[:, FFN_TH:]).astype(BF16)
    o_ref[...] = x + jnp.dot(act_ref[...], wd_ref[...], preferred_element_type=F32)


def _resident(shape):
    return pl.BlockSpec(shape, lambda *_: (0,) * len(shape), pipeline_mode=pl.Buffered(1))


def _ffn(x, g, wgu, wd):
    t, d = x.shape
    hid = wd.shape[0]
    return pl.pallas_call(
        _ffn_body,
        grid=(t // FFN_TM,),
        in_specs=[
            pl.BlockSpec((FFN_TM, d), lambda i: (i, 0)),
            _resident((1, d)),
            _resident(wgu.shape),
            _resident(wd.shape),
        ],
        out_specs=pl.BlockSpec((FFN_TM, d), lambda i: (i, 0)),
        out_shape=jax.ShapeDtypeStruct((t, d), F32),
        scratch_shapes=[pltpu.VMEM((FFN_TM, d), BF16), pltpu.VMEM((FFN_TM, hid), BF16)],
        compiler_params=_params("parallel"),
        name="ffn",
    )(x, g, wgu, wd)


def _prep_ffn_weights(w_gate_up, w_down):
    d, two_h = w_gate_up.shape
    hid = two_h // 2
    nh = hid // FFN_TH
    wg = w_gate_up[:, :hid].reshape(d, nh, FFN_TH)
    wu = w_gate_up[:, hid:].reshape(d, nh, FFN_TH)
    wgu = jnp.concatenate([wg, wu], axis=2).reshape(d, 2 * hid).astype(BF16)
    return wgu, w_down.astype(BF16)


PROJ_TM = 512


def _proj_body(n_in, *refs):
    x_ref = refs[0]
    a_refs = refs[1:1 + n_in]
    w_refs = refs[1 + n_in:1 + 2 * n_in]
    o_ref = refs[1 + 2 * n_in]
    acc = x_ref[...]
    for a_ref, w_ref in zip(a_refs, w_refs):
        acc = acc + jnp.dot(a_ref[...], w_ref[...], preferred_element_type=F32)
    o_ref[...] = acc


def _proj_residual(x, acts, weights):
    t, d = x.shape
    n_in = len(acts)
    in_specs = [pl.BlockSpec((PROJ_TM, d), lambda i: (i, 0))]
    in_specs += [pl.BlockSpec((PROJ_TM, a.shape[1]), lambda i: (i, 0)) for a in acts]
    in_specs += [_resident(w.shape) for w in weights]
    return pl.pallas_call(
        functools.partial(_proj_body, n_in),
        grid=(t // PROJ_TM,),
        in_specs=in_specs,
        out_specs=pl.BlockSpec((PROJ_TM, d), lambda i: (i, 0)),
        out_shape=jax.ShapeDtypeStruct((t, d), F32),
        compiler_params=_params("parallel"),
        name="proj_residual",
    )(x, *acts, *weights)


def _segment_perm(rows):
    steps = rows // SUBLANES
    rho = np.arange(rows)
    time = (rho % SUBLANES) * steps + rho // SUBLANES
    perm = time[:, None] == np.arange(rows)[None, :]
    return jnp.asarray(perm, dtype=BF16), jnp.asarray(perm.T, dtype=BF16)


def _conv_segments(x, halo_ref, w_ref, b_ref):
    rows, width = x.shape
    sub0 = lax.broadcasted_iota(jnp.int32, (SUBLANES, width), 0) == 0
    halos = []
    for k in range(1, CONV_WIDTH):
        tail = pltpu.roll(x[rows - k * SUBLANES:rows - (k - 1) * SUBLANES, :], 1, 0)
        halos.append(jnp.where(sub0, halo_ref[k - 1], tail))
        halo_ref[k - 1] = tail
    y = b_ref[...] + w_ref[CONV_WIDTH - 1:CONV_WIDTH, :] * x
    for k in range(1, CONV_WIDTH):
        back = jnp.concatenate(halos[k - 1::-1] + [x[:rows - k * SUBLANES, :]], axis=0)
        y = y + w_ref[CONV_WIDTH - 1 - k:CONV_WIDTH - k, :] * back
    return y


LRU_TL = 256


def _lru_body(x_ref, g_ref, w_ref, cw_ref, cb_ref, wr_ref, wi_ref, br_ref, bi_ref, lam_ref, perm_ref, unperm_ref,
              o_ref, halo_ref, a_ref, u_ref, carry_ref):
    first = pl.program_id(1) == 0
    tl, w = o_ref.shape
    steps = tl // SUBLANES

    @pl.when(first)
    def _():
        carry_ref[...] = jnp.zeros_like(carry_ref)
        halo_ref[...] = jnp.zeros_like(halo_ref)

    hn = _rmsnorm(x_ref[...], g_ref[...]).astype(BF16)
    hn = jnp.dot(perm_ref[...], hn, preferred_element_type=F32).astype(BF16)
    proj = jnp.dot(hn, w_ref[...], preferred_element_type=F32)
    gate = proj[:, w:]

    xc = _conv_segments(proj[:, :w], halo_ref, cw_ref, cb_ref)
    xcb = xc.astype(BF16)
    pre_r, pre_i = [], []
    for s in range(w // GATE_SLAB):
        slab = xcb[:, s * GATE_SLAB:(s + 1) * GATE_SLAB]
        pre_r.append(jnp.dot(slab, wr_ref[s], preferred_element_type=F32))
        pre_i.append(jnp.dot(slab, wi_ref[s], preferred_element_type=F32))
    r = _sigmoid(jnp.concatenate(pre_r, axis=1) + br_ref[...])
    i = _sigmoid(jnp.concatenate(pre_i, axis=1) + bi_ref[...])
    log_a = (-LRU_C * r) * _softplus(-lam_ref[...])
    a = jnp.exp(log_a)
    z = 1.0 - a * a
    u = jnp.where(z > 0.0, z * lax.rsqrt(z), 0.0) * (i * xc)

    a_ref[...] = a
    u_ref[...] = u

    def step(j, carry):
        h_prev, p_prev = carry
        rows = pl.ds(pl.multiple_of(j * SUBLANES, SUBLANES), SUBLANES)
        a_j = a_ref[rows, :]
        h_j = a_j * h_prev + u_ref[rows, :]
        p_j = a_j * p_prev
        u_ref[rows, :] = h_j
        a_ref[rows, :] = p_j
        return h_j, p_j

    h_end, p_end = lax.fori_loop(0, steps, step, (jnp.zeros((SUBLANES, w), F32), jnp.ones((SUBLANES, w), F32)),
                                 unroll=4)
    c = carry_ref[0:1, :]
    entering = []
    for seg in range(SUBLANES):
        entering.append(c)
        c = p_end[seg:seg + 1, :] * c + h_end[seg:seg + 1, :]
    carry_ref[0:1, :] = c
    enter = jnp.concatenate(entering, axis=0)
    h = u_ref[...] + a_ref[...] * jnp.concatenate([enter] * steps, axis=0)
    out = (h * _gelu_tanh(gate)).astype(BF16)
    o_ref[...] = jnp.dot(unperm_ref[...], out, preferred_element_type=F32).astype(BF16)


def _lru(x3, g, w_lru, cw, cb, wr_bd, wi_bd, br, bi, lam):
    b, l, d = x3.shape
    width = lam.shape[1]
    perms = _segment_perm(LRU_TL)
    return pl.pallas_call(
        _lru_body,
        grid=(b, l // LRU_TL),
        in_specs=[pl.BlockSpec((None, LRU_TL, d), lambda bi_, li: (bi_, li, 0))]
        + [_resident(a.shape) for a in (g, w_lru, cw, cb, wr_bd, wi_bd, br, bi, lam, *perms)],
        out_specs=pl.BlockSpec((None, LRU_TL, width), lambda bi_, li: (bi_, li, 0)),
        out_shape=jax.ShapeDtypeStruct((b, l, width), BF16),
        scratch_shapes=[
            pltpu.VMEM((CONV_WIDTH - 1, SUBLANES, width), F32),
            pltpu.VMEM((LRU_TL, width), F32),
            pltpu.VMEM((LRU_TL, width), F32),
            pltpu.VMEM((SUBLANES, width), F32),
        ],
        compiler_params=_params("parallel", "arbitrary"),
        name="lru",
    )(x3, g, w_lru, cw, cb, wr_bd, wi_bd, br, bi, lam, *perms)


def _block_diag_slabs(w):
    nb, bs, _ = w.shape
    per = GATE_SLAB // bs
    w = w.reshape(nb // per, per, bs, bs)
    eye = jnp.eye(per, dtype=w.dtype)
    bd = jnp.einsum("spij,pq->spiqj", w, eye).reshape(nb // per, GATE_SLAB, GATE_SLAB)
    return bd.astype(BF16)


def _ssd_body(n_heads, x_ref, g_ref, w_ref, cw_ref, cb_ref, dtb_ref, alog_ref, dvec_ref, nrm_ref,
              exp_ref, perm_ref, unperm_ref, o_ref, proj_ref, halo_ref, state_ref):
    first = pl.program_id(1) == 0
    t = SSD_CHUNK
    width = n_heads * SSD_HEAD_DIM
    gw = SSD_STATE
    conv_ch = width + 2 * SSD_GROUPS * gw
    heads_per_group = n_heads // SSD_GROUPS
    n_chunks = x_ref.shape[0] // t

    @pl.when(first)
    def _():
        state_ref[...] = jnp.zeros_like(state_ref)
        halo_ref[...] = jnp.zeros_like(halo_ref)

    hn = _rmsnorm(x_ref[...], g_ref[...]).astype(BF16)
    hn = jnp.concatenate([jnp.dot(perm_ref[...], hn[c * t:(c + 1) * t, :], preferred_element_type=F32).astype(BF16)
                          for c in range(n_chunks)], axis=0)
    proj_ref[...] = jnp.dot(hn, w_ref[...], preferred_element_type=F32)

    steps = t // SUBLANES
    time_of = lambda i: (i & (SUBLANES - 1)) * steps + lax.shift_right_logical(i, 3)
    causal = (time_of(lax.broadcasted_iota(jnp.int32, (t, t), 0))
              >= time_of(lax.broadcasted_iota(jnp.int32, (t, t), 1)))
    tri = jnp.where(causal, 1.0, 0.0).astype(F32)
    lane = lax.broadcasted_iota(jnp.int32, (t, LANES), 1)
    lo = lane < SSD_HEAD_DIM
    rlo = lax.broadcasted_iota(jnp.int32, (LANES, gw), 0) < SSD_HEAD_DIM

    def chunk(c, carry):
        rows = pl.ds(pl.multiple_of(c * t, t), t)
        out = _ssd_chunk(proj_ref[rows, 0:width], proj_ref[rows, width:width + conv_ch],
                         proj_ref[rows, width + conv_ch:], n_heads, causal, tri, lo, rlo, halo_ref, state_ref,
                         cw_ref, cb_ref, dtb_ref, alog_ref, dvec_ref, nrm_ref, exp_ref)
        o_ref[rows, :] = jnp.dot(unperm_ref[...], out, preferred_element_type=F32).astype(BF16)
        return carry

    lax.fori_loop(0, n_chunks, chunk, 0)


def _ssd_chunk(z, xbc_raw, dt_raw, n_heads, causal, tri, lo, rlo, halo_ref, state_ref,
               cw_ref, cb_ref, dtb_ref, alog_ref, dvec_ref, nrm_ref, exp_ref):
    t = SSD_CHUNK
    width = n_heads * SSD_HEAD_DIM
    gw = SSD_STATE
    heads_per_group = n_heads // SSD_GROUPS
    xbc = _silu(_conv_segments(xbc_raw, halo_ref, cw_ref, cb_ref))
    xs = xbc[:, :width]
    dt = _softplus(dt_raw + dtb_ref[...])
    adt = dt * (-jnp.exp(alog_ref[...]))
    cs = jnp.dot(tri, adt, preferred_element_type=F32, precision=lax.Precision.HIGHEST)
    cs_t = cs.T
    cs_last = cs[t - 1:t, :]
    cdec = jnp.exp(cs_last)
    dt_w = _split_dot(dt, exp_ref[...], parts=3)
    cs_w = _split_dot(cs, exp_ref[...], parts=3)
    ecs_w = jnp.exp(cs_w)
    dte_w = jnp.exp(cs_w[t - 1:t, :] - cs_w)

    cbs = []
    for g in range(SSD_GROUPS):
        bm = xbc[:, width + g * gw: width + (g + 1) * gw].astype(BF16)
        cm = xbc[:, width + (SSD_GROUPS + g) * gw: width + (SSD_GROUPS + g + 1) * gw].astype(BF16)
        cb = lax.dot_general(cm, bm, (((1,), (1,)), ((), ())), preferred_element_type=F32)
        cbs.append((bm, cm, cb))

    ys = []
    for p in range(n_heads // 2):
        h0, h1 = 2 * p, 2 * p + 1
        bm, cm, cb = cbs[h0 // heads_per_group]
        sl = slice(p * LANES, (p + 1) * LANES)
        xs_p = xs[:, sl]
        xdt = xs_p * dt_w[:, sl]
        ms = []
        for h in (h0, h1):
            seg = cs[:, h:h + 1] - cs_t[h:h + 1, :]
            ms.append((cb * jnp.where(causal, jnp.exp(seg), 0.0)).astype(BF16))
        m_cat = jnp.concatenate(ms, axis=1)
        xdt_bd = jnp.concatenate([jnp.where(lo, xdt, 0.0), jnp.where(lo, 0.0, xdt)], axis=0).astype(BF16)
        y = jnp.dot(m_cat, xdt_bd, preferred_element_type=F32)
        prev = state_ref[sl, :]
        y_off = lax.dot_general(cm, prev.astype(BF16), (((1,), (1,)), ((), ())), preferred_element_type=F32)
        y = y + y_off * ecs_w[:, sl]
        xw = (xdt * dte_w[:, sl]).astype(BF16)
        st = lax.dot_general(xw, bm, (((0,), (0,)), ((), ())), preferred_element_type=F32)
        dec = jnp.where(rlo, cdec[:, h0:h0 + 1], cdec[:, h1:h1 + 1])
        state_ref[sl, :] = prev * dec + st
        ys.append(y + dvec_ref[:, sl] * xs_p)

    y = jnp.concatenate(ys, axis=1) * _silu(z)
    gsz = width // SSD_GROUPS
    outs = []
    for g in range(SSD_GROUPS):
        yg = y[:, g * gsz:(g + 1) * gsz]
        outs.append(yg * lax.rsqrt(jnp.mean(yg * yg, axis=-1, keepdims=True) + NORM_EPS))
    return (jnp.concatenate(outs, axis=1) * nrm_ref[...]).astype(BF16)


SSD_TS = 512


def _ssd(x3, g, w_ssd, cw, cb, dt_bias, a_log, d_vec, nrm, n_heads):
    b, l, d = x3.shape
    width = n_heads * SSD_HEAD_DIM
    conv_ch = width + 2 * SSD_GROUPS * SSD_STATE
    expand = jnp.asarray(np.arange(LANES)[:, None] == np.arange(width)[None, :] // SSD_HEAD_DIM,
                         dtype=BF16)
    consts = (g, w_ssd, cw, cb, dt_bias, a_log, d_vec, nrm, expand, *_segment_perm(SSD_CHUNK))
    return pl.pallas_call(
        functools.partial(_ssd_body, n_heads),
        grid=(b, l // SSD_TS),
        in_specs=[pl.BlockSpec((None, SSD_TS, d), lambda bi_, ci: (bi_, ci, 0))]
        + [_resident(a.shape) for a in consts],
        out_specs=pl.BlockSpec((None, SSD_TS, width), lambda bi_, ci: (bi_, ci, 0)),
        out_shape=jax.ShapeDtypeStruct((b, l, width), BF16),
        scratch_shapes=[
            pltpu.VMEM((SSD_TS, w_ssd.shape[1]), F32),
            pltpu.VMEM((CONV_WIDTH - 1, SUBLANES, conv_ch), F32),
            pltpu.VMEM((width, SSD_STATE), F32),
        ],
        compiler_params=_params("parallel", "arbitrary"),
        name="ssd",
    )(x3, *consts)


QKV_TM = 512
PERM_ROWS = 256


def _split_dot(x, m, parts=2):
    acc, rem = None, x
    for k in range(parts):
        piece = rem.astype(BF16)
        term = jnp.dot(piece, m, preferred_element_type=F32)
        acc = term if acc is None else acc + term
        if k + 1 < parts:
            rem = rem - piece.astype(F32)
    return acc


def _slab_map(fn, x):
    return jnp.concatenate([fn(x[:, c * GATE_SLAB:(c + 1) * GATE_SLAB]) for c in range(x.shape[1] // GATE_SLAB)],
                           axis=1)


def _head_norm_rope(y, gain, cos, sin, gsum, rot):
    ss = _slab_map(lambda v: _split_dot(v, gsum), y * y)
    yn = y * lax.rsqrt(ss * (1.0 / ATT_HEAD_DIM) + NORM_EPS) * gain
    partner = _slab_map(lambda v: _split_dot(v, rot), yn)
    tile = lambda tbl: jnp.concatenate([tbl] * (y.shape[1] // LANES), axis=1)
    return yn * tile(cos) + partner * tile(sin)


def _qkv_body(x_ref, g_ref, w_ref, qg_ref, kg_ref, cos_ref, sin_ref, gsum_ref, rot_ref, p4_ref, p16_ref,
              *refs):
    outs, h_ref = refs[:9], refs[9]
    j = pl.program_id(2)

    @pl.when(j == 0)
    def _():
        h_ref[...] = _rmsnorm(x_ref[...], g_ref[...]).astype(BF16)

    acc = jnp.dot(h_ref[...], w_ref[...], preferred_element_type=F32)

    def emit(y, o1, o4, o16):
        yb = y.astype(BF16)
        o1[...] = yb
        for perm_ref, o in ((p4_ref, o4), (p16_ref, o16)):
            sub = perm_ref.shape[0]
            dil = o.shape[0]
            rows = sub // dil
            for t in range(yb.shape[0] // sub):
                yp = jnp.dot(perm_ref[...], yb[t * sub:(t + 1) * sub, :], preferred_element_type=F32).astype(BF16)
                for r in range(dil):
                    o[r, t * rows:(t + 1) * rows, :] = yp[r * rows:(r + 1) * rows, :]

    rope = (cos_ref[...], sin_ref[...], gsum_ref[...], rot_ref[...])

    @pl.when(j == 0)
    def _():
        emit(_head_norm_rope(acc, qg_ref[...], *rope) * (ATT_HEAD_DIM ** -0.5), *outs[0:3])

    @pl.when(j == 1)
    def _():
        emit(_head_norm_rope(acc, kg_ref[...], *rope), *outs[3:6])

    @pl.when(j == 2)
    def _():
        emit(acc, *outs[6:9])


def _qkv(x3, g, w_qkv, qg, kg, cos, sin):
    b, l, d = x3.shape
    tm = QKV_TM
    out_specs, out_shape = [], []
    for _ in range(3):
        out_specs.append(pl.BlockSpec((None, tm, d), lambda bi_, i, j: (bi_, i, 0)))
        out_shape.append(jax.ShapeDtypeStruct((b, l, d), BF16))
        for dil in DILATIONS[1:]:
            out_specs.append(pl.BlockSpec((None, dil, tm // dil, d), lambda bi_, i, j: (bi_, 0, i, 0)))
            out_shape.append(jax.ShapeDtypeStruct((b, dil, l // dil, d), BF16))
    consts = _qkv_constants(tm)
    tbl = lambda: pl.BlockSpec((tm, LANES), lambda bi_, i, j: (i, 0))
    return pl.pallas_call(
        _qkv_body,
        grid=(b, l // tm, 3),
        in_specs=[
            pl.BlockSpec((None, tm, d), lambda bi_, i, j: (bi_, i, 0)),
            _resident((1, d)),
            pl.BlockSpec((d, d), lambda bi_, i, j: (0, j)),
            _resident((1, d)), _resident((1, d)), tbl(), tbl(),
        ] + [_resident(c.shape) for c in consts],
        out_specs=out_specs,
        out_shape=out_shape,
        scratch_shapes=[pltpu.VMEM((tm, d), BF16)],
        compiler_params=_params("parallel", "parallel", "arbitrary"),
        name="qkv",
    )(x3, g, w_qkv, qg, kg, cos, sin, *consts)


def _qkv_constants(tm):
    lane = np.arange(GATE_SLAB)
    gsum = (lane[:, None] // ATT_HEAD_DIM == lane[None, :] // ATT_HEAD_DIM)
    half = ROPE_DIM // 2
    pos = lane % ATT_HEAD_DIM
    partner = np.where(pos < half, lane + half, np.where(pos < ROPE_DIM, lane - half, -1))
    rot = lane[:, None] == partner[None, :]
    perms = []
    for dil in DILATIONS[1:]:
        sub = PERM_ROWS
        rows = sub // dil
        assert rows % BF16_ROWS == 0 and tm % sub == 0
        out_row = np.arange(sub)
        src = (out_row % rows) * dil + out_row // rows
        perms.append(src[:, None] == np.arange(sub)[None, :])
    return tuple(jnp.asarray(m, dtype=BF16) for m in (gsum, rot, *perms))


def _rope_tables(l):
    half = ROPE_DIM // 2
    pos = jnp.arange(l, dtype=F32)
    inv = ROPE_THETA ** (-2.0 * jnp.arange(half, dtype=F32) / ROPE_DIM)
    ang = pos[:, None] * inv[None, :]
    cos, sin = jnp.cos(ang), jnp.sin(ang)
    pad = ATT_HEAD_DIM - ROPE_DIM
    cos_h = jnp.concatenate([cos, cos, jnp.ones((l, pad), F32)], axis=1)
    sin_h = jnp.concatenate([-sin, sin, jnp.zeros((l, pad), F32)], axis=1)
    rep = LANES // ATT_HEAD_DIM
    return tuple(jnp.concatenate([tb] * rep, axis=1) for tb in (cos_h, sin_h))


def _attn_body(q1, k1c, k1p, v1c, v1p, q4, k4c, k4p, v4c, v4p, q16, k16c, k16p, v16c, v16p,
               o_ref, kb1, vb1, kb4, vb4, kb16, vb16, s_scr, p_scr, mx_scr, bias_scr, acc_ref, m_ref, l_ref):
    first = pl.program_id(2) == 0
    n_pairs = acc_ref.shape[0]
    n_blocks = SUPER // SPAN
    trans_b = (((1,), (1,)), ((), ()))

    for buf, prev, cur in ((kb1, k1p, k1c), (vb1, v1p, v1c)):
        buf[0, 0:SPAN, :] = prev[...]
        buf[0, SPAN:, :] = cur[...]
    for buf, prev, cur in ((kb4, k4p, k4c), (vb4, v4p, v4c), (kb16, k16p, k16c), (vb16, v16p, v16c)):
        buf[:, 0:SPAN, :] = prev[...]
        buf[:, SPAN:, :] = cur[...]

    qi = lax.broadcasted_iota(jnp.int32, (SPAN, 2 * SPAN), 0)
    kj = lax.broadcasted_iota(jnp.int32, (SPAN, 2 * SPAN), 1)
    band = (kj >= qi) & (kj <= qi + SPAN)
    bias_scr[0] = jnp.where(band, 0.0, -jnp.inf)
    bias_scr[1] = jnp.where(band & (kj >= SPAN), 0.0, -jnp.inf)

    lo =lax.broadcasted_iota(jnp.int32, (SPAN, LANES), 1) < ATT_HEAD_DIM
    ones_v = jnp.ones((2 * SPAN, LANES), BF16)
    lanes = lambda p: slice(p * LANES, (p + 1) * LANES)

    def run_pattern(dil, q_get, kb, vb):
        per_res = SUPER // (SPAN * dil)
        overwrite = dil == DILATIONS[0]

        def split(f):
            if per_res == 1:
                return f, 0
            if dil == 1:
                return 0, f
            if isinstance(f, int):
                return f // per_res, f % per_res
            return lax.shift_right_logical(f, per_res.bit_length() - 1), f & (per_res - 1)

        def slab_rows(nb):
            return pl.ds(nb * SPAN if isinstance(nb, int) else pl.multiple_of(nb * SPAN, SPAN), 2 * SPAN)

        def stage_scores(f, slot):
            r, nb = split(f)
            if isinstance(nb, int):
                flag = jnp.where(first, 1, 0) if nb == 0 else 0
            else:
                flag = jnp.where(first & (nb == 0), 1, 0)
            bias = bias_scr[flag]
            bias2 = jnp.concatenate([bias, bias], axis=0)
            q = q_get(r, nb)
            ks = kb[r, slab_rows(nb), :]
            for p in range(n_pairs):
                qp = q[:, lanes(p)]
                zero = jnp.zeros_like(qp)
                q2 = jnp.concatenate([jnp.where(lo, qp, zero), jnp.where(lo, zero, qp)], axis=0)
                s_scr[slot, p] = lax.dot_general(q2, ks[:, lanes(p)], trans_b, preferred_element_type=F32) + bias2

        def stage_softmax(slot):
            for p in range(n_pairs):
                s = s_scr[slot, p]
                mx = jnp.max(s, axis=1, keepdims=True)
                p_scr[slot, p] = jnp.exp(s - mx).astype(BF16)
                mx_scr[slot, p] = jnp.where(lo, mx[:SPAN], mx[SPAN:])

        def stage_values(f, slot):
            r, nb = split(f)
            vs = vb[r, slab_rows(nb), :]
            start = nb * (SPAN * dil) + r
            rows = pl.ds(start, SPAN) if dil == 1 else pl.ds(start, SPAN, stride=dil)
            for p in range(n_pairs):
                v_aug = jnp.concatenate([vs[:, lanes(p)], ones_v], axis=1)
                pv = jnp.dot(p_scr[slot, p], v_aug, preferred_element_type=F32)
                o_new = jnp.where(lo, pv[:SPAN, :LANES], pv[SPAN:, :LANES])
                l_new = jnp.where(lo, pv[:SPAN, LANES:], pv[SPAN:, LANES:])
                m_new = mx_scr[slot, p]
                if overwrite:
                    m_ref[p, rows, :], l_ref[p, rows, :], acc_ref[p, rows, :] = m_new, l_new, o_new
                    continue
                m_old = m_ref[p, rows, :]
                m = jnp.maximum(m_old, m_new)
                e_old = jnp.exp(m_old - m)
                e_new = jnp.exp(m_new - m)
                l_ref[p, rows, :] = e_old * l_ref[p, rows, :] + e_new * l_new
                acc_ref[p, rows, :] = e_old * acc_ref[p, rows, :] + e_new * o_new
                m_ref[p, rows, :] = m

        stage_scores(0, 0)
        stage_scores(1, 1)
        stage_softmax(0)

        def body(i, c):
            f = 2 * i + 2
            stage_scores(f, 0)
            stage_softmax(1)
            stage_values(f - 2, 0)
            stage_scores(f + 1, 1)
            stage_softmax(0)
            stage_values(f - 1, 1)
            return c

        lax.fori_loop(0, (n_blocks - 2) // 2, body, 0)
        stage_softmax(1)
        stage_values(n_blocks - 2, 0)
        stage_values(n_blocks - 1, 1)

    def q_rows(nb):
        return pl.ds(nb * SPAN if isinstance(nb, int) else pl.multiple_of(nb * SPAN, SPAN), SPAN)

    run_pattern(DILATIONS[0], lambda r, nb: q1[q_rows(nb), :], kb1, vb1)
    run_pattern(DILATIONS[1], lambda r, nb: q4[r, q_rows(nb), :], kb4, vb4)
    run_pattern(DILATIONS[2], lambda r, nb: q16[r], kb16, vb16)

    for c in range(n_pairs):
        o_ref[:, lanes(c)] = (acc_ref[c] / l_ref[c]).astype(BF16)


def _attention(qkv_outs):
    q1, q4, q16, k1, k4, k16, v1, v4, v16 = qkv_outs
    b, l, d = q1.shape
    w = ATT_LANES
    n_pairs = w // LANES
    d4, d16 = DILATIONS[1], DILATIONS[2]
    cur1 = pl.BlockSpec((None, SUPER, w), lambda bi_, g, s: (bi_, s, g))
    prev1 = pl.BlockSpec((None, SPAN, w), lambda bi_, g, s: (bi_, jnp.maximum(s * (SUPER // SPAN) - 1, 0), g))
    cur4 = pl.BlockSpec((None, d4, SUPER // d4, w), lambda bi_, g, s: (bi_, 0, s, g))
    prev4 = pl.BlockSpec((None, d4, SPAN, w),
                         lambda bi_, g, s: (bi_, 0, jnp.maximum(s * (SUPER // (d4 * SPAN)) - 1, 0), g))
    cur16 = pl.BlockSpec((None, d16, SPAN, w), lambda bi_, g, s: (bi_, 0, s, g))
    prev16 = pl.BlockSpec((None, d16, SPAN, w), lambda bi_, g, s: (bi_, 0, jnp.maximum(s - 1, 0), g))
    return pl.pallas_call(
        _attn_body,
        grid=(b, d // w, l // SUPER),
        in_specs=[cur1, cur1, prev1, cur1, prev1,
                  cur4, cur4, prev4, cur4, prev4,
                  cur16, cur16, prev16, cur16, prev16],
        out_specs=pl.BlockSpec((None, SUPER, w), lambda bi_, g, s: (bi_, s, g)),
        out_shape=jax.ShapeDtypeStruct((b, l, d), BF16),
        scratch_shapes=[
            pltpu.VMEM((1, SPAN + SUPER, w), BF16), pltpu.VMEM((1, SPAN + SUPER, w), BF16),
            pltpu.VMEM((d4, SPAN + SUPER // d4, w), BF16), pltpu.VMEM((d4, SPAN + SUPER // d4, w), BF16),
            pltpu.VMEM((d16, 2 * SPAN, w), BF16), pltpu.VMEM((d16, 2 * SPAN, w), BF16),
            pltpu.VMEM((2, n_pairs, 2 * SPAN, 2 * SPAN), F32),
            pltpu.VMEM((2, n_pairs, 2 * SPAN, 2 * SPAN), BF16),
            pltpu.VMEM((2, n_pairs, SPAN, LANES), F32),
            pltpu.VMEM((2, SPAN, 2 * SPAN), F32),
        ] + [pltpu.VMEM((n_pairs, SUPER, LANES), F32)] * 3,
        compiler_params=_params("parallel", "parallel", "arbitrary"),
        name="dilated_attention",
    )(q1, k1, k1, v1, v1, q4, k4, k4, v4, v4, q16, k16, k16, v16, v16)


def _recurrent_layer(x, b, l, norm_g, w_in, lru_conv_w, lru_conv_b, lru_w_r, lru_b_r, lru_w_i, lru_b_i,
                     lru_lambda, ssd_conv_w, ssd_conv_b, ssd_dt_bias, ssd_a_log, ssd_d, ssd_norm, w_out):
    d = x.shape[1]
    width = lru_lambda.shape[0]
    n_heads = ssd_a_log.shape[0]
    n_main = w_in.shape[1] - n_heads
    row = lambda v: v.reshape(1, -1)
    pad_lanes = lambda v: jnp.pad(v.reshape(1, -1), ((0, 0), (0, LANES - v.shape[0])))
    w_lru = w_in[:, :2 * width].astype(BF16)
    w_ssd = jnp.pad(w_in[:, 2 * width:], ((0, 0), (0, LANES - n_heads))).astype(BF16)
    x3 = x.reshape(b, l, d)
    out_a = _lru(x3, row(norm_g), w_lru, lru_conv_w, row(lru_conv_b), _block_diag_slabs(lru_w_r),
                 _block_diag_slabs(lru_w_i), row(lru_b_r), row(lru_b_i), row(lru_lambda))
    out_b = _ssd(x3, row(norm_g), w_ssd, ssd_conv_w, row(ssd_conv_b), pad_lanes(ssd_dt_bias),
                 pad_lanes(ssd_a_log), row(jnp.repeat(ssd_d, SSD_HEAD_DIM)), row(ssd_norm), n_heads)
    w_out = w_out.astype(BF16)
    return _proj_residual(x, [out_a.reshape(b * l, width), out_b.reshape(b * l, -1)],
                          [w_out[:width], w_out[width:]])


def _attention_layer(x, b, l, norm_g, w_qkv, q_norm, k_norm, w_out):
    d = x.shape[1]
    heads = d // ATT_HEAD_DIM
    row = lambda v: v.reshape(1, -1)
    outs = _qkv(x.reshape(b, l, d), row(norm_g), w_qkv.astype(BF16),
                row(jnp.tile(q_norm, heads)), row(jnp.tile(k_norm, heads)), *_rope_tables(l))
    o = _attention(outs)
    return _proj_residual(x, [o.reshape(b * l, d)], [w_out.astype(BF16)])


def kernel(x, rec_norm, rec_w_in, lru_conv_w, lru_conv_b, lru_w_r, lru_b_r, lru_w_i, lru_b_i, lru_lambda,
           ssd_conv_w, ssd_conv_b, ssd_dt_bias, ssd_a_log, ssd_d, ssd_norm, rec_w_out, att_norm, att_w_qkv,
           att_q_norm, att_k_norm, att_w_out, ffn_norm, ffn_w_gate_up, ffn_w_down):
    b, l, d = x.shape
    depth = ffn_norm.shape[0]
    h = x.reshape(b * l, d)
    for layer in range(depth):
        i = layer // 2
        if layer % 2 == 0:
            h = _recurrent_layer(h, b, l, rec_norm[i], rec_w_in[i], lru_conv_w[i], lru_conv_b[i], lru_w_r[i],
                                 lru_b_r[i], lru_w_i[i], lru_b_i[i], lru_lambda[i], ssd_conv_w[i],
                                 ssd_conv_b[i], ssd_dt_bias[i], ssd_a_log[i], ssd_d[i], ssd_norm[i],
                                 rec_w_out[i])
        else:
            h = _attention_layer(h, b, l, att_norm[i], att_w_qkv[i], att_q_norm[i], att_k_norm[i],
                                 att_w_out[i])
        wgu, wd = _prep_ffn_weights(ffn_w_gate_up[layer], ffn_w_down[layer])
        h = _ffn(h, ffn_norm[layer].reshape(1, d), wgu, wd)
    return h.reshape(b, l, d)
```

```python
import functools
import math

import jax
import jax.numpy as jnp
import numpy as np
from jax import lax
from jax.experimental import pallas as pl
from jax.experimental.pallas import tpu as pltpu

F32 = jnp.float32
BF16 = jnp.bfloat16

NORM_EPS = 1e-6
CONV_WIDTH = 4
LRU_BLOCK = 64
LRU_C = 8.0
GATE_SLAB = 256
SSD_HEAD_DIM = 64
SSD_GROUPS = 2
SSD_STATE = 128
SSD_CHUNK = 128
ATT_HEAD_DIM = 64
ROPE_DIM = 16
ROPE_THETA = 500000.0
SPAN = 128
DILATIONS = (1, 4, 16)
SUPER = SPAN * DILATIONS[-1]
ATT_LANES = 256
ACC_DIL = DILATIONS[1]
LANES = 128
SUBLANES = 8
BF16_ROWS = 16
VMEM_LIMIT_BYTES = 56 * 1024 * 1024


def _params(*semantics):
    return pltpu.CompilerParams(dimension_semantics=semantics, vmem_limit_bytes=VMEM_LIMIT_BYTES)


def _rmsnorm(x, g):
    ms = jnp.mean(x * x, axis=-1, keepdims=True)
    return x * lax.rsqrt(ms + NORM_EPS) * g


def _sigmoid(x):
    return 1.0 / (1.0 + jnp.exp(-x))


def _silu(x):
    return x * _sigmoid(x)


def _softplus(x):
    return jnp.maximum(x, 0.0) + jnp.log1p(jnp.exp(-jnp.abs(x)))


def _gelu_tanh(x):
    c = math.sqrt(2.0 / math.pi)
    return x * (0.5 * (1.0 + jnp.tanh(c * (x + 0.044715 * (x * x * x)))))


FFN_TM = 512
FFN_TH = 256


def _ffn_body(n_mix, x_ref, *refs):
    mix_refs, wo_refs = refs[:n_mix], refs[n_mix:2 * n_mix]
    g_ref, wgu_ref, wd_ref, o_ref, h_ref, act_ref = refs[2 * n_mix:]
    x1 = x_ref[...]
    for mix_ref, wo_ref in zip(mix_refs, wo_refs):
        x1 = x1 + jnp.dot(mix_ref[...], wo_ref[...], preferred_element_type=F32)
    o_ref[...] = x1
    h_ref[...] = _rmsnorm(x1, g_ref[...]).astype(BF16)
    for c in range(act_ref.shape[1] // FFN_TH):
        gu = jnp.dot(h_ref[...], wgu_ref[:, 2 * c * FFN_TH:2 * (c + 1) * FFN_TH], preferred_element_type=F32)
        act_ref[:, c * FFN_TH:(c + 1) * FFN_TH] = (_silu(gu[:, :FFN_TH]) * gu[:, FFN_TH:]).astype(BF16)
    o_ref[...] += jnp.dot(act_ref[...], wd_ref[...], preferred_element_type=F32)


def _resident(shape):
    return pl.BlockSpec(shape, lambda *_: (0,) * len(shape), pipeline_mode=pl.Buffered(1))


def _mix_ffn(x, mixes, wos, g, wgu, wd):
    t, d = x.shape
    hid = wd.shape[0]
    rows = lambda a: pl.BlockSpec((FFN_TM, a.shape[1]), lambda i: (i, 0))
    return pl.pallas_call(
        functools.partial(_ffn_body, len(mixes)),
        grid=(t // FFN_TM,),
        in_specs=[rows(x)] + [rows(a) for a in mixes] + [_resident(a.shape) for a in (*wos, g, wgu, wd)],
        out_specs=rows(x),
        out_shape=jax.ShapeDtypeStruct((t, d), F32),
        scratch_shapes=[pltpu.VMEM((FFN_TM, d), BF16), pltpu.VMEM((FFN_TM, hid), BF16)],
        compiler_params=_params("parallel"),
        name="mix_ffn",
    )(x, *mixes, *wos, g, wgu, wd)


def _prep_ffn_weights(w_gate_up, w_down):
    d, two_h = w_gate_up.shape
    hid = two_h // 2
    nh = hid // FFN_TH
    wg = w_gate_up[:, :hid].reshape(d, nh, FFN_TH)
    wu = w_gate_up[:, hid:].reshape(d, nh, FFN_TH)
    wgu = jnp.concatenate([wg, wu], axis=2).reshape(d, 2 * hid).astype(BF16)
    return wgu, w_down.astype(BF16)


def _segment_perm(rows):
    steps = rows // SUBLANES
    rho = np.arange(rows)
    time = (rho % SUBLANES) * steps + rho // SUBLANES
    perm = time[:, None] == np.arange(rows)[None, :]
    return jnp.asarray(perm, dtype=BF16), jnp.asarray(perm.T, dtype=BF16)


def _conv_segments(x, halo_ref, w_ref, b_ref):
    rows, width = x.shape
    sub0 = lax.broadcasted_iota(jnp.int32, (SUBLANES, width), 0) == 0
    halos = []
    for k in range(1, CONV_WIDTH):
        tail = pltpu.roll(x[rows - k * SUBLANES:rows - (k - 1) * SUBLANES, :], 1, 0)
        halos.append(jnp.where(sub0, halo_ref[k - 1], tail))
        halo_ref[k - 1] = tail
    y = b_ref[...] + w_ref[CONV_WIDTH - 1:CONV_WIDTH, :] * x
    for k in range(1, CONV_WIDTH):
        back = jnp.concatenate(halos[k - 1::-1] + [x[:rows - k * SUBLANES, :]], axis=0)
        y = y + w_ref[CONV_WIDTH - 1 - k:CONV_WIDTH - k, :] * back
    return y


LRU_TL = 256


def _lru_body(x_ref, g_ref, w_ref, cw_ref, cb_ref, wr_ref, wi_ref, br_ref, bi_ref, lam_ref, perm_ref, unperm_ref,
              o_ref, halo_ref, a_ref, u_ref, carry_ref):
    first = pl.program_id(1) == 0
    tl, w = o_ref.shape
    steps = tl // SUBLANES

    @pl.when(first)
    def _():
        carry_ref[...] = jnp.zeros_like(carry_ref)
        halo_ref[...] = jnp.zeros_like(halo_ref)

    hn = _rmsnorm(x_ref[...], g_ref[...]).astype(BF16)
    hn = jnp.dot(perm_ref[...], hn, preferred_element_type=F32).astype(BF16)
    proj = jnp.dot(hn, w_ref[...], preferred_element_type=F32)
    gate = proj[:, w:]

    xc = _conv_segments(proj[:, :w], halo_ref, cw_ref, cb_ref)
    xcb = xc.astype(BF16)
    pre_r, pre_i = [], []
    for s in range(w // GATE_SLAB):
        slab = xcb[:, s * GATE_SLAB:(s + 1) * GATE_SLAB]
        pre_r.append(jnp.dot(slab, wr_ref[s], preferred_element_type=F32))
        pre_i.append(jnp.dot(slab, wi_ref[s], preferred_element_type=F32))
    r = _sigmoid(jnp.concatenate(pre_r, axis=1) + br_ref[...])
    i = _sigmoid(jnp.concatenate(pre_i, axis=1) + bi_ref[...])
    log_a = (-LRU_C * r) * _softplus(-lam_ref[...])
    a = jnp.exp(log_a)
    z = 1.0 - a * a
    u = jnp.where(z > 0.0, z * lax.rsqrt(z), 0.0) * (i * xc)

    a_ref[...] = a
    u_ref[...] = u

    def step(j, carry):
        h_prev, p_prev = carry
        rows = pl.ds(pl.multiple_of(j * SUBLANES, SUBLANES), SUBLANES)
        a_j = a_ref[rows, :]
        h_j = a_j * h_prev + u_ref[rows, :]
        p_j = a_j * p_prev
        u_ref[rows, :] = h_j
        a_ref[rows, :] = p_j
        return h_j, p_j

    h_end, p_end = lax.fori_loop(0, steps, step, (jnp.zeros((SUBLANES, w), F32), jnp.ones((SUBLANES, w), F32)),
                                 unroll=4)
    c = carry_ref[0:1, :]
    entering = []
    for seg in range(SUBLANES):
        entering.append(c)
        c = p_end[seg:seg + 1, :] * c + h_end[seg:seg + 1, :]
    carry_ref[0:1, :] = c
    enter = jnp.concatenate(entering, axis=0)
    h = u_ref[...] + a_ref[...] * jnp.concatenate([enter] * steps, axis=0)
    out = (h * _gelu_tanh(gate)).astype(BF16)
    o_ref[...] = jnp.dot(unperm_ref[...], out, preferred_element_type=F32).astype(BF16)


def _lru(x3, g, w_lru, cw, cb, wr_bd, wi_bd, br, bi, lam):
    b, l, d = x3.shape
    width = lam.shape[1]
    perms = _segment_perm(LRU_TL)
    return pl.pallas_call(
        _lru_body,
        grid=(b, l // LRU_TL),
        in_specs=[pl.BlockSpec((None, LRU_TL, d), lambda bi_, li: (bi_, li, 0))]
        + [_resident(a.shape) for a in (g, w_lru, cw, cb, wr_bd, wi_bd, br, bi, lam, *perms)],
        out_specs=pl.BlockSpec((None, LRU_TL, width), lambda bi_, li: (bi_, li, 0)),
        out_shape=jax.ShapeDtypeStruct((b, l, width), BF16),
        scratch_shapes=[
            pltpu.VMEM((CONV_WIDTH - 1, SUBLANES, width), F32),
            pltpu.VMEM((LRU_TL, width), F32),
            pltpu.VMEM((LRU_TL, width), F32),
            pltpu.VMEM((SUBLANES, width), F32),
        ],
        compiler_params=_params("parallel", "arbitrary"),
        name="lru",
    )(x3, g, w_lru, cw, cb, wr_bd, wi_bd, br, bi, lam, *perms)


def _block_diag_slabs(w):
    nb, bs, _ = w.shape
    per = GATE_SLAB // bs
    w = w.reshape(nb // per, per, bs, bs)
    eye = jnp.eye(per, dtype=w.dtype)
    bd = jnp.einsum("spij,pq->spiqj", w, eye).reshape(nb // per, GATE_SLAB, GATE_SLAB)
    return bd.astype(BF16)


def _ssd_body(n_heads, x_ref, g_ref, w_ref, cw_ref, cb_ref, dtb_ref, alog_ref, dvec_ref, nrm_ref,
              exp_ref, perm_ref, unperm_ref, o_ref, proj_ref, halo_ref, state_ref):
    first = pl.program_id(1) == 0
    t = SSD_CHUNK
    width = n_heads * SSD_HEAD_DIM
    gw = SSD_STATE
    conv_ch = width + 2 * SSD_GROUPS * gw
    heads_per_group = n_heads // SSD_GROUPS
    n_chunks = x_ref.shape[0] // t

    @pl.when(first)
    def _():
        state_ref[...] = jnp.zeros_like(state_ref)
        halo_ref[...] = jnp.zeros_like(halo_ref)

    hn = _rmsnorm(x_ref[...], g_ref[...]).astype(BF16)
    hn = jnp.concatenate([jnp.dot(perm_ref[...], hn[c * t:(c + 1) * t, :], preferred_element_type=F32).astype(BF16)
                          for c in range(n_chunks)], axis=0)
    proj_ref[...] = jnp.dot(hn, w_ref[...], preferred_element_type=F32)

    steps = t // SUBLANES
    time_of = lambda i: (i & (SUBLANES - 1)) * steps + lax.shift_right_logical(i, 3)
    causal = (time_of(lax.broadcasted_iota(jnp.int32, (t, t), 0))
              >= time_of(lax.broadcasted_iota(jnp.int32, (t, t), 1)))
    tri = jnp.where(causal, 1.0, 0.0).astype(F32)
    lane = lax.broadcasted_iota(jnp.int32, (t, LANES), 1)
    lo = lane < SSD_HEAD_DIM
    rlo = lax.broadcasted_iota(jnp.int32, (LANES, gw), 0) < SSD_HEAD_DIM

    def chunk(c, carry):
        rows = pl.ds(pl.multiple_of(c * t, t), t)
        out = _ssd_chunk(proj_ref[rows, 0:width], proj_ref[rows, width:width + conv_ch],
                         proj_ref[rows, width + conv_ch:], n_heads, causal, tri, lo, rlo, halo_ref, state_ref,
                         cw_ref, cb_ref, dtb_ref, alog_ref, dvec_ref, nrm_ref, exp_ref)
        o_ref[rows, :] = jnp.dot(unperm_ref[...], out, preferred_element_type=F32).astype(BF16)
        return carry

    lax.fori_loop(0, n_chunks, chunk, 0)


def _ssd_chunk(z, xbc_raw, dt_raw, n_heads, causal, tri, lo, rlo, halo_ref, state_ref,
               cw_ref, cb_ref, dtb_ref, alog_ref, dvec_ref, nrm_ref, exp_ref):
    t = SSD_CHUNK
    width = n_heads * SSD_HEAD_DIM
    gw = SSD_STATE
    heads_per_group = n_heads // SSD_GROUPS
    xbc = _silu(_conv_segments(xbc_raw, halo_ref, cw_ref, cb_ref))
    xs = xbc[:, :width]
    dt = _softplus(dt_raw + dtb_ref[...])
    adt = dt * (-jnp.exp(alog_ref[...]))
    cs = jnp.dot(tri, adt, preferred_element_type=F32, precision=lax.Precision.HIGHEST)
    cs_t = cs.T
    cs_last = cs[t - 1:t, :]
    cdec = jnp.exp(cs_last)
    dt_w = _split_dot(dt, exp_ref[...], parts=3)
    cs_w = _split_dot(cs, exp_ref[...], parts=3)
    ecs_w = jnp.exp(cs_w)
    dte_w = jnp.exp(cs_w[t - 1:t, :] - cs_w)

    cbs = []
    for g in range(SSD_GROUPS):
        bm = xbc[:, width + g * gw: width + (g + 1) * gw].astype(BF16)
        cm = xbc[:, width + (SSD_GROUPS + g) * gw: width + (SSD_GROUPS + g + 1) * gw].astype(BF16)
        cb = lax.dot_general(cm, bm, (((1,), (1,)), ((), ())), preferred_element_type=F32)
        cbs.append((bm, cm, cb))

    ys = []
    for p in range(n_heads // 2):
        h0, h1 = 2 * p, 2 * p + 1
        bm, cm, cb = cbs[h0 // heads_per_group]
        sl = slice(p * LANES, (p + 1) * LANES)
        xs_p = xs[:, sl]
        xdt = xs_p * dt_w[:, sl]
        ms = []
        for h in (h0, h1):
            seg = cs[:, h:h + 1] - cs_t[h:h + 1, :]
            ms.append((cb * jnp.where(causal, jnp.exp(seg), 0.0)).astype(BF16))
        m_cat = jnp.concatenate(ms, axis=1)
        xdt_bd = jnp.concatenate([jnp.where(lo, xdt, 0.0), jnp.where(lo, 0.0, xdt)], axis=0).astype(BF16)
        y = jnp.dot(m_cat, xdt_bd, preferred_element_type=F32)
        prev = state_ref[sl, :]
        y_off = lax.dot_general(cm, prev.astype(BF16), (((1,), (1,)), ((), ())), preferred_element_type=F32)
        y = y + y_off * ecs_w[:, sl]
        xw = (xdt * dte_w[:, sl]).astype(BF16)
        st = lax.dot_general(xw, bm, (((0,), (0,)), ((), ())), preferred_element_type=F32)
        dec = jnp.where(rlo, cdec[:, h0:h0 + 1], cdec[:, h1:h1 + 1])
        state_ref[sl, :] = prev * dec + st
        ys.append(y + dvec_ref[:, sl] * xs_p)

    y = jnp.concatenate(ys, axis=1) * _silu(z)
    gsz = width // SSD_GROUPS
    outs = []
    for g in range(SSD_GROUPS):
        yg = y[:, g * gsz:(g + 1) * gsz]
        outs.append(yg * lax.rsqrt(jnp.mean(yg * yg, axis=-1, keepdims=True) + NORM_EPS))
    return (jnp.concatenate(outs, axis=1) * nrm_ref[...]).astype(BF16)


SSD_TS = 512


def _ssd(x3, g, w_ssd, cw, cb, dt_bias, a_log, d_vec, nrm, n_heads):
    b, l, d = x3.shape
    width = n_heads * SSD_HEAD_DIM
    conv_ch = width + 2 * SSD_GROUPS * SSD_STATE
    expand = jnp.asarray(np.arange(LANES)[:, None] == np.arange(width)[None, :] // SSD_HEAD_DIM,
                         dtype=BF16)
    consts = (g, w_ssd, cw, cb, dt_bias, a_log, d_vec, nrm, expand, *_segment_perm(SSD_CHUNK))
    return pl.pallas_call(
        functools.partial(_ssd_body, n_heads),
        grid=(b, l // SSD_TS),
        in_specs=[pl.BlockSpec((None, SSD_TS, d), lambda bi_, ci: (bi_, ci, 0))]
        + [_resident(a.shape) for a in consts],
        out_specs=pl.BlockSpec((None, SSD_TS, width), lambda bi_, ci: (bi_, ci, 0)),
        out_shape=jax.ShapeDtypeStruct((b, l, width), BF16),
        scratch_shapes=[
            pltpu.VMEM((SSD_TS, w_ssd.shape[1]), F32),
            pltpu.VMEM((CONV_WIDTH - 1, SUBLANES, conv_ch), F32),
            pltpu.VMEM((width, SSD_STATE), F32),
        ],
        compiler_params=_params("parallel", "arbitrary"),
        name="ssd",
    )(x3, *consts)


QKV_TM = 512
PERM_ROWS = 256


def _split_dot(x, m, parts=2):
    acc, rem = None, x
    for k in range(parts):
        piece = rem.astype(BF16)
        term = jnp.dot(piece, m, preferred_element_type=F32)
        acc = term if acc is None else acc + term
        if k + 1 < parts:
            rem = rem - piece.astype(F32)
    return acc


def _slab_map(fn, x):
    return jnp.concatenate([fn(x[:, c * GATE_SLAB:(c + 1) * GATE_SLAB]) for c in range(x.shape[1] // GATE_SLAB)],
                           axis=1)


def _head_norm_rope(y, gain, cos, sin, gsum, rot):
    ss = _slab_map(lambda v: _split_dot(v, gsum), y * y)
    yn = y * lax.rsqrt(ss * (1.0 / ATT_HEAD_DIM) + NORM_EPS) * gain
    partner = _slab_map(lambda v: _split_dot(v, rot), yn)
    tile = lambda tbl: jnp.concatenate([tbl] * (y.shape[1] // LANES), axis=1)
    return yn * tile(cos) + partner * tile(sin)


def _qkv_body(x_ref, g_ref, w_ref, qg_ref, kg_ref, cos_ref, sin_ref, gsum_ref, rot_ref, p1_ref, p4_ref, p16_ref,
              *refs):
    outs, h_ref = refs[:9], refs[9]
    j = pl.program_id(2)

    @pl.when(j == 0)
    def _():
        h_ref[...] = _rmsnorm(x_ref[...], g_ref[...]).astype(BF16)

    acc = jnp.dot(h_ref[...], w_ref[...], preferred_element_type=F32)

    def emit(y, o1, o4, o16, grouped=False):
        yb = y.astype(BF16)
        if grouped:
            sub = p1_ref.shape[0]
            for t in range(yb.shape[0] // sub):
                o1[t * sub:(t + 1) * sub, :] = jnp.dot(p1_ref[...], yb[t * sub:(t + 1) * sub, :],
                                                       preferred_element_type=F32).astype(BF16)
        else:
            o1[...] = yb
        for perm_ref, o in ((p4_ref, o4), (p16_ref, o16)):
            sub = perm_ref.shape[0]
            dil = o.shape[0]
            rows = sub // dil
            for t in range(yb.shape[0] // sub):
                yp = jnp.dot(perm_ref[...], yb[t * sub:(t + 1) * sub, :], preferred_element_type=F32).astype(BF16)
                for r in range(dil):
                    o[r, t * rows:(t + 1) * rows, :] = yp[r * rows:(r + 1) * rows, :]

    rope = (cos_ref[...], sin_ref[...], gsum_ref[...], rot_ref[...])

    @pl.when(j == 0)
    def _():
        emit(_head_norm_rope(acc, qg_ref[...], *rope) * (ATT_HEAD_DIM ** -0.5), *outs[0:3], grouped=True)

    @pl.when(j == 1)
    def _():
        emit(_head_norm_rope(acc, kg_ref[...], *rope), *outs[3:6])

    @pl.when(j == 2)
    def _():
        emit(acc, *outs[6:9])


def _qkv(x3, g, w_qkv, qg, kg, cos, sin):
    b, l, d = x3.shape
    tm = QKV_TM
    out_specs, out_shape = [], []
    for _ in range(3):
        out_specs.append(pl.BlockSpec((None, tm, d), lambda bi_, i, j: (bi_, i, 0)))
        out_shape.append(jax.ShapeDtypeStruct((b, l, d), BF16))
        for dil in DILATIONS[1:]:
            out_specs.append(pl.BlockSpec((None, dil, tm // dil, d), lambda bi_, i, j: (bi_, 0, i, 0)))
            out_shape.append(jax.ShapeDtypeStruct((b, dil, l // dil, d), BF16))
    consts = _qkv_constants(tm)
    tbl = lambda: pl.BlockSpec((tm, LANES), lambda bi_, i, j: (i, 0))
    return pl.pallas_call(
        _qkv_body,
        grid=(b, l // tm, 3),
        in_specs=[
            pl.BlockSpec((None, tm, d), lambda bi_, i, j: (bi_, i, 0)),
            _resident((1, d)),
            pl.BlockSpec((d, d), lambda bi_, i, j: (0, j)),
            _resident((1, d)), _resident((1, d)), tbl(), tbl(),
        ] + [_resident(c.shape) for c in consts],
        out_specs=out_specs,
        out_shape=out_shape,
        scratch_shapes=[pltpu.VMEM((tm, d), BF16)],
        compiler_params=_params("parallel", "parallel", "arbitrary"),
        name="qkv",
    )(x3, g, w_qkv, qg, kg, cos, sin, *consts)


def _qkv_constants(tm):
    lane = np.arange(GATE_SLAB)
    gsum = (lane[:, None] // ATT_HEAD_DIM == lane[None, :] // ATT_HEAD_DIM)
    half = ROPE_DIM // 2
    pos = lane % ATT_HEAD_DIM
    partner = np.where(pos < half, lane + half, np.where(pos < ROPE_DIM, lane - half, -1))
    rot = lane[:, None] == partner[None, :]
    out_row = np.arange(PERM_ROWS)
    group = SPAN // ACC_DIL
    within = out_row % SPAN
    src = out_row - within + (within % group) * ACC_DIL + within // group
    perms = [src[:, None] == np.arange(PERM_ROWS)[None, :]]
    for dil in DILATIONS[1:]:
        sub = PERM_ROWS
        rows = sub // dil
        assert rows % BF16_ROWS == 0 and tm % sub == 0
        out_row = np.arange(sub)
        src = (out_row % rows) * dil + out_row // rows
        perms.append(src[:, None] == np.arange(sub)[None, :])
    return tuple(jnp.asarray(m, dtype=BF16) for m in (gsum, rot, *perms))


def _rope_tables(l):
    half = ROPE_DIM // 2
    pos = jnp.arange(l, dtype=F32)
    inv = ROPE_THETA ** (-2.0 * jnp.arange(half, dtype=F32) / ROPE_DIM)
    ang = pos[:, None] * inv[None, :]
    cos, sin = jnp.cos(ang), jnp.sin(ang)
    pad = ATT_HEAD_DIM - ROPE_DIM
    cos_h = jnp.concatenate([cos, cos, jnp.ones((l, pad), F32)], axis=1)
    sin_h = jnp.concatenate([-sin, sin, jnp.zeros((l, pad), F32)], axis=1)
    rep = LANES // ATT_HEAD_DIM
    return tuple(jnp.concatenate([tb] * rep, axis=1) for tb in (cos_h, sin_h))


def _attn_body(q1, k1c, k1p, v1c, v1p, q4, k4c, k4p, v4c, v4p, q16, k16c, k16p, v16c, v16p,
               o_ref, kb1, vb1, kb4, vb4, kb16, vb16, s_scr, p_scr, mx_scr, bias_scr, tok_scr, acc_ref, m_ref, l_ref):
    first = pl.program_id(2) == 0
    n_pairs = acc_ref.shape[0]
    n_blocks = SUPER // SPAN
    trans_b = (((1,), (1,)), ((), ()))

    for buf, prev, cur in ((kb1, k1p, k1c), (vb1, v1p, v1c)):
        buf[0, 0:SPAN, :] = prev[...]
        buf[0, SPAN:, :] = cur[...]
    for buf, prev, cur in ((kb4, k4p, k4c), (vb4, v4p, v4c), (kb16, k16p, k16c), (vb16, v16p, v16c)):
        buf[:, 0:SPAN, :] = prev[...]
        buf[:, SPAN:, :] = cur[...]

    row = lax.broadcasted_iota(jnp.int32, (SPAN, 2 * SPAN), 0)
    kj = lax.broadcasted_iota(jnp.int32, (SPAN, 2 * SPAN), 1)
    group = SPAN // ACC_DIL
    for base, qi in ((0, row), (2, (row % group) * ACC_DIL + row // group)):
        band = (kj >= qi) & (kj <= qi + SPAN)
        bias_scr[base] = jnp.where(band, 0.0, -jnp.inf)
        bias_scr[base + 1] = jnp.where(band & (kj >= SPAN), 0.0, -jnp.inf)

    lo =lax.broadcasted_iota(jnp.int32, (SPAN, LANES), 1) < ATT_HEAD_DIM
    ones_v = jnp.ones((2 * SPAN, LANES), BF16)
    lanes = lambda p: slice(p * LANES, (p + 1) * LANES)

    def run_pattern(dil, q_get, kb, vb):
        per_res = SUPER // (SPAN * dil)
        overwrite = dil == DILATIONS[0]

        def split(f):
            if per_res == 1:
                return f, 0
            if dil == 1:
                return 0, f
            if isinstance(f, int):
                return f // per_res, f % per_res
            return lax.shift_right_logical(f, per_res.bit_length() - 1), f & (per_res - 1)

        def slab_rows(nb):
            return pl.ds(nb * SPAN if isinstance(nb, int) else pl.multiple_of(nb * SPAN, SPAN), 2 * SPAN)

        def stage_scores(f, slot):
            r, nb = split(f)
            if isinstance(nb, int):
                flag = jnp.where(first, 1, 0) if nb == 0 else 0
            else:
                flag = jnp.where(first & (nb == 0), 1, 0)
            bias = bias_scr[flag + (2 if overwrite else 0)]
            bias2 = jnp.concatenate([bias, bias], axis=0)
            q = q_get(r, nb)
            ks = kb[r, slab_rows(nb), :]
            for p in range(n_pairs):
                qp = q[:, lanes(p)]
                zero = jnp.zeros_like(qp)
                q2 = jnp.concatenate([jnp.where(lo, qp, zero), jnp.where(lo, zero, qp)], axis=0)
                s_scr[slot, p] = lax.dot_general(q2, ks[:, lanes(p)], trans_b, preferred_element_type=F32) + bias2

        def stage_softmax(slot):
            for p in range(n_pairs):
                s = s_scr[slot, p]
                mx = jnp.max(s, axis=1, keepdims=True)
                p_scr[slot, p] = jnp.exp(s - mx).astype(BF16)
                mx_scr[slot, p] = jnp.where(lo, mx[:SPAN], mx[SPAN:])

        def stage_values(f, slot):
            r, nb = split(f)
            vs = vb[r, slab_rows(nb), :]
            if overwrite:
                grp = SPAN // ACC_DIL
                start = nb * grp if isinstance(nb, int) else pl.multiple_of(nb * grp, grp)
                pieces = [(slice(c * grp, (c + 1) * grp), c, pl.ds(start, grp)) for c in range(ACC_DIL)]
            elif dil == ACC_DIL:
                pieces = [(slice(0, SPAN), r, q_rows(nb))]
            else:
                sub = dil // ACC_DIL
                pieces = [(slice(0, SPAN), r & (ACC_DIL - 1),
                           pl.ds(lax.shift_right_logical(r, 2) if not isinstance(r, int) else r // ACC_DIL,
                                 SPAN, stride=sub))]
            for p in range(n_pairs):
                v_aug = jnp.concatenate([vs[:, lanes(p)], ones_v], axis=1)
                pv = jnp.dot(p_scr[slot, p], v_aug, preferred_element_type=F32)
                o_blk = jnp.where(lo, pv[:SPAN, :LANES], pv[SPAN:, :LANES])
                l_blk = jnp.where(lo, pv[:SPAN, LANES:], pv[SPAN:, LANES:])
                m_blk = mx_scr[slot, p]
                for src, cls, rows in pieces:
                    m_new, l_new, o_new = m_blk[src], l_blk[src], o_blk[src]
                    if overwrite:
                        m_ref[p, cls, rows, :], l_ref[p, cls, rows, :], acc_ref[p, cls, rows, :] = m_new, l_new, o_new
                        continue
                    m_old = m_ref[p, cls, rows, :]
                    m = jnp.maximum(m_old, m_new)
                    e_old = jnp.exp(m_old - m)
                    e_new = jnp.exp(m_new - m)
                    l_ref[p, cls, rows, :] = e_old * l_ref[p, cls, rows, :] + e_new * l_new
                    acc_ref[p, cls, rows, :] = e_old * acc_ref[p, cls, rows, :] + e_new * o_new
                    m_ref[p, cls, rows, :] = m

        stage_scores(0, 0)
        stage_scores(1, 1)
        stage_softmax(0)

        def body(i, c):
            f = 2 * i + 2
            stage_scores(f, 0)
            stage_softmax(1)
            stage_values(f - 2, 0)
            stage_scores(f + 1, 1)
            stage_softmax(0)
            stage_values(f - 1, 1)
            return c

        lax.fori_loop(0, (n_blocks - 2) // 2, body, 0)
        stage_softmax(1)
        stage_values(n_blocks - 2, 0)
        stage_values(n_blocks - 1, 1)

    def q_rows(nb):
        return pl.ds(nb * SPAN if isinstance(nb, int) else pl.multiple_of(nb * SPAN, SPAN), SPAN)

    run_pattern(DILATIONS[0], lambda r, nb: q1[q_rows(nb), :], kb1, vb1)
    run_pattern(DILATIONS[1], lambda r, nb: q4[r, q_rows(nb), :], kb4, vb4)
    run_pattern(DILATIONS[2], lambda r, nb: q16[r], kb16, vb16)

    for p in range(n_pairs):
        for cls in range(ACC_DIL):
            tok_scr[p, pl.ds(cls, SUPER // ACC_DIL, stride=ACC_DIL), :] = acc_ref[p, cls] / l_ref[p, cls]
        o_ref[:, lanes(p)] = tok_scr[p].astype(BF16)


def _attention(qkv_outs):
    q1, q4, q16, k1, k4, k16, v1, v4, v16 = qkv_outs
    b, l, d = q1.shape
    w = ATT_LANES
    n_pairs = w // LANES
    d4, d16 = DILATIONS[1], DILATIONS[2]
    cur1 = pl.BlockSpec((None, SUPER, w), lambda bi_, g, s: (bi_, s, g))
    prev1 = pl.BlockSpec((None, SPAN, w), lambda bi_, g, s: (bi_, jnp.maximum(s * (SUPER // SPAN) - 1, 0), g))
    cur4 = pl.BlockSpec((None, d4, SUPER // d4, w), lambda bi_, g, s: (bi_, 0, s, g))
    prev4 = pl.BlockSpec((None, d4, SPAN, w),
                         lambda bi_, g, s: (bi_, 0, jnp.maximum(s * (SUPER // (d4 * SPAN)) - 1, 0), g))
    cur16 = pl.BlockSpec((None, d16, SPAN, w), lambda bi_, g, s: (bi_, 0, s, g))
    prev16 = pl.BlockSpec((None, d16, SPAN, w), lambda bi_, g, s: (bi_, 0, jnp.maximum(s - 1, 0), g))
    return pl.pallas_call(
        _attn_body,
        grid=(b, d // w, l // SUPER),
        in_specs=[cur1, cur1, prev1, cur1, prev1,
                  cur4, cur4, prev4, cur4, prev4,
                  cur16, cur16, prev16, cur16, prev16],
        out_specs=pl.BlockSpec((None, SUPER, w), lambda bi_, g, s: (bi_, s, g)),
        out_shape=jax.ShapeDtypeStruct((b, l, d), BF16),
        scratch_shapes=[
            pltpu.VMEM((1, SPAN + SUPER, w), BF16), pltpu.VMEM((1, SPAN + SUPER, w), BF16),
            pltpu.VMEM((d4, SPAN + SUPER // d4, w), BF16), pltpu.VMEM((d4, SPAN + SUPER // d4, w), BF16),
            pltpu.VMEM((d16, 2 * SPAN, w), BF16), pltpu.VMEM((d16, 2 * SPAN, w), BF16),
            pltpu.VMEM((2, n_pairs, 2 * SPAN, 2 * SPAN), F32),
            pltpu.VMEM((2, n_pairs, 2 * SPAN, 2 * SPAN), BF16),
            pltpu.VMEM((2, n_pairs, SPAN, LANES), F32),
            pltpu.VMEM((4, SPAN, 2 * SPAN), F32),
            pltpu.VMEM((n_pairs, SUPER, LANES), F32),
        ] + [pltpu.VMEM((n_pairs, ACC_DIL, SUPER // ACC_DIL, LANES), F32)] * 3,
        compiler_params=_params("parallel", "parallel", "arbitrary"),
        name="dilated_attention",
    )(q1, k1, k1, v1, v1, q4, k4, k4, v4, v4, q16, k16, k16, v16, v16)


def _recurrent_layer(x, b, l, norm_g, w_in, lru_conv_w, lru_conv_b, lru_w_r, lru_b_r, lru_w_i, lru_b_i,
                     lru_lambda, ssd_conv_w, ssd_conv_b, ssd_dt_bias, ssd_a_log, ssd_d, ssd_norm, w_out):
    d = x.shape[1]
    width = lru_lambda.shape[0]
    n_heads = ssd_a_log.shape[0]
    n_main = w_in.shape[1] - n_heads
    row = lambda v: v.reshape(1, -1)
    pad_lanes = lambda v: jnp.pad(v.reshape(1, -1), ((0, 0), (0, LANES - v.shape[0])))
    w_lru = w_in[:, :2 * width].astype(BF16)
    w_ssd = jnp.pad(w_in[:, 2 * width:], ((0, 0), (0, LANES - n_heads))).astype(BF16)
    x3 = x.reshape(b, l, d)
    out_a = _lru(x3, row(norm_g), w_lru, lru_conv_w, row(lru_conv_b), _block_diag_slabs(lru_w_r),
                 _block_diag_slabs(lru_w_i), row(lru_b_r), row(lru_b_i), row(lru_lambda))
    out_b = _ssd(x3, row(norm_g), w_ssd, ssd_conv_w, row(ssd_conv_b), pad_lanes(ssd_dt_bias),
                 pad_lanes(ssd_a_log), row(jnp.repeat(ssd_d, SSD_HEAD_DIM)), row(ssd_norm), n_heads)
    w_out = w_out.astype(BF16)
    return [out_a.reshape(b * l, width), out_b.reshape(b * l, -1)], [w_out[:width], w_out[width:]]


def _attention_layer(x, b, l, norm_g, w_qkv, q_norm, k_norm, w_out):
    d = x.shape[1]
    heads = d // ATT_HEAD_DIM
    row = lambda v: v.reshape(1, -1)
    outs = _qkv(x.reshape(b, l, d), row(norm_g), w_qkv.astype(BF16),
                row(jnp.tile(q_norm, heads)), row(jnp.tile(k_norm, heads)), *_rope_tables(l))
    o = _attention(outs)
    return [o.reshape(b * l, d)], [w_out.astype(BF16)]


def kernel(x, rec_norm, rec_w_in, lru_conv_w, lru_conv_b, lru_w_r, lru_b_r, lru_w_i, lru_b_i, lru_lambda,
           ssd_conv_w, ssd_conv_b, ssd_dt_bias, ssd_a_log, ssd_d, ssd_norm, rec_w_out, att_norm, att_w_qkv,
           att_q_norm, att_k_norm, att_w_out, ffn_norm, ffn_w_gate_up, ffn_w_down):
    b, l, d = x.shape
    depth = ffn_norm.shape[0]
    h = x.reshape(b * l, d)
    for layer in range(depth):
        i = layer // 2
        if layer % 2 == 0:
            mixes, wos = _recurrent_layer(h, b, l, rec_norm[i], rec_w_in[i], lru_conv_w[i], lru_conv_b[i],
                                          lru_w_r[i], lru_b_r[i], lru_w_i[i], lru_b_i[i], lru_lambda[i],
                                          ssd_conv_w[i], ssd_conv_b[i], ssd_dt_bias[i], ssd_a_log[i], ssd_d[i],
                                          ssd_norm[i], rec_w_out[i])
        else:
            mixes, wos = _attention_layer(h, b, l, att_norm[i], att_w_qkv[i], att_q_norm[i], att_k_norm[i],
                                          att_w_out[i])
        wgu, wd = _prep_ffn_weights(ffn_w_gate_up[layer], ffn_w_down[layer])
        h = _mix_ffn(h, mixes, wos, ffn_norm[layer].reshape(1, d), wgu, wd)
    return h.reshape(b, l, d)
```

```python
import functools
import math

import jax
import jax.numpy as jnp
import numpy as np
from jax import lax
from jax.experimental import pallas as pl
from jax.experimental.pallas import tpu as pltpu

F32 = jnp.float32
BF16 = jnp.bfloat16

NORM_EPS = 1e-6
CONV_WIDTH = 4
LRU_BLOCK = 64
LRU_C = 8.0
GATE_SLAB = 256
SSD_HEAD_DIM = 64
SSD_GROUPS = 2
SSD_STATE = 128
SSD_CHUNK = 128
ATT_HEAD_DIM = 64
ROPE_DIM = 16
ROPE_THETA = 500000.0
SPAN = 128
DILATIONS = (1, 4, 16)
SUPER = SPAN * DILATIONS[-1]
ATT_LANES = 256
ACC_DIL = DILATIONS[1]
LANES = 128
SUBLANES = 8
BF16_ROWS = 16
VMEM_LIMIT_BYTES = 56 * 1024 * 1024


def _params(*semantics):
    return pltpu.CompilerParams(dimension_semantics=semantics, vmem_limit_bytes=VMEM_LIMIT_BYTES)


def _rmsnorm(x, g):
    ms = jnp.mean(x * x, axis=-1, keepdims=True)
    return x * lax.rsqrt(ms + NORM_EPS) * g


def _sigmoid(x):
    return 1.0 / (1.0 + jnp.exp(-x))


def _silu(x):
    return x * _sigmoid(x)


def _softplus(x):
    return jnp.maximum(x, 0.0) + jnp.log1p(jnp.exp(-jnp.abs(x)))


def _gelu_tanh(x):
    c = math.sqrt(2.0 / math.pi)
    return x * (0.5 * (1.0 + jnp.tanh(c * (x + 0.044715 * (x * x * x)))))


FFN_TM = 512
FFN_TH = 256


def _ffn_body(n_mix, x_ref, *refs):
    mix_refs, wo_refs = refs[:n_mix], refs[n_mix:2 * n_mix]
    g_ref, wgu_ref, wd_ref, o_ref, h_ref, act_ref = refs[2 * n_mix:]
    x1 = x_ref[...]
    for mix_ref, wo_ref in zip(mix_refs, wo_refs):
        x1 = x1 + jnp.dot(mix_ref[...], wo_ref[...], preferred_element_type=F32)
    o_ref[...] = x1
    h_ref[...] = _rmsnorm(x1, g_ref[...]).astype(BF16)
    for c in range(act_ref.shape[1] // FFN_TH):
        gu = jnp.dot(h_ref[...], wgu_ref[:, 2 * c * FFN_TH:2 * (c + 1) * FFN_TH], preferred_element_type=F32)
        act_ref[:, c * FFN_TH:(c + 1) * FFN_TH] = (_silu(gu[:, :FFN_TH]) * gu[:, FFN_TH:]).astype(BF16)
    o_ref[...] += jnp.dot(act_ref[...], wd_ref[...], preferred_element_type=F32)


def _resident(shape):
    return pl.BlockSpec(shape, lambda *_: (0,) * len(shape), pipeline_mode=pl.Buffered(1))


def _mix_ffn(x, mixes, wos, g, wgu, wd):
    t, d = x.shape
    hid = wd.shape[0]
    rows = lambda a: pl.BlockSpec((FFN_TM, a.shape[1]), lambda i: (i, 0))
    return pl.pallas_call(
        functools.partial(_ffn_body, len(mixes)),
        grid=(t // FFN_TM,),
        in_specs=[rows(x)] + [rows(a) for a in mixes] + [_resident(a.shape) for a in (*wos, g, wgu, wd)],
        out_specs=rows(x),
        out_shape=jax.ShapeDtypeStruct((t, d), F32),
        scratch_shapes=[pltpu.VMEM((FFN_TM, d), BF16), pltpu.VMEM((FFN_TM, hid), BF16)],
        compiler_params=_params("parallel"),
        name="mix_ffn",
    )(x, *mixes, *wos, g, wgu, wd)


def _prep_ffn_weights(w_gate_up, w_down):
    d, two_h = w_gate_up.shape
    hid = two_h // 2
    nh = hid // FFN_TH
    wg = w_gate_up[:, :hid].reshape(d, nh, FFN_TH)
    wu = w_gate_up[:, hid:].reshape(d, nh, FFN_TH)
    wgu = jnp.concatenate([wg, wu], axis=2).reshape(d, 2 * hid).astype(BF16)
    return wgu, w_down.astype(BF16)


def _segment_perm(rows):
    steps = rows // SUBLANES
    rho = np.arange(rows)
    time = (rho % SUBLANES) * steps + rho // SUBLANES
    perm = time[:, None] == np.arange(rows)[None, :]
    return jnp.asarray(perm, dtype=BF16), jnp.asarray(perm.T, dtype=BF16)


def _conv_segments(x, halo_ref, w_ref, b_ref):
    rows, width = x.shape
    sub0 = lax.broadcasted_iota(jnp.int32, (SUBLANES, width), 0) == 0
    halos = []
    for k in range(1, CONV_WIDTH):
        tail = pltpu.roll(x[rows - k * SUBLANES:rows - (k - 1) * SUBLANES, :], 1, 0)
        halos.append(jnp.where(sub0, halo_ref[k - 1], tail))
        halo_ref[k - 1] = tail
    y = b_ref[...] + w_ref[CONV_WIDTH - 1:CONV_WIDTH, :] * x
    for k in range(1, CONV_WIDTH):
        back = jnp.concatenate(halos[k - 1::-1] + [x[:rows - k * SUBLANES, :]], axis=0)
        y = y + w_ref[CONV_WIDTH - 1 - k:CONV_WIDTH - k, :] * back
    return y


LRU_TL = 512
LRU_SUB = 256


def _lru_body(x_ref, g_ref, w_ref, cw_ref, cb_ref, wr_ref, wi_ref, br_ref, bi_ref, lam_ref, perm_ref, unperm_ref,
              o_ref, halo_ref, carry_ref):
    first = pl.program_id(1) == 0
    w = o_ref.shape[1]
    sub = perm_ref.shape[0]
    steps = sub // SUBLANES

    @pl.when(first)
    def _():
        carry_ref[...] = jnp.zeros_like(carry_ref)
        halo_ref[...] = jnp.zeros_like(halo_ref)

    for t in range(o_ref.shape[0] // sub):
        rows = slice(t * sub, (t + 1) * sub)
        hn = _rmsnorm(x_ref[rows, :], g_ref[...]).astype(BF16)
        hn = jnp.dot(perm_ref[...], hn, preferred_element_type=F32).astype(BF16)
        proj = jnp.dot(hn, w_ref[...], preferred_element_type=F32)
        gate = proj[:, w:]

        xc = _conv_segments(proj[:, :w], halo_ref, cw_ref, cb_ref)
        xcb = xc.astype(BF16)
        pre_r, pre_i = [], []
        for s in range(w // GATE_SLAB):
            slab = xcb[:, s * GATE_SLAB:(s + 1) * GATE_SLAB]
            pre_r.append(jnp.dot(slab, wr_ref[s], preferred_element_type=F32))
            pre_i.append(jnp.dot(slab, wi_ref[s], preferred_element_type=F32))
        r = _sigmoid(jnp.concatenate(pre_r, axis=1) + br_ref[...])
        i = _sigmoid(jnp.concatenate(pre_i, axis=1) + bi_ref[...])
        log_a = (-LRU_C * r) * _softplus(-lam_ref[...])
        a = jnp.exp(log_a)
        z = 1.0 - a * a
        u = jnp.where(z > 0.0, z * lax.rsqrt(z), 0.0) * (i * xc)

        h_j = jnp.zeros((SUBLANES, w), F32)
        p_j = jnp.ones((SUBLANES, w), F32)
        hs, ps = [], []
        for j in range(steps):
            a_j = a[j * SUBLANES:(j + 1) * SUBLANES, :]
            h_j = a_j * h_j + u[j * SUBLANES:(j + 1) * SUBLANES, :]
            p_j = a_j * p_j
            hs.append(h_j)
            ps.append(p_j)
        c = carry_ref[0:1, :]
        entering = []
        for seg in range(SUBLANES):
            entering.append(c)
            c = p_j[seg:seg + 1, :] * c + h_j[seg:seg + 1, :]
        carry_ref[0:1, :] = c
        enter = jnp.concatenate(entering, axis=0)
        h = jnp.concatenate([hs[j] + ps[j] * enter for j in range(steps)], axis=0)
        out = (h * _gelu_tanh(gate)).astype(BF16)
        o_ref[rows, :] = jnp.dot(unperm_ref[...], out, preferred_element_type=F32).astype(BF16)


def _lru(x3, g, w_lru, cw, cb, wr_bd, wi_bd, br, bi, lam):
    b, l, d = x3.shape
    width = lam.shape[1]
    perms = _segment_perm(LRU_SUB)
    return pl.pallas_call(
        _lru_body,
        grid=(b, l // LRU_TL),
        in_specs=[pl.BlockSpec((None, LRU_TL, d), lambda bi_, li: (bi_, li, 0))]
        + [_resident(a.shape) for a in (g, w_lru, cw, cb, wr_bd, wi_bd, br, bi, lam, *perms)],
        out_specs=pl.BlockSpec((None, LRU_TL, width), lambda bi_, li: (bi_, li, 0)),
        out_shape=jax.ShapeDtypeStruct((b, l, width), BF16),
        scratch_shapes=[
            pltpu.VMEM((CONV_WIDTH - 1, SUBLANES, width), F32),
            pltpu.VMEM((SUBLANES, width), F32),
        ],
        compiler_params=_params("parallel", "arbitrary"),
        name="lru",
    )(x3, g, w_lru, cw, cb, wr_bd, wi_bd, br, bi, lam, *perms)


def _block_diag_slabs(w):
    nb, bs, _ = w.shape
    per = GATE_SLAB // bs
    w = w.reshape(nb // per, per, bs, bs)
    eye = jnp.eye(per, dtype=w.dtype)
    bd = jnp.einsum("spij,pq->spiqj", w, eye).reshape(nb // per, GATE_SLAB, GATE_SLAB)
    return bd.astype(BF16)


def _ssd_body(n_heads, x_ref, g_ref, w_ref, cw_ref, cb_ref, dtb_ref, alog_ref, dvec_ref, nrm_ref,
              exp_ref, perm_ref, unperm_ref, o_ref, proj_ref, halo_ref, state_ref):
    first = pl.program_id(1) == 0
    t = SSD_CHUNK
    width = n_heads * SSD_HEAD_DIM
    gw = SSD_STATE
    conv_ch = width + 2 * SSD_GROUPS * gw
    heads_per_group = n_heads // SSD_GROUPS
    n_chunks = x_ref.shape[0] // t

    @pl.when(first)
    def _():
        state_ref[...] = jnp.zeros_like(state_ref)
        halo_ref[...] = jnp.zeros_like(halo_ref)

    hn = _rmsnorm(x_ref[...], g_ref[...]).astype(BF16)
    hn = jnp.concatenate([jnp.dot(perm_ref[...], hn[c * t:(c + 1) * t, :], preferred_element_type=F32).astype(BF16)
                          for c in range(n_chunks)], axis=0)
    proj_ref[...] = jnp.dot(hn, w_ref[...], preferred_element_type=F32)

    steps = t // SUBLANES
    time_of = lambda i: (i & (SUBLANES - 1)) * steps + lax.shift_right_logical(i, 3)
    causal = (time_of(lax.broadcasted_iota(jnp.int32, (t, t), 0))
              >= time_of(lax.broadcasted_iota(jnp.int32, (t, t), 1)))
    tri = jnp.where(causal, 1.0, 0.0).astype(BF16)
    lane = lax.broadcasted_iota(jnp.int32, (t, LANES), 1)
    lo = lane < SSD_HEAD_DIM
    rlo = lax.broadcasted_iota(jnp.int32, (LANES, gw), 0) < SSD_HEAD_DIM

    for c in range(n_chunks):
        rows = slice(c * t, (c + 1) * t)
        out = _ssd_chunk(proj_ref[rows, 0:width], proj_ref[rows, width:width + conv_ch],
                         proj_ref[rows, width + conv_ch:], n_heads, causal, tri, lo, rlo, halo_ref, state_ref,
                         cw_ref, cb_ref, dtb_ref, alog_ref, dvec_ref, nrm_ref, exp_ref)
        o_ref[rows, :] = jnp.dot(unperm_ref[...], out, preferred_element_type=F32).astype(BF16)


def _ssd_chunk(z, xbc_raw, dt_raw, n_heads, causal, tri, lo, rlo, halo_ref, state_ref,
               cw_ref, cb_ref, dtb_ref, alog_ref, dvec_ref, nrm_ref, exp_ref):
    t = SSD_CHUNK
    width = n_heads * SSD_HEAD_DIM
    gw = SSD_STATE
    heads_per_group = n_heads // SSD_GROUPS
    xbc = _silu(_conv_segments(xbc_raw, halo_ref, cw_ref, cb_ref))
    xs = xbc[:, :width]
    dt = _softplus(dt_raw + dtb_ref[...])
    adt = dt * (-jnp.exp(alog_ref[...]))
    cs = _split_dot(adt, tri, parts=3, left=True)
    cs_t = cs.T
    cs_last = cs[t - 1:t, :]
    cdec = jnp.exp(cs_last)
    dt_w = _split_dot(dt, exp_ref[...], parts=2)
    cs_w = _split_dot(cs, exp_ref[...], parts=3)
    ecs_w = jnp.exp(cs_w)
    dte_w = jnp.exp(cs_w[t - 1:t, :] - cs_w)

    cbs = []
    for g in range(SSD_GROUPS):
        bm = xbc[:, width + g * gw: width + (g + 1) * gw].astype(BF16)
        cm = xbc[:, width + (SSD_GROUPS + g) * gw: width + (SSD_GROUPS + g + 1) * gw].astype(BF16)
        cb = lax.dot_general(cm, bm, (((1,), (1,)), ((), ())), preferred_element_type=F32)
        cbs.append((bm, cm, cb))

    ys = []
    for p in range(n_heads // 2):
        h0, h1 = 2 * p, 2 * p + 1
        bm, cm, cb = cbs[h0 // heads_per_group]
        sl = slice(p * LANES, (p + 1) * LANES)
        xs_p = xs[:, sl]
        xdt = xs_p * dt_w[:, sl]
        ms = []
        for h in (h0, h1):
            seg = cs[:, h:h + 1] - cs_t[h:h + 1, :]
            ms.append((cb * jnp.where(causal, jnp.exp(seg), 0.0)).astype(BF16))
        m_cat = jnp.concatenate(ms, axis=1)
        xdt_bd = jnp.concatenate([jnp.where(lo, xdt, 0.0), jnp.where(lo, 0.0, xdt)], axis=0).astype(BF16)
        y = jnp.dot(m_cat, xdt_bd, preferred_element_type=F32)
        prev = state_ref[sl, :]
        y_off = lax.dot_general(cm, prev.astype(BF16), (((1,), (1,)), ((), ())), preferred_element_type=F32)
        y = y + y_off * ecs_w[:, sl]
        xw = (xdt * dte_w[:, sl]).astype(BF16)
        st = lax.dot_general(xw, bm, (((0,), (0,)), ((), ())), preferred_element_type=F32)
        dec = jnp.where(rlo, cdec[:, h0:h0 + 1], cdec[:, h1:h1 + 1])
        state_ref[sl, :] = prev * dec + st
        ys.append(y + dvec_ref[:, sl] * xs_p)

    y = jnp.concatenate(ys, axis=1) * _silu(z)
    gsz = width // SSD_GROUPS
    outs = []
    for g in range(SSD_GROUPS):
        yg = y[:, g * gsz:(g + 1) * gsz]
        outs.append(yg * lax.rsqrt(jnp.mean(yg * yg, axis=-1, keepdims=True) + NORM_EPS))
    return (jnp.concatenate(outs, axis=1) * nrm_ref[...]).astype(BF16)


SSD_TS = 512


def _ssd(x3, g, w_ssd, cw, cb, dt_bias, a_log, d_vec, nrm, n_heads):
    b, l, d = x3.shape
    width = n_heads * SSD_HEAD_DIM
    conv_ch = width + 2 * SSD_GROUPS * SSD_STATE
    expand = jnp.asarray(np.arange(LANES)[:, None] == np.arange(width)[None, :] // SSD_HEAD_DIM,
                         dtype=BF16)
    consts = (g, w_ssd, cw, cb, dt_bias, a_log, d_vec, nrm, expand, *_segment_perm(SSD_CHUNK))
    return pl.pallas_call(
        functools.partial(_ssd_body, n_heads),
        grid=(b, l // SSD_TS),
        in_specs=[pl.BlockSpec((None, SSD_TS, d), lambda bi_, ci: (bi_, ci, 0))]
        + [_resident(a.shape) for a in consts],
        out_specs=pl.BlockSpec((None, SSD_TS, width), lambda bi_, ci: (bi_, ci, 0)),
        out_shape=jax.ShapeDtypeStruct((b, l, width), BF16),
        scratch_shapes=[
            pltpu.VMEM((SSD_TS, w_ssd.shape[1]), F32),
            pltpu.VMEM((CONV_WIDTH - 1, SUBLANES, conv_ch), F32),
            pltpu.VMEM((width, SSD_STATE), F32),
        ],
        compiler_params=_params("parallel", "arbitrary"),
        name="ssd",
    )(x3, *consts)


QKV_TM = 512
PERM_ROWS = 256


def _split_dot(x, m, parts=2, left=False):
    acc, rem = None, x
    for k in range(parts):
        piece = rem.astype(BF16)
        term = jnp.dot(m, piece, preferred_element_type=F32) if left else jnp.dot(piece, m, preferred_element_type=F32)
        acc = term if acc is None else acc + term
        if k + 1 < parts:
            rem = rem - piece.astype(F32)
    return acc


def _slab_map(fn, x):
    return jnp.concatenate([fn(x[:, c * GATE_SLAB:(c + 1) * GATE_SLAB]) for c in range(x.shape[1] // GATE_SLAB)],
                           axis=1)


def _head_norm_rope(y, gain, cos, sin, gsum, rot):
    ss = _slab_map(lambda v: _split_dot(v, gsum), y * y)
    yn = y * lax.rsqrt(ss * (1.0 / ATT_HEAD_DIM) + NORM_EPS) * gain
    partner = _slab_map(lambda v: _split_dot(v, rot), yn)
    tile = lambda tbl: jnp.concatenate([tbl] * (y.shape[1] // LANES), axis=1)
    return yn * tile(cos) + partner * tile(sin)


def _qkv_body(x_ref, g_ref, w_ref, qg_ref, kg_ref, cos_ref, sin_ref, gsum_ref, rot_ref, p1_ref, p4_ref, p16_ref,
              *refs):
    outs, h_ref = refs[:9], refs[9]
    j = pl.program_id(2)

    @pl.when(j == 0)
    def _():
        h_ref[...] = _rmsnorm(x_ref[...], g_ref[...]).astype(BF16)

    sub = PERM_ROWS

    def section(o1, o4, o16, gain_ref=None, scale=None, grouped=False):
        y = jnp.dot(h_ref[...], w_ref[...], preferred_element_type=F32)
        if gain_ref is not None:
            y = _head_norm_rope(y, gain_ref[...], cos_ref[...], sin_ref[...], gsum_ref[...], rot_ref[...])
        if scale is not None:
            y = y * scale
        y_all = y.astype(BF16)
        for t in range(x_ref.shape[0] // sub):
            rows = slice(t * sub, (t + 1) * sub)
            yb = y_all[rows, :]
            if grouped:
                o1[rows, :] = jnp.dot(p1_ref[...], yb, preferred_element_type=F32).astype(BF16)
            else:
                o1[rows, :] = yb
            for perm_ref, o in ((p4_ref, o4), (p16_ref, o16)):
                dil = o.shape[0]
                n = sub // dil
                yp = jnp.dot(perm_ref[...], yb, preferred_element_type=F32).astype(BF16)
                for r in range(dil):
                    o[r, t * n:(t + 1) * n, :] = yp[r * n:(r + 1) * n, :]

    @pl.when(j == 0)
    def _():
        section(*outs[0:3], gain_ref=qg_ref, scale=ATT_HEAD_DIM ** -0.5 * math.log2(math.e), grouped=True)

    @pl.when(j == 1)
    def _():
        section(*outs[3:6], gain_ref=kg_ref)

    @pl.when(j == 2)
    def _():
        section(*outs[6:9])


def _qkv(x3, g, w_qkv, qg, kg, cos, sin):
    b, l, d = x3.shape
    tm = QKV_TM
    out_specs, out_shape = [], []
    for _ in range(3):
        out_specs.append(pl.BlockSpec((None, tm, d), lambda bi_, i, j: (bi_, i, 0)))
        out_shape.append(jax.ShapeDtypeStruct((b, l, d), BF16))
        for dil in DILATIONS[1:]:
            out_specs.append(pl.BlockSpec((None, dil, tm // dil, d), lambda bi_, i, j: (bi_, 0, i, 0)))
            out_shape.append(jax.ShapeDtypeStruct((b, dil, l // dil, d), BF16))
    consts = _qkv_constants(tm)
    tbl = lambda: pl.BlockSpec((tm, LANES), lambda bi_, i, j: (i, 0))
    return pl.pallas_call(
        _qkv_body,
        grid=(b, l // tm, 3),
        in_specs=[
            pl.BlockSpec((None, tm, d), lambda bi_, i, j: (bi_, i, 0)),
            _resident((1, d)),
            pl.BlockSpec((d, d), lambda bi_, i, j: (0, j)),
            _resident((1, d)), _resident((1, d)), tbl(), tbl(),
        ] + [_resident(c.shape) for c in consts],
        out_specs=out_specs,
        out_shape=out_shape,
        scratch_shapes=[pltpu.VMEM((tm, d), BF16)],
        compiler_params=_params("parallel", "parallel", "arbitrary"),
        name="qkv",
    )(x3, g, w_qkv, qg, kg, cos, sin, *consts)


def _qkv_constants(tm):
    lane = np.arange(GATE_SLAB)
    gsum = (lane[:, None] // ATT_HEAD_DIM == lane[None, :] // ATT_HEAD_DIM)
    half = ROPE_DIM // 2
    pos = lane % ATT_HEAD_DIM
    partner = np.where(pos < half, lane + half, np.where(pos < ROPE_DIM, lane - half, -1))
    rot = lane[:, None] == partner[None, :]
    out_row = np.arange(PERM_ROWS)
    group = SPAN // ACC_DIL
    within = out_row % SPAN
    src = out_row - within + (within % group) * ACC_DIL + within // group
    perms = [src[:, None] == np.arange(PERM_ROWS)[None, :]]
    for dil in DILATIONS[1:]:
        sub = PERM_ROWS
        rows = sub // dil
        assert rows % BF16_ROWS == 0 and tm % sub == 0
        out_row = np.arange(sub)
        src = (out_row % rows) * dil + out_row // rows
        perms.append(src[:, None] == np.arange(sub)[None, :])
    return tuple(jnp.asarray(m, dtype=BF16) for m in (gsum, rot, *perms))


def _rope_tables(l):
    half = ROPE_DIM // 2
    pos = jnp.arange(l, dtype=F32)
    inv = ROPE_THETA ** (-2.0 * jnp.arange(half, dtype=F32) / ROPE_DIM)
    ang = pos[:, None] * inv[None, :]
    cos, sin = jnp.cos(ang), jnp.sin(ang)
    pad = ATT_HEAD_DIM - ROPE_DIM
    cos_h = jnp.concatenate([cos, cos, jnp.ones((l, pad), F32)], axis=1)
    sin_h = jnp.concatenate([-sin, sin, jnp.zeros((l, pad), F32)], axis=1)
    rep = LANES // ATT_HEAD_DIM
    return tuple(jnp.concatenate([tb] * rep, axis=1) for tb in (cos_h, sin_h))


def _attn_body(q1, k1c, k1p, v1c, v1p, q4, k4c, k4p, v4c, v4p, q16, k16c, k16p, v16c, v16p,
               o_ref, kb1, vb1, kb4, vb4, kb16, vb16, s_scr, p_scr, mx_scr, bias_scr, tok_scr, acc_ref, m_ref, l_ref):
    first = pl.program_id(2) == 0
    n_pairs = acc_ref.shape[0]
    n_blocks = SUPER // SPAN
    trans_b = (((1,), (1,)), ((), ()))

    for buf, prev, cur in ((kb1, k1p, k1c), (vb1, v1p, v1c)):
        buf[0, 0:SPAN, :] = prev[...]
        buf[0, SPAN:, :] = cur[...]
    for buf, prev, cur in ((kb4, k4p, k4c), (vb4, v4p, v4c), (kb16, k16p, k16c), (vb16, v16p, v16c)):
        buf[:, 0:SPAN, :] = prev[...]
        buf[:, SPAN:, :] = cur[...]

    row = lax.broadcasted_iota(jnp.int32, (SPAN, 2 * SPAN), 0)
    kj = lax.broadcasted_iota(jnp.int32, (SPAN, 2 * SPAN), 1)
    group = SPAN // ACC_DIL
    for base, qi in ((0, row), (2, (row % group) * ACC_DIL + row // group)):
        band = (kj >= qi) & (kj <= qi + SPAN)
        bias_scr[base] = jnp.where(band, 0.0, -jnp.inf)
        bias_scr[base + 1] = jnp.where(band & (kj >= SPAN), 0.0, -jnp.inf)

    lo =lax.broadcasted_iota(jnp.int32, (SPAN, LANES), 1) < ATT_HEAD_DIM
    ones_v = jnp.ones((2 * SPAN, LANES), BF16)
    lanes = lambda p: slice(p * LANES, (p + 1) * LANES)

    def run_pattern(dil, q_get, kb, vb):
        per_res = SUPER // (SPAN * dil)
        overwrite = dil == DILATIONS[0]

        def split(f):
            if per_res == 1:
                return f, 0
            if dil == 1:
                return 0, f
            if isinstance(f, int):
                return f // per_res, f % per_res
            return lax.shift_right_logical(f, per_res.bit_length() - 1), f & (per_res - 1)

        def slab_rows(nb):
            return pl.ds(nb * SPAN if isinstance(nb, int) else pl.multiple_of(nb * SPAN, SPAN), 2 * SPAN)

        def stage_scores(f, slot):
            r, nb = split(f)
            if isinstance(nb, int):
                flag = jnp.where(first, 1, 0) if nb == 0 else 0
            else:
                flag = jnp.where(first & (nb == 0), 1, 0)
            bias = bias_scr[flag + (2 if overwrite else 0)]
            bias2 = jnp.concatenate([bias, bias], axis=0)
            q = q_get(r, nb)
            ks = kb[r, slab_rows(nb), :]
            for p in range(n_pairs):
                qp = q[:, lanes(p)]
                zero = jnp.zeros_like(qp)
                q2 = jnp.concatenate([jnp.where(lo, qp, zero), jnp.where(lo, zero, qp)], axis=0)
                s_scr[slot, p] = lax.dot_general(q2, ks[:, lanes(p)], trans_b, preferred_element_type=F32) + bias2

        def stage_softmax(slot):
            for p in range(n_pairs):
                s = s_scr[slot, p]
                mx = jnp.max(s, axis=1, keepdims=True)
                p_scr[slot, p] = jnp.exp2(s - mx).astype(BF16)
                mx_scr[slot, p] = jnp.where(lo, mx[:SPAN], mx[SPAN:])

        def stage_values(f, slot):
            r, nb = split(f)
            vs = vb[r, slab_rows(nb), :]
            if overwrite:
                grp = SPAN // ACC_DIL
                start = nb * grp if isinstance(nb, int) else pl.multiple_of(nb * grp, grp)
                pieces = [(slice(c * grp, (c + 1) * grp), c, pl.ds(start, grp)) for c in range(ACC_DIL)]
            elif dil == ACC_DIL:
                pieces = [(slice(0, SPAN), r, q_rows(nb))]
            else:
                sub = dil // ACC_DIL
                pieces = [(slice(0, SPAN), r & (ACC_DIL - 1),
                           pl.ds(lax.shift_right_logical(r, 2) if not isinstance(r, int) else r // ACC_DIL,
                                 SPAN, stride=sub))]
            for p in range(n_pairs):
                v_aug = jnp.concatenate([vs[:, lanes(p)], ones_v], axis=1)
                pv = jnp.dot(p_scr[slot, p], v_aug, preferred_element_type=F32)
                o_blk = jnp.where(lo, pv[:SPAN, :LANES], pv[SPAN:, :LANES])
                l_blk = jnp.where(lo, pv[:SPAN, LANES:], pv[SPAN:, LANES:])
                m_blk = mx_scr[slot, p]
                for src, cls, rows in pieces:
                    m_new, l_new, o_new = m_blk[src], l_blk[src], o_blk[src]
                    if overwrite:
                        m_ref[p, cls, rows, :], l_ref[p, cls, rows, :], acc_ref[p, cls, rows, :] = m_new, l_new, o_new
                        continue
                    m_old = m_ref[p, cls, rows, :]
                    m = jnp.maximum(m_old, m_new)
                    e_old = jnp.exp2(m_old - m)
                    e_new = jnp.exp2(m_new - m)
                    l_ref[p, cls, rows, :] = e_old * l_ref[p, cls, rows, :] + e_new * l_new
                    acc_ref[p, cls, rows, :] = e_old * acc_ref[p, cls, rows, :] + e_new * o_new
                    m_ref[p, cls, rows, :] = m

        stage_scores(0, 0)
        stage_scores(1, 1)
        stage_softmax(0)

        def body(i, c):
            f = 2 * i + 2
            stage_scores(f, 0)
            stage_softmax(1)
            stage_values(f - 2, 0)
            stage_scores(f + 1, 1)
            stage_softmax(0)
            stage_values(f - 1, 1)
            return c

        lax.fori_loop(0, (n_blocks - 2) // 2, body, 0)
        stage_softmax(1)
        stage_values(n_blocks - 2, 0)
        stage_values(n_blocks - 1, 1)

    def q_rows(nb):
        return pl.ds(nb * SPAN if isinstance(nb, int) else pl.multiple_of(nb * SPAN, SPAN), SPAN)

    run_pattern(DILATIONS[0], lambda r, nb: q1[q_rows(nb), :], kb1, vb1)
    run_pattern(DILATIONS[1], lambda r, nb: q4[r, q_rows(nb), :], kb4, vb4)
    run_pattern(DILATIONS[2], lambda r, nb: q16[r], kb16, vb16)

    for p in range(n_pairs):
        for cls in range(ACC_DIL):
            tok_scr[p, pl.ds(cls, SUPER // ACC_DIL, stride=ACC_DIL), :] = acc_ref[p, cls] / l_ref[p, cls]
        o_ref[:, lanes(p)] = tok_scr[p].astype(BF16)


def _attention(qkv_outs):
    q1, q4, q16, k1, k4, k16, v1, v4, v16 = qkv_outs
    b, l, d = q1.shape
    w = ATT_LANES
    n_pairs = w // LANES
    d4, d16 = DILATIONS[1], DILATIONS[2]
    cur1 = pl.BlockSpec((None, SUPER, w), lambda bi_, g, s: (bi_, s, g))
    prev1 = pl.BlockSpec((None, SPAN, w), lambda bi_, g, s: (bi_, jnp.maximum(s * (SUPER // SPAN) - 1, 0), g))
    cur4 = pl.BlockSpec((None, d4, SUPER // d4, w), lambda bi_, g, s: (bi_, 0, s, g))
    prev4 = pl.BlockSpec((None, d4, SPAN, w),
                         lambda bi_, g, s: (bi_, 0, jnp.maximum(s * (SUPER // (d4 * SPAN)) - 1, 0), g))
    cur16 = pl.BlockSpec((None, d16, SPAN, w), lambda bi_, g, s: (bi_, 0, s, g))
    prev16 = pl.BlockSpec((None, d16, SPAN, w), lambda bi_, g, s: (bi_, 0, jnp.maximum(s - 1, 0), g))
    return pl.pallas_call(
        _attn_body,
        grid=(b, d // w, l // SUPER),
        in_specs=[cur1, cur1, prev1, cur1, prev1,
                  cur4, cur4, prev4, cur4, prev4,
                  cur16, cur16, prev16, cur16, prev16],
        out_specs=pl.BlockSpec((None, SUPER, w), lambda bi_, g, s: (bi_, s, g)),
        out_shape=jax.ShapeDtypeStruct((b, l, d), BF16),
        scratch_shapes=[
            pltpu.VMEM((1, SPAN + SUPER, w), BF16), pltpu.VMEM((1, SPAN + SUPER, w), BF16),
            pltpu.VMEM((d4, SPAN + SUPER // d4, w), BF16), pltpu.VMEM((d4, SPAN + SUPER // d4, w), BF16),
            pltpu.VMEM((d16, 2 * SPAN, w), BF16), pltpu.VMEM((d16, 2 * SPAN, w), BF16),
            pltpu.VMEM((2, n_pairs, 2 * SPAN, 2 * SPAN), F32),
            pltpu.VMEM((2, n_pairs, 2 * SPAN, 2 * SPAN), BF16),
            pltpu.VMEM((2, n_pairs, SPAN, LANES), F32),
            pltpu.VMEM((4, SPAN, 2 * SPAN), F32),
            pltpu.VMEM((n_pairs, SUPER, LANES), F32),
        ] + [pltpu.VMEM((n_pairs, ACC_DIL, SUPER // ACC_DIL, LANES), F32)] * 3,
        compiler_params=_params("parallel", "parallel", "arbitrary"),
        name="dilated_attention",
    )(q1, k1, k1, v1, v1, q4, k4, k4, v4, v4, q16, k16, k16, v16, v16)


def _recurrent_layer(x, b, l, norm_g, w_in, lru_conv_w, lru_conv_b, lru_w_r, lru_b_r, lru_w_i, lru_b_i,
                     lru_lambda, ssd_conv_w, ssd_conv_b, ssd_dt_bias, ssd_a_log, ssd_d, ssd_norm, w_out):
    d = x.shape[1]
    width = lru_lambda.shape[0]
    n_heads = ssd_a_log.shape[0]
    n_main = w_in.shape[1] - n_heads
    row = lambda v: v.reshape(1, -1)
    pad_lanes = lambda v: jnp.pad(v.reshape(1, -1), ((0, 0), (0, LANES - v.shape[0])))
    w_lru = w_in[:, :2 * width].astype(BF16)
    w_ssd = jnp.pad(w_in[:, 2 * width:], ((0, 0), (0, LANES - n_heads))).astype(BF16)
    x3 = x.reshape(b, l, d)
    out_a = _lru(x3, row(norm_g), w_lru, lru_conv_w, row(lru_conv_b), _block_diag_slabs(lru_w_r),
                 _block_diag_slabs(lru_w_i), row(lru_b_r), row(lru_b_i), row(lru_lambda))
    out_b = _ssd(x3, row(norm_g), w_ssd, ssd_conv_w, row(ssd_conv_b), pad_lanes(ssd_dt_bias),
                 pad_lanes(ssd_a_log), row(jnp.repeat(ssd_d, SSD_HEAD_DIM)), row(ssd_norm), n_heads)
    w_out = w_out.astype(BF16)
    return [out_a.reshape(b * l, width), out_b.reshape(b * l, -1)], [w_out[:width], w_out[width:]]


def _attention_layer(x, b, l, norm_g, w_qkv, q_norm, k_norm, w_out):
    d = x.shape[1]
    heads = d // ATT_HEAD_DIM
    row = lambda v: v.reshape(1, -1)
    outs = _qkv(x.reshape(b, l, d), row(norm_g), w_qkv.astype(BF16),
                row(jnp.tile(q_norm, heads)), row(jnp.tile(k_norm, heads)), *_rope_tables(l))
    o = _attention(outs)
    return [o.reshape(b * l, d)], [w_out.astype(BF16)]


def kernel(x, rec_norm, rec_w_in, lru_conv_w, lru_conv_b, lru_w_r, lru_b_r, lru_w_i, lru_b_i, lru_lambda,
           ssd_conv_w, ssd_conv_b, ssd_dt_bias, ssd_a_log, ssd_d, ssd_norm, rec_w_out, att_norm, att_w_qkv,
           att_q_norm, att_k_norm, att_w_out, ffn_norm, ffn_w_gate_up, ffn_w_down):
    b, l, d = x.shape
    depth = ffn_norm.shape[0]
    h = x.reshape(b * l, d)
    for layer in range(depth):
        i = layer // 2
        if layer % 2 == 0:
            mixes, wos = _recurrent_layer(h, b, l, rec_norm[i], rec_w_in[i], lru_conv_w[i], lru_conv_b[i],
                                          lru_w_r[i], lru_b_r[i], lru_w_i[i], lru_b_i[i], lru_lambda[i],
                                          ssd_conv_w[i], ssd_conv_b[i], ssd_dt_bias[i], ssd_a_log[i], ssd_d[i],
                                          ssd_norm[i], rec_w_out[i])
        else:
            mixes, wos = _attention_layer(h, b, l, att_norm[i], att_w_qkv[i], att_q_norm[i], att_k_norm[i],
                                          att_w_out[i])
        wgu, wd = _prep_ffn_weights(ffn_w_gate_up[layer], ffn_w_down[layer])
        h = _mix_ffn(h, mixes, wos, ffn_norm[layer].reshape(1, d), wgu, wd)
    return h.reshape(b, l, d)
```

```python
import functools
import math

import jax
import jax.numpy as jnp
import numpy as np
from jax import lax
from jax.experimental import pallas as pl
from jax.experimental.pallas import tpu as pltpu

F32 = jnp.float32
BF16 = jnp.bfloat16

NORM_EPS = 1e-6
CONV_WIDTH = 4
LRU_BLOCK = 64
LRU_C = 8.0
GATE_SLAB = 256
SSD_HEAD_DIM = 64
SSD_GROUPS = 2
SSD_STATE = 128
SSD_CHUNK = 128
ATT_HEAD_DIM = 64
ROPE_DIM = 16
ROPE_THETA = 500000.0
SPAN = 128
DILATIONS = (1, 4, 16)
SUPER = SPAN * DILATIONS[-1]
ATT_LANES = 256
ACC_DIL = DILATIONS[1]
LANES = 128
SUBLANES = 8
BF16_ROWS = 16
VMEM_LIMIT_BYTES = 56 * 1024 * 1024


def _params(*semantics):
    return pltpu.CompilerParams(dimension_semantics=semantics, vmem_limit_bytes=VMEM_LIMIT_BYTES)


def _rmsnorm(x, g):
    ms = jnp.mean(x * x, axis=-1, keepdims=True)
    return x * lax.rsqrt(ms + NORM_EPS) * g


def _sigmoid(x):
    return 1.0 / (1.0 + jnp.exp(-x))


def _silu(x):
    return x * _sigmoid(x)


def _softplus(x):
    return jnp.maximum(x, 0.0) + jnp.log1p(jnp.exp(-jnp.abs(x)))


def _gelu_tanh(x):
    c = math.sqrt(2.0 / math.pi)
    return x * (0.5 * (1.0 + jnp.tanh(c * (x + 0.044715 * (x * x * x)))))


FFN_TM = 512
FFN_TH = 256


def _ffn_body(n_mix, x_ref, *refs):
    mix_refs, wo_refs = refs[:n_mix], refs[n_mix:2 * n_mix]
    g_ref, wgu_ref, wd_ref, o_ref, h_ref, act_ref = refs[2 * n_mix:]
    x1 = x_ref[...]
    for mix_ref, wo_ref in zip(mix_refs, wo_refs):
        x1 = x1 + jnp.dot(mix_ref[...], wo_ref[...], preferred_element_type=F32)
    o_ref[...] = x1
    h_ref[...] = _rmsnorm(x1, g_ref[...]).astype(BF16)
    hid = act_ref.shape[1]
    for c in range(hid // FFN_TH):
        cols = slice(c * FFN_TH, (c + 1) * FFN_TH)
        up_cols = slice(hid + c * FFN_TH, hid + (c + 1) * FFN_TH)
        gate = jnp.dot(h_ref[...], wgu_ref[:, cols], preferred_element_type=F32)
        up = jnp.dot(h_ref[...], wgu_ref[:, up_cols], preferred_element_type=F32)
        act_ref[:, cols] = (_silu(gate) * up).astype(BF16)
    o_ref[...] += jnp.dot(act_ref[...], wd_ref[...], preferred_element_type=F32)


def _resident(shape):
    return pl.BlockSpec(shape, lambda *_: (0,) * len(shape), pipeline_mode=pl.Buffered(1))


def _layer_block(stack, layer, block=None, index=None):
    block = tuple(stack.shape[1:]) if block is None else tuple(block)
    index = (layer,) + (tuple(index) if index is not None else (0,) * len(block))
    return stack, pl.BlockSpec((None,) + block, lambda *_: index, pipeline_mode=pl.Buffered(1))


def _mix_ffn(x, mixes, wos, g, wgu, wd):
    t, d = x.shape
    hid = wd[0].shape[1]
    weights = (*wos, (g, _resident(g.shape)), wgu, wd)
    rows = lambda a: pl.BlockSpec((FFN_TM, a.shape[1]), lambda i: (i, 0))
    return pl.pallas_call(
        functools.partial(_ffn_body, len(mixes)),
        grid=(t // FFN_TM,),
        in_specs=[rows(x)] + [rows(a) for a in mixes] + [spec for _, spec in weights],
        out_specs=rows(x),
        out_shape=jax.ShapeDtypeStruct((t, d), F32),
        scratch_shapes=[pltpu.VMEM((FFN_TM, d), BF16), pltpu.VMEM((FFN_TM, hid), BF16)],
        compiler_params=_params("parallel"),
        name="mix_ffn",
    )(x, *mixes, *[a for a, _ in weights])


def _segment_perm(rows):
    steps = rows // SUBLANES
    rho = np.arange(rows)
    time = (rho % SUBLANES) * steps + rho // SUBLANES
    perm = time[:, None] == np.arange(rows)[None, :]
    return jnp.asarray(perm, dtype=BF16), jnp.asarray(perm.T, dtype=BF16)


def _conv_segments(x, halo_ref, w_ref, b_ref):
    rows, width = x.shape
    sub0 = lax.broadcasted_iota(jnp.int32, (SUBLANES, width), 0) == 0
    halos = []
    for k in range(1, CONV_WIDTH):
        tail = pltpu.roll(x[rows - k * SUBLANES:rows - (k - 1) * SUBLANES, :], 1, 0)
        halos.append(jnp.where(sub0, halo_ref[k - 1], tail))
        halo_ref[k - 1] = tail
    y = b_ref[...] + w_ref[CONV_WIDTH - 1:CONV_WIDTH, :] * x
    for k in range(1, CONV_WIDTH):
        back = jnp.concatenate(halos[k - 1::-1] + [x[:rows - k * SUBLANES, :]], axis=0)
        y = y + w_ref[CONV_WIDTH - 1 - k:CONV_WIDTH - k, :] * back
    return y


LRU_TL = 512
LRU_SUB = 256


def _lru_body(x_ref, g_ref, w_ref, cw_ref, cb_ref, wr_ref, wi_ref, br_ref, bi_ref, lam_ref, perm_ref, unperm_ref,
              o_ref, halo_ref, carry_ref):
    first = pl.program_id(1) == 0
    w = o_ref.shape[1]
    sub = perm_ref.shape[0]
    steps = sub // SUBLANES

    @pl.when(first)
    def _():
        carry_ref[...] = jnp.zeros_like(carry_ref)
        halo_ref[...] = jnp.zeros_like(halo_ref)

    for t in range(o_ref.shape[0] // sub):
        rows = slice(t * sub, (t + 1) * sub)
        hn = _rmsnorm(x_ref[rows, :], g_ref[...]).astype(BF16)
        hn = jnp.dot(perm_ref[...], hn, preferred_element_type=F32).astype(BF16)
        proj = jnp.dot(hn, w_ref[...], preferred_element_type=F32)
        gate = proj[:, w:]

        xc = _conv_segments(proj[:, :w], halo_ref, cw_ref, cb_ref)
        xcb = xc.astype(BF16)
        pre_r, pre_i = [], []
        for s in range(w // GATE_SLAB):
            slab = xcb[:, s * GATE_SLAB:(s + 1) * GATE_SLAB]
            pre_r.append(jnp.dot(slab, wr_ref[s], preferred_element_type=F32))
            pre_i.append(jnp.dot(slab, wi_ref[s], preferred_element_type=F32))
        r = _sigmoid(jnp.concatenate(pre_r, axis=1) + br_ref[...])
        i = _sigmoid(jnp.concatenate(pre_i, axis=1) + bi_ref[...])
        log_a = (-LRU_C * r) * _softplus(-lam_ref[...])
        a = jnp.exp(log_a)
        z = 1.0 - a * a
        u = jnp.where(z > 0.0, z * lax.rsqrt(z), 0.0) * (i * xc)

        h_j = jnp.zeros((SUBLANES, w), F32)
        p_j = jnp.ones((SUBLANES, w), F32)
        hs, ps = [], []
        for j in range(steps):
            a_j = a[j * SUBLANES:(j + 1) * SUBLANES, :]
            h_j = a_j * h_j + u[j * SUBLANES:(j + 1) * SUBLANES, :]
            p_j = a_j * p_j
            hs.append(h_j)
            ps.append(p_j)
        c = carry_ref[0:1, :]
        entering = []
        for seg in range(SUBLANES):
            entering.append(c)
            c = p_j[seg:seg + 1, :] * c + h_j[seg:seg + 1, :]
        carry_ref[0:1, :] = c
        enter = jnp.concatenate(entering, axis=0)
        h = jnp.concatenate([hs[j] + ps[j] * enter for j in range(steps)], axis=0)
        out = (h * _gelu_tanh(gate)).astype(BF16)
        o_ref[rows, :] = jnp.dot(unperm_ref[...], out, preferred_element_type=F32).astype(BF16)


def _lru(x3, g, w_lru, cw, cb, wr_bd, wi_bd, br, bi, lam):
    b, l, d = x3.shape
    width = lam.shape[1]
    perms = _segment_perm(LRU_SUB)
    return pl.pallas_call(
        _lru_body,
        grid=(b, l // LRU_TL),
        in_specs=[pl.BlockSpec((None, LRU_TL, d), lambda bi_, li: (bi_, li, 0))]
        + [_resident(g.shape), w_lru[1]]
        + [_resident(a.shape) for a in (cw, cb, wr_bd, wi_bd, br, bi, lam, *perms)],
        out_specs=pl.BlockSpec((None, LRU_TL, width), lambda bi_, li: (bi_, li, 0)),
        out_shape=jax.ShapeDtypeStruct((b, l, width), BF16),
        scratch_shapes=[
            pltpu.VMEM((CONV_WIDTH - 1, SUBLANES, width), F32),
            pltpu.VMEM((SUBLANES, width), F32),
        ],
        compiler_params=_params("parallel", "arbitrary"),
        name="lru",
    )(x3, g, w_lru[0], cw, cb, wr_bd, wi_bd, br, bi, lam, *perms)


def _block_diag_slabs(w):
    nb, bs, _ = w.shape
    per = GATE_SLAB // bs
    w = w.reshape(nb // per, per, bs, bs)
    eye = jnp.eye(per, dtype=w.dtype)
    bd = jnp.einsum("spij,pq->spiqj", w, eye).reshape(nb // per, GATE_SLAB, GATE_SLAB)
    return bd.astype(BF16)


def _ssd_body(n_heads, x_ref, g_ref, w_ref, cw_ref, cb_ref, dtb_ref, alog_ref, dvec_ref, nrm_ref,
              exp_ref, perm_ref, unperm_ref, o_ref, proj_ref, halo_ref, state_ref):
    first = pl.program_id(1) == 0
    t = SSD_CHUNK
    width = n_heads * SSD_HEAD_DIM
    gw = SSD_STATE
    conv_ch = width + 2 * SSD_GROUPS * gw
    heads_per_group = n_heads // SSD_GROUPS
    n_chunks = x_ref.shape[0] // t

    @pl.when(first)
    def _():
        state_ref[...] = jnp.zeros_like(state_ref)
        halo_ref[...] = jnp.zeros_like(halo_ref)

    hn = _rmsnorm(x_ref[...], g_ref[...]).astype(BF16)
    hn = jnp.concatenate([jnp.dot(perm_ref[...], hn[c * t:(c + 1) * t, :], preferred_element_type=F32).astype(BF16)
                          for c in range(n_chunks)], axis=0)
    proj_ref[...] = jnp.dot(hn, w_ref[...], preferred_element_type=F32)

    steps = t // SUBLANES
    time_of = lambda i: (i & (SUBLANES - 1)) * steps + lax.shift_right_logical(i, 3)
    causal = (time_of(lax.broadcasted_iota(jnp.int32, (t, t), 0))
              >= time_of(lax.broadcasted_iota(jnp.int32, (t, t), 1)))
    tri = jnp.where(causal, 1.0, 0.0).astype(BF16)
    lane = lax.broadcasted_iota(jnp.int32, (t, LANES), 1)
    lo = lane < SSD_HEAD_DIM
    rlo = lax.broadcasted_iota(jnp.int32, (LANES, gw), 0) < SSD_HEAD_DIM

    dt = _softplus(proj_ref[:, width + conv_ch:] + dtb_ref[...])
    adt = dt * (-jnp.exp(alog_ref[...]))
    cs = jnp.concatenate([_split_dot(adt[c * t:(c + 1) * t, :], tri, parts=3, left=True)
                          for c in range(n_chunks)], axis=0)
    dt_w = _split_dot(dt, exp_ref[...], parts=2)
    cs_w = _split_dot(cs, exp_ref[...], parts=3)

    for c in range(n_chunks):
        rows = slice(c * t, (c + 1) * t)
        out = _ssd_chunk(proj_ref[rows, 0:width], proj_ref[rows, width:width + conv_ch],
                         cs[rows, :], dt_w[rows, :], cs_w[rows, :], n_heads, causal, lo, rlo, halo_ref, state_ref,
                         cw_ref, cb_ref, dvec_ref, nrm_ref)
        o_ref[rows, :] = jnp.dot(unperm_ref[...], out, preferred_element_type=F32).astype(BF16)


def _ssd_chunk(z, xbc_raw, cs, dt_w, cs_w, n_heads, causal, lo, rlo, halo_ref, state_ref,
               cw_ref, cb_ref, dvec_ref, nrm_ref):
    t = SSD_CHUNK
    width = n_heads * SSD_HEAD_DIM
    gw = SSD_STATE
    heads_per_group = n_heads // SSD_GROUPS
    xbc = _silu(_conv_segments(xbc_raw, halo_ref, cw_ref, cb_ref))
    xs = xbc[:, :width]
    cs_t = cs.T
    cs_last = cs[t - 1:t, :]
    cdec = jnp.exp(cs_last)
    ecs_w = jnp.exp(cs_w)
    dte_w = jnp.exp(cs_w[t - 1:t, :] - cs_w)

    cbs = []
    for g in range(SSD_GROUPS):
        bm = xbc[:, width + g * gw: width + (g + 1) * gw].astype(BF16)
        cm = xbc[:, width + (SSD_GROUPS + g) * gw: width + (SSD_GROUPS + g + 1) * gw].astype(BF16)
        cb = lax.dot_general(cm, bm, (((1,), (1,)), ((), ())), preferred_element_type=F32)
        cbs.append((bm, cm, cb))

    ys = []
    for p in range(n_heads // 2):
        h0, h1 = 2 * p, 2 * p + 1
        bm, cm, cb = cbs[h0 // heads_per_group]
        sl = slice(p * LANES, (p + 1) * LANES)
        xs_p = xs[:, sl]
        xdt = xs_p * dt_w[:, sl]
        ms = []
        for h in (h0, h1):
            seg = cs[:, h:h + 1] - cs_t[h:h + 1, :]
            ms.append((cb * jnp.where(causal, jnp.exp(seg), 0.0)).astype(BF16))
        m_cat = jnp.concatenate(ms, axis=1)
        xdt_bd = jnp.concatenate([jnp.where(lo, xdt, 0.0), jnp.where(lo, 0.0, xdt)], axis=0).astype(BF16)
        y = jnp.dot(m_cat, xdt_bd, preferred_element_type=F32)
        prev = state_ref[sl, :]
        y_off = lax.dot_general(cm, prev.astype(BF16), (((1,), (1,)), ((), ())), preferred_element_type=F32)
        y = y + y_off * ecs_w[:, sl]
        xw = (xdt * dte_w[:, sl]).astype(BF16)
        st = lax.dot_general(xw, bm, (((0,), (0,)), ((), ())), preferred_element_type=F32)
        dec = jnp.where(rlo, cdec[:, h0:h0 + 1], cdec[:, h1:h1 + 1])
        state_ref[sl, :] = prev * dec + st
        ys.append(y + dvec_ref[:, sl] * xs_p)

    y = jnp.concatenate(ys, axis=1) * _silu(z)
    gsz = width // SSD_GROUPS
    outs = []
    for g in range(SSD_GROUPS):
        yg = y[:, g * gsz:(g + 1) * gsz]
        outs.append(yg * lax.rsqrt(jnp.mean(yg * yg, axis=-1, keepdims=True) + NORM_EPS))
    return (jnp.concatenate(outs, axis=1) * nrm_ref[...]).astype(BF16)


SSD_TS = 512


def _ssd(x3, g, w_ssd, cw, cb, dt_bias, a_log, d_vec, nrm, n_heads):
    b, l, d = x3.shape
    width = n_heads * SSD_HEAD_DIM
    conv_ch = width + 2 * SSD_GROUPS * SSD_STATE
    expand = jnp.asarray(np.arange(LANES)[:, None] == np.arange(width)[None, :] // SSD_HEAD_DIM,
                         dtype=BF16)
    consts = (cw, cb, dt_bias, a_log, d_vec, nrm, expand, *_segment_perm(SSD_CHUNK))
    return pl.pallas_call(
        functools.partial(_ssd_body, n_heads),
        grid=(b, l // SSD_TS),
        in_specs=[pl.BlockSpec((None, SSD_TS, d), lambda bi_, ci: (bi_, ci, 0))]
        + [_resident(g.shape), w_ssd[1]] + [_resident(a.shape) for a in consts],
        out_specs=pl.BlockSpec((None, SSD_TS, width), lambda bi_, ci: (bi_, ci, 0)),
        out_shape=jax.ShapeDtypeStruct((b, l, width), BF16),
        scratch_shapes=[
            pltpu.VMEM((SSD_TS, w_ssd[0].shape[-1]), F32),
            pltpu.VMEM((CONV_WIDTH - 1, SUBLANES, conv_ch), F32),
            pltpu.VMEM((width, SSD_STATE), F32),
        ],
        compiler_params=_params("parallel", "arbitrary"),
        name="ssd",
    )(x3, g, w_ssd[0], *consts)


QKV_TM = 512
PERM_ROWS = 256


def _split_dot(x, m, parts=2, left=False):
    acc, rem = None, x
    for k in range(parts):
        piece = rem.astype(BF16)
        term = jnp.dot(m, piece, preferred_element_type=F32) if left else jnp.dot(piece, m, preferred_element_type=F32)
        acc = term if acc is None else acc + term
        if k + 1 < parts:
            rem = rem - piece.astype(F32)
    return acc


def _slab_map(fn, x):
    return jnp.concatenate([fn(x[:, c * GATE_SLAB:(c + 1) * GATE_SLAB]) for c in range(x.shape[1] // GATE_SLAB)],
                           axis=1)


def _head_norm_rope(y, gain, cos, sin, gsum, rot):
    ss = _slab_map(lambda v: _split_dot(v, gsum), y * y)
    yn = y * lax.rsqrt(ss * (1.0 / ATT_HEAD_DIM) + NORM_EPS) * gain
    partner = _slab_map(lambda v: _split_dot(v, rot), yn)
    tile = lambda tbl: jnp.concatenate([tbl] * (y.shape[1] // LANES), axis=1)
    return yn * tile(cos) + partner * tile(sin)


def _qkv_body(x_ref, g_ref, w_ref, qg_ref, kg_ref, cos_ref, sin_ref, gsum_ref, rot_ref, p1_ref, p4_ref, p16_ref,
              *refs):
    outs, h_ref = refs[:9], refs[9]
    j = pl.program_id(2)

    @pl.when(j == 0)
    def _():
        h_ref[...] = _rmsnorm(x_ref[...], g_ref[...]).astype(BF16)

    sub = PERM_ROWS

    def section(o1, o4, o16, gain_ref=None, scale=None, grouped=False):
        y = jnp.dot(h_ref[...], w_ref[...], preferred_element_type=F32)
        if gain_ref is not None:
            y = _head_norm_rope(y, gain_ref[...], cos_ref[...], sin_ref[...], gsum_ref[...], rot_ref[...])
        if scale is not None:
            y = y * scale
        y_all = y.astype(BF16)
        for t in range(x_ref.shape[0] // sub):
            rows = slice(t * sub, (t + 1) * sub)
            yb = y_all[rows, :]
            if grouped:
                o1[rows, :] = jnp.dot(p1_ref[...], yb, preferred_element_type=F32).astype(BF16)
            else:
                o1[rows, :] = yb
            for perm_ref, o in ((p4_ref, o4), (p16_ref, o16)):
                dil = o.shape[0]
                n = sub // dil
                yp = jnp.dot(perm_ref[...], yb, preferred_element_type=F32).astype(BF16)
                for r in range(dil):
                    o[r, t * n:(t + 1) * n, :] = yp[r * n:(r + 1) * n, :]

    @pl.when(j == 0)
    def _():
        section(*outs[0:3], gain_ref=qg_ref, scale=ATT_HEAD_DIM ** -0.5 * math.log2(math.e), grouped=True)

    @pl.when(j == 1)
    def _():
        section(*outs[3:6], gain_ref=kg_ref)

    @pl.when(j == 2)
    def _():
        section(*outs[6:9])


def _qkv(x3, g, w_qkv, layer, qg, kg, cos, sin):
    b, l, d = x3.shape
    tm = QKV_TM
    out_specs, out_shape = [], []
    for _ in range(3):
        out_specs.append(pl.BlockSpec((None, tm, d), lambda bi_, i, j: (bi_, i, 0)))
        out_shape.append(jax.ShapeDtypeStruct((b, l, d), BF16))
        for dil in DILATIONS[1:]:
            out_specs.append(pl.BlockSpec((None, dil, tm // dil, d), lambda bi_, i, j: (bi_, 0, i, 0)))
            out_shape.append(jax.ShapeDtypeStruct((b, dil, l // dil, d), BF16))
    consts = _qkv_constants(tm)
    tbl = lambda: pl.BlockSpec((tm, LANES), lambda bi_, i, j: (i, 0))
    return pl.pallas_call(
        _qkv_body,
        grid=(b, l // tm, 3),
        in_specs=[
            pl.BlockSpec((None, tm, d), lambda bi_, i, j: (bi_, i, 0)),
            _resident((1, d)),
            pl.BlockSpec((None, d, d), lambda bi_, i, j: (layer, 0, j)),
            _resident((1, d)), _resident((1, d)), tbl(), tbl(),
        ] + [_resident(c.shape) for c in consts],
        out_specs=out_specs,
        out_shape=out_shape,
        scratch_shapes=[pltpu.VMEM((tm, d), BF16)],
        compiler_params=_params("parallel", "parallel", "arbitrary"),
        name="qkv",
    )(x3, g, w_qkv, qg, kg, cos, sin, *consts)


def _qkv_constants(tm):
    lane = np.arange(GATE_SLAB)
    gsum = (lane[:, None] // ATT_HEAD_DIM == lane[None, :] // ATT_HEAD_DIM)
    half = ROPE_DIM // 2
    pos = lane % ATT_HEAD_DIM
    partner = np.where(pos < half, lane + half, np.where(pos < ROPE_DIM, lane - half, -1))
    rot = lane[:, None] == partner[None, :]
    out_row = np.arange(PERM_ROWS)
    group = SPAN // ACC_DIL
    within = out_row % SPAN
    src = out_row - within + (within % group) * ACC_DIL + within // group
    perms = [src[:, None] == np.arange(PERM_ROWS)[None, :]]
    for dil in DILATIONS[1:]:
        sub = PERM_ROWS
        rows = sub // dil
        assert rows % BF16_ROWS == 0 and tm % sub == 0
        out_row = np.arange(sub)
        src = (out_row % rows) * dil + out_row // rows
        perms.append(src[:, None] == np.arange(sub)[None, :])
    return tuple(jnp.asarray(m, dtype=BF16) for m in (gsum, rot, *perms))


def _rope_tables(l):
    half = ROPE_DIM // 2
    pos = jnp.arange(l, dtype=F32)
    inv = ROPE_THETA ** (-2.0 * jnp.arange(half, dtype=F32) / ROPE_DIM)
    ang = pos[:, None] * inv[None, :]
    cos, sin = jnp.cos(ang), jnp.sin(ang)
    pad = ATT_HEAD_DIM - ROPE_DIM
    cos_h = jnp.concatenate([cos, cos, jnp.ones((l, pad), F32)], axis=1)
    sin_h = jnp.concatenate([-sin, sin, jnp.zeros((l, pad), F32)], axis=1)
    rep = LANES // ATT_HEAD_DIM
    return tuple(jnp.concatenate([tb] * rep, axis=1) for tb in (cos_h, sin_h))


def _attn_body(q1, k1c, k1p, v1c, v1p, q4, k4c, k4p, v4c, v4p, q16, k16c, k16p, v16c, v16p,
               o_ref, kb1, vb1, kb4, vb4, kb16, vb16, s_scr, p_scr, mx_scr, bias_scr, tok_scr, acc_ref, m_ref, l_ref):
    first = pl.program_id(2) == 0
    n_pairs = acc_ref.shape[0]
    n_blocks = SUPER // SPAN
    trans_b = (((1,), (1,)), ((), ()))

    for buf, prev, cur in ((kb1, k1p, k1c), (vb1, v1p, v1c)):
        buf[0, 0:SPAN, :] = prev[...]
        buf[0, SPAN:, :] = cur[...]
    for buf, prev, cur in ((kb4, k4p, k4c), (vb4, v4p, v4c), (kb16, k16p, k16c), (vb16, v16p, v16c)):
        buf[:, 0:SPAN, :] = prev[...]
        buf[:, SPAN:, :] = cur[...]

    row = lax.broadcasted_iota(jnp.int32, (SPAN, 2 * SPAN), 0)
    kj = lax.broadcasted_iota(jnp.int32, (SPAN, 2 * SPAN), 1)
    group = SPAN // ACC_DIL
    for base, qi in ((0, row), (2, (row % group) * ACC_DIL + row // group)):
        band = (kj >= qi) & (kj <= qi + SPAN)
        bias_scr[base] = jnp.where(band, 0.0, -jnp.inf)
        bias_scr[base + 1] = jnp.where(band & (kj >= SPAN), 0.0, -jnp.inf)

    lo =lax.broadcasted_iota(jnp.int32, (SPAN, LANES), 1) < ATT_HEAD_DIM
    ones_v = jnp.ones((2 * SPAN, LANES), BF16)
    lanes = lambda p: slice(p * LANES, (p + 1) * LANES)

    def run_pattern(dil, q_get, kb, vb):
        per_res = SUPER // (SPAN * dil)
        overwrite = dil == DILATIONS[0]

        def split(f):
            if per_res == 1:
                return f, 0
            if dil == 1:
                return 0, f
            if isinstance(f, int):
                return f // per_res, f % per_res
            return lax.shift_right_logical(f, per_res.bit_length() - 1), f & (per_res - 1)

        def slab_rows(nb):
            return pl.ds(nb * SPAN if isinstance(nb, int) else pl.multiple_of(nb * SPAN, SPAN), 2 * SPAN)

        def stage_scores(f, slot):
            r, nb = split(f)
            if isinstance(nb, int):
                flag = jnp.where(first, 1, 0) if nb == 0 else 0
            else:
                flag = jnp.where(first & (nb == 0), 1, 0)
            bias = bias_scr[flag + (2 if overwrite else 0)]
            bias2 = jnp.concatenate([bias, bias], axis=0)
            q = q_get(r, nb)
            ks = kb[r, slab_rows(nb), :]
            for p in range(n_pairs):
                qp = q[:, lanes(p)]
                zero = jnp.zeros_like(qp)
                q2 = jnp.concatenate([jnp.where(lo, qp, zero), jnp.where(lo, zero, qp)], axis=0)
                s_scr[slot, p] = lax.dot_general(q2, ks[:, lanes(p)], trans_b, preferred_element_type=F32) + bias2

        def stage_softmax(slot):
            for p in range(n_pairs):
                s = s_scr[slot, p]
                mx = jnp.max(s, axis=1, keepdims=True)
                p_scr[slot, p] = jnp.exp2(s - mx).astype(BF16)
                mx_scr[slot, p] = jnp.where(lo, mx[:SPAN], mx[SPAN:])

        def stage_values(f, slot):
            r, nb = split(f)
            vs = vb[r, slab_rows(nb), :]
            if overwrite:
                grp = SPAN // ACC_DIL
                start = nb * grp if isinstance(nb, int) else pl.multiple_of(nb * grp, grp)
                pieces = [(slice(c * grp, (c + 1) * grp), c, pl.ds(start, grp)) for c in range(ACC_DIL)]
            elif dil == ACC_DIL:
                pieces = [(slice(0, SPAN), r, q_rows(nb))]
            else:
                sub = dil // ACC_DIL
                pieces = [(slice(0, SPAN), r & (ACC_DIL - 1),
                           pl.ds(lax.shift_right_logical(r, 2) if not isinstance(r, int) else r // ACC_DIL,
                                 SPAN, stride=sub))]
            for p in range(n_pairs):
                v_aug = jnp.concatenate([vs[:, lanes(p)], ones_v], axis=1)
                pv = jnp.dot(p_scr[slot, p], v_aug, preferred_element_type=F32)
                o_blk = jnp.where(lo, pv[:SPAN, :LANES], pv[SPAN:, :LANES])
                l_blk = jnp.where(lo, pv[:SPAN, LANES:], pv[SPAN:, LANES:])
                m_blk = mx_scr[slot, p]
                for src, cls, rows in pieces:
                    m_new, l_new, o_new = m_blk[src], l_blk[src], o_blk[src]
                    if overwrite:
                        m_ref[p, cls, rows, :], l_ref[p, cls, rows, :], acc_ref[p, cls, rows, :] = m_new, l_new, o_new
                        continue
                    m_old = m_ref[p, cls, rows, :]
                    m = jnp.maximum(m_old, m_new)
                    e_old = jnp.exp2(m_old - m)
                    e_new = jnp.exp2(m_new - m)
                    l_ref[p, cls, rows, :] = e_old * l_ref[p, cls, rows, :] + e_new * l_new
                    acc_ref[p, cls, rows, :] = e_old * acc_ref[p, cls, rows, :] + e_new * o_new
                    m_ref[p, cls, rows, :] = m

        stage_scores(0, 0)
        stage_scores(1, 1)
        stage_softmax(0)

        def body(i, c):
            f = 2 * i + 2
            stage_scores(f, 0)
            stage_softmax(1)
            stage_values(f - 2, 0)
            stage_scores(f + 1, 1)
            stage_softmax(0)
            stage_values(f - 1, 1)
            return c

        lax.fori_loop(0, (n_blocks - 2) // 2, body, 0)
        stage_softmax(1)
        stage_values(n_blocks - 2, 0)
        stage_values(n_blocks - 1, 1)

    def q_rows(nb):
        return pl.ds(nb * SPAN if isinstance(nb, int) else pl.multiple_of(nb * SPAN, SPAN), SPAN)

    run_pattern(DILATIONS[0], lambda r, nb: q1[q_rows(nb), :], kb1, vb1)
    run_pattern(DILATIONS[1], lambda r, nb: q4[r, q_rows(nb), :], kb4, vb4)
    run_pattern(DILATIONS[2], lambda r, nb: q16[r], kb16, vb16)

    for p in range(n_pairs):
        for cls in range(ACC_DIL):
            tok_scr[p, pl.ds(cls, SUPER // ACC_DIL, stride=ACC_DIL), :] = acc_ref[p, cls] / l_ref[p, cls]
        o_ref[:, lanes(p)] = tok_scr[p].astype(BF16)


def _attention(qkv_outs):
    q1, q4, q16, k1, k4, k16, v1, v4, v16 = qkv_outs
    b, l, d = q1.shape
    w = ATT_LANES
    n_pairs = w // LANES
    d4, d16 = DILATIONS[1], DILATIONS[2]
    cur1 = pl.BlockSpec((None, SUPER, w), lambda bi_, g, s: (bi_, s, g))
    prev1 = pl.BlockSpec((None, SPAN, w), lambda bi_, g, s: (bi_, jnp.maximum(s * (SUPER // SPAN) - 1, 0), g))
    cur4 = pl.BlockSpec((None, d4, SUPER // d4, w), lambda bi_, g, s: (bi_, 0, s, g))
    prev4 = pl.BlockSpec((None, d4, SPAN, w),
                         lambda bi_, g, s: (bi_, 0, jnp.maximum(s * (SUPER // (d4 * SPAN)) - 1, 0), g))
    cur16 = pl.BlockSpec((None, d16, SPAN, w), lambda bi_, g, s: (bi_, 0, s, g))
    prev16 = pl.BlockSpec((None, d16, SPAN, w), lambda bi_, g, s: (bi_, 0, jnp.maximum(s - 1, 0), g))
    return pl.pallas_call(
        _attn_body,
        grid=(b, d // w, l // SUPER),
        in_specs=[cur1, cur1, prev1, cur1, prev1,
                  cur4, cur4, prev4, cur4, prev4,
                  cur16, cur16, prev16, cur16, prev16],
        out_specs=pl.BlockSpec((None, SUPER, w), lambda bi_, g, s: (bi_, s, g)),
        out_shape=jax.ShapeDtypeStruct((b, l, d), BF16),
        scratch_shapes=[
            pltpu.VMEM((1, SPAN + SUPER, w), BF16), pltpu.VMEM((1, SPAN + SUPER, w), BF16),
            pltpu.VMEM((d4, SPAN + SUPER // d4, w), BF16), pltpu.VMEM((d4, SPAN + SUPER // d4, w), BF16),
            pltpu.VMEM((d16, 2 * SPAN, w), BF16), pltpu.VMEM((d16, 2 * SPAN, w), BF16),
            pltpu.VMEM((2, n_pairs, 2 * SPAN, 2 * SPAN), F32),
            pltpu.VMEM((2, n_pairs, 2 * SPAN, 2 * SPAN), BF16),
            pltpu.VMEM((2, n_pairs, SPAN, LANES), F32),
            pltpu.VMEM((4, SPAN, 2 * SPAN), F32),
            pltpu.VMEM((n_pairs, SUPER, LANES), F32),
        ] + [pltpu.VMEM((n_pairs, ACC_DIL, SUPER // ACC_DIL, LANES), F32)] * 3,
        compiler_params=_params("parallel", "parallel", "arbitrary"),
        name="dilated_attention",
    )(q1, k1, k1, v1, v1, q4, k4, k4, v4, v4, q16, k16, k16, v16, v16)


def _recurrent_layer(x, b, l, i, norm_g, w_lru, w_ssd, w_out, lru_conv_w, lru_conv_b, lru_w_r, lru_b_r, lru_w_i,
                     lru_b_i, lru_lambda, ssd_conv_w, ssd_conv_b, ssd_dt_bias, ssd_a_log, ssd_d, ssd_norm):
    d = x.shape[1]
    width = lru_lambda.shape[0]
    n_heads = ssd_a_log.shape[0]
    row = lambda v: v.reshape(1, -1)
    pad_lanes = lambda v: jnp.pad(v.reshape(1, -1), ((0, 0), (0, LANES - v.shape[0])))
    x3 = x.reshape(b, l, d)
    out_a = _lru(x3, row(norm_g), _layer_block(w_lru, i, (d, 2 * width)), lru_conv_w, row(lru_conv_b),
                 _block_diag_slabs(lru_w_r), _block_diag_slabs(lru_w_i), row(lru_b_r), row(lru_b_i),
                 row(lru_lambda))
    out_b = _ssd(x3, row(norm_g), _layer_block(w_ssd, i), ssd_conv_w, row(ssd_conv_b), pad_lanes(ssd_dt_bias),
                 pad_lanes(ssd_a_log), row(jnp.repeat(ssd_d, SSD_HEAD_DIM)), row(ssd_norm), n_heads)
    return ([out_a.reshape(b * l, width), out_b.reshape(b * l, -1)],
            [_layer_block(w_out, i, (width, d), (0, 0)), _layer_block(w_out, i, (width, d), (1, 0))])


def _attention_layer(x, b, l, i, norm_g, w_qkv, w_out, q_norm, k_norm):
    d = x.shape[1]
    heads = d // ATT_HEAD_DIM
    row = lambda v: v.reshape(1, -1)
    outs = _qkv(x.reshape(b, l, d), row(norm_g), w_qkv, i,
                row(jnp.tile(q_norm, heads)), row(jnp.tile(k_norm, heads)), *_rope_tables(l))
    o = _attention(outs)
    return [o.reshape(b * l, d)], [_layer_block(w_out, i)]


def kernel(x, rec_norm, rec_w_in, lru_conv_w, lru_conv_b, lru_w_r, lru_b_r, lru_w_i, lru_b_i, lru_lambda,
           ssd_conv_w, ssd_conv_b, ssd_dt_bias, ssd_a_log, ssd_d, ssd_norm, rec_w_out, att_norm, att_w_qkv,
           att_q_norm, att_k_norm, att_w_out, ffn_norm, ffn_w_gate_up, ffn_w_down):
    b, l, d = x.shape
    depth = ffn_norm.shape[0]
    h = x.reshape(b * l, d)
    lru_cols = 2 * lru_lambda.shape[1]
    w_lru = rec_w_in.astype(BF16)
    dt_pad = LANES - ssd_a_log.shape[1]
    w_ssd = jnp.pad(rec_w_in[:, :, lru_cols:], ((0, 0), (0, 0), (0, dt_pad))).astype(BF16)
    rec_wo, att_wqkv, att_wo = rec_w_out.astype(BF16), att_w_qkv.astype(BF16), att_w_out.astype(BF16)
    ffn_wgu, ffn_wd = ffn_w_gate_up.astype(BF16), ffn_w_down.astype(BF16)
    for layer in range(depth):
        i = layer // 2
        if layer % 2 == 0:
            mixes, wos = _recurrent_layer(h, b, l, i, rec_norm[i], w_lru, w_ssd, rec_wo, lru_conv_w[i],
                                          lru_conv_b[i], lru_w_r[i], lru_b_r[i], lru_w_i[i], lru_b_i[i],
                                          lru_lambda[i], ssd_conv_w[i], ssd_conv_b[i], ssd_dt_bias[i],
                                          ssd_a_log[i], ssd_d[i], ssd_norm[i])
        else:
            mixes, wos = _attention_layer(h, b, l, i, att_norm[i], att_wqkv, att_wo, att_q_norm[i], att_k_norm[i])
        h = _mix_ffn(h, mixes, wos, ffn_norm[layer].reshape(1, d), _layer_block(ffn_wgu, layer),
                     _layer_block(ffn_wd, layer))
    return h.reshape(b, l, d)
```

```python
import functools
import math

import jax
import jax.numpy as jnp
import numpy as np
from jax import lax
from jax.experimental import pallas as pl
from jax.experimental.pallas import tpu as pltpu

F32 = jnp.float32
BF16 = jnp.bfloat16

NORM_EPS = 1e-6
CONV_WIDTH = 4
LRU_BLOCK = 64
LRU_C = 8.0
GATE_SLAB = 256
SSD_HEAD_DIM = 64
SSD_GROUPS = 2
SSD_STATE = 128
SSD_CHUNK = 128
ATT_HEAD_DIM = 64
ROPE_DIM = 16
ROPE_THETA = 500000.0
SPAN = 128
DILATIONS = (1, 4, 16)
SUPER = SPAN * DILATIONS[-1]
ATT_LANES = 256
ACC_DIL = DILATIONS[1]
LANES = 128
SUBLANES = 8
BF16_ROWS = 16
VMEM_LIMIT_BYTES = 56 * 1024 * 1024


def _params(*semantics):
    return pltpu.CompilerParams(dimension_semantics=semantics, vmem_limit_bytes=VMEM_LIMIT_BYTES)


def _rmsnorm(x, g):
    ms = jnp.mean(x * x, axis=-1, keepdims=True)
    return x * lax.rsqrt(ms + NORM_EPS) * g


def _sigmoid(x):
    return 1.0 / (1.0 + jnp.exp(-x))


def _silu(x):
    return x * _sigmoid(x)


def _softplus(x):
    return jnp.maximum(x, 0.0) + jnp.log1p(jnp.exp(-jnp.abs(x)))


def _gelu_tanh(x):
    c = math.sqrt(2.0 / math.pi)
    return x * (0.5 * (1.0 + jnp.tanh(c * (x + 0.044715 * (x * x * x)))))


FFN_TM = 512
FFN_TH = 256


def _ffn_body(n_mix, x_ref, *refs):
    mix_refs, wo_refs = refs[:n_mix], refs[n_mix:2 * n_mix]
    g_ref, wgu_ref, wd_ref, o_ref, h_ref, act_ref = refs[2 * n_mix:]
    x1 = x_ref[...]
    for mix_ref, wo_ref in zip(mix_refs, wo_refs):
        x1 = x1 + jnp.dot(mix_ref[...], wo_ref[...], preferred_element_type=F32)
    o_ref[...] = x1
    h_ref[...] = _rmsnorm(x1, g_ref[...]).astype(BF16)
    hid = act_ref.shape[1]
    for c in range(hid // FFN_TH):
        cols = slice(c * FFN_TH, (c + 1) * FFN_TH)
        up_cols = slice(hid + c * FFN_TH, hid + (c + 1) * FFN_TH)
        gate = jnp.dot(h_ref[...], wgu_ref[:, cols], preferred_element_type=F32)
        up = jnp.dot(h_ref[...], wgu_ref[:, up_cols], preferred_element_type=F32)
        act_ref[:, cols] = (_silu(gate) * up).astype(BF16)
    o_ref[...] += jnp.dot(act_ref[...], wd_ref[...], preferred_element_type=F32)


def _resident(shape):
    return pl.BlockSpec(shape, lambda *_: (0,) * len(shape), pipeline_mode=pl.Buffered(1))


def _layer_block(stack, layer, block=None, index=None):
    block = tuple(stack.shape[1:]) if block is None else tuple(block)
    index = (layer,) + (tuple(index) if index is not None else (0,) * len(block))
    return stack, pl.BlockSpec((None,) + block, lambda *_: index, pipeline_mode=pl.Buffered(1))


def _mix_ffn(x, mixes, wos, g, wgu, wd):
    t, d = x.shape
    hid = wd[0].shape[1]
    weights = (*wos, (g, _resident(g.shape)), wgu, wd)
    rows = lambda a: pl.BlockSpec((FFN_TM, a.shape[1]), lambda i: (i, 0))
    return pl.pallas_call(
        functools.partial(_ffn_body, len(mixes)),
        grid=(t // FFN_TM,),
        in_specs=[rows(x)] + [rows(a) for a in mixes] + [spec for _, spec in weights],
        out_specs=rows(x),
        out_shape=jax.ShapeDtypeStruct((t, d), F32),
        scratch_shapes=[pltpu.VMEM((FFN_TM, d), BF16), pltpu.VMEM((FFN_TM, hid), BF16)],
        compiler_params=_params("parallel"),
        name="mix_ffn",
    )(x, *mixes, *[a for a, _ in weights])


def _segment_perm(rows):
    steps = rows // SUBLANES
    rho = np.arange(rows)
    time = (rho % SUBLANES) * steps + rho // SUBLANES
    perm = time[:, None] == np.arange(rows)[None, :]
    return jnp.asarray(perm, dtype=BF16), jnp.asarray(perm.T, dtype=BF16)


def _conv_segments(x, halo_ref, w_ref, b_ref):
    rows, width = x.shape
    sub0 = lax.broadcasted_iota(jnp.int32, (SUBLANES, width), 0) == 0
    halos = []
    for k in range(1, CONV_WIDTH):
        tail = pltpu.roll(x[rows - k * SUBLANES:rows - (k - 1) * SUBLANES, :], 1, 0)
        halos.append(jnp.where(sub0, halo_ref[k - 1], tail))
        halo_ref[k - 1] = tail
    y = b_ref[...] + w_ref[CONV_WIDTH - 1:CONV_WIDTH, :] * x
    for k in range(1, CONV_WIDTH):
        back = jnp.concatenate(halos[k - 1::-1] + [x[:rows - k * SUBLANES, :]], axis=0)
        y = y + w_ref[CONV_WIDTH - 1 - k:CONV_WIDTH - k, :] * back
    return y


LRU_TL = 512
LRU_SUB = 256


def _lru_body(x_ref, g_ref, w_ref, cw_ref, cb_ref, wr_ref, wi_ref, br_ref, bi_ref, lam_ref, perm_ref, unperm_ref,
              o_ref, hn_ref, halo_ref, carry_ref):
    first = pl.program_id(1) == 0
    w = o_ref.shape[1]
    sub = perm_ref.shape[0]
    steps = sub // SUBLANES

    @pl.when(first)
    def _():
        carry_ref[...] = jnp.zeros_like(carry_ref)
        halo_ref[...] = jnp.zeros_like(halo_ref)

    for t in range(o_ref.shape[0] // sub):
        rows = slice(t * sub, (t + 1) * sub)
        hn = _rmsnorm(x_ref[rows, :], g_ref[...]).astype(BF16)
        hn_ref[rows, :] = hn
        hn = jnp.dot(perm_ref[...], hn, preferred_element_type=F32).astype(BF16)
        proj = jnp.dot(hn, w_ref[...], preferred_element_type=F32)
        gate = proj[:, w:]

        xc = _conv_segments(proj[:, :w], halo_ref, cw_ref, cb_ref)
        xcb = xc.astype(BF16)
        pre_r, pre_i = [], []
        for s in range(w // GATE_SLAB):
            slab = xcb[:, s * GATE_SLAB:(s + 1) * GATE_SLAB]
            pre_r.append(jnp.dot(slab, wr_ref[s], preferred_element_type=F32))
            pre_i.append(jnp.dot(slab, wi_ref[s], preferred_element_type=F32))
        r = _sigmoid(jnp.concatenate(pre_r, axis=1) + br_ref[...])
        i = _sigmoid(jnp.concatenate(pre_i, axis=1) + bi_ref[...])
        log_a = (-LRU_C * r) * _softplus(-lam_ref[...])
        a = jnp.exp(log_a)
        z = 1.0 - a * a
        u = jnp.where(z > 0.0, z * lax.rsqrt(z), 0.0) * (i * xc)

        h_j = jnp.zeros((SUBLANES, w), F32)
        p_j = jnp.ones((SUBLANES, w), F32)
        hs, ps = [], []
        for j in range(steps):
            a_j = a[j * SUBLANES:(j + 1) * SUBLANES, :]
            h_j = a_j * h_j + u[j * SUBLANES:(j + 1) * SUBLANES, :]
            p_j = a_j * p_j
            hs.append(h_j)
            ps.append(p_j)
        c = carry_ref[0:1, :]
        entering = []
        for seg in range(SUBLANES):
            entering.append(c)
            c = p_j[seg:seg + 1, :] * c + h_j[seg:seg + 1, :]
        carry_ref[0:1, :] = c
        enter = jnp.concatenate(entering, axis=0)
        h = jnp.concatenate([hs[j] + ps[j] * enter for j in range(steps)], axis=0)
        out = (h * _gelu_tanh(gate)).astype(BF16)
        o_ref[rows, :] = jnp.dot(unperm_ref[...], out, preferred_element_type=F32).astype(BF16)


def _lru(x3, g, w_lru, cw, cb, wr_bd, wi_bd, br, bi, lam):
    b, l, d = x3.shape
    width = lam.shape[1]
    perms = _segment_perm(LRU_SUB)
    return pl.pallas_call(
        _lru_body,
        grid=(b, l // LRU_TL),
        in_specs=[pl.BlockSpec((None, LRU_TL, d), lambda bi_, li: (bi_, li, 0))]
        + [_resident(g.shape), w_lru[1]]
        + [_resident(a.shape) for a in (cw, cb, wr_bd, wi_bd, br, bi, lam, *perms)],
        out_specs=[pl.BlockSpec((None, LRU_TL, width), lambda bi_, li: (bi_, li, 0)),
                   pl.BlockSpec((None, LRU_TL, d), lambda bi_, li: (bi_, li, 0))],
        out_shape=[jax.ShapeDtypeStruct((b, l, width), BF16),
                   jax.ShapeDtypeStruct((b, l, d), BF16)],
        scratch_shapes=[
            pltpu.VMEM((CONV_WIDTH - 1, SUBLANES, width), F32),
            pltpu.VMEM((SUBLANES, width), F32),
        ],
        compiler_params=_params("parallel", "arbitrary"),
        name="lru",
    )(x3, g, w_lru[0], cw, cb, wr_bd, wi_bd, br, bi, lam, *perms)


def _block_diag_slabs(w):
    nb, bs, _ = w.shape
    per = GATE_SLAB // bs
    w = w.reshape(nb // per, per, bs, bs)
    eye = jnp.eye(per, dtype=w.dtype)
    bd = jnp.einsum("spij,pq->spiqj", w, eye).reshape(nb // per, GATE_SLAB, GATE_SLAB)
    return bd.astype(BF16)


def _ssd_body(n_heads, hn_ref, w_ref, cw_ref, cb_ref, dtb_ref, alog_ref, dvec_ref, nrm_ref,
              exp_ref, perm_ref, unperm_ref, o_ref, proj_ref, halo_ref, state_ref):
    first = pl.program_id(1) == 0
    t = SSD_CHUNK
    width = n_heads * SSD_HEAD_DIM
    gw = SSD_STATE
    conv_ch = width + 2 * SSD_GROUPS * gw
    n_chunks = hn_ref.shape[0] // t

    @pl.when(first)
    def _():
        state_ref[...] = jnp.zeros_like(state_ref)
        halo_ref[...] = jnp.zeros_like(halo_ref)

    hn = jnp.concatenate([jnp.dot(perm_ref[...], hn_ref[c * t:(c + 1) * t, :], preferred_element_type=F32).astype(BF16)
                          for c in range(n_chunks)], axis=0)
    proj_ref[...] = jnp.dot(hn, w_ref[...], preferred_element_type=F32)

    steps = t // SUBLANES
    time_of = lambda i: (i & (SUBLANES - 1)) * steps + lax.shift_right_logical(i, 3)
    causal = (time_of(lax.broadcasted_iota(jnp.int32, (t, t), 0))
              >= time_of(lax.broadcasted_iota(jnp.int32, (t, t), 1)))
    tri = jnp.where(causal, 1.0, 0.0).astype(BF16)
    lane = lax.broadcasted_iota(jnp.int32, (t, LANES), 1)
    lo = lane < SSD_HEAD_DIM
    rlo = lax.broadcasted_iota(jnp.int32, (LANES, gw), 0) < SSD_HEAD_DIM

    dt = _softplus(proj_ref[:, width + conv_ch:] + dtb_ref[...])
    adt = dt * (-jnp.exp(alog_ref[...]))
    cs = jnp.concatenate([_split_dot(adt[c * t:(c + 1) * t, :], tri, parts=3, left=True)
                          for c in range(n_chunks)], axis=0)
    dt_w = _split_dot(dt, exp_ref[...], parts=2)
    cs_w = _split_dot(cs, exp_ref[...], parts=3)

    for c in range(n_chunks):
        rows = slice(c * t, (c + 1) * t)
        out = _ssd_chunk(proj_ref[rows, 0:width], proj_ref[rows, width:width + conv_ch],
                         cs[rows, :], dt_w[rows, :], cs_w[rows, :], n_heads, causal, lo, rlo, halo_ref, state_ref,
                         cw_ref, cb_ref, dvec_ref, nrm_ref)
        o_ref[rows, :] = jnp.dot(unperm_ref[...], out, preferred_element_type=F32).astype(BF16)


def _ssd_chunk(z, xbc_raw, cs, dt_w, cs_w, n_heads, causal, lo, rlo, halo_ref, state_ref,
               cw_ref, cb_ref, dvec_ref, nrm_ref):
    t = SSD_CHUNK
    width = n_heads * SSD_HEAD_DIM
    gw = SSD_STATE
    heads_per_group = n_heads // SSD_GROUPS
    xbc = _silu(_conv_segments(xbc_raw, halo_ref, cw_ref, cb_ref))
    xs = xbc[:, :width]
    cs_t = cs.T
    cs_last = cs[t - 1:t, :]
    cdec = jnp.exp(cs_last)
    ecs_w = jnp.exp(cs_w)
    dte_w = jnp.exp(cs_w[t - 1:t, :] - cs_w)

    cbs = []
    for g in range(SSD_GROUPS):
        bm = xbc[:, width + g * gw: width + (g + 1) * gw].astype(BF16)
        cm = xbc[:, width + (SSD_GROUPS + g) * gw: width + (SSD_GROUPS + g + 1) * gw].astype(BF16)
        cb = lax.dot_general(cm, bm, (((1,), (1,)), ((), ())), preferred_element_type=F32)
        cbs.append((bm, cm, cb))

    ys = []
    for p in range(n_heads // 2):
        h0, h1 = 2 * p, 2 * p + 1
        bm, cm, cb = cbs[h0 // heads_per_group]
        sl = slice(p * LANES, (p + 1) * LANES)
        xs_p = xs[:, sl]
        xdt = xs_p * dt_w[:, sl]
        ms = []
        for h in (h0, h1):
            seg = cs[:, h:h + 1] - cs_t[h:h + 1, :]
            ms.append((cb * jnp.where(causal, jnp.exp(seg), 0.0)).astype(BF16))
        m_cat = jnp.concatenate(ms, axis=1)
        xdt_bd = jnp.concatenate([jnp.where(lo, xdt, 0.0), jnp.where(lo, 0.0, xdt)], axis=0).astype(BF16)
        y = jnp.dot(m_cat, xdt_bd, preferred_element_type=F32)
        prev = state_ref[sl, :]
        y_off = lax.dot_general(cm, prev.astype(BF16), (((1,), (1,)), ((), ())), preferred_element_type=F32)
        y = y + y_off * ecs_w[:, sl]
        xw = (xdt * dte_w[:, sl]).astype(BF16)
        st = lax.dot_general(xw, bm, (((0,), (0,)), ((), ())), preferred_element_type=F32)
        dec = jnp.where(rlo, cdec[:, h0:h0 + 1], cdec[:, h1:h1 + 1])
        state_ref[sl, :] = prev * dec + st
        ys.append(y + dvec_ref[:, sl] * xs_p)

    y = jnp.concatenate(ys, axis=1) * _silu(z)
    gsz = width // SSD_GROUPS
    outs = []
    for g in range(SSD_GROUPS):
        yg = y[:, g * gsz:(g + 1) * gsz]
        outs.append(yg * lax.rsqrt(jnp.mean(yg * yg, axis=-1, keepdims=True) + NORM_EPS))
    return (jnp.concatenate(outs, axis=1) * nrm_ref[...]).astype(BF16)


SSD_TS = 512


def _ssd(hn3, w_ssd, cw, cb, dt_bias, a_log, d_vec, nrm, n_heads):
    b, l, d = hn3.shape
    width = n_heads * SSD_HEAD_DIM
    conv_ch = width + 2 * SSD_GROUPS * SSD_STATE
    expand = jnp.asarray(np.arange(LANES)[:, None] == np.arange(width)[None, :] // SSD_HEAD_DIM,
                         dtype=BF16)
    consts = (cw, cb, dt_bias, a_log, d_vec, nrm, expand, *_segment_perm(SSD_CHUNK))
    return pl.pallas_call(
        functools.partial(_ssd_body, n_heads),
        grid=(b, l // SSD_TS),
        in_specs=[pl.BlockSpec((None, SSD_TS, d), lambda bi_, ci: (bi_, ci, 0))]
        + [w_ssd[1]] + [_resident(a.shape) for a in consts],
        out_specs=pl.BlockSpec((None, SSD_TS, width), lambda bi_, ci: (bi_, ci, 0)),
        out_shape=jax.ShapeDtypeStruct((b, l, width), BF16),
        scratch_shapes=[
            pltpu.VMEM((SSD_TS, w_ssd[0].shape[-1]), F32),
            pltpu.VMEM((CONV_WIDTH - 1, SUBLANES, conv_ch), F32),
            pltpu.VMEM((width, SSD_STATE), F32),
        ],
        compiler_params=_params("parallel", "arbitrary"),
        name="ssd",
    )(hn3, w_ssd[0], *consts)


QKV_TM = 512
Q_SCALE = ATT_HEAD_DIM ** -0.5 * math.log2(math.e)
PERM_ROWS = 256


def _split_dot(x, m, parts=2, left=False):
    acc, rem = None, x
    for k in range(parts):
        piece = rem.astype(BF16)
        term = jnp.dot(m, piece, preferred_element_type=F32) if left else jnp.dot(piece, m, preferred_element_type=F32)
        acc = term if acc is None else acc + term
        if k + 1 < parts:
            rem = rem - piece.astype(F32)
    return acc


def _slab_map(fn, x):
    return jnp.concatenate([fn(x[:, c * GATE_SLAB:(c + 1) * GATE_SLAB]) for c in range(x.shape[1] // GATE_SLAB)],
                           axis=1)


def _head_norm_rope(y, gain, cos, sin, gsum, rot):
    ss = _slab_map(lambda v: _split_dot(v, gsum, parts=1), y * y)
    yn = y * lax.rsqrt(ss * (1.0 / ATT_HEAD_DIM) + NORM_EPS) * gain
    partner = _slab_map(lambda v: _split_dot(v, rot), yn)
    tile = lambda tbl: jnp.concatenate([tbl] * (y.shape[1] // LANES), axis=1)
    return yn * tile(cos) + partner * tile(sin)


def _qkv_body(x_ref, g_ref, w_ref, qg_ref, kg_ref, cos_ref, sin_ref, gsum_ref, rot_ref, p1_ref, p4_ref, p16_ref,
              *refs):
    outs, h_ref = refs[:9], refs[9]
    j = pl.program_id(2)

    @pl.when(j == 0)
    def _():
        h_ref[...] = _rmsnorm(x_ref[...], g_ref[...]).astype(BF16)

    sub = PERM_ROWS

    def section(o1, o4, o16, gain_ref=None, grouped=False):
        y = jnp.dot(h_ref[...], w_ref[...], preferred_element_type=F32)
        if gain_ref is not None:
            y = _head_norm_rope(y, gain_ref[...], cos_ref[...], sin_ref[...], gsum_ref[...], rot_ref[...])
        y_all = y.astype(BF16)
        for t in range(x_ref.shape[0] // sub):
            rows = slice(t * sub, (t + 1) * sub)
            yb = y_all[rows, :]
            if grouped:
                o1[rows, :] = jnp.dot(p1_ref[...], yb, preferred_element_type=F32).astype(BF16)
            else:
                o1[rows, :] = yb
            for perm_ref, o in ((p4_ref, o4), (p16_ref, o16)):
                dil = o.shape[0]
                n = sub // dil
                yp = jnp.dot(perm_ref[...], yb, preferred_element_type=F32).astype(BF16)
                for r in range(dil):
                    o[r, t * n:(t + 1) * n, :] = yp[r * n:(r + 1) * n, :]

    @pl.when(j == 0)
    def _():
        section(*outs[0:3], gain_ref=qg_ref, grouped=True)

    @pl.when(j == 1)
    def _():
        section(*outs[3:6], gain_ref=kg_ref)

    @pl.when(j == 2)
    def _():
        section(*outs[6:9])


def _qkv(x3, g, w_qkv, layer, qg, kg, cos, sin):
    b, l, d = x3.shape
    tm = QKV_TM
    out_specs, out_shape = [], []
    for _ in range(3):
        out_specs.append(pl.BlockSpec((None, tm, d), lambda bi_, i, j: (bi_, i, 0)))
        out_shape.append(jax.ShapeDtypeStruct((b, l, d), BF16))
        for dil in DILATIONS[1:]:
            out_specs.append(pl.BlockSpec((None, dil, tm // dil, d), lambda bi_, i, j: (bi_, 0, i, 0)))
            out_shape.append(jax.ShapeDtypeStruct((b, dil, l // dil, d), BF16))
    consts = _qkv_constants(tm)
    tbl = lambda: pl.BlockSpec((tm, LANES), lambda bi_, i, j: (i, 0))
    return pl.pallas_call(
        _qkv_body,
        grid=(b, l // tm, 3),
        in_specs=[
            pl.BlockSpec((None, tm, d), lambda bi_, i, j: (bi_, i, 0)),
            _resident((1, d)),
            pl.BlockSpec((None, d, d), lambda bi_, i, j: (layer, 0, j)),
            _resident((1, d)), _resident((1, d)), tbl(), tbl(),
        ] + [_resident(c.shape) for c in consts],
        out_specs=out_specs,
        out_shape=out_shape,
        scratch_shapes=[pltpu.VMEM((tm, d), BF16)],
        compiler_params=_params("parallel", "parallel", "arbitrary"),
        name="qkv",
    )(x3, g, w_qkv, qg, kg, cos, sin, *consts)


def _qkv_constants(tm):
    lane = np.arange(GATE_SLAB)
    gsum = (lane[:, None] // ATT_HEAD_DIM == lane[None, :] // ATT_HEAD_DIM)
    half = ROPE_DIM // 2
    pos = lane % ATT_HEAD_DIM
    partner = np.where(pos < half, lane + half, np.where(pos < ROPE_DIM, lane - half, -1))
    rot = lane[:, None] == partner[None, :]
    out_row = np.arange(PERM_ROWS)
    group = SPAN // ACC_DIL
    within = out_row % SPAN
    src = out_row - within + (within % group) * ACC_DIL + within // group
    perms = [src[:, None] == np.arange(PERM_ROWS)[None, :]]
    for dil in DILATIONS[1:]:
        sub = PERM_ROWS
        rows = sub // dil
        assert rows % BF16_ROWS == 0 and tm % sub == 0
        out_row = np.arange(sub)
        src = (out_row % rows) * dil + out_row // rows
        perms.append(src[:, None] == np.arange(sub)[None, :])
    return tuple(jnp.asarray(m, dtype=BF16) for m in (gsum, rot, *perms))


def _rope_tables(l):
    half = ROPE_DIM // 2
    pos = jnp.arange(l, dtype=F32)
    inv = ROPE_THETA ** (-2.0 * jnp.arange(half, dtype=F32) / ROPE_DIM)
    ang = pos[:, None] * inv[None, :]
    cos, sin = jnp.cos(ang), jnp.sin(ang)
    pad = ATT_HEAD_DIM - ROPE_DIM
    cos_h = jnp.concatenate([cos, cos, jnp.ones((l, pad), F32)], axis=1)
    sin_h = jnp.concatenate([-sin, sin, jnp.zeros((l, pad), F32)], axis=1)
    rep = LANES // ATT_HEAD_DIM
    return tuple(jnp.concatenate([tb] * rep, axis=1) for tb in (cos_h, sin_h))


def _attn_body(q1, k1c, k1p, v1c, v1p, q4, k4c, k4p, v4c, v4p, q16, k16c, k16p, v16c, v16p,
               o_ref, kb1, vb1, kb4, vb4, kb16, vb16, s_scr, p_scr, mx_scr, bias_scr, tok_scr, acc_ref, m_ref, l_ref):
    first = pl.program_id(2) == 0
    n_pairs = acc_ref.shape[0]
    n_blocks = SUPER // SPAN
    trans_b = (((1,), (1,)), ((), ()))

    def gather_rows(pairs):
        for buf, prev, cur in pairs:
            if len(prev.shape) == 2:
                buf[0, 0:SPAN, :] = prev[...]
                buf[0, SPAN:, :] = cur[...]
            else:
                buf[:, 0:SPAN, :] = prev[...]
                buf[:, SPAN:, :] = cur[...]

    gather_rows(((kb1, k1p, k1c), (vb1, v1p, v1c)))

    row = lax.broadcasted_iota(jnp.int32, (SPAN, 2 * SPAN), 0)
    kj = lax.broadcasted_iota(jnp.int32, (SPAN, 2 * SPAN), 1)
    group = SPAN // ACC_DIL
    for base, qi in ((0, row), (2, (row % group) * ACC_DIL + row // group)):
        band = (kj >= qi) & (kj <= qi + SPAN)
        bias_scr[base] = jnp.where(band, 0.0, -jnp.inf)
        bias_scr[base + 1] = jnp.where(band & (kj >= SPAN), 0.0, -jnp.inf)

    lo = lax.broadcasted_iota(jnp.int32, (SPAN, LANES), 1) < ATT_HEAD_DIM
    ones_v = jnp.ones((2 * SPAN, LANES), BF16)
    lanes = lambda p: slice(p * LANES, (p + 1) * LANES)

    def run_pattern(dil, q_get, kb, vb):
        per_res = SUPER // (SPAN * dil)
        overwrite = dil == DILATIONS[0]

        def split(f):
            if per_res == 1:
                return f, 0
            if dil == 1:
                return 0, f
            if isinstance(f, int):
                return f // per_res, f % per_res
            return lax.shift_right_logical(f, per_res.bit_length() - 1), f & (per_res - 1)

        def slab_rows(nb):
            return pl.ds(nb * SPAN if isinstance(nb, int) else pl.multiple_of(nb * SPAN, SPAN), 2 * SPAN)

        def stage_scores(f, slot):
            r, nb = split(f)
            if isinstance(nb, int):
                flag = jnp.where(first, 1, 0) if nb == 0 else 0
            else:
                flag = jnp.where(first & (nb == 0), 1, 0)
            bias = bias_scr[flag + (2 if overwrite else 0)]
            bias2 = jnp.concatenate([bias, bias], axis=0)
            q = q_get(r, nb)
            ks = kb[r, slab_rows(nb), :]
            for p in range(n_pairs):
                qp = q[:, lanes(p)]
                zero = jnp.zeros_like(qp)
                q2 = jnp.concatenate([jnp.where(lo, qp, zero), jnp.where(lo, zero, qp)], axis=0)
                s_scr[slot, p] = lax.dot_general(q2, ks[:, lanes(p)], trans_b, preferred_element_type=F32) + bias2

        def stage_softmax(slot):
            for p in range(n_pairs):
                s = s_scr[slot, p]
                mx = jnp.max(s, axis=1, keepdims=True)
                p_scr[slot, p] = jnp.exp2(s - mx).astype(BF16)
                mx_scr[slot, p] = jnp.where(lo, mx[:SPAN], mx[SPAN:])

        def stage_values(f, slot):
            r, nb = split(f)
            vs = vb[r, slab_rows(nb), :]
            if overwrite:
                grp = SPAN // ACC_DIL
                start = nb * grp if isinstance(nb, int) else pl.multiple_of(nb * grp, grp)
                pieces = [(slice(c * grp, (c + 1) * grp), c, pl.ds(start, grp)) for c in range(ACC_DIL)]
            elif dil == ACC_DIL:
                pieces = [(slice(0, SPAN), r, q_rows(nb))]
            else:
                sub = dil // ACC_DIL
                pieces = [(slice(0, SPAN), r & (ACC_DIL - 1),
                           pl.ds(lax.shift_right_logical(r, 2) if not isinstance(r, int) else r // ACC_DIL,
                                 SPAN, stride=sub))]
            for p in range(n_pairs):
                v_aug = jnp.concatenate([vs[:, lanes(p)], ones_v], axis=1)
                pv = jnp.dot(p_scr[slot, p], v_aug, preferred_element_type=F32)
                o_blk = jnp.where(lo, pv[:SPAN, :LANES], pv[SPAN:, :LANES])
                l_blk = jnp.where(lo, pv[:SPAN, LANES:], pv[SPAN:, LANES:])
                m_blk = mx_scr[slot, p]
                for src, cls, rows in pieces:
                    m_new, l_new, o_new = m_blk[src], l_blk[src], o_blk[src]
                    if overwrite:
                        m_ref[p, cls, rows, :], l_ref[p, cls, rows, :], acc_ref[p, cls, rows, :] = m_new, l_new, o_new
                        continue
                    m_old = m_ref[p, cls, rows, :]
                    m = jnp.maximum(m_old, m_new)
                    e_old = jnp.exp2(m_old - m)
                    e_new = jnp.exp2(m_new - m)
                    l_ref[p, cls, rows, :] = e_old * l_ref[p, cls, rows, :] + e_new * l_new
                    acc_ref[p, cls, rows, :] = e_old * acc_ref[p, cls, rows, :] + e_new * o_new
                    m_ref[p, cls, rows, :] = m

        stage_scores(0, 0)
        stage_scores(1, 1)
        stage_softmax(0)

        def body(i, c):
            f = 2 * i + 2
            stage_scores(f, 0)
            stage_softmax(1)
            stage_values(f - 2, 0)
            stage_scores(f + 1, 1)
            stage_softmax(0)
            stage_values(f - 1, 1)
            return c

        lax.fori_loop(0, (n_blocks - 2) // 2, body, 0)
        stage_softmax(1)
        stage_values(n_blocks - 2, 0)
        stage_values(n_blocks - 1, 1)

    def q_rows(nb):
        return pl.ds(nb * SPAN if isinstance(nb, int) else pl.multiple_of(nb * SPAN, SPAN), SPAN)

    run_pattern(DILATIONS[0], lambda r, nb: q1[q_rows(nb), :], kb1, vb1)
    gather_rows(((kb4, k4p, k4c), (vb4, v4p, v4c)))
    run_pattern(DILATIONS[1], lambda r, nb: q4[r, q_rows(nb), :], kb4, vb4)
    gather_rows(((kb16, k16p, k16c), (vb16, v16p, v16c)))
    run_pattern(DILATIONS[2], lambda r, nb: q16[r], kb16, vb16)

    for p in range(n_pairs):
        for cls in range(ACC_DIL):
            tok_scr[p, pl.ds(cls, SUPER // ACC_DIL, stride=ACC_DIL), :] = acc_ref[p, cls] / l_ref[p, cls]
        o_ref[:, lanes(p)] = tok_scr[p].astype(BF16)


def _attention(qkv_outs):
    q1, q4, q16, k1, k4, k16, v1, v4, v16 = qkv_outs
    b, l, d = q1.shape
    w = ATT_LANES
    n_pairs = w // LANES
    d4, d16 = DILATIONS[1], DILATIONS[2]
    cur1 = pl.BlockSpec((None, SUPER, w), lambda bi_, g, s: (bi_, s, g))
    prev1 = pl.BlockSpec((None, SPAN, w), lambda bi_, g, s: (bi_, jnp.maximum(s * (SUPER // SPAN) - 1, 0), g))
    cur4 = pl.BlockSpec((None, d4, SUPER // d4, w), lambda bi_, g, s: (bi_, 0, s, g))
    prev4 = pl.BlockSpec((None, d4, SPAN, w),
                         lambda bi_, g, s: (bi_, 0, jnp.maximum(s * (SUPER // (d4 * SPAN)) - 1, 0), g))
    cur16 = pl.BlockSpec((None, d16, SPAN, w), lambda bi_, g, s: (bi_, 0, s, g))
    prev16 = pl.BlockSpec((None, d16, SPAN, w), lambda bi_, g, s: (bi_, 0, jnp.maximum(s - 1, 0), g))
    return pl.pallas_call(
        _attn_body,
        grid=(b, d // w, l // SUPER),
        in_specs=[cur1, cur1, prev1, cur1, prev1,
                  cur4, cur4, prev4, cur4, prev4,
                  cur16, cur16, prev16, cur16, prev16],
        out_specs=pl.BlockSpec((None, SUPER, w), lambda bi_, g, s: (bi_, s, g)),
        out_shape=jax.ShapeDtypeStruct((b, l, d), BF16),
        scratch_shapes=[
            pltpu.VMEM((1, SPAN + SUPER, w), BF16), pltpu.VMEM((1, SPAN + SUPER, w), BF16),
            pltpu.VMEM((d4, SPAN + SUPER // d4, w), BF16), pltpu.VMEM((d4, SPAN + SUPER // d4, w), BF16),
            pltpu.VMEM((d16, 2 * SPAN, w), BF16), pltpu.VMEM((d16, 2 * SPAN, w), BF16),
            pltpu.VMEM((2, n_pairs, 2 * SPAN, 2 * SPAN), F32),
            pltpu.VMEM((2, n_pairs, 2 * SPAN, 2 * SPAN), BF16),
            pltpu.VMEM((2, n_pairs, SPAN, LANES), F32),
            pltpu.VMEM((4, SPAN, 2 * SPAN), F32),
            pltpu.VMEM((n_pairs, SUPER, LANES), F32),
        ] + [pltpu.VMEM((n_pairs, ACC_DIL, SUPER // ACC_DIL, LANES), F32)] * 3,
        compiler_params=_params("parallel", "parallel", "arbitrary"),
        name="dilated_attention",
    )(q1, k1, k1, v1, v1, q4, k4, k4, v4, v4, q16, k16, k16, v16, v16)


def _recurrent_layer(x, b, l, i, norm_g, w_lru, w_ssd, w_out, lru_conv_w, lru_conv_b, lru_w_r, lru_b_r, lru_w_i,
                     lru_b_i, lru_lambda, ssd_conv_w, ssd_conv_b, ssd_dt_bias, ssd_a_log, ssd_d, ssd_norm):
    d = x.shape[1]
    width = lru_lambda.shape[0]
    n_heads = ssd_a_log.shape[0]
    row = lambda v: v.reshape(1, -1)
    pad_lanes = lambda v: jnp.pad(v.reshape(1, -1), ((0, 0), (0, LANES - v.shape[0])))
    x3 = x.reshape(b, l, d)
    out_a, hn3 = _lru(x3, row(norm_g), _layer_block(w_lru, i, (d, 2 * width)), lru_conv_w, row(lru_conv_b),
                      _block_diag_slabs(lru_w_r), _block_diag_slabs(lru_w_i), row(lru_b_r), row(lru_b_i),
                      row(lru_lambda))
    out_b = _ssd(hn3, _layer_block(w_ssd, i), ssd_conv_w, row(ssd_conv_b), pad_lanes(ssd_dt_bias),
                 pad_lanes(ssd_a_log), row(jnp.repeat(ssd_d, SSD_HEAD_DIM)), row(ssd_norm), n_heads)
    return ([out_a.reshape(b * l, width), out_b.reshape(b * l, -1)],
            [_layer_block(w_out, i, (width, d), (0, 0)), _layer_block(w_out, i, (width, d), (1, 0))])


def _attention_layer(x, b, l, i, norm_g, w_qkv, w_out, q_norm, k_norm):
    d = x.shape[1]
    heads = d // ATT_HEAD_DIM
    row = lambda v: v.reshape(1, -1)
    outs = _qkv(x.reshape(b, l, d), row(norm_g), w_qkv, i,
                row(jnp.tile(q_norm, heads) * Q_SCALE), row(jnp.tile(k_norm, heads)), *_rope_tables(l))
    o = _attention(outs)
    return [o.reshape(b * l, d)], [_layer_block(w_out, i)]


def kernel(x, rec_norm, rec_w_in, lru_conv_w, lru_conv_b, lru_w_r, lru_b_r, lru_w_i, lru_b_i, lru_lambda,
           ssd_conv_w, ssd_conv_b, ssd_dt_bias, ssd_a_log, ssd_d, ssd_norm, rec_w_out, att_norm, att_w_qkv,
           att_q_norm, att_k_norm, att_w_out, ffn_norm, ffn_w_gate_up, ffn_w_down):
    b, l, d = x.shape
    depth = ffn_norm.shape[0]
    h = x.reshape(b * l, d)
    lru_cols = 2 * lru_lambda.shape[1]
    w_lru = rec_w_in.astype(BF16)
    dt_pad = LANES - ssd_a_log.shape[1]
    w_ssd = jnp.pad(rec_w_in[:, :, lru_cols:], ((0, 0), (0, 0), (0, dt_pad))).astype(BF16)
    rec_wo, att_wqkv, att_wo = rec_w_out.astype(BF16), att_w_qkv.astype(BF16), att_w_out.astype(BF16)
    ffn_wgu, ffn_wd = ffn_w_gate_up.astype(BF16), ffn_w_down.astype(BF16)
    for layer in range(depth):
        i = layer // 2
        if layer % 2 == 0:
            mixes, wos = _recurrent_layer(h, b, l, i, rec_norm[i], w_lru, w_ssd, rec_wo, lru_conv_w[i],
                                          lru_conv_b[i], lru_w_r[i], lru_b_r[i], lru_w_i[i], lru_b_i[i],
                                          lru_lambda[i], ssd_conv_w[i], ssd_conv_b[i], ssd_dt_bias[i],
                                          ssd_a_log[i], ssd_d[i], ssd_norm[i])
        else:
            mixes, wos = _attention_layer(h, b, l, i, att_norm[i], att_wqkv, att_wo, att_q_norm[i], att_k_norm[i])
        h = _mix_ffn(h, mixes, wos, ffn_norm[layer].reshape(1, d), _layer_block(ffn_wgu, layer),
                     _layer_block(ffn_wd, layer))
    return h.reshape(b, l, d)
```

```python
import functools
import math

import jax
import jax.numpy as jnp
import numpy as np
from jax import lax
from jax.experimental import pallas as pl
from jax.experimental.pallas import tpu as pltpu

F32 = jnp.float32
BF16 = jnp.bfloat16

NORM_EPS = 1e-6
LOG2_E = math.log2(math.e)
CONV_WIDTH = 4
LRU_BLOCK = 64
LRU_C = 8.0
GATE_SLAB = 256
SSD_HEAD_DIM = 64
SSD_GROUPS = 2
SSD_STATE = 128
SSD_CHUNK = 128
ATT_HEAD_DIM = 64
ROPE_DIM = 16
ROPE_THETA = 500000.0
SPAN = 128
DILATIONS = (1, 4, 16)
SUPER = SPAN * DILATIONS[-1]
ATT_LANES = 256
ACC_DIL = DILATIONS[1]
LANES = 128
SUBLANES = 8
BF16_ROWS = 16
VMEM_LIMIT_BYTES = 56 * 1024 * 1024


def _params(*semantics):
    return pltpu.CompilerParams(dimension_semantics=semantics, vmem_limit_bytes=VMEM_LIMIT_BYTES)


def _rmsnorm(x, g):
    ms = jnp.mean(x * x, axis=-1, keepdims=True)
    return x * lax.rsqrt(ms + NORM_EPS) * g


def _sigmoid(x):
    return 1.0 / (1.0 + jnp.exp2(x * (-LOG2_E)))


def _silu(x):
    return x * _sigmoid(x)


def _softplus(x):
    return jnp.maximum(x, 0.0) + jnp.log1p(jnp.exp(-jnp.abs(x)))


def _gelu_tanh(x):
    c = math.sqrt(2.0 / math.pi)
    half_x = 0.5 * x
    return half_x + half_x * jnp.tanh(x * (c + (c * 0.044715) * (x * x)))


FFN_TM = 1024
FFN_TH = 256


def _ffn_body(n_mix, x_ref, *refs):
    mix_refs, wo_refs = refs[:n_mix], refs[n_mix:2 * n_mix]
    g_ref, wgu_ref, wd_ref, o_ref, h_ref, act_ref = refs[2 * n_mix:]
    x1 = x_ref[...]
    for mix_ref, wo_ref in zip(mix_refs, wo_refs):
        x1 = x1 + jnp.dot(mix_ref[...], wo_ref[...], preferred_element_type=F32)
    o_ref[...] = x1
    h_ref[...] = _rmsnorm(x1, g_ref[...]).astype(BF16)
    hid = act_ref.shape[1]
    for c in range(hid // FFN_TH):
        cols = slice(c * FFN_TH, (c + 1) * FFN_TH)
        up_cols = slice(hid + c * FFN_TH, hid + (c + 1) * FFN_TH)
        gate = jnp.dot(h_ref[...], wgu_ref[:, cols], preferred_element_type=F32)
        up = jnp.dot(h_ref[...], wgu_ref[:, up_cols], preferred_element_type=F32)
        act_ref[:, cols] = (_silu(gate) * up).astype(BF16)
    o_ref[...] += jnp.dot(act_ref[...], wd_ref[...], preferred_element_type=F32)


def _resident(shape):
    return pl.BlockSpec(shape, lambda *_: (0,) * len(shape), pipeline_mode=pl.Buffered(1))


def _layer_block(stack, layer, block=None, index=None):
    block = tuple(stack.shape[1:]) if block is None else tuple(block)
    index = (layer,) + (tuple(index) if index is not None else (0,) * len(block))
    return stack, pl.BlockSpec((None,) + block, lambda *_: index, pipeline_mode=pl.Buffered(1))


def _mix_ffn(x, mixes, wos, g, wgu, wd):
    t, d = x.shape
    hid = wd[0].shape[1]
    weights = (*wos, (g, _resident(g.shape)), wgu, wd)
    rows = lambda a: pl.BlockSpec((FFN_TM, a.shape[1]), lambda i: (i, 0))
    return pl.pallas_call(
        functools.partial(_ffn_body, len(mixes)),
        grid=(t // FFN_TM,),
        in_specs=[rows(x)] + [rows(a) for a in mixes] + [spec for _, spec in weights],
        out_specs=rows(x),
        out_shape=jax.ShapeDtypeStruct((t, d), F32),
        scratch_shapes=[pltpu.VMEM((FFN_TM, d), BF16), pltpu.VMEM((FFN_TM, hid), BF16)],
        compiler_params=_params("parallel"),
        name="mix_ffn",
    )(x, *mixes, *[a for a, _ in weights])


def _segment_perm(rows):
    steps = rows // SUBLANES
    rho = np.arange(rows)
    time = (rho % SUBLANES) * steps + rho // SUBLANES
    perm = time[:, None] == np.arange(rows)[None, :]
    return jnp.asarray(perm, dtype=BF16), jnp.asarray(perm.T, dtype=BF16)


def _conv_segments(x, halo_ref, w_ref, b_ref):
    rows, width = x.shape
    sub0 = lax.broadcasted_iota(jnp.int32, (SUBLANES, width), 0) == 0
    halos = []
    for k in range(1, CONV_WIDTH):
        tail = pltpu.roll(x[rows - k * SUBLANES:rows - (k - 1) * SUBLANES, :], 1, 0)
        halos.append(jnp.where(sub0, halo_ref[k - 1], tail))
        halo_ref[k - 1] = tail
    y = b_ref[...] + w_ref[CONV_WIDTH - 1:CONV_WIDTH, :] * x
    for k in range(1, CONV_WIDTH):
        back = jnp.concatenate(halos[k - 1::-1] + [x[:rows - k * SUBLANES, :]], axis=0)
        y = y + w_ref[CONV_WIDTH - 1 - k:CONV_WIDTH - k, :] * back
    return y


LRU_TL = 512
LRU_SUB = 256


def _lru_body(x_ref, g_ref, w_ref, cw_ref, cb_ref, wr_ref, wi_ref, br_ref, bi_ref, lam_ref, perm_ref, unperm_ref,
              o_ref, hn_ref, halo_ref, carry_ref):
    first = pl.program_id(1) == 0
    w = o_ref.shape[1]
    sub = perm_ref.shape[0]
    steps = sub // SUBLANES

    @pl.when(first)
    def _():
        carry_ref[...] = jnp.zeros_like(carry_ref)
        halo_ref[...] = jnp.zeros_like(halo_ref)

    for t in range(o_ref.shape[0] // sub):
        rows = slice(t * sub, (t + 1) * sub)
        hn = _rmsnorm(x_ref[rows, :], g_ref[...]).astype(BF16)
        hn_ref[rows, :] = hn
        hn = jnp.dot(perm_ref[...], hn, preferred_element_type=F32).astype(BF16)
        proj = jnp.dot(hn, w_ref[...], preferred_element_type=F32)
        gate = proj[:, w:]

        xc = _conv_segments(proj[:, :w], halo_ref, cw_ref, cb_ref)
        xcb = xc.astype(BF16)
        pre_r, pre_i = [], []
        for s in range(w // GATE_SLAB):
            slab = xcb[:, s * GATE_SLAB:(s + 1) * GATE_SLAB]
            pre_r.append(jnp.dot(slab, wr_ref[s], preferred_element_type=F32))
            pre_i.append(jnp.dot(slab, wi_ref[s], preferred_element_type=F32))
        r = _sigmoid(jnp.concatenate(pre_r, axis=1) + br_ref[...])
        i = _sigmoid(jnp.concatenate(pre_i, axis=1) + bi_ref[...])
        log2_a = (-LRU_C * LOG2_E * r) * _softplus(-lam_ref[...])
        a = jnp.exp2(log2_a)
        z = 1.0 - a * a
        u = jnp.where(z > 0.0, z * lax.rsqrt(z), 0.0) * (i * xc)

        h_j = jnp.zeros((SUBLANES, w), F32)
        p_j = jnp.ones((SUBLANES, w), F32)
        hs, ps = [], []
        for j in range(steps):
            a_j = a[j * SUBLANES:(j + 1) * SUBLANES, :]
            h_j = a_j * h_j + u[j * SUBLANES:(j + 1) * SUBLANES, :]
            p_j = a_j * p_j
            hs.append(h_j)
            ps.append(p_j)
        c = carry_ref[0:1, :]
        entering = []
        for seg in range(SUBLANES):
            entering.append(c)
            c = p_j[seg:seg + 1, :] * c + h_j[seg:seg + 1, :]
        carry_ref[0:1, :] = c
        enter = jnp.concatenate(entering, axis=0)
        h = jnp.concatenate([hs[j] + ps[j] * enter for j in range(steps)], axis=0)
        out = (h * _gelu_tanh(gate)).astype(BF16)
        o_ref[rows, :] = jnp.dot(unperm_ref[...], out, preferred_element_type=F32).astype(BF16)


def _lru(x3, g, w_lru, cw, cb, wr_bd, wi_bd, br, bi, lam):
    b, l, d = x3.shape
    width = lam.shape[1]
    perms = _segment_perm(LRU_SUB)
    return pl.pallas_call(
        _lru_body,
        grid=(b, l // LRU_TL),
        in_specs=[pl.BlockSpec((None, LRU_TL, d), lambda bi_, li: (bi_, li, 0))]
        + [_resident(g.shape), w_lru[1]]
        + [_resident(a.shape) for a in (cw, cb, wr_bd, wi_bd, br, bi, lam, *perms)],
        out_specs=[pl.BlockSpec((None, LRU_TL, width), lambda bi_, li: (bi_, li, 0)),
                   pl.BlockSpec((None, LRU_TL, d), lambda bi_, li: (bi_, li, 0))],
        out_shape=[jax.ShapeDtypeStruct((b, l, width), BF16),
                   jax.ShapeDtypeStruct((b, l, d), BF16)],
        scratch_shapes=[
            pltpu.VMEM((CONV_WIDTH - 1, SUBLANES, width), F32),
            pltpu.VMEM((SUBLANES, width), F32),
        ],
        compiler_params=_params("parallel", "arbitrary"),
        name="lru",
    )(x3, g, w_lru[0], cw, cb, wr_bd, wi_bd, br, bi, lam, *perms)


def _block_diag_slabs(w):
    nb, bs, _ = w.shape
    per = GATE_SLAB // bs
    w = w.reshape(nb // per, per, bs, bs)
    eye = jnp.eye(per, dtype=w.dtype)
    bd = jnp.einsum("spij,pq->spiqj", w, eye).reshape(nb // per, GATE_SLAB, GATE_SLAB)
    return bd.astype(BF16)


def _ssd_body(n_heads, hn_ref, w_ref, cw_ref, cb_ref, dtb_ref, alog_ref, dvec_ref, nrm_ref,
              exp_ref, perm_ref, unperm_ref, o_ref, proj_ref, halo_ref, state_ref):
    first = pl.program_id(1) == 0
    t = SSD_CHUNK
    width = n_heads * SSD_HEAD_DIM
    gw = SSD_STATE
    conv_ch = width + 2 * SSD_GROUPS * gw
    n_chunks = hn_ref.shape[0] // t

    @pl.when(first)
    def _():
        state_ref[...] = jnp.zeros_like(state_ref)
        halo_ref[...] = jnp.zeros_like(halo_ref)

    hn = jnp.concatenate([jnp.dot(perm_ref[...], hn_ref[c * t:(c + 1) * t, :], preferred_element_type=F32).astype(BF16)
                          for c in range(n_chunks)], axis=0)
    proj_ref[...] = jnp.dot(hn, w_ref[...], preferred_element_type=F32)

    steps = t // SUBLANES
    time_of = lambda i: (i & (SUBLANES - 1)) * steps + lax.shift_right_logical(i, 3)
    causal = (time_of(lax.broadcasted_iota(jnp.int32, (t, t), 0))
              >= time_of(lax.broadcasted_iota(jnp.int32, (t, t), 1)))
    tri = jnp.where(causal, 1.0, 0.0).astype(BF16)
    lane = lax.broadcasted_iota(jnp.int32, (t, LANES), 1)
    lo = lane < SSD_HEAD_DIM
    rlo = lax.broadcasted_iota(jnp.int32, (LANES, gw), 0) < SSD_HEAD_DIM

    dt = _softplus(proj_ref[:, width + conv_ch:] + dtb_ref[...])
    adt = dt * (-LOG2_E * jnp.exp(alog_ref[...]))
    cs = jnp.concatenate([_split_dot(adt[c * t:(c + 1) * t, :], tri, parts=3, left=True)
                          for c in range(n_chunks)], axis=0)
    dt_w = _split_dot(dt, exp_ref[...], parts=2)
    cs_w = _split_dot(cs, exp_ref[...], parts=2)

    for c in range(n_chunks):
        rows = slice(c * t, (c + 1) * t)
        out = _ssd_chunk(proj_ref[rows, 0:width], proj_ref[rows, width:width + conv_ch],
                         cs[rows, :], dt_w[rows, :], cs_w[rows, :], n_heads, causal, lo, rlo, halo_ref, state_ref,
                         cw_ref, cb_ref, dvec_ref, nrm_ref)
        o_ref[rows, :] = jnp.dot(unperm_ref[...], out, preferred_element_type=F32).astype(BF16)


def _ssd_chunk(z, xbc_raw, cs, dt_w, cs_w, n_heads, causal, lo, rlo, halo_ref, state_ref,
               cw_ref, cb_ref, dvec_ref, nrm_ref):
    t = SSD_CHUNK
    width = n_heads * SSD_HEAD_DIM
    gw = SSD_STATE
    heads_per_group = n_heads // SSD_GROUPS
    xbc = _silu(_conv_segments(xbc_raw, halo_ref, cw_ref, cb_ref))
    xs = xbc[:, :width]
    cs_t = cs.T
    cs_last = cs[t - 1:t, :]
    cdec = jnp.exp2(cs_last)
    ecs_w = jnp.exp2(cs_w)
    dte_w = jnp.exp2(cs_w[t - 1:t, :] - cs_w)

    cbs = []
    for g in range(SSD_GROUPS):
        bm = xbc[:, width + g * gw: width + (g + 1) * gw].astype(BF16)
        cm = xbc[:, width + (SSD_GROUPS + g) * gw: width + (SSD_GROUPS + g + 1) * gw].astype(BF16)
        cb = lax.dot_general(cm, bm, (((1,), (1,)), ((), ())), preferred_element_type=F32)
        cbs.append((bm, cm, cb))

    ys = []
    for p in range(n_heads // 2):
        h0, h1 = 2 * p, 2 * p + 1
        bm, cm, cb = cbs[h0 // heads_per_group]
        sl = slice(p * LANES, (p + 1) * LANES)
        xs_p = xs[:, sl]
        xdt = xs_p * dt_w[:, sl]
        ms = []
        for h in (h0, h1):
            seg = cs[:, h:h + 1] - cs_t[h:h + 1, :]
            ms.append((cb * jnp.where(causal, jnp.exp2(seg), 0.0)).astype(BF16))
        m_cat = jnp.concatenate(ms, axis=1)
        xdt_bd = jnp.concatenate([jnp.where(lo, xdt, 0.0), jnp.where(lo, 0.0, xdt)], axis=0).astype(BF16)
        y = jnp.dot(m_cat, xdt_bd, preferred_element_type=F32)
        prev = state_ref[sl, :]
        y_off = lax.dot_general(cm, prev.astype(BF16), (((1,), (1,)), ((), ())), preferred_element_type=F32)
        y = y + y_off * ecs_w[:, sl]
        xw = (xdt * dte_w[:, sl]).astype(BF16)
        st = lax.dot_general(xw, bm, (((0,), (0,)), ((), ())), preferred_element_type=F32)
        dec = jnp.where(rlo, cdec[:, h0:h0 + 1], cdec[:, h1:h1 + 1])
        state_ref[sl, :] = prev * dec + st
        ys.append(y + dvec_ref[:, sl] * xs_p)

    y = jnp.concatenate(ys, axis=1) * _silu(z)
    gsz = width // SSD_GROUPS
    outs = []
    for g in range(SSD_GROUPS):
        yg = y[:, g * gsz:(g + 1) * gsz]
        outs.append(yg * lax.rsqrt(jnp.mean(yg * yg, axis=-1, keepdims=True) + NORM_EPS))
    return (jnp.concatenate(outs, axis=1) * nrm_ref[...]).astype(BF16)


SSD_TS = 512


def _ssd(hn3, w_ssd, cw, cb, dt_bias, a_log, d_vec, nrm, n_heads):
    b, l, d = hn3.shape
    width = n_heads * SSD_HEAD_DIM
    conv_ch = width + 2 * SSD_GROUPS * SSD_STATE
    expand = jnp.asarray(np.arange(LANES)[:, None] == np.arange(width)[None, :] // SSD_HEAD_DIM,
                         dtype=BF16)
    consts = (cw, cb, dt_bias, a_log, d_vec, nrm, expand, *_segment_perm(SSD_CHUNK))
    return pl.pallas_call(
        functools.partial(_ssd_body, n_heads),
        grid=(b, l // SSD_TS),
        in_specs=[pl.BlockSpec((None, SSD_TS, d), lambda bi_, ci: (bi_, ci, 0))]
        + [w_ssd[1]] + [_resident(a.shape) for a in consts],
        out_specs=pl.BlockSpec((None, SSD_TS, width), lambda bi_, ci: (bi_, ci, 0)),
        out_shape=jax.ShapeDtypeStruct((b, l, width), BF16),
        scratch_shapes=[
            pltpu.VMEM((SSD_TS, w_ssd[0].shape[-1]), F32),
            pltpu.VMEM((CONV_WIDTH - 1, SUBLANES, conv_ch), F32),
            pltpu.VMEM((width, SSD_STATE), F32),
        ],
        compiler_params=_params("parallel", "arbitrary"),
        name="ssd",
    )(hn3, w_ssd[0], *consts)


QKV_TM = 512
Q_SCALE = ATT_HEAD_DIM ** -0.5 * LOG2_E
PERM_ROWS = 256


def _split_dot(x, m, parts=2, left=False):
    acc, rem = None, x
    for k in range(parts):
        piece = rem.astype(BF16)
        term = jnp.dot(m, piece, preferred_element_type=F32) if left else jnp.dot(piece, m, preferred_element_type=F32)
        acc = term if acc is None else acc + term
        if k + 1 < parts:
            rem = rem - piece.astype(F32)
    return acc


def _slab_map(fn, x):
    return jnp.concatenate([fn(x[:, c * GATE_SLAB:(c + 1) * GATE_SLAB]) for c in range(x.shape[1] // GATE_SLAB)],
                           axis=1)


def _head_norm_rope(y, gain, cos, sin, gsum, rot):
    ss = _slab_map(lambda v: _split_dot(v, gsum, parts=1), y * y)
    yn = y * lax.rsqrt(ss * (1.0 / ATT_HEAD_DIM) + NORM_EPS) * gain
    partner = _slab_map(lambda v: _split_dot(v, rot), yn)
    tile = lambda tbl: jnp.concatenate([tbl] * (y.shape[1] // LANES), axis=1)
    return yn * tile(cos) + partner * tile(sin)


def _qkv_body(x_ref, g_ref, w_ref, qg_ref, kg_ref, cos_ref, sin_ref, gsum_ref, rot_ref, p1_ref, p4_ref, p16_ref,
              *refs):
    outs, h_ref = refs[:9], refs[9]
    j = pl.program_id(2)

    @pl.when(j == 0)
    def _():
        h_ref[...] = _rmsnorm(x_ref[...], g_ref[...]).astype(BF16)

    sub = PERM_ROWS

    def section(o1, o4, o16, gain_ref=None, grouped=False):
        y = jnp.dot(h_ref[...], w_ref[...], preferred_element_type=F32)
        if gain_ref is not None:
            y = _head_norm_rope(y, gain_ref[...], cos_ref[...], sin_ref[...], gsum_ref[...], rot_ref[...])
        y_all = y.astype(BF16)
        for t in range(x_ref.shape[0] // sub):
            rows = slice(t * sub, (t + 1) * sub)
            yb = y_all[rows, :]
            if grouped:
                o1[rows, :] = jnp.dot(p1_ref[...], yb, preferred_element_type=F32).astype(BF16)
            else:
                o1[rows, :] = yb
            for perm_ref, o in ((p4_ref, o4), (p16_ref, o16)):
                dil = o.shape[0]
                n = sub // dil
                yp = jnp.dot(perm_ref[...], yb, preferred_element_type=F32).astype(BF16)
                for r in range(dil):
                    o[r, t * n:(t + 1) * n, :] = yp[r * n:(r + 1) * n, :]

    @pl.when(j == 0)
    def _():
        section(*outs[0:3], gain_ref=qg_ref, grouped=True)

    @pl.when(j == 1)
    def _():
        section(*outs[3:6], gain_ref=kg_ref)

    @pl.when(j == 2)
    def _():
        section(*outs[6:9])


def _qkv(x3, g, w_qkv, layer, qg, kg, cos, sin):
    b, l, d = x3.shape
    tm = QKV_TM
    out_specs, out_shape = [], []
    for _ in range(3):
        out_specs.append(pl.BlockSpec((None, tm, d), lambda bi_, i, j: (bi_, i, 0)))
        out_shape.append(jax.ShapeDtypeStruct((b, l, d), BF16))
        for dil in DILATIONS[1:]:
            out_specs.append(pl.BlockSpec((None, dil, tm // dil, d), lambda bi_, i, j: (bi_, 0, i, 0)))
            out_shape.append(jax.ShapeDtypeStruct((b, dil, l // dil, d), BF16))
    consts = _qkv_constants(tm)
    tbl = lambda: pl.BlockSpec((tm, LANES), lambda bi_, i, j: (i, 0))
    return pl.pallas_call(
        _qkv_body,
        grid=(b, l // tm, 3),
        in_specs=[
            pl.BlockSpec((None, tm, d), lambda bi_, i, j: (bi_, i, 0)),
            _resident((1, d)),
            pl.BlockSpec((None, d, d), lambda bi_, i, j: (layer, 0, j)),
            _resident((1, d)), _resident((1, d)), tbl(), tbl(),
        ] + [_resident(c.shape) for c in consts],
        out_specs=out_specs,
        out_shape=out_shape,
        scratch_shapes=[pltpu.VMEM((tm, d), BF16)],
        compiler_params=_params("parallel", "parallel", "arbitrary"),
        name="qkv",
    )(x3, g, w_qkv, qg, kg, cos, sin, *consts)


def _qkv_constants(tm):
    lane = np.arange(GATE_SLAB)
    gsum = (lane[:, None] // ATT_HEAD_DIM == lane[None, :] // ATT_HEAD_DIM)
    half = ROPE_DIM // 2
    pos = lane % ATT_HEAD_DIM
    partner = np.where(pos < half, lane + half, np.where(pos < ROPE_DIM, lane - half, -1))
    rot = lane[:, None] == partner[None, :]
    out_row = np.arange(PERM_ROWS)
    group = SPAN // ACC_DIL
    within = out_row % SPAN
    src = out_row - within + (within % group) * ACC_DIL + within // group
    perms = [src[:, None] == np.arange(PERM_ROWS)[None, :]]
    for dil in DILATIONS[1:]:
        sub = PERM_ROWS
        rows = sub // dil
        assert rows % BF16_ROWS == 0 and tm % sub == 0
        out_row = np.arange(sub)
        src = (out_row % rows) * dil + out_row // rows
        perms.append(src[:, None] == np.arange(sub)[None, :])
    return tuple(jnp.asarray(m, dtype=BF16) for m in (gsum, rot, *perms))


def _rope_tables(l):
    half = ROPE_DIM // 2
    pos = jnp.arange(l, dtype=F32)
    inv = ROPE_THETA ** (-2.0 * jnp.arange(half, dtype=F32) / ROPE_DIM)
    ang = pos[:, None] * inv[None, :]
    cos, sin = jnp.cos(ang), jnp.sin(ang)
    pad = ATT_HEAD_DIM - ROPE_DIM
    cos_h = jnp.concatenate([cos, cos, jnp.ones((l, pad), F32)], axis=1)
    sin_h = jnp.concatenate([-sin, sin, jnp.zeros((l, pad), F32)], axis=1)
    rep = LANES // ATT_HEAD_DIM
    return tuple(jnp.concatenate([tb] * rep, axis=1) for tb in (cos_h, sin_h))


def _attn_body(q1, k1c, k1p, v1c, v1p, q4, k4c, k4p, v4c, v4p, q16, k16c, k16p, v16c, v16p,
               o_ref, kb1, vb1, kb4, vb4, kb16, vb16, s_scr, p_scr, mx_scr, bias_scr, tok_scr, acc_ref, m_ref, l_ref):
    first = pl.program_id(2) == 0
    n_pairs = acc_ref.shape[0]
    n_blocks = SUPER // SPAN
    trans_b = (((1,), (1,)), ((), ()))

    def gather_rows(pairs):
        for buf, prev, cur in pairs:
            if len(prev.shape) == 2:
                buf[0, 0:SPAN, :] = prev[...]
                buf[0, SPAN:, :] = cur[...]
            else:
                buf[:, 0:SPAN, :] = prev[...]
                buf[:, SPAN:, :] = cur[...]

    gather_rows(((kb1, k1p, k1c), (vb1, v1p, v1c)))

    row = lax.broadcasted_iota(jnp.int32, (SPAN, 2 * SPAN), 0)
    kj = lax.broadcasted_iota(jnp.int32, (SPAN, 2 * SPAN), 1)
    group = SPAN // ACC_DIL
    for base, qi in ((0, row), (2, (row % group) * ACC_DIL + row // group)):
        band = (kj >= qi) & (kj <= qi + SPAN)
        bias_scr[base] = jnp.where(band, 0.0, -jnp.inf)
        bias_scr[base + 1] = jnp.where(band & (kj >= SPAN), 0.0, -jnp.inf)

    lo = lax.broadcasted_iota(jnp.int32, (SPAN, LANES), 1) < ATT_HEAD_DIM
    ones_v = jnp.ones((2 * SPAN, LANES), BF16)
    lanes = lambda p: slice(p * LANES, (p + 1) * LANES)

    def run_pattern(dil, q_get, kb, vb):
        per_res = SUPER // (SPAN * dil)
        overwrite = dil == DILATIONS[0]

        def split(f):
            if per_res == 1:
                return f, 0
            if dil == 1:
                return 0, f
            if isinstance(f, int):
                return f // per_res, f % per_res
            return lax.shift_right_logical(f, per_res.bit_length() - 1), f & (per_res - 1)

        def slab_rows(nb):
            return pl.ds(nb * SPAN if isinstance(nb, int) else pl.multiple_of(nb * SPAN, SPAN), 2 * SPAN)

        def stage_scores(f, slot):
            r, nb = split(f)
            if isinstance(nb, int):
                flag = jnp.where(first, 1, 0) if nb == 0 else 0
            else:
                flag = jnp.where(first & (nb == 0), 1, 0)
            bias = bias_scr[flag + (2 if overwrite else 0)]
            bias2 = jnp.concatenate([bias, bias], axis=0)
            q = q_get(r, nb)
            ks = kb[r, slab_rows(nb), :]
            for p in range(n_pairs):
                qp = q[:, lanes(p)]
                zero = jnp.zeros_like(qp)
                q2 = jnp.concatenate([jnp.where(lo, qp, zero), jnp.where(lo, zero, qp)], axis=0)
                s_scr[slot, p] = lax.dot_general(q2, ks[:, lanes(p)], trans_b, preferred_element_type=F32) + bias2

        def stage_softmax(slot):
            for p in range(n_pairs):
                s = s_scr[slot, p]
                mx = jnp.max(s, axis=1, keepdims=True)
                p_scr[slot, p] = jnp.exp2(s - mx).astype(BF16)
                mx_scr[slot, p] = jnp.where(lo, mx[:SPAN], mx[SPAN:])

        def stage_values(f, slot):
            r, nb = split(f)
            vs = vb[r, slab_rows(nb), :]
            if overwrite:
                grp = SPAN // ACC_DIL
                start = nb * grp if isinstance(nb, int) else pl.multiple_of(nb * grp, grp)
                pieces = [(slice(c * grp, (c + 1) * grp), c, pl.ds(start, grp)) for c in range(ACC_DIL)]
            elif dil == ACC_DIL:
                pieces = [(slice(0, SPAN), r, q_rows(nb))]
            else:
                sub = dil // ACC_DIL
                pieces = [(slice(0, SPAN), r & (ACC_DIL - 1),
                           pl.ds(lax.shift_right_logical(r, 2) if not isinstance(r, int) else r // ACC_DIL,
                                 SPAN, stride=sub))]
            for p in range(n_pairs):
                v_aug = jnp.concatenate([vs[:, lanes(p)], ones_v], axis=1)
                pv = jnp.dot(p_scr[slot, p], v_aug, preferred_element_type=F32)
                o_blk = jnp.where(lo, pv[:SPAN, :LANES], pv[SPAN:, :LANES])
                l_blk = jnp.where(lo, pv[:SPAN, LANES:], pv[SPAN:, LANES:])
                m_blk = mx_scr[slot, p]
                for src, cls, rows in pieces:
                    m_new, l_new, o_new = m_blk[src], l_blk[src], o_blk[src]
                    if overwrite:
                        m_ref[p, cls, rows, :], l_ref[p, cls, rows, :], acc_ref[p, cls, rows, :] = m_new, l_new, o_new
                        continue
                    m_old = m_ref[p, cls, rows, :]
                    m = jnp.maximum(m_old, m_new)
                    e_old = jnp.exp2(m_old - m)
                    e_new = jnp.exp2(m_new - m)
                    l_ref[p, cls, rows, :] = e_old * l_ref[p, cls, rows, :] + e_new * l_new
                    acc_ref[p, cls, rows, :] = e_old * acc_ref[p, cls, rows, :] + e_new * o_new
                    m_ref[p, cls, rows, :] = m

        stage_scores(0, 0)
        stage_scores(1, 1)
        stage_softmax(0)

        def body(i, c):
            f = 2 * i + 2
            stage_scores(f, 0)
            stage_softmax(1)
            stage_values(f - 2, 0)
            stage_scores(f + 1, 1)
            stage_softmax(0)
            stage_values(f - 1, 1)
            return c

        lax.fori_loop(0, (n_blocks - 2) // 2, body, 0)
        stage_softmax(1)
        stage_values(n_blocks - 2, 0)
        stage_values(n_blocks - 1, 1)

    def q_rows(nb):
        return pl.ds(nb * SPAN if isinstance(nb, int) else pl.multiple_of(nb * SPAN, SPAN), SPAN)

    run_pattern(DILATIONS[0], lambda r, nb: q1[q_rows(nb), :], kb1, vb1)
    gather_rows(((kb4, k4p, k4c), (vb4, v4p, v4c)))
    run_pattern(DILATIONS[1], lambda r, nb: q4[r, q_rows(nb), :], kb4, vb4)
    gather_rows(((kb16, k16p, k16c), (vb16, v16p, v16c)))
    run_pattern(DILATIONS[2], lambda r, nb: q16[r], kb16, vb16)

    for p in range(n_pairs):
        for cls in range(ACC_DIL):
            tok_scr[p, pl.ds(cls, SUPER // ACC_DIL, stride=ACC_DIL), :] = acc_ref[p, cls] / l_ref[p, cls]
        o_ref[:, lanes(p)] = tok_scr[p].astype(BF16)


def _attention(qkv_outs):
    q1, q4, q16, k1, k4, k16, v1, v4, v16 = qkv_outs
    b, l, d = q1.shape
    w = ATT_LANES
    n_pairs = w // LANES
    d4, d16 = DILATIONS[1], DILATIONS[2]
    cur1 = pl.BlockSpec((None, SUPER, w), lambda bi_, g, s: (bi_, s, g))
    prev1 = pl.BlockSpec((None, SPAN, w), lambda bi_, g, s: (bi_, jnp.maximum(s * (SUPER // SPAN) - 1, 0), g))
    cur4 = pl.BlockSpec((None, d4, SUPER // d4, w), lambda bi_, g, s: (bi_, 0, s, g))
    prev4 = pl.BlockSpec((None, d4, SPAN, w),
                         lambda bi_, g, s: (bi_, 0, jnp.maximum(s * (SUPER // (d4 * SPAN)) - 1, 0), g))
    cur16 = pl.BlockSpec((None, d16, SPAN, w), lambda bi_, g, s: (bi_, 0, s, g))
    prev16 = pl.BlockSpec((None, d16, SPAN, w), lambda bi_, g, s: (bi_, 0, jnp.maximum(s - 1, 0), g))
    return pl.pallas_call(
        _attn_body,
        grid=(b, d // w, l // SUPER),
        in_specs=[cur1, cur1, prev1, cur1, prev1,
                  cur4, cur4, prev4, cur4, prev4,
                  cur16, cur16, prev16, cur16, prev16],
        out_specs=pl.BlockSpec((None, SUPER, w), lambda bi_, g, s: (bi_, s, g)),
        out_shape=jax.ShapeDtypeStruct((b, l, d), BF16),
        scratch_shapes=[
            pltpu.VMEM((1, SPAN + SUPER, w), BF16), pltpu.VMEM((1, SPAN + SUPER, w), BF16),
            pltpu.VMEM((d4, SPAN + SUPER // d4, w), BF16), pltpu.VMEM((d4, SPAN + SUPER // d4, w), BF16),
            pltpu.VMEM((d16, 2 * SPAN, w), BF16), pltpu.VMEM((d16, 2 * SPAN, w), BF16),
            pltpu.VMEM((2, n_pairs, 2 * SPAN, 2 * SPAN), F32),
            pltpu.VMEM((2, n_pairs, 2 * SPAN, 2 * SPAN), BF16),
            pltpu.VMEM((2, n_pairs, SPAN, LANES), F32),
            pltpu.VMEM((4, SPAN, 2 * SPAN), F32),
            pltpu.VMEM((n_pairs, SUPER, LANES), F32),
        ] + [pltpu.VMEM((n_pairs, ACC_DIL, SUPER // ACC_DIL, LANES), F32)] * 3,
        compiler_params=_params("parallel", "parallel", "arbitrary"),
        name="dilated_attention",
    )(q1, k1, k1, v1, v1, q4, k4, k4, v4, v4, q16, k16, k16, v16, v16)


def _recurrent_layer(x, b, l, i, norm_g, w_lru, w_ssd, w_out, lru_conv_w, lru_conv_b, lru_w_r, lru_b_r, lru_w_i,
                     lru_b_i, lru_lambda, ssd_conv_w, ssd_conv_b, ssd_dt_bias, ssd_a_log, ssd_d, ssd_norm):
    d = x.shape[1]
    width = lru_lambda.shape[0]
    n_heads = ssd_a_log.shape[0]
    row = lambda v: v.reshape(1, -1)
    pad_lanes = lambda v: jnp.pad(v.reshape(1, -1), ((0, 0), (0, LANES - v.shape[0])))
    x3 = x.reshape(b, l, d)
    out_a, hn3 = _lru(x3, row(norm_g), _layer_block(w_lru, i, (d, 2 * width)), lru_conv_w, row(lru_conv_b),
                      _block_diag_slabs(lru_w_r), _block_diag_slabs(lru_w_i), row(lru_b_r), row(lru_b_i),
                      row(lru_lambda))
    out_b = _ssd(hn3, _layer_block(w_ssd, i), ssd_conv_w, row(ssd_conv_b), pad_lanes(ssd_dt_bias),
                 pad_lanes(ssd_a_log), row(jnp.repeat(ssd_d, SSD_HEAD_DIM)), row(ssd_norm), n_heads)
    return ([out_a.reshape(b * l, width), out_b.reshape(b * l, -1)],
            [_layer_block(w_out, i, (width, d), (0, 0)), _layer_block(w_out, i, (width, d), (1, 0))])


def _attention_layer(x, b, l, i, norm_g, w_qkv, w_out, q_norm, k_norm):
    d = x.shape[1]
    heads = d // ATT_HEAD_DIM
    row = lambda v: v.reshape(1, -1)
    outs = _qkv(x.reshape(b, l, d), row(norm_g), w_qkv, i,
                row(jnp.tile(q_norm, heads) * Q_SCALE), row(jnp.tile(k_norm, heads)), *_rope_tables(l))
    o = _attention(outs)
    return [o.reshape(b * l, d)], [_layer_block(w_out, i)]


def kernel(x, rec_norm, rec_w_in, lru_conv_w, lru_conv_b, lru_w_r, lru_b_r, lru_w_i, lru_b_i, lru_lambda,
           ssd_conv_w, ssd_conv_b, ssd_dt_bias, ssd_a_log, ssd_d, ssd_norm, rec_w_out, att_norm, att_w_qkv,
           att_q_norm, att_k_norm, att_w_out, ffn_norm, ffn_w_gate_up, ffn_w_down):
    b, l, d = x.shape
    depth = ffn_norm.shape[0]
    h = x.reshape(b * l, d)
    lru_cols = 2 * lru_lambda.shape[1]
    w_lru = rec_w_in.astype(BF16)
    dt_pad = LANES - ssd_a_log.shape[1]
    w_ssd = jnp.pad(rec_w_in[:, :, lru_cols:], ((0, 0), (0, 0), (0, dt_pad))).astype(BF16)
    rec_wo, att_wqkv, att_wo = rec_w_out.astype(BF16), att_w_qkv.astype(BF16), att_w_out.astype(BF16)
    ffn_wgu, ffn_wd = ffn_w_gate_up.astype(BF16), ffn_w_down.astype(BF16)
    for layer in range(depth):
        i = layer // 2
        if layer % 2 == 0:
            mixes, wos = _recurrent_layer(h, b, l, i, rec_norm[i], w_lru, w_ssd, rec_wo, lru_conv_w[i],
                                          lru_conv_b[i], lru_w_r[i], lru_b_r[i], lru_w_i[i], lru_b_i[i],
                                          lru_lambda[i], ssd_conv_w[i], ssd_conv_b[i], ssd_dt_bias[i],
                                          ssd_a_log[i], ssd_d[i], ssd_norm[i])
        else:
            mixes, wos = _attention_layer(h, b, l, i, att_norm[i], att_wqkv, att_wo, att_q_norm[i], att_k_norm[i])
        h = _mix_ffn(h, mixes, wos, ffn_norm[layer].reshape(1, d), _layer_block(ffn_wgu, layer),
                     _layer_block(ffn_wd, layer))
    return h.reshape(b, l, d)
```

```python
import functools
import math

import jax
import jax.numpy as jnp
import numpy as np
from jax import lax
from jax.experimental import pallas as pl
from jax.experimental.pallas import tpu as pltpu

F32 = jnp.float32
BF16 = jnp.bfloat16

NORM_EPS = 1e-6
LOG2_E = math.log2(math.e)
CONV_WIDTH = 4
LRU_BLOCK = 64
LRU_C = 8.0
GATE_SLAB = 256
SSD_HEAD_DIM = 64
SSD_GROUPS = 2
SSD_STATE = 128
SSD_CHUNK = 128
ATT_HEAD_DIM = 64
ROPE_DIM = 16
ROPE_THETA = 500000.0
SPAN = 128
DILATIONS = (1, 4, 16)
SUPER = SPAN * DILATIONS[-1]
ATT_LANES = 256
ACC_DIL = DILATIONS[1]
LANES = 128
SUBLANES = 8
BF16_ROWS = 16
VMEM_LIMIT_BYTES = 56 * 1024 * 1024


def _params(*semantics):
    return pltpu.CompilerParams(dimension_semantics=semantics, vmem_limit_bytes=VMEM_LIMIT_BYTES)


def _rmsnorm(x, g):
    ms = jnp.mean(x * x, axis=-1, keepdims=True)
    return x * lax.rsqrt(ms + NORM_EPS) * g


def _sigmoid(x):
    return 1.0 / (1.0 + jnp.exp2(x * (-LOG2_E)))


def _silu(x):
    return x * _sigmoid(x)


def _softplus(x):
    return jnp.maximum(x, 0.0) + jnp.log1p(jnp.exp(-jnp.abs(x)))


def _gelu_tanh(x):
    c = math.sqrt(2.0 / math.pi)
    half_x = 0.5 * x
    return half_x + half_x * jnp.tanh(x * (c + (c * 0.044715) * (x * x)))


FFN_TM = 1024
FFN_TH = 256


def _ffn_body(n_mix, x_ref, *refs):
    mix_refs, wo_refs = refs[:n_mix], refs[n_mix:2 * n_mix]
    g_ref, wgu_ref, wd_ref, o_ref, h_ref, act_ref = refs[2 * n_mix:]
    x1 = x_ref[...]
    for mix_ref, wo_ref in zip(mix_refs, wo_refs):
        x1 = x1 + jnp.dot(mix_ref[...], wo_ref[...], preferred_element_type=F32)
    o_ref[...] = x1
    h_ref[...] = _rmsnorm(x1, g_ref[...]).astype(BF16)
    hid = act_ref.shape[1]
    for c in range(hid // FFN_TH):
        cols = slice(c * FFN_TH, (c + 1) * FFN_TH)
        up_cols = slice(hid + c * FFN_TH, hid + (c + 1) * FFN_TH)
        gate = jnp.dot(h_ref[...], wgu_ref[:, cols], preferred_element_type=F32)
        up = jnp.dot(h_ref[...], wgu_ref[:, up_cols], preferred_element_type=F32)
        act_ref[:, cols] = (_silu(gate) * up).astype(BF16)
    o_ref[...] += jnp.dot(act_ref[...], wd_ref[...], preferred_element_type=F32)


def _resident(shape):
    return pl.BlockSpec(shape, lambda *_: (0,) * len(shape), pipeline_mode=pl.Buffered(1))


def _layer_block(stack, layer, block=None, index=None):
    block = tuple(stack.shape[1:]) if block is None else tuple(block)
    index = (layer,) + (tuple(index) if index is not None else (0,) * len(block))
    return stack, pl.BlockSpec((None,) + block, lambda *_: index, pipeline_mode=pl.Buffered(1))


def _mix_ffn(x, mixes, wos, g, wgu, wd):
    t, d = x.shape
    hid = wd[0].shape[1]
    weights = (*wos, (g, _resident(g.shape)), wgu, wd)
    rows = lambda a: pl.BlockSpec((FFN_TM, a.shape[1]), lambda i: (i, 0))
    return pl.pallas_call(
        functools.partial(_ffn_body, len(mixes)),
        grid=(t // FFN_TM,),
        in_specs=[rows(x)] + [rows(a) for a in mixes] + [spec for _, spec in weights],
        out_specs=rows(x),
        out_shape=jax.ShapeDtypeStruct((t, d), F32),
        scratch_shapes=[pltpu.VMEM((FFN_TM, d), BF16), pltpu.VMEM((FFN_TM, hid), BF16)],
        compiler_params=_params("parallel"),
        name="mix_ffn",
    )(x, *mixes, *[a for a, _ in weights])


def _segment_perm(rows):
    steps = rows // SUBLANES
    rho = np.arange(rows)
    time = (rho % SUBLANES) * steps + rho // SUBLANES
    perm = time[:, None] == np.arange(rows)[None, :]
    return jnp.asarray(perm, dtype=BF16), jnp.asarray(perm.T, dtype=BF16)


def _conv_segments(x, halo_ref, w_ref, b_ref):
    rows, width = x.shape
    sub0 = lax.broadcasted_iota(jnp.int32, (SUBLANES, width), 0) == 0
    halos = []
    for k in range(1, CONV_WIDTH):
        tail = pltpu.roll(x[rows - k * SUBLANES:rows - (k - 1) * SUBLANES, :], 1, 0)
        halos.append(jnp.where(sub0, halo_ref[k - 1], tail))
        halo_ref[k - 1] = tail
    y = b_ref[...] + w_ref[CONV_WIDTH - 1:CONV_WIDTH, :] * x
    for k in range(1, CONV_WIDTH):
        back = jnp.concatenate(halos[k - 1::-1] + [x[:rows - k * SUBLANES, :]], axis=0)
        y = y + w_ref[CONV_WIDTH - 1 - k:CONV_WIDTH - k, :] * back
    return y


LRU_TL = 512
LRU_SUB = 256


def _lru_body(x_ref, g_ref, w_ref, cw_ref, cb_ref, wr_ref, wi_ref, br_ref, bi_ref, lam_ref, perm_ref, unperm_ref,
              o_ref, hn_ref, halo_ref, carry_ref):
    first = pl.program_id(1) == 0
    w = o_ref.shape[1]
    sub = perm_ref.shape[0]
    steps = sub // SUBLANES

    @pl.when(first)
    def _():
        carry_ref[...] = jnp.zeros_like(carry_ref)
        halo_ref[...] = jnp.zeros_like(halo_ref)

    for t in range(o_ref.shape[0] // sub):
        rows = slice(t * sub, (t + 1) * sub)
        hn = _rmsnorm(x_ref[rows, :], g_ref[...]).astype(BF16)
        hn_ref[rows, :] = hn
        hn = jnp.dot(perm_ref[...], hn, preferred_element_type=F32).astype(BF16)
        proj = jnp.dot(hn, w_ref[...], preferred_element_type=F32)
        gate = proj[:, w:]

        xc = _conv_segments(proj[:, :w], halo_ref, cw_ref, cb_ref)
        xcb = xc.astype(BF16)
        pre_r, pre_i = [], []
        for s in range(w // GATE_SLAB):
            slab = xcb[:, s * GATE_SLAB:(s + 1) * GATE_SLAB]
            pre_r.append(jnp.dot(slab, wr_ref[s], preferred_element_type=F32))
            pre_i.append(jnp.dot(slab, wi_ref[s], preferred_element_type=F32))
        r = _sigmoid(jnp.concatenate(pre_r, axis=1) + br_ref[...])
        i = _sigmoid(jnp.concatenate(pre_i, axis=1) + bi_ref[...])
        log2_a = (-LRU_C * LOG2_E * r) * _softplus(-lam_ref[...])
        a = jnp.exp2(log2_a)
        z = 1.0 - a * a
        u = jnp.where(z > 0.0, z * lax.rsqrt(z), 0.0) * (i * xc)

        h_j = jnp.zeros((SUBLANES, w), F32)
        p_j = jnp.ones((SUBLANES, w), F32)
        hs, ps = [], []
        for j in range(steps):
            a_j = a[j * SUBLANES:(j + 1) * SUBLANES, :]
            h_j = a_j * h_j + u[j * SUBLANES:(j + 1) * SUBLANES, :]
            p_j = a_j * p_j
            hs.append(h_j)
            ps.append(p_j)
        c = carry_ref[0:1, :]
        entering = []
        for seg in range(SUBLANES):
            entering.append(c)
            c = p_j[seg:seg + 1, :] * c + h_j[seg:seg + 1, :]
        carry_ref[0:1, :] = c
        enter = jnp.concatenate(entering, axis=0)
        h = jnp.concatenate([hs[j] + ps[j] * enter for j in range(steps)], axis=0)
        out = (h * _gelu_tanh(gate)).astype(BF16)
        o_ref[rows, :] = jnp.dot(unperm_ref[...], out, preferred_element_type=F32).astype(BF16)


def _lru(x3, g, w_lru, cw, cb, wr_bd, wi_bd, br, bi, lam):
    b, l, d = x3.shape
    width = lam.shape[1]
    perms = _segment_perm(LRU_SUB)
    return pl.pallas_call(
        _lru_body,
        grid=(b, l // LRU_TL),
        in_specs=[pl.BlockSpec((None, LRU_TL, d), lambda bi_, li: (bi_, li, 0))]
        + [_resident(g.shape), w_lru[1]]
        + [_resident(a.shape) for a in (cw, cb, wr_bd, wi_bd, br, bi, lam, *perms)],
        out_specs=[pl.BlockSpec((None, LRU_TL, width), lambda bi_, li: (bi_, li, 0)),
                   pl.BlockSpec((None, LRU_TL, d), lambda bi_, li: (bi_, li, 0))],
        out_shape=[jax.ShapeDtypeStruct((b, l, width), BF16),
                   jax.ShapeDtypeStruct((b, l, d), BF16)],
        scratch_shapes=[
            pltpu.VMEM((CONV_WIDTH - 1, SUBLANES, width), F32),
            pltpu.VMEM((SUBLANES, width), F32),
        ],
        compiler_params=_params("parallel", "arbitrary"),
        name="lru",
    )(x3, g, w_lru[0], cw, cb, wr_bd, wi_bd, br, bi, lam, *perms)


def _block_diag_slabs(w):
    nb, bs, _ = w.shape
    per = GATE_SLAB // bs
    w = w.reshape(nb // per, per, bs, bs)
    eye = jnp.eye(per, dtype=w.dtype)
    bd = jnp.einsum("spij,pq->spiqj", w, eye).reshape(nb // per, GATE_SLAB, GATE_SLAB)
    return bd.astype(BF16)


def _ssd_body(n_heads, hn_ref, w_ref, cw_ref, cb_ref, dtb_ref, alog_ref, dvec_ref, nrm_ref,
              exp_ref, perm_ref, unperm_ref, o_ref, proj_ref, halo_ref, state_ref):
    first = pl.program_id(1) == 0
    t = SSD_CHUNK
    width = n_heads * SSD_HEAD_DIM
    gw = SSD_STATE
    conv_ch = width + 2 * SSD_GROUPS * gw
    n_chunks = hn_ref.shape[0] // t

    @pl.when(first)
    def _():
        state_ref[...] = jnp.zeros_like(state_ref)
        halo_ref[...] = jnp.zeros_like(halo_ref)

    hn = jnp.concatenate([jnp.dot(perm_ref[...], hn_ref[c * t:(c + 1) * t, :], preferred_element_type=F32).astype(BF16)
                          for c in range(n_chunks)], axis=0)
    proj_ref[...] = jnp.dot(hn, w_ref[...], preferred_element_type=F32)

    steps = t // SUBLANES
    time_of = lambda i: (i & (SUBLANES - 1)) * steps + lax.shift_right_logical(i, 3)
    causal = (time_of(lax.broadcasted_iota(jnp.int32, (t, t), 0))
              >= time_of(lax.broadcasted_iota(jnp.int32, (t, t), 1)))
    tri = jnp.where(causal, 1.0, 0.0).astype(BF16)
    lane = lax.broadcasted_iota(jnp.int32, (t, LANES), 1)
    lo = lane < SSD_HEAD_DIM
    rlo = lax.broadcasted_iota(jnp.int32, (LANES, gw), 0) < SSD_HEAD_DIM

    dt = _softplus(proj_ref[:, width + conv_ch:] + dtb_ref[...])
    adt = dt * (-LOG2_E * jnp.exp(alog_ref[...]))
    cs = jnp.concatenate([_split_dot(adt[c * t:(c + 1) * t, :], tri, parts=3, left=True)
                          for c in range(n_chunks)], axis=0)
    dt_w = _split_dot(dt, exp_ref[...], parts=2)
    cs_w = _split_dot(cs, exp_ref[...], parts=2)

    for c in range(n_chunks):
        rows = slice(c * t, (c + 1) * t)
        out = _ssd_chunk(proj_ref[rows, 0:width], proj_ref[rows, width:width + conv_ch],
                         cs[rows, :], dt_w[rows, :], cs_w[rows, :], n_heads, causal, lo, rlo, halo_ref, state_ref,
                         cw_ref, cb_ref, dvec_ref, nrm_ref)
        o_ref[rows, :] = jnp.dot(unperm_ref[...], out, preferred_element_type=F32).astype(BF16)


def _ssd_chunk(z, xbc_raw, cs, dt_w, cs_w, n_heads, causal, lo, rlo, halo_ref, state_ref,
               cw_ref, cb_ref, dvec_ref, nrm_ref):
    t = SSD_CHUNK
    width = n_heads * SSD_HEAD_DIM
    gw = SSD_STATE
    heads_per_group = n_heads // SSD_GROUPS
    xbc = _silu(_conv_segments(xbc_raw, halo_ref, cw_ref, cb_ref))
    xs = xbc[:, :width]
    cs_t = cs.T
    cs_last = cs[t - 1:t, :]
    cdec = jnp.exp2(cs_last)
    ecs_w = jnp.exp2(cs_w)
    dte_w = jnp.exp2(cs_w[t - 1:t, :] - cs_w)

    cbs = []
    for g in range(SSD_GROUPS):
        bm = xbc[:, width + g * gw: width + (g + 1) * gw].astype(BF16)
        cm = xbc[:, width + (SSD_GROUPS + g) * gw: width + (SSD_GROUPS + g + 1) * gw].astype(BF16)
        cb = lax.dot_general(cm, bm, (((1,), (1,)), ((), ())), preferred_element_type=F32)
        cbs.append((bm, cm, cb))

    ys = []
    for p in range(n_heads // 2):
        h0, h1 = 2 * p, 2 * p + 1
        bm, cm, cb = cbs[h0 // heads_per_group]
        sl = slice(p * LANES, (p + 1) * LANES)
        xs_p = xs[:, sl]
        xdt = xs_p * dt_w[:, sl]
        ms = []
        for h in (h0, h1):
            seg = cs[:, h:h + 1] - cs_t[h:h + 1, :]
            ms.append((cb * jnp.where(causal, jnp.exp2(seg), 0.0)).astype(BF16))
        m_cat = jnp.concatenate(ms, axis=1)
        xdt_bd = jnp.concatenate([jnp.where(lo, xdt, 0.0), jnp.where(lo, 0.0, xdt)], axis=0).astype(BF16)
        y = jnp.dot(m_cat, xdt_bd, preferred_element_type=F32)
        prev = state_ref[sl, :]
        y_off = lax.dot_general(cm, prev.astype(BF16), (((1,), (1,)), ((), ())), preferred_element_type=F32)
        y = y + y_off * ecs_w[:, sl]
        xw = (xdt * dte_w[:, sl]).astype(BF16)
        st = lax.dot_general(xw, bm, (((0,), (0,)), ((), ())), preferred_element_type=F32)
        dec = jnp.where(rlo, cdec[:, h0:h0 + 1], cdec[:, h1:h1 + 1])
        state_ref[sl, :] = prev * dec + st
        ys.append(y + dvec_ref[:, sl] * xs_p)

    y = jnp.concatenate(ys, axis=1) * _silu(z)
    gsz = width // SSD_GROUPS
    outs = []
    for g in range(SSD_GROUPS):
        yg = y[:, g * gsz:(g + 1) * gsz]
        outs.append(yg * lax.rsqrt(jnp.mean(yg * yg, axis=-1, keepdims=True) + NORM_EPS))
    return (jnp.concatenate(outs, axis=1) * nrm_ref[...]).astype(BF16)


SSD_TS = 512


def _ssd(hn3, w_ssd, cw, cb, dt_bias, a_log, d_vec, nrm, n_heads):
    b, l, d = hn3.shape
    width = n_heads * SSD_HEAD_DIM
    conv_ch = width + 2 * SSD_GROUPS * SSD_STATE
    expand = jnp.asarray(np.arange(LANES)[:, None] == np.arange(width)[None, :] // SSD_HEAD_DIM,
                         dtype=BF16)
    consts = (cw, cb, dt_bias, a_log, d_vec, nrm, expand, *_segment_perm(SSD_CHUNK))
    return pl.pallas_call(
        functools.partial(_ssd_body, n_heads),
        grid=(b, l // SSD_TS),
        in_specs=[pl.BlockSpec((None, SSD_TS, d), lambda bi_, ci: (bi_, ci, 0))]
        + [w_ssd[1]] + [_resident(a.shape) for a in consts],
        out_specs=pl.BlockSpec((None, SSD_TS, width), lambda bi_, ci: (bi_, ci, 0)),
        out_shape=jax.ShapeDtypeStruct((b, l, width), BF16),
        scratch_shapes=[
            pltpu.VMEM((SSD_TS, w_ssd[0].shape[-1]), F32),
            pltpu.VMEM((CONV_WIDTH - 1, SUBLANES, conv_ch), F32),
            pltpu.VMEM((width, SSD_STATE), F32),
        ],
        compiler_params=_params("parallel", "arbitrary"),
        name="ssd",
    )(hn3, w_ssd[0], *consts)


QKV_TM = 512
Q_SCALE = ATT_HEAD_DIM ** -0.5 * LOG2_E
PERM_ROWS = 256


def _split_dot(x, m, parts=2, left=False):
    acc, rem = None, x
    for k in range(parts):
        piece = rem.astype(BF16)
        term = jnp.dot(m, piece, preferred_element_type=F32) if left else jnp.dot(piece, m, preferred_element_type=F32)
        acc = term if acc is None else acc + term
        if k + 1 < parts:
            rem = rem - piece.astype(F32)
    return acc


def _slab_map(fn, x):
    return jnp.concatenate([fn(x[:, c * GATE_SLAB:(c + 1) * GATE_SLAB]) for c in range(x.shape[1] // GATE_SLAB)],
                           axis=1)


def _head_norm_rope(y, gain, cos, sin, gsum, rot):
    ss = _slab_map(lambda v: _split_dot(v, gsum, parts=1), y * y)
    yn = y * lax.rsqrt(ss * (1.0 / ATT_HEAD_DIM) + NORM_EPS) * gain
    partner = _slab_map(lambda v: _split_dot(v, rot), yn)
    tile = lambda tbl: jnp.concatenate([tbl] * (y.shape[1] // LANES), axis=1)
    return yn * tile(cos) + partner * tile(sin)


def _qkv_body(x_ref, g_ref, w_ref, qg_ref, kg_ref, cos_ref, sin_ref, gsum_ref, rot_ref, p1_ref, p4_ref, p16_ref,
              *refs):
    outs, h_ref = refs[:9], refs[9]
    j = pl.program_id(2)

    @pl.when(j == 0)
    def _():
        h_ref[...] = _rmsnorm(x_ref[...], g_ref[...]).astype(BF16)

    sub = PERM_ROWS

    def section(o1, o4, o16, gain_ref=None, grouped=False):
        y = jnp.dot(h_ref[...], w_ref[...], preferred_element_type=F32)
        if gain_ref is not None:
            y = _head_norm_rope(y, gain_ref[...], cos_ref[...], sin_ref[...], gsum_ref[...], rot_ref[...])
        y_all = y.astype(BF16)
        for t in range(x_ref.shape[0] // sub):
            rows = slice(t * sub, (t + 1) * sub)
            yb = y_all[rows, :]
            if grouped:
                o1[rows, :] = jnp.dot(p1_ref[...], yb, preferred_element_type=F32).astype(BF16)
            else:
                o1[rows, :] = yb
            for perm_ref, o in ((p4_ref, o4), (p16_ref, o16)):
                dil = o.shape[0]
                n = sub // dil
                yp = jnp.dot(perm_ref[...], yb, preferred_element_type=F32).astype(BF16)
                for r in range(dil):
                    o[r, t * n:(t + 1) * n, :] = yp[r * n:(r + 1) * n, :]

    @pl.when(j == 0)
    def _():
        section(*outs[0:3], gain_ref=qg_ref, grouped=True)

    @pl.when(j == 1)
    def _():
        section(*outs[3:6], gain_ref=kg_ref)

    @pl.when(j == 2)
    def _():
        section(*outs[6:9])


def _qkv(x3, g, w_qkv, layer, qg, kg, cos, sin):
    b, l, d = x3.shape
    tm = QKV_TM
    out_specs, out_shape = [], []
    for _ in range(3):
        out_specs.append(pl.BlockSpec((None, tm, d), lambda bi_, i, j: (bi_, i, 0)))
        out_shape.append(jax.ShapeDtypeStruct((b, l, d), BF16))
        for dil in DILATIONS[1:]:
            out_specs.append(pl.BlockSpec((None, dil, tm // dil, d), lambda bi_, i, j: (bi_, 0, i, 0)))
            out_shape.append(jax.ShapeDtypeStruct((b, dil, l // dil, d), BF16))
    consts = _qkv_constants(tm)
    tbl = lambda: pl.BlockSpec((tm, LANES), lambda bi_, i, j: (i, 0))
    return pl.pallas_call(
        _qkv_body,
        grid=(b, l // tm, 3),
        in_specs=[
            pl.BlockSpec((None, tm, d), lambda bi_, i, j: (bi_, i, 0)),
            _resident((1, d)),
            pl.BlockSpec((None, d, d), lambda bi_, i, j: (layer, 0, j)),
            _resident((1, d)), _resident((1, d)), tbl(), tbl(),
        ] + [_resident(c.shape) for c in consts],
        out_specs=out_specs,
        out_shape=out_shape,
        scratch_shapes=[pltpu.VMEM((tm, d), BF16)],
        compiler_params=_params("parallel", "parallel", "arbitrary"),
        name="qkv",
    )(x3, g, w_qkv, qg, kg, cos, sin, *consts)


def _qkv_constants(tm):
    lane = np.arange(GATE_SLAB)
    gsum = (lane[:, None] // ATT_HEAD_DIM == lane[None, :] // ATT_HEAD_DIM)
    half = ROPE_DIM // 2
    pos = lane % ATT_HEAD_DIM
    partner = np.where(pos < half, lane + half, np.where(pos < ROPE_DIM, lane - half, -1))
    rot = lane[:, None] == partner[None, :]
    out_row = np.arange(PERM_ROWS)
    group = SPAN // ACC_DIL
    within = out_row % SPAN
    src = out_row - within + (within % group) * ACC_DIL + within // group
    perms = [src[:, None] == np.arange(PERM_ROWS)[None, :]]
    for dil in DILATIONS[1:]:
        sub = PERM_ROWS
        rows = sub // dil
        assert rows % BF16_ROWS == 0 and tm % sub == 0
        out_row = np.arange(sub)
        src = (out_row % rows) * dil + out_row // rows
        perms.append(src[:, None] == np.arange(sub)[None, :])
    return tuple(jnp.asarray(m, dtype=BF16) for m in (gsum, rot, *perms))


def _rope_tables(l):
    half = ROPE_DIM // 2
    pos = jnp.arange(l, dtype=F32)
    inv = ROPE_THETA ** (-2.0 * jnp.arange(half, dtype=F32) / ROPE_DIM)
    ang = pos[:, None] * inv[None, :]
    cos, sin = jnp.cos(ang), jnp.sin(ang)
    pad = ATT_HEAD_DIM - ROPE_DIM
    cos_h = jnp.concatenate([cos, cos, jnp.ones((l, pad), F32)], axis=1)
    sin_h = jnp.concatenate([-sin, sin, jnp.zeros((l, pad), F32)], axis=1)
    rep = LANES // ATT_HEAD_DIM
    return tuple(jnp.concatenate([tb] * rep, axis=1) for tb in (cos_h, sin_h))


def _attn_body(q1, k1c, k1p, v1c, v1p, q4, k4c, k4p, v4c, v4p, q16, k16c, k16p, v16c, v16p,
               o_ref, kb1, vb1, kb4, vb4, kb16, vb16, s_scr, p_scr, mx_scr, bias_scr, tok_scr, acc_ref, m_ref, l_ref):
    first = pl.program_id(2) == 0
    n_pairs = acc_ref.shape[0]
    n_blocks = SUPER // SPAN
    trans_b = (((1,), (1,)), ((), ()))

    def gather_rows(pairs):
        for buf, prev, cur in pairs:
            if len(prev.shape) == 2:
                buf[0, 0:SPAN, :] = prev[...]
                buf[0, SPAN:, :] = cur[...]
            else:
                buf[:, 0:SPAN, :] = prev[...]
                buf[:, SPAN:, :] = cur[...]

    gather_rows(((kb1, k1p, k1c), (vb1, v1p, v1c)))

    row = lax.broadcasted_iota(jnp.int32, (SPAN, 2 * SPAN), 0)
    kj = lax.broadcasted_iota(jnp.int32, (SPAN, 2 * SPAN), 1)
    group = SPAN // ACC_DIL
    for base, qi in ((0, row), (2, (row % group) * ACC_DIL + row // group)):
        band = (kj >= qi) & (kj <= qi + SPAN)
        bias_scr[base] = jnp.where(band, 0.0, -jnp.inf)
        bias_scr[base + 1] = jnp.where(band & (kj >= SPAN), 0.0, -jnp.inf)

    lo = lax.broadcasted_iota(jnp.int32, (SPAN, LANES), 1) < ATT_HEAD_DIM
    lo2 = lax.broadcasted_iota(jnp.int32, (2 * SPAN, LANES), 1) < ATT_HEAD_DIM
    ones_lo = jnp.where(lo2, 1.0, 0.0).astype(BF16)
    ones_hi = jnp.where(lo2, 0.0, 1.0).astype(BF16)
    lanes = lambda p: slice(p * LANES, (p + 1) * LANES)

    def run_pattern(dil, q_get, kb, vb):
        per_res = SUPER // (SPAN * dil)
        overwrite = dil == DILATIONS[0]

        def split(f):
            if per_res == 1:
                return f, 0
            if dil == 1:
                return 0, f
            if isinstance(f, int):
                return f // per_res, f % per_res
            return lax.shift_right_logical(f, per_res.bit_length() - 1), f & (per_res - 1)

        def slab_rows(nb):
            return pl.ds(nb * SPAN if isinstance(nb, int) else pl.multiple_of(nb * SPAN, SPAN), 2 * SPAN)

        def stage_scores(f, slot):
            r, nb = split(f)
            if isinstance(nb, int):
                flag = jnp.where(first, 1, 0) if nb == 0 else 0
            else:
                flag = jnp.where(first & (nb == 0), 1, 0)
            bias = bias_scr[flag + (2 if overwrite else 0)]
            bias2 = jnp.concatenate([bias, bias], axis=0)
            q = q_get(r, nb)
            ks = kb[r, slab_rows(nb), :]
            for p in range(n_pairs):
                qp = q[:, lanes(p)]
                zero = jnp.zeros_like(qp)
                q2 = jnp.concatenate([jnp.where(lo, qp, zero), jnp.where(lo, zero, qp)], axis=0)
                s_scr[slot, p] = lax.dot_general(q2, ks[:, lanes(p)], trans_b, preferred_element_type=F32) + bias2

        def stage_softmax(src, dst):
            for p in range(n_pairs):
                s = s_scr[src, p]
                mx = jnp.max(s, axis=1, keepdims=True)
                pr = jnp.exp2(s - mx).astype(BF16)
                p_scr[dst, p, :, 0:2 * SPAN] = pr[:SPAN]
                p_scr[dst, p, :, 2 * SPAN:] = pr[SPAN:]
                mx_scr[dst, p] = jnp.where(lo, mx[:SPAN], mx[SPAN:])

        def stage_values(f, slot):
            r, nb = split(f)
            vs = vb[r, slab_rows(nb), :]
            if overwrite:
                grp = SPAN // ACC_DIL
                start = nb * grp if isinstance(nb, int) else pl.multiple_of(nb * grp, grp)
                pieces = [(slice(c * grp, (c + 1) * grp), c, pl.ds(start, grp)) for c in range(ACC_DIL)]
            elif dil == ACC_DIL:
                pieces = [(slice(0, SPAN), r, q_rows(nb))]
            else:
                sub = dil // ACC_DIL
                pieces = [(slice(0, SPAN), r & (ACC_DIL - 1),
                           pl.ds(lax.shift_right_logical(r, 2) if not isinstance(r, int) else r // ACC_DIL,
                                 SPAN, stride=sub))]
            for p in range(n_pairs):
                vp = vs[:, lanes(p)]
                zero = jnp.zeros_like(vp)
                v_bd = jnp.concatenate([jnp.concatenate([jnp.where(lo2, vp, zero), ones_lo], axis=1),
                                        jnp.concatenate([jnp.where(lo2, zero, vp), ones_hi], axis=1)], axis=0)
                pv = jnp.dot(p_scr[slot, p], v_bd, preferred_element_type=F32)
                o_blk, l_blk = pv[:, :LANES], pv[:, LANES:]
                m_blk = mx_scr[slot, p]
                for src, cls, rows in pieces:
                    m_new, l_new, o_new = m_blk[src], l_blk[src], o_blk[src]
                    if overwrite:
                        m_ref[p, cls, rows, :], l_ref[p, cls, rows, :], acc_ref[p, cls, rows, :] = m_new, l_new, o_new
                        continue
                    m_old = m_ref[p, cls, rows, :]
                    m = jnp.maximum(m_old, m_new)
                    e_old = jnp.exp2(m_old - m)
                    e_new = jnp.exp2(m_new - m)
                    l_ref[p, cls, rows, :] = e_old * l_ref[p, cls, rows, :] + e_new * l_new
                    acc_ref[p, cls, rows, :] = e_old * acc_ref[p, cls, rows, :] + e_new * o_new
                    m_ref[p, cls, rows, :] = m

        n_rounds = n_blocks // 2

        def round_(k, odd, scores=True, softmax=True, values=True):
            wr = 2 if odd else 0
            rd = 2 - wr
            if values:
                stage_values(2 * k - 4, rd)
                stage_values(2 * k - 3, rd + 1)
            if softmax:
                stage_softmax(rd, wr)
                stage_softmax(rd + 1, wr + 1)
            if scores:
                stage_scores(2 * k, wr)
                stage_scores(2 * k + 1, wr + 1)

        round_(0, False, softmax=False, values=False)
        round_(1, True, values=False)

        def body(j, c):
            round_(2 * j, False)
            round_(2 * j + 1, True)
            return c

        lax.fori_loop(1, n_rounds // 2, body, 0)
        round_(n_rounds, False, scores=False)
        round_(n_rounds + 1, True, scores=False, softmax=False)

    def q_rows(nb):
        return pl.ds(nb * SPAN if isinstance(nb, int) else pl.multiple_of(nb * SPAN, SPAN), SPAN)

    run_pattern(DILATIONS[0], lambda r, nb: q1[q_rows(nb), :], kb1, vb1)
    gather_rows(((kb4, k4p, k4c), (vb4, v4p, v4c)))
    run_pattern(DILATIONS[1], lambda r, nb: q4[r, q_rows(nb), :], kb4, vb4)
    gather_rows(((kb16, k16p, k16c), (vb16, v16p, v16c)))
    run_pattern(DILATIONS[2], lambda r, nb: q16[r], kb16, vb16)

    for p in range(n_pairs):
        for cls in range(ACC_DIL):
            tok_scr[p, pl.ds(cls, SUPER // ACC_DIL, stride=ACC_DIL), :] = acc_ref[p, cls] / l_ref[p, cls]
        o_ref[:, lanes(p)] = tok_scr[p].astype(BF16)


def _attention(qkv_outs):
    q1, q4, q16, k1, k4, k16, v1, v4, v16 = qkv_outs
    b, l, d = q1.shape
    w = ATT_LANES
    n_pairs = w // LANES
    d4, d16 = DILATIONS[1], DILATIONS[2]
    cur1 = pl.BlockSpec((None, SUPER, w), lambda bi_, g, s: (bi_, s, g))
    prev1 = pl.BlockSpec((None, SPAN, w), lambda bi_, g, s: (bi_, jnp.maximum(s * (SUPER // SPAN) - 1, 0), g))
    cur4 = pl.BlockSpec((None, d4, SUPER // d4, w), lambda bi_, g, s: (bi_, 0, s, g))
    prev4 = pl.BlockSpec((None, d4, SPAN, w),
                         lambda bi_, g, s: (bi_, 0, jnp.maximum(s * (SUPER // (d4 * SPAN)) - 1, 0), g))
    cur16 = pl.BlockSpec((None, d16, SPAN, w), lambda bi_, g, s: (bi_, 0, s, g))
    prev16 = pl.BlockSpec((None, d16, SPAN, w), lambda bi_, g, s: (bi_, 0, jnp.maximum(s - 1, 0), g))
    return pl.pallas_call(
        _attn_body,
        grid=(b, d // w, l // SUPER),
        in_specs=[cur1, cur1, prev1, cur1, prev1,
                  cur4, cur4, prev4, cur4, prev4,
                  cur16, cur16, prev16, cur16, prev16],
        out_specs=pl.BlockSpec((None, SUPER, w), lambda bi_, g, s: (bi_, s, g)),
        out_shape=jax.ShapeDtypeStruct((b, l, d), BF16),
        scratch_shapes=[
            pltpu.VMEM((1, SPAN + SUPER, w), BF16), pltpu.VMEM((1, SPAN + SUPER, w), BF16),
            pltpu.VMEM((d4, SPAN + SUPER // d4, w), BF16), pltpu.VMEM((d4, SPAN + SUPER // d4, w), BF16),
            pltpu.VMEM((d16, 2 * SPAN, w), BF16), pltpu.VMEM((d16, 2 * SPAN, w), BF16),
            pltpu.VMEM((4, n_pairs, 2 * SPAN, 2 * SPAN), F32),
            pltpu.VMEM((4, n_pairs, SPAN, 4 * SPAN), BF16),
            pltpu.VMEM((4, n_pairs, SPAN, LANES), F32),
            pltpu.VMEM((4, SPAN, 2 * SPAN), F32),
            pltpu.VMEM((n_pairs, SUPER, LANES), F32),
        ] + [pltpu.VMEM((n_pairs, ACC_DIL, SUPER // ACC_DIL, LANES), F32)] * 3,
        compiler_params=_params("parallel", "parallel", "arbitrary"),
        name="dilated_attention",
    )(q1, k1, k1, v1, v1, q4, k4, k4, v4, v4, q16, k16, k16, v16, v16)


def _recurrent_layer(x, b, l, i, norm_g, w_lru, w_ssd, w_out, lru_conv_w, lru_conv_b, lru_w_r, lru_b_r, lru_w_i,
                     lru_b_i, lru_lambda, ssd_conv_w, ssd_conv_b, ssd_dt_bias, ssd_a_log, ssd_d, ssd_norm):
    d = x.shape[1]
    width = lru_lambda.shape[0]
    n_heads = ssd_a_log.shape[0]
    row = lambda v: v.reshape(1, -1)
    pad_lanes = lambda v: jnp.pad(v.reshape(1, -1), ((0, 0), (0, LANES - v.shape[0])))
    x3 = x.reshape(b, l, d)
    out_a, hn3 = _lru(x3, row(norm_g), _layer_block(w_lru, i, (d, 2 * width)), lru_conv_w, row(lru_conv_b),
                      _block_diag_slabs(lru_w_r), _block_diag_slabs(lru_w_i), row(lru_b_r), row(lru_b_i),
                      row(lru_lambda))
    out_b = _ssd(hn3, _layer_block(w_ssd, i), ssd_conv_w, row(ssd_conv_b), pad_lanes(ssd_dt_bias),
                 pad_lanes(ssd_a_log), row(jnp.repeat(ssd_d, SSD_HEAD_DIM)), row(ssd_norm), n_heads)
    return ([out_a.reshape(b * l, width), out_b.reshape(b * l, -1)],
            [_layer_block(w_out, i, (width, d), (0, 0)), _layer_block(w_out, i, (width, d), (1, 0))])


def _attention_layer(x, b, l, i, norm_g, w_qkv, w_out, q_norm, k_norm):
    d = x.shape[1]
    heads = d // ATT_HEAD_DIM
    row = lambda v: v.reshape(1, -1)
    outs = _qkv(x.reshape(b, l, d), row(norm_g), w_qkv, i,
                row(jnp.tile(q_norm, heads) * Q_SCALE), row(jnp.tile(k_norm, heads)), *_rope_tables(l))
    o = _attention(outs)
    return [o.reshape(b * l, d)], [_layer_block(w_out, i)]


def kernel(x, rec_norm, rec_w_in, lru_conv_w, lru_conv_b, lru_w_r, lru_b_r, lru_w_i, lru_b_i, lru_lambda,
           ssd_conv_w, ssd_conv_b, ssd_dt_bias, ssd_a_log, ssd_d, ssd_norm, rec_w_out, att_norm, att_w_qkv,
           att_q_norm, att_k_norm, att_w_out, ffn_norm, ffn_w_gate_up, ffn_w_down):
    b, l, d = x.shape
    depth = ffn_norm.shape[0]
    h = x.reshape(b * l, d)
    lru_cols = 2 * lru_lambda.shape[1]
    w_lru = rec_w_in.astype(BF16)
    dt_pad = LANES - ssd_a_log.shape[1]
    w_ssd = jnp.pad(rec_w_in[:, :, lru_cols:], ((0, 0), (0, 0), (0, dt_pad))).astype(BF16)
    rec_wo, att_wqkv, att_wo = rec_w_out.astype(BF16), att_w_qkv.astype(BF16), att_w_out.astype(BF16)
    ffn_wgu, ffn_wd = ffn_w_gate_up.astype(BF16), ffn_w_down.astype(BF16)
    for layer in range(depth):
        i = layer // 2
        if layer % 2 == 0:
            mixes, wos = _recurrent_layer(h, b, l, i, rec_norm[i], w_lru, w_ssd, rec_wo, lru_conv_w[i],
                                          lru_conv_b[i], lru_w_r[i], lru_b_r[i], lru_w_i[i], lru_b_i[i],
                                          lru_lambda[i], ssd_conv_w[i], ssd_conv_b[i], ssd_dt_bias[i],
                                          ssd_a_log[i], ssd_d[i], ssd_norm[i])
        else:
            mixes, wos = _attention_layer(h, b, l, i, att_norm[i], att_wqkv, att_wo, att_q_norm[i], att_k_norm[i])
        h = _mix_ffn(h, mixes, wos, ffn_norm[layer].reshape(1, d), _layer_block(ffn_wgu, layer),
                     _layer_block(ffn_wd, layer))
    return h.reshape(b, l, d)
```

```python
import functools
import math

import jax
import jax.numpy as jnp
import numpy as np
from jax import lax
from jax.experimental import pallas as pl
from jax.experimental.pallas import tpu as pltpu

F32 = jnp.float32
BF16 = jnp.bfloat16

NORM_EPS = 1e-6
LOG2_E = math.log2(math.e)
CONV_WIDTH = 4
LRU_BLOCK = 64
LRU_C = 8.0
GATE_SLAB = 256
SSD_HEAD_DIM = 64
SSD_GROUPS = 2
SSD_STATE = 128
SSD_CHUNK = 128
ATT_HEAD_DIM = 64
ROPE_DIM = 16
ROPE_THETA = 500000.0
SPAN = 128
DILATIONS = (1, 4, 16)
SUPER = SPAN * DILATIONS[-1]
ATT_LANES = 256
ACC_DIL = DILATIONS[1]
LANES = 128
SUBLANES = 8
BF16_ROWS = 16
VMEM_LIMIT_BYTES = 56 * 1024 * 1024


def _params(*semantics):
    return pltpu.CompilerParams(dimension_semantics=semantics, vmem_limit_bytes=VMEM_LIMIT_BYTES)


def _rmsnorm(x, g):
    ms = jnp.mean(x * x, axis=-1, keepdims=True)
    return x * lax.rsqrt(ms + NORM_EPS) * g


def _sigmoid(x):
    return 1.0 / (1.0 + jnp.exp2(x * (-LOG2_E)))


def _silu(x):
    return x * _sigmoid(x)


def _softplus(x):
    return jnp.maximum(x, 0.0) + jnp.log1p(jnp.exp(-jnp.abs(x)))


def _gelu_tanh(x):
    c = math.sqrt(2.0 / math.pi)
    half_x = 0.5 * x
    return half_x + half_x * jnp.tanh(x * (c + (c * 0.044715) * (x * x)))


FFN_TM = 1024
FFN_TH = 256


def _ffn_body(n_mix, x_ref, *refs):
    mix_refs, wo_refs = refs[:n_mix], refs[n_mix:2 * n_mix]
    g_ref, wgu_ref, wd_ref, o_ref, h_ref, act_ref = refs[2 * n_mix:]
    x1 = x_ref[...]
    for mix_ref, wo_ref in zip(mix_refs, wo_refs):
        x1 = x1 + jnp.dot(mix_ref[...], wo_ref[...], preferred_element_type=F32)
    o_ref[...] = x1
    h_ref[...] = _rmsnorm(x1, g_ref[...]).astype(BF16)
    hid = act_ref.shape[1]
    for c in range(hid // FFN_TH):
        cols = slice(c * FFN_TH, (c + 1) * FFN_TH)
        up_cols = slice(hid + c * FFN_TH, hid + (c + 1) * FFN_TH)
        gate = jnp.dot(h_ref[...], wgu_ref[:, cols], preferred_element_type=F32)
        up = jnp.dot(h_ref[...], wgu_ref[:, up_cols], preferred_element_type=F32)
        act_ref[:, cols] = (_silu(gate) * up).astype(BF16)
    o_ref[...] += jnp.dot(act_ref[...], wd_ref[...], preferred_element_type=F32)


def _resident(shape):
    return pl.BlockSpec(shape, lambda *_: (0,) * len(shape), pipeline_mode=pl.Buffered(1))


def _layer_block(stack, layer, block=None, index=None):
    block = tuple(stack.shape[1:]) if block is None else tuple(block)
    index = (layer,) + (tuple(index) if index is not None else (0,) * len(block))
    return stack, pl.BlockSpec((None,) + block, lambda *_: index, pipeline_mode=pl.Buffered(1))


def _mix_ffn(x, mixes, wos, g, wgu, wd):
    t, d = x.shape
    hid = wd[0].shape[1]
    weights = (*wos, (g, _resident(g.shape)), wgu, wd)
    rows = lambda a: pl.BlockSpec((FFN_TM, a.shape[1]), lambda i: (i, 0))
    return pl.pallas_call(
        functools.partial(_ffn_body, len(mixes)),
        grid=(t // FFN_TM,),
        in_specs=[rows(x)] + [rows(a) for a in mixes] + [spec for _, spec in weights],
        out_specs=rows(x),
        out_shape=jax.ShapeDtypeStruct((t, d), F32),
        scratch_shapes=[pltpu.VMEM((FFN_TM, d), BF16), pltpu.VMEM((FFN_TM, hid), BF16)],
        compiler_params=_params("parallel"),
        name="mix_ffn",
    )(x, *mixes, *[a for a, _ in weights])


def _segment_perm(rows):
    steps = rows // SUBLANES
    rho = np.arange(rows)
    time = (rho % SUBLANES) * steps + rho // SUBLANES
    perm = time[:, None] == np.arange(rows)[None, :]
    return jnp.asarray(perm, dtype=BF16), jnp.asarray(perm.T, dtype=BF16)


def _conv_segments(x, halo_ref, w_ref, b_ref):
    rows, width = x.shape
    sub0 = lax.broadcasted_iota(jnp.int32, (SUBLANES, width), 0) == 0
    halos = []
    for k in range(1, CONV_WIDTH):
        tail = pltpu.roll(x[rows - k * SUBLANES:rows - (k - 1) * SUBLANES, :], 1, 0)
        halos.append(jnp.where(sub0, halo_ref[k - 1], tail))
        halo_ref[k - 1] = tail
    y = b_ref[...] + w_ref[CONV_WIDTH - 1:CONV_WIDTH, :] * x
    for k in range(1, CONV_WIDTH):
        back = jnp.concatenate(halos[k - 1::-1] + [x[:rows - k * SUBLANES, :]], axis=0)
        y = y + w_ref[CONV_WIDTH - 1 - k:CONV_WIDTH - k, :] * back
    return y


LRU_TL = 512
LRU_SUB = 256


def _lru_body(x_ref, g_ref, w_ref, cw_ref, cb_ref, wr_ref, wi_ref, br_ref, bi_ref, lam_ref, perm_ref, unperm_ref,
              o_ref, hn_ref, halo_ref, carry_ref):
    first = pl.program_id(1) == 0
    w = o_ref.shape[1]
    sub = perm_ref.shape[0]
    steps = sub // SUBLANES

    @pl.when(first)
    def _():
        carry_ref[...] = jnp.zeros_like(carry_ref)
        halo_ref[...] = jnp.zeros_like(halo_ref)

    for t in range(o_ref.shape[0] // sub):
        rows = slice(t * sub, (t + 1) * sub)
        hn = _rmsnorm(x_ref[rows, :], g_ref[...]).astype(BF16)
        hn_ref[rows, :] = hn
        hn = jnp.dot(perm_ref[...], hn, preferred_element_type=F32).astype(BF16)
        proj = jnp.dot(hn, w_ref[...], preferred_element_type=F32)
        gate = proj[:, w:]

        xc = _conv_segments(proj[:, :w], halo_ref, cw_ref, cb_ref)
        xcb = xc.astype(BF16)
        pre_r, pre_i = [], []
        for s in range(w // GATE_SLAB):
            slab = xcb[:, s * GATE_SLAB:(s + 1) * GATE_SLAB]
            pre_r.append(jnp.dot(slab, wr_ref[s], preferred_element_type=F32))
            pre_i.append(jnp.dot(slab, wi_ref[s], preferred_element_type=F32))
        r = _sigmoid(jnp.concatenate(pre_r, axis=1) + br_ref[...])
        i = _sigmoid(jnp.concatenate(pre_i, axis=1) + bi_ref[...])
        log2_a = (-LRU_C * LOG2_E * r) * _softplus(-lam_ref[...])
        a = jnp.exp2(log2_a)
        z = 1.0 - a * a
        u = jnp.where(z > 0.0, z * lax.rsqrt(z), 0.0) * (i * xc)

        h_j = jnp.zeros((SUBLANES, w), F32)
        p_j = jnp.ones((SUBLANES, w), F32)
        hs, ps = [], []
        for j in range(steps):
            a_j = a[j * SUBLANES:(j + 1) * SUBLANES, :]
            h_j = a_j * h_j + u[j * SUBLANES:(j + 1) * SUBLANES, :]
            p_j = a_j * p_j
            hs.append(h_j)
            ps.append(p_j)
        c = carry_ref[0:1, :]
        entering = []
        for seg in range(SUBLANES):
            entering.append(c)
            c = p_j[seg:seg + 1, :] * c + h_j[seg:seg + 1, :]
        carry_ref[0:1, :] = c
        enter = jnp.concatenate(entering, axis=0)
        h = jnp.concatenate([hs[j] + ps[j] * enter for j in range(steps)], axis=0)
        out = (h * _gelu_tanh(gate)).astype(BF16)
        o_ref[rows, :] = jnp.dot(unperm_ref[...], out, preferred_element_type=F32).astype(BF16)


def _lru(x3, g, w_lru, cw, cb, wr_bd, wi_bd, br, bi, lam):
    b, l, d = x3.shape
    width = lam.shape[1]
    perms = _segment_perm(LRU_SUB)
    return pl.pallas_call(
        _lru_body,
        grid=(b, l // LRU_TL),
        in_specs=[pl.BlockSpec((None, LRU_TL, d), lambda bi_, li: (bi_, li, 0))]
        + [_resident(g.shape), w_lru[1]]
        + [_resident(a.shape) for a in (cw, cb, wr_bd, wi_bd, br, bi, lam, *perms)],
        out_specs=[pl.BlockSpec((None, LRU_TL, width), lambda bi_, li: (bi_, li, 0)),
                   pl.BlockSpec((None, LRU_TL, d), lambda bi_, li: (bi_, li, 0))],
        out_shape=[jax.ShapeDtypeStruct((b, l, width), BF16),
                   jax.ShapeDtypeStruct((b, l, d), BF16)],
        scratch_shapes=[
            pltpu.VMEM((CONV_WIDTH - 1, SUBLANES, width), F32),
            pltpu.VMEM((SUBLANES, width), F32),
        ],
        compiler_params=_params("parallel", "arbitrary"),
        name="lru",
    )(x3, g, w_lru[0], cw, cb, wr_bd, wi_bd, br, bi, lam, *perms)


def _block_diag_slabs(w):
    nb, bs, _ = w.shape
    per = GATE_SLAB // bs
    w = w.reshape(nb // per, per, bs, bs)
    eye = jnp.eye(per, dtype=w.dtype)
    bd = jnp.einsum("spij,pq->spiqj", w, eye).reshape(nb // per, GATE_SLAB, GATE_SLAB)
    return bd.astype(BF16)


def _ssd_body(n_heads, hn_ref, w_ref, cw_ref, cb_ref, dtb_ref, alog_ref, dvec_ref, nrm_ref,
              exp_ref, perm_ref, unperm_ref, o_ref, proj_ref, halo_ref, state_ref):
    first = pl.program_id(1) == 0
    t = SSD_CHUNK
    width = n_heads * SSD_HEAD_DIM
    gw = SSD_STATE
    conv_ch = width + 2 * SSD_GROUPS * gw
    n_chunks = hn_ref.shape[0] // t

    @pl.when(first)
    def _():
        state_ref[...] = jnp.zeros_like(state_ref)
        halo_ref[...] = jnp.zeros_like(halo_ref)

    hn = jnp.concatenate([jnp.dot(perm_ref[...], hn_ref[c * t:(c + 1) * t, :], preferred_element_type=F32).astype(BF16)
                          for c in range(n_chunks)], axis=0)
    proj_ref[...] = jnp.dot(hn, w_ref[...], preferred_element_type=F32)

    steps = t // SUBLANES
    time_of = lambda i: (i & (SUBLANES - 1)) * steps + lax.shift_right_logical(i, 3)
    causal = (time_of(lax.broadcasted_iota(jnp.int32, (t, t), 0))
              >= time_of(lax.broadcasted_iota(jnp.int32, (t, t), 1)))
    tri = jnp.where(causal, 1.0, 0.0).astype(BF16)
    lane = lax.broadcasted_iota(jnp.int32, (t, LANES), 1)
    lo = lane < SSD_HEAD_DIM
    rlo = lax.broadcasted_iota(jnp.int32, (LANES, gw), 0) < SSD_HEAD_DIM

    dt = _softplus(proj_ref[:, width + conv_ch:] + dtb_ref[...])
    adt = dt * (-LOG2_E * jnp.exp(alog_ref[...]))
    cs = jnp.concatenate([_split_dot(adt[c * t:(c + 1) * t, :], tri, parts=3, left=True)
                          for c in range(n_chunks)], axis=0)
    dt_w = _split_dot(dt, exp_ref[...], parts=2)
    cs_w = _split_dot(cs, exp_ref[...], parts=2)

    for c in range(n_chunks):
        rows = slice(c * t, (c + 1) * t)
        out = _ssd_chunk(proj_ref[rows, 0:width], proj_ref[rows, width:width + conv_ch],
                         cs[rows, :], dt_w[rows, :], cs_w[rows, :], n_heads, causal, lo, rlo, halo_ref, state_ref,
                         cw_ref, cb_ref, dvec_ref, nrm_ref)
        o_ref[rows, :] = jnp.dot(unperm_ref[...], out, preferred_element_type=F32).astype(BF16)


def _ssd_chunk(z, xbc_raw, cs, dt_w, cs_w, n_heads, causal, lo, rlo, halo_ref, state_ref,
               cw_ref, cb_ref, dvec_ref, nrm_ref):
    t = SSD_CHUNK
    width = n_heads * SSD_HEAD_DIM
    gw = SSD_STATE
    heads_per_group = n_heads // SSD_GROUPS
    xbc = _silu(_conv_segments(xbc_raw, halo_ref, cw_ref, cb_ref))
    xs = xbc[:, :width]
    cs_t = cs.T
    cs_last = cs[t - 1:t, :]
    cdec = jnp.exp2(cs_last)
    ecs_w = jnp.exp2(cs_w)
    dte_w = jnp.exp2(cs_w[t - 1:t, :] - cs_w)

    cbs = []
    for g in range(SSD_GROUPS):
        bm = xbc[:, width + g * gw: width + (g + 1) * gw].astype(BF16)
        cm = xbc[:, width + (SSD_GROUPS + g) * gw: width + (SSD_GROUPS + g + 1) * gw].astype(BF16)
        cb = lax.dot_general(cm, bm, (((1,), (1,)), ((), ())), preferred_element_type=F32)
        cbs.append((bm, cm, cb))

    ys = []
    for p in range(n_heads // 2):
        h0, h1 = 2 * p, 2 * p + 1
        bm, cm, cb = cbs[h0 // heads_per_group]
        sl = slice(p * LANES, (p + 1) * LANES)
        xs_p = xs[:, sl]
        xdt = xs_p * dt_w[:, sl]
        ms = []
        for h in (h0, h1):
            seg = cs[:, h:h + 1] - cs_t[h:h + 1, :]
            ms.append((cb * jnp.where(causal, jnp.exp2(seg), 0.0)).astype(BF16))
        m_cat = jnp.concatenate(ms, axis=1)
        xdt_bd = jnp.concatenate([jnp.where(lo, xdt, 0.0), jnp.where(lo, 0.0, xdt)], axis=0).astype(BF16)
        y = jnp.dot(m_cat, xdt_bd, preferred_element_type=F32)
        prev = state_ref[sl, :]
        y_off = lax.dot_general(cm, prev.astype(BF16), (((1,), (1,)), ((), ())), preferred_element_type=F32)
        y = y + y_off * ecs_w[:, sl]
        xw = (xdt * dte_w[:, sl]).astype(BF16)
        st = lax.dot_general(xw, bm, (((0,), (0,)), ((), ())), preferred_element_type=F32)
        dec = jnp.where(rlo, cdec[:, h0:h0 + 1], cdec[:, h1:h1 + 1])
        state_ref[sl, :] = prev * dec + st
        ys.append(y + dvec_ref[:, sl] * xs_p)

    y = jnp.concatenate(ys, axis=1) * _silu(z)
    gsz = width // SSD_GROUPS
    outs = []
    for g in range(SSD_GROUPS):
        yg = y[:, g * gsz:(g + 1) * gsz]
        outs.append(yg * lax.rsqrt(jnp.mean(yg * yg, axis=-1, keepdims=True) + NORM_EPS))
    return (jnp.concatenate(outs, axis=1) * nrm_ref[...]).astype(BF16)


SSD_TS = 512


def _ssd(hn3, w_ssd, cw, cb, dt_bias, a_log, d_vec, nrm, n_heads):
    b, l, d = hn3.shape
    width = n_heads * SSD_HEAD_DIM
    conv_ch = width + 2 * SSD_GROUPS * SSD_STATE
    expand = jnp.asarray(np.arange(LANES)[:, None] == np.arange(width)[None, :] // SSD_HEAD_DIM,
                         dtype=BF16)
    consts = (cw, cb, dt_bias, a_log, d_vec, nrm, expand, *_segment_perm(SSD_CHUNK))
    return pl.pallas_call(
        functools.partial(_ssd_body, n_heads),
        grid=(b, l // SSD_TS),
        in_specs=[pl.BlockSpec((None, SSD_TS, d), lambda bi_, ci: (bi_, ci, 0))]
        + [w_ssd[1]] + [_resident(a.shape) for a in consts],
        out_specs=pl.BlockSpec((None, SSD_TS, width), lambda bi_, ci: (bi_, ci, 0)),
        out_shape=jax.ShapeDtypeStruct((b, l, width), BF16),
        scratch_shapes=[
            pltpu.VMEM((SSD_TS, w_ssd[0].shape[-1]), F32),
            pltpu.VMEM((CONV_WIDTH - 1, SUBLANES, conv_ch), F32),
            pltpu.VMEM((width, SSD_STATE), F32),
        ],
        compiler_params=_params("parallel", "arbitrary"),
        name="ssd",
    )(hn3, w_ssd[0], *consts)


QKV_TM = 512
Q_SCALE = ATT_HEAD_DIM ** -0.5 * LOG2_E
PERM_ROWS = 256


def _split_dot(x, m, parts=2, left=False):
    acc, rem = None, x
    for k in range(parts):
        piece = rem.astype(BF16)
        term = jnp.dot(m, piece, preferred_element_type=F32) if left else jnp.dot(piece, m, preferred_element_type=F32)
        acc = term if acc is None else acc + term
        if k + 1 < parts:
            rem = rem - piece.astype(F32)
    return acc


def _slab_map(fn, x):
    return jnp.concatenate([fn(x[:, c * GATE_SLAB:(c + 1) * GATE_SLAB]) for c in range(x.shape[1] // GATE_SLAB)],
                           axis=1)


def _head_norm_rope(y, gain, cos, sin, gsum, rot):
    ss = _slab_map(lambda v: _split_dot(v, gsum, parts=1), y * y)
    yn = y * lax.rsqrt(ss * (1.0 / ATT_HEAD_DIM) + NORM_EPS) * gain
    partner = _slab_map(lambda v: _split_dot(v, rot), yn)
    tile = lambda tbl: jnp.concatenate([tbl] * (y.shape[1] // LANES), axis=1)
    return yn * tile(cos) + partner * tile(sin)


def _qkv_body(x_ref, g_ref, w_ref, qg_ref, kg_ref, cos_ref, sin_ref, gsum_ref, rot_ref, p1_ref, p4_ref, p16_ref,
              *refs):
    outs, h_ref = refs[:9], refs[9]
    j = pl.program_id(2)

    @pl.when(j == 0)
    def _():
        h_ref[...] = _rmsnorm(x_ref[...], g_ref[...]).astype(BF16)

    sub = PERM_ROWS

    def section(o1, o4, o16, gain_ref=None, grouped=False):
        y = jnp.dot(h_ref[...], w_ref[...], preferred_element_type=F32)
        if gain_ref is not None:
            y = _head_norm_rope(y, gain_ref[...], cos_ref[...], sin_ref[...], gsum_ref[...], rot_ref[...])
        y_all = y.astype(BF16)
        for t in range(x_ref.shape[0] // sub):
            rows = slice(t * sub, (t + 1) * sub)
            yb = y_all[rows, :]
            if grouped:
                o1[rows, :] = jnp.dot(p1_ref[...], yb, preferred_element_type=F32).astype(BF16)
            else:
                o1[rows, :] = yb
            for perm_ref, o in ((p4_ref, o4), (p16_ref, o16)):
                dil = o.shape[0]
                n = sub // dil
                yp = jnp.dot(perm_ref[...], yb, preferred_element_type=F32).astype(BF16)
                for r in range(dil):
                    o[r, t * n:(t + 1) * n, :] = yp[r * n:(r + 1) * n, :]

    @pl.when(j == 0)
    def _():
        section(*outs[0:3], gain_ref=qg_ref, grouped=True)

    @pl.when(j == 1)
    def _():
        section(*outs[3:6], gain_ref=kg_ref)

    @pl.when(j == 2)
    def _():
        section(*outs[6:9])


def _qkv(x3, g, w_qkv, layer, qg, kg, cos, sin):
    b, l, d = x3.shape
    tm = QKV_TM
    out_specs, out_shape = [], []
    for _ in range(3):
        out_specs.append(pl.BlockSpec((None, tm, d), lambda bi_, i, j: (bi_, i, 0)))
        out_shape.append(jax.ShapeDtypeStruct((b, l, d), BF16))
        for dil in DILATIONS[1:]:
            out_specs.append(pl.BlockSpec((None, dil, tm // dil, d), lambda bi_, i, j: (bi_, 0, i, 0)))
            out_shape.append(jax.ShapeDtypeStruct((b, dil, l // dil, d), BF16))
    consts = _qkv_constants(tm)
    tbl = lambda: pl.BlockSpec((tm, LANES), lambda bi_, i, j: (i, 0))
    return pl.pallas_call(
        _qkv_body,
        grid=(b, l // tm, 3),
        in_specs=[
            pl.BlockSpec((None, tm, d), lambda bi_, i, j: (bi_, i, 0)),
            _resident((1, d)),
            pl.BlockSpec((None, d, d), lambda bi_, i, j: (layer, 0, j)),
            _resident((1, d)), _resident((1, d)), tbl(), tbl(),
        ] + [_resident(c.shape) for c in consts],
        out_specs=out_specs,
        out_shape=out_shape,
        scratch_shapes=[pltpu.VMEM((tm, d), BF16)],
        compiler_params=_params("parallel", "parallel", "arbitrary"),
        name="qkv",
    )(x3, g, w_qkv, qg, kg, cos, sin, *consts)


def _qkv_constants(tm):
    lane = np.arange(GATE_SLAB)
    gsum = (lane[:, None] // ATT_HEAD_DIM == lane[None, :] // ATT_HEAD_DIM)
    half = ROPE_DIM // 2
    pos = lane % ATT_HEAD_DIM
    partner = np.where(pos < half, lane + half, np.where(pos < ROPE_DIM, lane - half, -1))
    rot = lane[:, None] == partner[None, :]
    out_row = np.arange(PERM_ROWS)
    group = SPAN // ACC_DIL
    within = out_row % SPAN
    src = out_row - within + (within % group) * ACC_DIL + within // group
    perms = [src[:, None] == np.arange(PERM_ROWS)[None, :]]
    for dil in DILATIONS[1:]:
        sub = PERM_ROWS
        rows = sub // dil
        assert rows % BF16_ROWS == 0 and tm % sub == 0
        out_row = np.arange(sub)
        src = (out_row % rows) * dil + out_row // rows
        perms.append(src[:, None] == np.arange(sub)[None, :])
    return tuple(jnp.asarray(m, dtype=BF16) for m in (gsum, rot, *perms))


def _rope_tables(l):
    half = ROPE_DIM // 2
    pos = jnp.arange(l, dtype=F32)
    inv = ROPE_THETA ** (-2.0 * jnp.arange(half, dtype=F32) / ROPE_DIM)
    ang = pos[:, None] * inv[None, :]
    cos, sin = jnp.cos(ang), jnp.sin(ang)
    pad = ATT_HEAD_DIM - ROPE_DIM
    cos_h = jnp.concatenate([cos, cos, jnp.ones((l, pad), F32)], axis=1)
    sin_h = jnp.concatenate([-sin, sin, jnp.zeros((l, pad), F32)], axis=1)
    rep = LANES // ATT_HEAD_DIM
    return tuple(jnp.concatenate([tb] * rep, axis=1) for tb in (cos_h, sin_h))


def _attn_body(q1, kw1, vw1, q4, kw4, vw4, q16, kw16, vw16,
               o_ref, s_scr, p_scr, mx_scr, bias_scr, tok_scr, acc_ref, m_ref, l_ref):
    first = pl.program_id(2) == 0
    lead = jnp.where(first, 0, SPAN)
    n_pairs = acc_ref.shape[0]
    n_blocks = SUPER // SPAN
    trans_b = (((1,), (1,)), ((), ()))

    row = lax.broadcasted_iota(jnp.int32, (SPAN, 2 * SPAN), 0)
    kj = lax.broadcasted_iota(jnp.int32, (SPAN, 2 * SPAN), 1)
    group = SPAN // ACC_DIL
    for base, qi in ((0, row), (2, (row % group) * ACC_DIL + row // group)):
        bias_scr[base] = jnp.where((kj >= qi) & (kj <= qi + SPAN), 0.0, -jnp.inf)
        bias_scr[base + 1] = jnp.where(kj <= qi, 0.0, -jnp.inf)

    lo = lax.broadcasted_iota(jnp.int32, (SPAN, LANES), 1) < ATT_HEAD_DIM
    lo2 = lax.broadcasted_iota(jnp.int32, (2 * SPAN, LANES), 1) < ATT_HEAD_DIM
    ones_lo = jnp.where(lo2, 1.0, 0.0).astype(BF16)
    ones_hi = jnp.where(lo2, 0.0, 1.0).astype(BF16)
    lanes = lambda p: slice(p * LANES, (p + 1) * LANES)

    def run_pattern(dil, q_get, kb, vb):
        per_res = SUPER // (SPAN * dil)
        overwrite = dil == DILATIONS[0]

        def split(f):
            if per_res == 1:
                return f, 0
            if dil == 1:
                return 0, f
            if isinstance(f, int):
                return f // per_res, f % per_res
            return lax.shift_right_logical(f, per_res.bit_length() - 1), f & (per_res - 1)

        def slab_rows(nb):
            start = jnp.maximum(lead + (nb - 1) * SPAN, 0)
            return pl.ds(pl.multiple_of(start, SPAN), 2 * SPAN)

        def stage_scores(f, slot):
            r, nb = split(f)
            if isinstance(nb, int):
                flag = jnp.where(first, 1, 0) if nb == 0 else 0
            else:
                flag = jnp.where(first & (nb == 0), 1, 0)
            bias = bias_scr[flag + (2 if overwrite else 0)]
            bias2 = jnp.concatenate([bias, bias], axis=0)
            q = q_get(r, nb)
            ks = kb[0, slab_rows(nb), :] if dil == 1 else kb[0, r, slab_rows(nb), :]
            for p in range(n_pairs):
                qp = q[:, lanes(p)]
                zero = jnp.zeros_like(qp)
                q2 = jnp.concatenate([jnp.where(lo, qp, zero), jnp.where(lo, zero, qp)], axis=0)
                s_scr[slot, p] = lax.dot_general(q2, ks[:, lanes(p)], trans_b, preferred_element_type=F32) + bias2

        def stage_softmax(src, dst):
            for p in range(n_pairs):
                s = s_scr[src, p]
                mx = jnp.max(s, axis=1, keepdims=True)
                pr = jnp.exp2(s - mx).astype(BF16)
                p_scr[dst, p, :, 0:2 * SPAN] = pr[:SPAN]
                p_scr[dst, p, :, 2 * SPAN:] = pr[SPAN:]
                mx_scr[dst, p] = jnp.where(lo, mx[:SPAN], mx[SPAN:])

        def stage_values(f, slot):
            r, nb = split(f)
            vs = vb[0, slab_rows(nb), :] if dil == 1 else vb[0, r, slab_rows(nb), :]
            if overwrite:
                grp = SPAN // ACC_DIL
                start = nb * grp if isinstance(nb, int) else pl.multiple_of(nb * grp, grp)
                pieces = [(slice(c * grp, (c + 1) * grp), c, pl.ds(start, grp)) for c in range(ACC_DIL)]
            elif dil == ACC_DIL:
                pieces = [(slice(0, SPAN), r, q_rows(nb))]
            else:
                sub = dil // ACC_DIL
                pieces = [(slice(0, SPAN), r & (ACC_DIL - 1),
                           pl.ds(lax.shift_right_logical(r, 2) if not isinstance(r, int) else r // ACC_DIL,
                                 SPAN, stride=sub))]
            for p in range(n_pairs):
                vp = vs[:, lanes(p)]
                zero = jnp.zeros_like(vp)
                v_bd = jnp.concatenate([jnp.concatenate([jnp.where(lo2, vp, zero), ones_lo], axis=1),
                                        jnp.concatenate([jnp.where(lo2, zero, vp), ones_hi], axis=1)], axis=0)
                pv = jnp.dot(p_scr[slot, p], v_bd, preferred_element_type=F32)
                o_blk, l_blk = pv[:, :LANES], pv[:, LANES:]
                m_blk = mx_scr[slot, p]
                for src, cls, rows in pieces:
                    m_new, l_new, o_new = m_blk[src], l_blk[src], o_blk[src]
                    if overwrite:
                        m_ref[p, cls, rows, :], l_ref[p, cls, rows, :], acc_ref[p, cls, rows, :] = m_new, l_new, o_new
                        continue
                    m_old = m_ref[p, cls, rows, :]
                    m = jnp.maximum(m_old, m_new)
                    e_old = jnp.exp2(m_old - m)
                    e_new = jnp.exp2(m_new - m)
                    l_ref[p, cls, rows, :] = e_old * l_ref[p, cls, rows, :] + e_new * l_new
                    acc_ref[p, cls, rows, :] = e_old * acc_ref[p, cls, rows, :] + e_new * o_new
                    m_ref[p, cls, rows, :] = m

        n_rounds = n_blocks // 2

        def round_(k, odd, scores=True, softmax=True, values=True):
            wr = 2 if odd else 0
            rd = 2 - wr
            if values:
                stage_values(2 * k - 4, rd)
                stage_values(2 * k - 3, rd + 1)
            if softmax:
                stage_softmax(rd, wr)
                stage_softmax(rd + 1, wr + 1)
            if scores:
                stage_scores(2 * k, wr)
                stage_scores(2 * k + 1, wr + 1)

        round_(0, False, softmax=False, values=False)
        round_(1, True, values=False)

        def body(j, c):
            round_(2 * j, False)
            round_(2 * j + 1, True)
            return c

        lax.fori_loop(1, n_rounds // 2, body, 0)
        round_(n_rounds, False, scores=False)
        round_(n_rounds + 1, True, scores=False, softmax=False)

    def q_rows(nb):
        return pl.ds(nb * SPAN if isinstance(nb, int) else pl.multiple_of(nb * SPAN, SPAN), SPAN)

    run_pattern(DILATIONS[0], lambda r, nb: q1[q_rows(nb), :], kw1, vw1)
    run_pattern(DILATIONS[1], lambda r, nb: q4[r, q_rows(nb), :], kw4, vw4)
    run_pattern(DILATIONS[2], lambda r, nb: q16[r], kw16, vw16)

    for p in range(n_pairs):
        for cls in range(ACC_DIL):
            tok_scr[p, pl.ds(cls, SUPER // ACC_DIL, stride=ACC_DIL), :] = acc_ref[p, cls] / l_ref[p, cls]
        o_ref[:, lanes(p)] = tok_scr[p].astype(BF16)


def _attention(qkv_outs):
    q1, q4, q16, k1, k4, k16, v1, v4, v16 = qkv_outs
    b, l, d = q1.shape
    w = ATT_LANES
    n_pairs = w // LANES
    d4, d16 = DILATIONS[1], DILATIONS[2]
    def rows_of(dil):
        n = SUPER // dil
        lead = (None,) if dil == 1 else (None, dil)
        zero = () if dil == 1 else (0,)
        cur = pl.BlockSpec(lead + (n, w), lambda bi_, g, s: (bi_,) + zero + (s, g))
        window = pl.BlockSpec(tuple(pl.Element(e) for e in (1,) + lead[1:] + (SPAN + n, w)),
                              lambda bi_, g, s: (bi_,) + zero + (pl.multiple_of(jnp.maximum(s * n - SPAN, 0), SPAN),
                                                                   pl.multiple_of(g * w, w)))
        return [cur, window, window]

    return pl.pallas_call(
        _attn_body,
        grid=(b, d // w, l // SUPER),
        in_specs=rows_of(1) + rows_of(d4) + rows_of(d16),
        out_specs=pl.BlockSpec((None, SUPER, w), lambda bi_, g, s: (bi_, s, g)),
        out_shape=jax.ShapeDtypeStruct((b, l, d), BF16),
        scratch_shapes=[
            pltpu.VMEM((4, n_pairs, 2 * SPAN, 2 * SPAN), F32),
            pltpu.VMEM((4, n_pairs, SPAN, 4 * SPAN), BF16),
            pltpu.VMEM((4, n_pairs, SPAN, LANES), F32),
            pltpu.VMEM((4, SPAN, 2 * SPAN), F32),
            pltpu.VMEM((n_pairs, SUPER, LANES), F32),
        ] + [pltpu.VMEM((n_pairs, ACC_DIL, SUPER // ACC_DIL, LANES), F32)] * 3,
        compiler_params=_params("parallel", "parallel", "arbitrary"),
        name="dilated_attention",
    )(q1, k1, v1, q4, k4, v4, q16, k16, v16)


def _recurrent_layer(x, b, l, i, norm_g, w_lru, w_ssd, w_out, lru_conv_w, lru_conv_b, lru_w_r, lru_b_r, lru_w_i,
                     lru_b_i, lru_lambda, ssd_conv_w, ssd_conv_b, ssd_dt_bias, ssd_a_log, ssd_d, ssd_norm):
    d = x.shape[1]
    width = lru_lambda.shape[0]
    n_heads = ssd_a_log.shape[0]
    row = lambda v: v.reshape(1, -1)
    pad_lanes = lambda v: jnp.pad(v.reshape(1, -1), ((0, 0), (0, LANES - v.shape[0])))
    x3 = x.reshape(b, l, d)
    out_a, hn3 = _lru(x3, row(norm_g), _layer_block(w_lru, i, (d, 2 * width)), lru_conv_w, row(lru_conv_b),
                      _block_diag_slabs(lru_w_r), _block_diag_slabs(lru_w_i), row(lru_b_r), row(lru_b_i),
                      row(lru_lambda))
    out_b = _ssd(hn3, _layer_block(w_ssd, i), ssd_conv_w, row(ssd_conv_b), pad_lanes(ssd_dt_bias),
                 pad_lanes(ssd_a_log), row(jnp.repeat(ssd_d, SSD_HEAD_DIM)), row(ssd_norm), n_heads)
    return ([out_a.reshape(b * l, width), out_b.reshape(b * l, -1)],
            [_layer_block(w_out, i, (width, d), (0, 0)), _layer_block(w_out, i, (width, d), (1, 0))])


def _attention_layer(x, b, l, i, norm_g, w_qkv, w_out, q_norm, k_norm):
    d = x.shape[1]
    heads = d // ATT_HEAD_DIM
    row = lambda v: v.reshape(1, -1)
    outs = _qkv(x.reshape(b, l, d), row(norm_g), w_qkv, i,
                row(jnp.tile(q_norm, heads) * Q_SCALE), row(jnp.tile(k_norm, heads)), *_rope_tables(l))
    o = _attention(outs)
    return [o.reshape(b * l, d)], [_layer_block(w_out, i)]


def kernel(x, rec_norm, rec_w_in, lru_conv_w, lru_conv_b, lru_w_r, lru_b_r, lru_w_i, lru_b_i, lru_lambda,
           ssd_conv_w, ssd_conv_b, ssd_dt_bias, ssd_a_log, ssd_d, ssd_norm, rec_w_out, att_norm, att_w_qkv,
           att_q_norm, att_k_norm, att_w_out, ffn_norm, ffn_w_gate_up, ffn_w_down):
    b, l, d = x.shape
    depth = ffn_norm.shape[0]
    h = x.reshape(b * l, d)
    lru_cols = 2 * lru_lambda.shape[1]
    w_lru = rec_w_in.astype(BF16)
    dt_pad = LANES - ssd_a_log.shape[1]
    w_ssd = jnp.pad(rec_w_in[:, :, lru_cols:], ((0, 0), (0, 0), (0, dt_pad))).astype(BF16)
    rec_wo, att_wqkv, att_wo = rec_w_out.astype(BF16), att_w_qkv.astype(BF16), att_w_out.astype(BF16)
    ffn_wgu, ffn_wd = ffn_w_gate_up.astype(BF16), ffn_w_down.astype(BF16)
    for layer in range(depth):
        i = layer // 2
        if layer % 2 == 0:
            mixes, wos = _recurrent_layer(h, b, l, i, rec_norm[i], w_lru, w_ssd, rec_wo, lru_conv_w[i],
                                          lru_conv_b[i], lru_w_r[i], lru_b_r[i], lru_w_i[i], lru_b_i[i],
                                          lru_lambda[i], ssd_conv_w[i], ssd_conv_b[i], ssd_dt_bias[i],
                                          ssd_a_log[i], ssd_d[i], ssd_norm[i])
        else:
            mixes, wos = _attention_layer(h, b, l, i, att_norm[i], att_wqkv, att_wo, att_q_norm[i], att_k_norm[i])
        h = _mix_ffn(h, mixes, wos, ffn_norm[layer].reshape(1, d), _layer_block(ffn_wgu, layer),
                     _layer_block(ffn_wd, layer))
    return h.reshape(b, l, d)
```

```python
import functools
import math

import jax
import jax.numpy as jnp
import numpy as np
from jax import lax
from jax.experimental import pallas as pl
from jax.experimental.pallas import tpu as pltpu

F32 = jnp.float32
BF16 = jnp.bfloat16

NORM_EPS = 1e-6
LOG2_E = math.log2(math.e)
CONV_WIDTH = 4
LRU_C = 8.0
GATE_SLAB = 256
SSD_HEAD_DIM = 64
SSD_GROUPS = 2
SSD_STATE = 128
SSD_CHUNK = 128
ATT_HEAD_DIM = 64
ROPE_DIM = 16
ROPE_THETA = 500000.0
SPAN = 128
DILATIONS = (1, 4, 16)
SUPER = SPAN * DILATIONS[-1]
ATT_LANES = 256
ACC_DIL = DILATIONS[1]
ACC_SHIFT = ACC_DIL.bit_length() - 1
LANES = 128
SUBLANES = 8
SUBLANE_SHIFT = SUBLANES.bit_length() - 1
BF16_ROWS = 16
VMEM_LIMIT_BYTES = 56 * 1024 * 1024


def _params(*semantics):
    return pltpu.CompilerParams(dimension_semantics=semantics, vmem_limit_bytes=VMEM_LIMIT_BYTES)


def _rmsnorm(x, g):
    ms = jnp.mean(x * x, axis=-1, keepdims=True)
    return x * lax.rsqrt(ms + NORM_EPS) * g


def _sigmoid(x):
    return 1.0 / (1.0 + jnp.exp2(x * (-LOG2_E)))


def _silu(x):
    return x * _sigmoid(x)


def _softplus(x):
    return jnp.maximum(x, 0.0) + jnp.log1p(jnp.exp(-jnp.abs(x)))


def _gelu_tanh(x):
    c = math.sqrt(2.0 / math.pi)
    half_x = 0.5 * x
    return half_x + half_x * jnp.tanh(x * (c + (c * 0.044715) * (x * x)))


FFN_TM = 1024
FFN_TH = 256


def _ffn_body(n_mix, x_ref, *refs):
    mix_refs, wo_refs = refs[:n_mix], refs[n_mix:2 * n_mix]
    g_ref, wgu_ref, wd_ref, o_ref, h_ref, act_ref = refs[2 * n_mix:]
    x1 = x_ref[...]
    for mix_ref, wo_ref in zip(mix_refs, wo_refs):
        x1 = x1 + jnp.dot(mix_ref[...], wo_ref[...], preferred_element_type=F32)
    o_ref[...] = x1
    h_ref[...] = _rmsnorm(x1, g_ref[...]).astype(BF16)
    hid = act_ref.shape[1]
    for c in range(hid // FFN_TH):
        cols = slice(c * FFN_TH, (c + 1) * FFN_TH)
        up_cols = slice(hid + c * FFN_TH, hid + (c + 1) * FFN_TH)
        gate = jnp.dot(h_ref[...], wgu_ref[:, cols], preferred_element_type=F32)
        up = jnp.dot(h_ref[...], wgu_ref[:, up_cols], preferred_element_type=F32)
        act_ref[:, cols] = (_silu(gate) * up).astype(BF16)
    o_ref[...] += jnp.dot(act_ref[...], wd_ref[...], preferred_element_type=F32)


def _resident(shape):
    return pl.BlockSpec(shape, lambda *_: (0,) * len(shape), pipeline_mode=pl.Buffered(1))


def _layer_block(stack, layer, block=None, index=None):
    block = tuple(stack.shape[1:]) if block is None else tuple(block)
    index = (layer,) + (tuple(index) if index is not None else (0,) * len(block))
    return stack, pl.BlockSpec((None,) + block, lambda *_: index, pipeline_mode=pl.Buffered(1))


def _mix_ffn(x, mixes, wos, g, wgu, wd):
    t, d = x.shape
    hid = wd[0].shape[1]
    weights = (*wos, (g, _resident(g.shape)), wgu, wd)
    rows = lambda a: pl.BlockSpec((FFN_TM, a.shape[1]), lambda i: (i, 0))
    return pl.pallas_call(
        functools.partial(_ffn_body, len(mixes)),
        grid=(t // FFN_TM,),
        in_specs=[rows(x)] + [rows(a) for a in mixes] + [spec for _, spec in weights],
        out_specs=rows(x),
        out_shape=jax.ShapeDtypeStruct((t, d), F32),
        scratch_shapes=[pltpu.VMEM((FFN_TM, d), BF16), pltpu.VMEM((FFN_TM, hid), BF16)],
        compiler_params=_params("parallel"),
        name="mix_ffn",
    )(x, *mixes, *[a for a, _ in weights])


def _segment_perm(rows):
    steps = rows // SUBLANES
    rho = np.arange(rows)
    time = (rho % SUBLANES) * steps + rho // SUBLANES
    perm = time[:, None] == np.arange(rows)[None, :]
    return jnp.asarray(perm, dtype=BF16), jnp.asarray(perm.T, dtype=BF16)


def _conv_segments(x, halo_ref, w_ref, b_ref):
    rows, width = x.shape
    sub0 = lax.broadcasted_iota(jnp.int32, (SUBLANES, width), 0) == 0
    halos = []
    for k in range(1, CONV_WIDTH):
        tail = pltpu.roll(x[rows - k * SUBLANES:rows - (k - 1) * SUBLANES, :], 1, 0)
        halos.append(jnp.where(sub0, halo_ref[k - 1], tail))
        halo_ref[k - 1] = tail
    y = b_ref[...] + w_ref[CONV_WIDTH - 1:CONV_WIDTH, :] * x
    for k in range(1, CONV_WIDTH):
        back = jnp.concatenate(halos[k - 1::-1] + [x[:rows - k * SUBLANES, :]], axis=0)
        y = y + w_ref[CONV_WIDTH - 1 - k:CONV_WIDTH - k, :] * back
    return y


LRU_TL = 512
LRU_SUB = 256


def _lru_body(x_ref, g_ref, w_ref, cw_ref, cb_ref, wr_ref, wi_ref, br_ref, bi_ref, lam_ref, perm_ref, unperm_ref,
              o_ref, hn_ref, halo_ref, carry_ref):
    first = pl.program_id(1) == 0
    w = o_ref.shape[1]
    sub = perm_ref.shape[0]
    steps = sub // SUBLANES

    @pl.when(first)
    def _():
        carry_ref[...] = jnp.zeros_like(carry_ref)
        halo_ref[...] = jnp.zeros_like(halo_ref)

    for t in range(o_ref.shape[0] // sub):
        rows = slice(t * sub, (t + 1) * sub)
        hn = _rmsnorm(x_ref[rows, :], g_ref[...]).astype(BF16)
        hn_ref[rows, :] = hn
        hn = jnp.dot(perm_ref[...], hn, preferred_element_type=F32).astype(BF16)
        proj = jnp.dot(hn, w_ref[...], preferred_element_type=F32)
        gate = proj[:, w:]

        xc = _conv_segments(proj[:, :w], halo_ref, cw_ref, cb_ref)
        xcb = xc.astype(BF16)
        pre_r, pre_i = [], []
        for s in range(w // GATE_SLAB):
            slab = xcb[:, s * GATE_SLAB:(s + 1) * GATE_SLAB]
            pre_r.append(jnp.dot(slab, wr_ref[s], preferred_element_type=F32))
            pre_i.append(jnp.dot(slab, wi_ref[s], preferred_element_type=F32))
        r = _sigmoid(jnp.concatenate(pre_r, axis=1) + br_ref[...])
        i = _sigmoid(jnp.concatenate(pre_i, axis=1) + bi_ref[...])
        log2_a = (-LRU_C * LOG2_E * r) * _softplus(-lam_ref[...])
        a = jnp.exp2(log2_a)
        z = 1.0 - a * a
        u = jnp.where(z > 0.0, z * lax.rsqrt(z), 0.0) * (i * xc)

        h_j = jnp.zeros((SUBLANES, w), F32)
        p_j = jnp.ones((SUBLANES, w), F32)
        hs, ps = [], []
        for j in range(steps):
            a_j = a[j * SUBLANES:(j + 1) * SUBLANES, :]
            h_j = a_j * h_j + u[j * SUBLANES:(j + 1) * SUBLANES, :]
            p_j = a_j * p_j
            hs.append(h_j)
            ps.append(p_j)
        c = carry_ref[0:1, :]
        entering = []
        for seg in range(SUBLANES):
            entering.append(c)
            c = p_j[seg:seg + 1, :] * c + h_j[seg:seg + 1, :]
        carry_ref[0:1, :] = c
        enter = jnp.concatenate(entering, axis=0)
        h = jnp.concatenate([hs[j] + ps[j] * enter for j in range(steps)], axis=0)
        out = (h * _gelu_tanh(gate)).astype(BF16)
        o_ref[rows, :] = jnp.dot(unperm_ref[...], out, preferred_element_type=F32).astype(BF16)


def _lru(x3, g, w_lru, cw, cb, wr_bd, wi_bd, br, bi, lam):
    b, l, d = x3.shape
    width = lam.shape[1]
    perms = _segment_perm(LRU_SUB)
    return pl.pallas_call(
        _lru_body,
        grid=(b, l // LRU_TL),
        in_specs=[pl.BlockSpec((None, LRU_TL, d), lambda bi_, li: (bi_, li, 0))]
        + [_resident(g.shape), w_lru[1]]
        + [_resident(a.shape) for a in (cw, cb, wr_bd, wi_bd, br, bi, lam, *perms)],
        out_specs=[pl.BlockSpec((None, LRU_TL, width), lambda bi_, li: (bi_, li, 0)),
                   pl.BlockSpec((None, LRU_TL, d), lambda bi_, li: (bi_, li, 0))],
        out_shape=[jax.ShapeDtypeStruct((b, l, width), BF16),
                   jax.ShapeDtypeStruct((b, l, d), BF16)],
        scratch_shapes=[
            pltpu.VMEM((CONV_WIDTH - 1, SUBLANES, width), F32),
            pltpu.VMEM((SUBLANES, width), F32),
        ],
        compiler_params=_params("parallel", "arbitrary"),
        name="lru",
    )(x3, g, w_lru[0], cw, cb, wr_bd, wi_bd, br, bi, lam, *perms)


def _block_diag_slabs(w):
    nb, bs, _ = w.shape
    per = GATE_SLAB // bs
    w = w.reshape(nb // per, per, bs, bs)
    eye = jnp.eye(per, dtype=w.dtype)
    bd = jnp.einsum("spij,pq->spiqj", w, eye).reshape(nb // per, GATE_SLAB, GATE_SLAB)
    return bd.astype(BF16)


def _ssd_body(n_heads, hn_ref, w_ref, cw_ref, cb_ref, dtb_ref, alog_ref, dvec_ref, nrm_ref,
              exp_ref, perm_ref, unperm_ref, o_ref, proj_ref, halo_ref, state_ref):
    first = pl.program_id(1) == 0
    t = SSD_CHUNK
    width = n_heads * SSD_HEAD_DIM
    gw = SSD_STATE
    conv_ch = width + 2 * SSD_GROUPS * gw
    n_chunks = hn_ref.shape[0] // t

    @pl.when(first)
    def _():
        state_ref[...] = jnp.zeros_like(state_ref)
        halo_ref[...] = jnp.zeros_like(halo_ref)

    hn = jnp.concatenate([jnp.dot(perm_ref[...], hn_ref[c * t:(c + 1) * t, :], preferred_element_type=F32).astype(BF16)
                          for c in range(n_chunks)], axis=0)
    proj_ref[...] = jnp.dot(hn, w_ref[...], preferred_element_type=F32)

    steps = t // SUBLANES
    time_of = lambda i: (i & (SUBLANES - 1)) * steps + lax.shift_right_logical(i, SUBLANE_SHIFT)
    causal = (time_of(lax.broadcasted_iota(jnp.int32, (t, t), 0))
              >= time_of(lax.broadcasted_iota(jnp.int32, (t, t), 1)))
    tri = jnp.where(causal, 1.0, 0.0).astype(BF16)
    lane = lax.broadcasted_iota(jnp.int32, (t, LANES), 1)
    lo = lane < SSD_HEAD_DIM
    rlo = lax.broadcasted_iota(jnp.int32, (LANES, gw), 0) < SSD_HEAD_DIM

    dt = _softplus(proj_ref[:, width + conv_ch:] + dtb_ref[...])
    adt = dt * (-LOG2_E * jnp.exp(alog_ref[...]))
    cs = jnp.concatenate([_split_dot(adt[c * t:(c + 1) * t, :], tri, parts=3, left=True)
                          for c in range(n_chunks)], axis=0)
    dt_w = _split_dot(dt, exp_ref[...], parts=2)
    cs_w = _split_dot(cs, exp_ref[...], parts=2)

    for c in range(n_chunks):
        rows = slice(c * t, (c + 1) * t)
        out = _ssd_chunk(proj_ref[rows, 0:width], proj_ref[rows, width:width + conv_ch],
                         cs[rows, :], dt_w[rows, :], cs_w[rows, :], n_heads, causal, lo, rlo, halo_ref, state_ref,
                         cw_ref, cb_ref, dvec_ref, nrm_ref)
        o_ref[rows, :] = jnp.dot(unperm_ref[...], out, preferred_element_type=F32).astype(BF16)


def _ssd_chunk(z, xbc_raw, cs, dt_w, cs_w, n_heads, causal, lo, rlo, halo_ref, state_ref,
               cw_ref, cb_ref, dvec_ref, nrm_ref):
    t = SSD_CHUNK
    width = n_heads * SSD_HEAD_DIM
    gw = SSD_STATE
    heads_per_group = n_heads // SSD_GROUPS
    xbc = _silu(_conv_segments(xbc_raw, halo_ref, cw_ref, cb_ref))
    xs = xbc[:, :width]
    cs_t = cs.T
    cs_last = cs[t - 1:t, :]
    cdec = jnp.exp2(cs_last)
    ecs_w = jnp.exp2(cs_w)
    dte_w = jnp.exp2(cs_w[t - 1:t, :] - cs_w)

    cbs = []
    for g in range(SSD_GROUPS):
        bm = xbc[:, width + g * gw: width + (g + 1) * gw].astype(BF16)
        cm = xbc[:, width + (SSD_GROUPS + g) * gw: width + (SSD_GROUPS + g + 1) * gw].astype(BF16)
        cb = lax.dot_general(cm, bm, (((1,), (1,)), ((), ())), preferred_element_type=F32)
        cbs.append((bm, cm, cb))

    ys = []
    for p in range(n_heads // 2):
        h0, h1 = 2 * p, 2 * p + 1
        bm, cm, cb = cbs[h0 // heads_per_group]
        sl = slice(p * LANES, (p + 1) * LANES)
        xs_p = xs[:, sl]
        xdt = xs_p * dt_w[:, sl]
        ms = []
        for h in (h0, h1):
            seg = cs[:, h:h + 1] - cs_t[h:h + 1, :]
            ms.append((cb * jnp.where(causal, jnp.exp2(seg), 0.0)).astype(BF16))
        m_cat = jnp.concatenate(ms, axis=1)
        xdt_bd = jnp.concatenate([jnp.where(lo, xdt, 0.0), jnp.where(lo, 0.0, xdt)], axis=0).astype(BF16)
        y = jnp.dot(m_cat, xdt_bd, preferred_element_type=F32)
        prev = state_ref[sl, :]
        y_off = lax.dot_general(cm, prev.astype(BF16), (((1,), (1,)), ((), ())), preferred_element_type=F32)
        y = y + y_off * ecs_w[:, sl]
        xw = (xdt * dte_w[:, sl]).astype(BF16)
        st = lax.dot_general(xw, bm, (((0,), (0,)), ((), ())), preferred_element_type=F32)
        dec = jnp.where(rlo, cdec[:, h0:h0 + 1], cdec[:, h1:h1 + 1])
        state_ref[sl, :] = prev * dec + st
        ys.append(y + dvec_ref[:, sl] * xs_p)

    y = jnp.concatenate(ys, axis=1) * _silu(z)
    gsz = width // SSD_GROUPS
    outs = []
    for g in range(SSD_GROUPS):
        yg = y[:, g * gsz:(g + 1) * gsz]
        outs.append(yg * lax.rsqrt(jnp.mean(yg * yg, axis=-1, keepdims=True) + NORM_EPS))
    return (jnp.concatenate(outs, axis=1) * nrm_ref[...]).astype(BF16)


SSD_TS = 512


def _ssd(hn3, w_ssd, cw, cb, dt_bias, a_log, d_vec, nrm, n_heads):
    b, l, d = hn3.shape
    width = n_heads * SSD_HEAD_DIM
    conv_ch = width + 2 * SSD_GROUPS * SSD_STATE
    expand = jnp.asarray(np.arange(LANES)[:, None] == np.arange(width)[None, :] // SSD_HEAD_DIM,
                         dtype=BF16)
    consts = (cw, cb, dt_bias, a_log, d_vec, nrm, expand, *_segment_perm(SSD_CHUNK))
    return pl.pallas_call(
        functools.partial(_ssd_body, n_heads),
        grid=(b, l // SSD_TS),
        in_specs=[pl.BlockSpec((None, SSD_TS, d), lambda bi_, ci: (bi_, ci, 0))]
        + [w_ssd[1]] + [_resident(a.shape) for a in consts],
        out_specs=pl.BlockSpec((None, SSD_TS, width), lambda bi_, ci: (bi_, ci, 0)),
        out_shape=jax.ShapeDtypeStruct((b, l, width), BF16),
        scratch_shapes=[
            pltpu.VMEM((SSD_TS, w_ssd[0].shape[-1]), F32),
            pltpu.VMEM((CONV_WIDTH - 1, SUBLANES, conv_ch), F32),
            pltpu.VMEM((width, SSD_STATE), F32),
        ],
        compiler_params=_params("parallel", "arbitrary"),
        name="ssd",
    )(hn3, w_ssd[0], *consts)


QKV_TM = 512
Q_SCALE = ATT_HEAD_DIM ** -0.5 * LOG2_E
PERM_ROWS = 256


def _split_dot(x, m, parts=2, left=False):
    acc, rem = None, x
    for k in range(parts):
        piece = rem.astype(BF16)
        term = jnp.dot(m, piece, preferred_element_type=F32) if left else jnp.dot(piece, m, preferred_element_type=F32)
        acc = term if acc is None else acc + term
        if k + 1 < parts:
            rem = rem - piece.astype(F32)
    return acc


def _slab_map(fn, x):
    return jnp.concatenate([fn(x[:, c * GATE_SLAB:(c + 1) * GATE_SLAB]) for c in range(x.shape[1] // GATE_SLAB)],
                           axis=1)


def _head_norm_rope(y, gain, cos, sin, gsum, rot):
    ss = _slab_map(lambda v: _split_dot(v, gsum, parts=1), y * y)
    yn = y * lax.rsqrt(ss * (1.0 / ATT_HEAD_DIM) + NORM_EPS) * gain
    partner = _slab_map(lambda v: _split_dot(v, rot), yn)
    tile = lambda tbl: jnp.concatenate([tbl] * (y.shape[1] // LANES), axis=1)
    return yn * tile(cos) + partner * tile(sin)


def _qkv_body(x_ref, g_ref, w_ref, qg_ref, kg_ref, cos_ref, sin_ref, gsum_ref, rot_ref, p1_ref, p4_ref, p16_ref,
              *refs):
    outs, h_ref = refs[:9], refs[9]
    j = pl.program_id(2)

    @pl.when(j == 0)
    def _():
        h_ref[...] = _rmsnorm(x_ref[...], g_ref[...]).astype(BF16)

    sub = PERM_ROWS

    def section(o1, o4, o16, gain_ref=None, grouped=False):
        y = jnp.dot(h_ref[...], w_ref[...], preferred_element_type=F32)
        if gain_ref is not None:
            y = _head_norm_rope(y, gain_ref[...], cos_ref[...], sin_ref[...], gsum_ref[...], rot_ref[...])
        y_all = y.astype(BF16)
        for t in range(x_ref.shape[0] // sub):
            rows = slice(t * sub, (t + 1) * sub)
            yb = y_all[rows, :]
            if grouped:
                o1[rows, :] = jnp.dot(p1_ref[...], yb, preferred_element_type=F32).astype(BF16)
            else:
                o1[rows, :] = yb
            for perm_ref, o in ((p4_ref, o4), (p16_ref, o16)):
                dil = o.shape[0]
                n = sub // dil
                yp = jnp.dot(perm_ref[...], yb, preferred_element_type=F32).astype(BF16)
                for r in range(dil):
                    o[r, t * n:(t + 1) * n, :] = yp[r * n:(r + 1) * n, :]

    @pl.when(j == 0)
    def _():
        section(*outs[0:3], gain_ref=qg_ref, grouped=True)

    @pl.when(j == 1)
    def _():
        section(*outs[3:6], gain_ref=kg_ref)

    @pl.when(j == 2)
    def _():
        section(*outs[6:9])


def _qkv(x3, g, w_qkv, layer, qg, kg, cos, sin):
    b, l, d = x3.shape
    tm = QKV_TM
    out_specs, out_shape = [], []
    for _ in range(3):
        out_specs.append(pl.BlockSpec((None, tm, d), lambda bi_, i, j: (bi_, i, 0)))
        out_shape.append(jax.ShapeDtypeStruct((b, l, d), BF16))
        for dil in DILATIONS[1:]:
            out_specs.append(pl.BlockSpec((None, dil, tm // dil, d), lambda bi_, i, j: (bi_, 0, i, 0)))
            out_shape.append(jax.ShapeDtypeStruct((b, dil, l // dil, d), BF16))
    consts = _qkv_constants(tm)
    tbl = lambda: pl.BlockSpec((tm, LANES), lambda bi_, i, j: (i, 0))
    return pl.pallas_call(
        _qkv_body,
        grid=(b, l // tm, 3),
        in_specs=[
            pl.BlockSpec((None, tm, d), lambda bi_, i, j: (bi_, i, 0)),
            _resident((1, d)),
            pl.BlockSpec((None, d, d), lambda bi_, i, j: (layer, 0, j)),
            _resident((1, d)), _resident((1, d)), tbl(), tbl(),
        ] + [_resident(c.shape) for c in consts],
        out_specs=out_specs,
        out_shape=out_shape,
        scratch_shapes=[pltpu.VMEM((tm, d), BF16)],
        compiler_params=_params("parallel", "parallel", "arbitrary"),
        name="qkv",
    )(x3, g, w_qkv, qg, kg, cos, sin, *consts)


def _qkv_constants(tm):
    lane = np.arange(GATE_SLAB)
    gsum = (lane[:, None] // ATT_HEAD_DIM == lane[None, :] // ATT_HEAD_DIM)
    half = ROPE_DIM // 2
    pos = lane % ATT_HEAD_DIM
    partner = np.where(pos < half, lane + half, np.where(pos < ROPE_DIM, lane - half, -1))
    rot = lane[:, None] == partner[None, :]
    out_row = np.arange(PERM_ROWS)
    group = SPAN // ACC_DIL
    within = out_row % SPAN
    src = out_row - within + (within % group) * ACC_DIL + within // group
    perms = [src[:, None] == np.arange(PERM_ROWS)[None, :]]
    for dil in DILATIONS[1:]:
        sub = PERM_ROWS
        rows = sub // dil
        assert rows % BF16_ROWS == 0 and tm % sub == 0
        out_row = np.arange(sub)
        src = (out_row % rows) * dil + out_row // rows
        perms.append(src[:, None] == np.arange(sub)[None, :])
    return tuple(jnp.asarray(m, dtype=BF16) for m in (gsum, rot, *perms))


def _rope_tables(l):
    half = ROPE_DIM // 2
    pos = jnp.arange(l, dtype=F32)
    inv = ROPE_THETA ** (-2.0 * jnp.arange(half, dtype=F32) / ROPE_DIM)
    ang = pos[:, None] * inv[None, :]
    cos, sin = jnp.cos(ang), jnp.sin(ang)
    pad = ATT_HEAD_DIM - ROPE_DIM
    cos_h = jnp.concatenate([cos, cos, jnp.ones((l, pad), F32)], axis=1)
    sin_h = jnp.concatenate([-sin, sin, jnp.zeros((l, pad), F32)], axis=1)
    rep = LANES // ATT_HEAD_DIM
    return tuple(jnp.concatenate([tb] * rep, axis=1) for tb in (cos_h, sin_h))


def _attn_body(q1, k1c, k1p, v1c, v1p, q4, k4c, k4p, v4c, v4p, q16, k16c, k16p, v16c, v16p,
               o_ref, s_scr, p_scr, mx_scr, bias_scr, tok_scr, acc_ref, m_ref, l_ref):
    first = pl.program_id(2) == 0
    n_pairs = acc_ref.shape[0]
    n_blocks = SUPER // SPAN
    trans_b = (((1,), (1,)), ((), ()))

    row = lax.broadcasted_iota(jnp.int32, (SPAN, 2 * SPAN), 0)
    kj = lax.broadcasted_iota(jnp.int32, (SPAN, 2 * SPAN), 1)
    group = SPAN // ACC_DIL
    for base, qi in ((0, row), (2, (row % group) * ACC_DIL + row // group)):
        band = (kj >= qi) & (kj <= qi + SPAN)
        bias_scr[base] = jnp.where(band, 0.0, -jnp.inf)
        bias_scr[base + 1] = jnp.where(band & (kj >= SPAN), 0.0, -jnp.inf)

    lo = lax.broadcasted_iota(jnp.int32, (SPAN, LANES), 1) < ATT_HEAD_DIM
    lo2 = lax.broadcasted_iota(jnp.int32, (2 * SPAN, LANES), 1) < ATT_HEAD_DIM
    ones_lo = jnp.where(lo2, 1.0, 0.0).astype(BF16)
    ones_hi = jnp.where(lo2, 0.0, 1.0).astype(BF16)
    lanes = lambda p: slice(p * LANES, (p + 1) * LANES)

    def run_pattern(dil, q_get, cur_prev_k, cur_prev_v):
        per_res = SUPER // (SPAN * dil)
        overwrite = dil == DILATIONS[0]

        def split(f):
            if per_res == 1:
                return f, 0
            if dil == 1:
                return 0, f
            if isinstance(f, int):
                return f % dil, f // dil
            return f & (dil - 1), lax.shift_right_logical(f, dil.bit_length() - 1)

        def slab(refs, r, nb):
            cur, prev = refs
            if dil > 1:
                cur, prev = cur.at[r], prev.at[r]
            if isinstance(nb, int) and nb == 0:
                return jnp.concatenate([prev[...], cur[0:SPAN, :]], axis=0)
            start = (nb - 1) * SPAN
            return cur[pl.ds(start if isinstance(nb, int) else pl.multiple_of(start, SPAN), 2 * SPAN), :]

        def stage_scores(f, slot):
            r, nb = split(f)
            if isinstance(nb, int):
                flag = jnp.where(first, 1, 0) if nb == 0 else 0
            else:
                flag = jnp.where(first & (nb == 0), 1, 0)
            bias = bias_scr[flag + (2 if overwrite else 0)]
            bias2 = jnp.concatenate([bias, bias], axis=0)
            q = q_get(r, nb)
            ks = slab(cur_prev_k, r, nb)
            for p in range(n_pairs):
                qp = q[:, lanes(p)]
                zero = jnp.zeros_like(qp)
                q2 = jnp.concatenate([jnp.where(lo, qp, zero), jnp.where(lo, zero, qp)], axis=0)
                s_scr[slot, p] = lax.dot_general(q2, ks[:, lanes(p)], trans_b, preferred_element_type=F32) + bias2

        def stage_softmax(src, dst):
            for p in range(n_pairs):
                s = s_scr[src, p]
                mx = jnp.max(s, axis=1, keepdims=True)
                pr = jnp.exp2(s - mx).astype(BF16)
                p_scr[dst, p, :, 0:2 * SPAN] = pr[:SPAN]
                p_scr[dst, p, :, 2 * SPAN:] = pr[SPAN:]
                mx_scr[dst, p] = jnp.where(lo, mx[:SPAN], mx[SPAN:])

        def stage_values(f, slot):
            r, nb = split(f)
            vs = slab(cur_prev_v, r, nb)
            if overwrite:
                grp = SPAN // ACC_DIL
                start = nb * grp if isinstance(nb, int) else pl.multiple_of(nb * grp, grp)
                pieces = [(slice(c * grp, (c + 1) * grp), c, pl.ds(start, grp)) for c in range(ACC_DIL)]
            elif dil == ACC_DIL:
                pieces = [(slice(0, SPAN), r, q_rows(nb))]
            else:
                sub = dil // ACC_DIL
                pieces = [(slice(0, SPAN), r & (ACC_DIL - 1),
                           pl.ds(lax.shift_right_logical(r, ACC_SHIFT) if not isinstance(r, int) else r // ACC_DIL,
                                 SPAN, stride=sub))]
            for p in range(n_pairs):
                vp = vs[:, lanes(p)]
                zero = jnp.zeros_like(vp)
                v_bd = jnp.concatenate([jnp.concatenate([jnp.where(lo2, vp, zero), ones_lo], axis=1),
                                        jnp.concatenate([jnp.where(lo2, zero, vp), ones_hi], axis=1)], axis=0)
                pv = jnp.dot(p_scr[slot, p], v_bd, preferred_element_type=F32)
                o_blk, l_blk = pv[:, :LANES], pv[:, LANES:]
                m_blk = mx_scr[slot, p]
                for src, cls, rows in pieces:
                    m_new, l_new, o_new = m_blk[src], l_blk[src], o_blk[src]
                    if overwrite:
                        m_ref[p, cls, rows, :], l_ref[p, cls, rows, :], acc_ref[p, cls, rows, :] = m_new, l_new, o_new
                        continue
                    m_old = m_ref[p, cls, rows, :]
                    m = jnp.maximum(m_old, m_new)
                    e_old = jnp.exp2(m_old - m)
                    e_new = jnp.exp2(m_new - m)
                    l_ref[p, cls, rows, :] = e_old * l_ref[p, cls, rows, :] + e_new * l_new
                    acc_ref[p, cls, rows, :] = e_old * acc_ref[p, cls, rows, :] + e_new * o_new
                    m_ref[p, cls, rows, :] = m

        n_rounds = n_blocks // 2

        def round_(k, odd, scores=True, softmax=True, values=True):
            wr = 2 if odd else 0
            rd = 2 - wr
            if values:
                stage_values(2 * k - 4, rd)
                stage_values(2 * k - 3, rd + 1)
            if softmax:
                stage_softmax(rd, wr)
                stage_softmax(rd + 1, wr + 1)
            if scores:
                stage_scores(2 * k, wr)
                stage_scores(2 * k + 1, wr + 1)

        round_(0, False, softmax=False, values=False)
        round_(1, True, values=False)
        round_(2, False)
        round_(3, True)

        def body(j, c):
            round_(2 * j, False)
            round_(2 * j + 1, True)
            return c

        lax.fori_loop(2, n_rounds // 2, body, 0)
        round_(n_rounds, False, scores=False)
        round_(n_rounds + 1, True, scores=False, softmax=False)

    def q_rows(nb):
        return pl.ds(nb * SPAN if isinstance(nb, int) else pl.multiple_of(nb * SPAN, SPAN), SPAN)

    run_pattern(DILATIONS[0], lambda r, nb: q1[q_rows(nb), :], (k1c, k1p), (v1c, v1p))
    run_pattern(DILATIONS[1], lambda r, nb: q4[r, q_rows(nb), :], (k4c, k4p), (v4c, v4p))
    run_pattern(DILATIONS[2], lambda r, nb: q16[r], (k16c, k16p), (v16c, v16p))

    for p in range(n_pairs):
        for cls in range(ACC_DIL):
            tok_scr[p, pl.ds(cls, SUPER // ACC_DIL, stride=ACC_DIL), :] = acc_ref[p, cls] / l_ref[p, cls]
        o_ref[:, lanes(p)] = tok_scr[p].astype(BF16)


def _attention(qkv_outs):
    q1, q4, q16, k1, k4, k16, v1, v4, v16 = qkv_outs
    b, l, d = q1.shape
    w = ATT_LANES
    n_pairs = w // LANES
    d4, d16 = DILATIONS[1], DILATIONS[2]
    cur1 = pl.BlockSpec((None, SUPER, w), lambda bi_, g, s: (bi_, s, g))
    prev1 = pl.BlockSpec((None, SPAN, w), lambda bi_, g, s: (bi_, jnp.maximum(s * (SUPER // SPAN) - 1, 0), g))
    cur4 = pl.BlockSpec((None, d4, SUPER // d4, w), lambda bi_, g, s: (bi_, 0, s, g))
    prev4 = pl.BlockSpec((None, d4, SPAN, w),
                         lambda bi_, g, s: (bi_, 0, jnp.maximum(s * (SUPER // (d4 * SPAN)) - 1, 0), g))
    cur16 = pl.BlockSpec((None, d16, SPAN, w), lambda bi_, g, s: (bi_, 0, s, g))
    prev16 = pl.BlockSpec((None, d16, SPAN, w), lambda bi_, g, s: (bi_, 0, jnp.maximum(s - 1, 0), g))
    return pl.pallas_call(
        _attn_body,
        grid=(b, d // w, l // SUPER),
        in_specs=[cur1, cur1, prev1, cur1, prev1,
                  cur4, cur4, prev4, cur4, prev4,
                  cur16, cur16, prev16, cur16, prev16],
        out_specs=pl.BlockSpec((None, SUPER, w), lambda bi_, g, s: (bi_, s, g)),
        out_shape=jax.ShapeDtypeStruct((b, l, d), BF16),
        scratch_shapes=[
            pltpu.VMEM((4, n_pairs, 2 * SPAN, 2 * SPAN), F32),
            pltpu.VMEM((4, n_pairs, SPAN, 4 * SPAN), BF16),
            pltpu.VMEM((4, n_pairs, SPAN, LANES), F32),
            pltpu.VMEM((4, SPAN, 2 * SPAN), F32),
            pltpu.VMEM((n_pairs, SUPER, LANES), F32),
        ] + [pltpu.VMEM((n_pairs, ACC_DIL, SUPER // ACC_DIL, LANES), F32)] * 3,
        compiler_params=_params("parallel", "parallel", "arbitrary"),
        name="dilated_attention",
    )(q1, k1, k1, v1, v1, q4, k4, k4, v4, v4, q16, k16, k16, v16, v16)


def _recurrent_layer(x, b, l, i, norm_g, w_lru, w_ssd, w_out, lru_conv_w, lru_conv_b, lru_w_r, lru_b_r, lru_w_i,
                     lru_b_i, lru_lambda, ssd_conv_w, ssd_conv_b, ssd_dt_bias, ssd_a_log, ssd_d, ssd_norm):
    d = x.shape[1]
    width = lru_lambda.shape[0]
    n_heads = ssd_a_log.shape[0]
    row = lambda v: v.reshape(1, -1)
    pad_lanes = lambda v: jnp.pad(v.reshape(1, -1), ((0, 0), (0, LANES - v.shape[0])))
    x3 = x.reshape(b, l, d)
    out_a, hn3 = _lru(x3, row(norm_g), _layer_block(w_lru, i, (d, 2 * width)), lru_conv_w, row(lru_conv_b),
                      _block_diag_slabs(lru_w_r), _block_diag_slabs(lru_w_i), row(lru_b_r), row(lru_b_i),
                      row(lru_lambda))
    out_b = _ssd(hn3, _layer_block(w_ssd, i), ssd_conv_w, row(ssd_conv_b), pad_lanes(ssd_dt_bias),
                 pad_lanes(ssd_a_log), row(jnp.repeat(ssd_d, SSD_HEAD_DIM)), row(ssd_norm), n_heads)
    return ([out_a.reshape(b * l, width), out_b.reshape(b * l, -1)],
            [_layer_block(w_out, i, (width, d), (0, 0)), _layer_block(w_out, i, (width, d), (1, 0))])


def _attention_layer(x, b, l, i, norm_g, w_qkv, w_out, q_norm, k_norm):
    d = x.shape[1]
    heads = d // ATT_HEAD_DIM
    row = lambda v: v.reshape(1, -1)
    outs = _qkv(x.reshape(b, l, d), row(norm_g), w_qkv, i,
                row(jnp.tile(q_norm, heads) * Q_SCALE), row(jnp.tile(k_norm, heads)), *_rope_tables(l))
    o = _attention(outs)
    return [o.reshape(b * l, d)], [_layer_block(w_out, i)]


def kernel(x, rec_norm, rec_w_in, lru_conv_w, lru_conv_b, lru_w_r, lru_b_r, lru_w_i, lru_b_i, lru_lambda,
           ssd_conv_w, ssd_conv_b, ssd_dt_bias, ssd_a_log, ssd_d, ssd_norm, rec_w_out, att_norm, att_w_qkv,
           att_q_norm, att_k_norm, att_w_out, ffn_norm, ffn_w_gate_up, ffn_w_down):
    b, l, d = x.shape
    depth = ffn_norm.shape[0]
    h = x.reshape(b * l, d)
    lru_cols = 2 * lru_lambda.shape[1]
    w_lru = rec_w_in.astype(BF16)
    dt_pad = LANES - ssd_a_log.shape[1]
    w_ssd = jnp.pad(rec_w_in[:, :, lru_cols:], ((0, 0), (0, 0), (0, dt_pad))).astype(BF16)
    rec_wo, att_wqkv, att_wo = rec_w_out.astype(BF16), att_w_qkv.astype(BF16), att_w_out.astype(BF16)
    ffn_wgu, ffn_wd = ffn_w_gate_up.astype(BF16), ffn_w_down.astype(BF16)
    for layer in range(depth):
        i = layer // 2
        if layer % 2 == 0:
            mixes, wos = _recurrent_layer(h, b, l, i, rec_norm[i], w_lru, w_ssd, rec_wo, lru_conv_w[i],
                                          lru_conv_b[i], lru_w_r[i], lru_b_r[i], lru_w_i[i], lru_b_i[i],
                                          lru_lambda[i], ssd_conv_w[i], ssd_conv_b[i], ssd_dt_bias[i],
                                          ssd_a_log[i], ssd_d[i], ssd_norm[i])
        else:
            mixes, wos = _attention_layer(h, b, l, i, att_norm[i], att_wqkv, att_wo, att_q_norm[i], att_k_norm[i])
        h = _mix_ffn(h, mixes, wos, ffn_norm[layer].reshape(1, d), _layer_block(ffn_wgu, layer),
                     _layer_block(ffn_wd, layer))
    return h.reshape(b, l, d)
```

```python
import functools
import math

import jax
import jax.numpy as jnp
import numpy as np
from jax import lax
from jax.experimental import pallas as pl
from jax.experimental.pallas import tpu as pltpu

F32 = jnp.float32
BF16 = jnp.bfloat16

NORM_EPS = 1e-6
LOG2_E = math.log2(math.e)
CONV_WIDTH = 4
LRU_C = 8.0
GATE_SLAB = 256
SSD_HEAD_DIM = 64
SSD_GROUPS = 2
SSD_STATE = 128
SSD_CHUNK = 128
ATT_HEAD_DIM = 64
ROPE_DIM = 16
ROPE_THETA = 500000.0
SPAN = 128
DILATIONS = (1, 4, 16)
SUPER = SPAN * DILATIONS[-1]
ATT_LANES = 256
ACC_DIL = DILATIONS[1]
ACC_SHIFT = ACC_DIL.bit_length() - 1
LANES = 128
SUBLANES = 8
SUBLANE_SHIFT = SUBLANES.bit_length() - 1
BF16_ROWS = 16
VMEM_LIMIT_BYTES = 56 * 1024 * 1024


def _params(*semantics):
    return pltpu.CompilerParams(dimension_semantics=semantics, vmem_limit_bytes=VMEM_LIMIT_BYTES)


def _rmsnorm(x, g):
    ms = jnp.mean(x * x, axis=-1, keepdims=True)
    return x * lax.rsqrt(ms + NORM_EPS) * g


def _sigmoid(x):
    return 1.0 / (1.0 + jnp.exp2(x * (-LOG2_E)))


def _silu(x):
    return x * _sigmoid(x)


def _softplus(x):
    return jnp.maximum(x, 0.0) + jnp.log1p(jnp.exp(-jnp.abs(x)))


def _gelu_tanh(x):
    c = math.sqrt(2.0 / math.pi)
    half_x = 0.5 * x
    return half_x + half_x * jnp.tanh(x * (c + (c * 0.044715) * (x * x)))


FFN_TM = 1024
FFN_TH = 256


def _ffn_body(n_mix, x_ref, *refs):
    mix_refs, wo_refs = refs[:n_mix], refs[n_mix:2 * n_mix]
    g_ref, wgu_ref, wd_ref, o_ref, h_ref, act_ref = refs[2 * n_mix:]
    x1 = x_ref[...]
    for mix_ref, wo_ref in zip(mix_refs, wo_refs):
        x1 = x1 + jnp.dot(mix_ref[...], wo_ref[...], preferred_element_type=F32)
    o_ref[...] = x1
    h_ref[...] = _rmsnorm(x1, g_ref[...]).astype(BF16)
    hid = act_ref.shape[1]
    for c in range(hid // FFN_TH):
        cols = slice(c * FFN_TH, (c + 1) * FFN_TH)
        up_cols = slice(hid + c * FFN_TH, hid + (c + 1) * FFN_TH)
        gate = jnp.dot(h_ref[...], wgu_ref[:, cols], preferred_element_type=F32)
        up = jnp.dot(h_ref[...], wgu_ref[:, up_cols], preferred_element_type=F32)
        act_ref[:, cols] = (_silu(gate) * up).astype(BF16)
    o_ref[...] += jnp.dot(act_ref[...], wd_ref[...], preferred_element_type=F32)


def _resident(shape):
    return pl.BlockSpec(shape, lambda *_: (0,) * len(shape), pipeline_mode=pl.Buffered(1))


def _layer_block(stack, layer, block=None, index=None):
    block = tuple(stack.shape[1:]) if block is None else tuple(block)
    index = (layer,) + (tuple(index) if index is not None else (0,) * len(block))
    return stack, pl.BlockSpec((None,) + block, lambda *_: index, pipeline_mode=pl.Buffered(1))


def _mix_ffn(x, mixes, wos, g, wgu, wd):
    t, d = x.shape
    hid = wd[0].shape[1]
    weights = (*wos, (g, _resident(g.shape)), wgu, wd)
    rows = lambda a: pl.BlockSpec((FFN_TM, a.shape[1]), lambda i: (i, 0))
    return pl.pallas_call(
        functools.partial(_ffn_body, len(mixes)),
        grid=(t // FFN_TM,),
        in_specs=[rows(x)] + [rows(a) for a in mixes] + [spec for _, spec in weights],
        out_specs=rows(x),
        out_shape=jax.ShapeDtypeStruct((t, d), F32),
        scratch_shapes=[pltpu.VMEM((FFN_TM, d), BF16), pltpu.VMEM((FFN_TM, hid), BF16)],
        compiler_params=_params("parallel"),
        name="mix_ffn",
    )(x, *mixes, *[a for a, _ in weights])


def _segment_perm(rows):
    steps = rows // SUBLANES
    rho = np.arange(rows)
    time = (rho % SUBLANES) * steps + rho // SUBLANES
    perm = time[:, None] == np.arange(rows)[None, :]
    return jnp.asarray(perm, dtype=BF16), jnp.asarray(perm.T, dtype=BF16)


def _conv_segments(x, halo_ref, w_ref, b_ref):
    rows, width = x.shape
    sub0 = lax.broadcasted_iota(jnp.int32, (SUBLANES, width), 0) == 0
    halos = []
    for k in range(1, CONV_WIDTH):
        tail = pltpu.roll(x[rows - k * SUBLANES:rows - (k - 1) * SUBLANES, :], 1, 0)
        halos.append(jnp.where(sub0, halo_ref[k - 1], tail))
        halo_ref[k - 1] = tail
    y = b_ref[...] + w_ref[CONV_WIDTH - 1:CONV_WIDTH, :] * x
    for k in range(1, CONV_WIDTH):
        back = jnp.concatenate(halos[k - 1::-1] + [x[:rows - k * SUBLANES, :]], axis=0)
        y = y + w_ref[CONV_WIDTH - 1 - k:CONV_WIDTH - k, :] * back
    return y


LRU_TL = 512
LRU_SUB = 256


def _lru_body(x_ref, g_ref, w_ref, cw_ref, cb_ref, wr_ref, wi_ref, br_ref, bi_ref, lam_ref, perm_ref, unperm_ref,
              o_ref, hn_ref, halo_ref, carry_ref):
    first = pl.program_id(1) == 0
    w = o_ref.shape[1]
    sub = perm_ref.shape[0]
    steps = sub // SUBLANES

    @pl.when(first)
    def _():
        carry_ref[...] = jnp.zeros_like(carry_ref)
        halo_ref[...] = jnp.zeros_like(halo_ref)

    for t in range(o_ref.shape[0] // sub):
        rows = slice(t * sub, (t + 1) * sub)
        hn = _rmsnorm(x_ref[rows, :], g_ref[...]).astype(BF16)
        hn_ref[rows, :] = hn
        hn = jnp.dot(perm_ref[...], hn, preferred_element_type=F32).astype(BF16)
        proj = jnp.dot(hn, w_ref[...], preferred_element_type=F32)
        gate = proj[:, w:]

        xc = _conv_segments(proj[:, :w], halo_ref, cw_ref, cb_ref)
        xcb = xc.astype(BF16)
        pre_r, pre_i = [], []
        for s in range(w // GATE_SLAB):
            slab = xcb[:, s * GATE_SLAB:(s + 1) * GATE_SLAB]
            pre_r.append(jnp.dot(slab, wr_ref[s], preferred_element_type=F32))
            pre_i.append(jnp.dot(slab, wi_ref[s], preferred_element_type=F32))
        r = _sigmoid(jnp.concatenate(pre_r, axis=1) + br_ref[...])
        i = _sigmoid(jnp.concatenate(pre_i, axis=1) + bi_ref[...])
        log2_a = (-LRU_C * LOG2_E * r) * _softplus(-lam_ref[...])
        a = jnp.exp2(log2_a)
        z = 1.0 - a * a
        u = jnp.where(z > 0.0, z * lax.rsqrt(z), 0.0) * (i * xc)

        h_j = jnp.zeros((SUBLANES, w), F32)
        p_j = jnp.ones((SUBLANES, w), F32)
        hs, ps = [], []
        for j in range(steps):
            a_j = a[j * SUBLANES:(j + 1) * SUBLANES, :]
            h_j = a_j * h_j + u[j * SUBLANES:(j + 1) * SUBLANES, :]
            p_j = a_j * p_j
            hs.append(h_j)
            ps.append(p_j)
        c = carry_ref[0:1, :]
        entering = []
        for seg in range(SUBLANES):
            entering.append(c)
            c = p_j[seg:seg + 1, :] * c + h_j[seg:seg + 1, :]
        carry_ref[0:1, :] = c
        enter = jnp.concatenate(entering, axis=0)
        h = jnp.concatenate([hs[j] + ps[j] * enter for j in range(steps)], axis=0)
        out = (h * _gelu_tanh(gate)).astype(BF16)
        o_ref[rows, :] = jnp.dot(unperm_ref[...], out, preferred_element_type=F32).astype(BF16)


def _lru(x3, g, w_lru, cw, cb, wr_bd, wi_bd, br, bi, lam):
    b, l, d = x3.shape
    width = lam.shape[1]
    perms = _segment_perm(LRU_SUB)
    return pl.pallas_call(
        _lru_body,
        grid=(b, l // LRU_TL),
        in_specs=[pl.BlockSpec((None, LRU_TL, d), lambda bi_, li: (bi_, li, 0))]
        + [_resident(g.shape), w_lru[1]]
        + [_resident(a.shape) for a in (cw, cb, wr_bd, wi_bd, br, bi, lam, *perms)],
        out_specs=[pl.BlockSpec((None, LRU_TL, width), lambda bi_, li: (bi_, li, 0)),
                   pl.BlockSpec((None, LRU_TL, d), lambda bi_, li: (bi_, li, 0))],
        out_shape=[jax.ShapeDtypeStruct((b, l, width), BF16),
                   jax.ShapeDtypeStruct((b, l, d), BF16)],
        scratch_shapes=[
            pltpu.VMEM((CONV_WIDTH - 1, SUBLANES, width), F32),
            pltpu.VMEM((SUBLANES, width), F32),
        ],
        compiler_params=_params("parallel", "arbitrary"),
        name="lru",
    )(x3, g, w_lru[0], cw, cb, wr_bd, wi_bd, br, bi, lam, *perms)


def _block_diag_slabs(w):
    nb, bs, _ = w.shape
    per = GATE_SLAB // bs
    w = w.reshape(nb // per, per, bs, bs)
    eye = jnp.eye(per, dtype=w.dtype)
    bd = jnp.einsum("spij,pq->spiqj", w, eye).reshape(nb // per, GATE_SLAB, GATE_SLAB)
    return bd.astype(BF16)


def _ssd_body(n_heads, hn_ref, w_ref, cw_ref, cb_ref, dtb_ref, alog_ref, dvec_ref, nrm_ref,
              exp_ref, perm_ref, unperm_ref, o_ref, proj_ref, halo_ref, state_ref):
    first = pl.program_id(1) == 0
    t = SSD_CHUNK
    width = n_heads * SSD_HEAD_DIM
    gw = SSD_STATE
    conv_ch = width + 2 * SSD_GROUPS * gw
    n_chunks = hn_ref.shape[0] // t

    @pl.when(first)
    def _():
        state_ref[...] = jnp.zeros_like(state_ref)
        halo_ref[...] = jnp.zeros_like(halo_ref)

    hn = jnp.concatenate([jnp.dot(perm_ref[...], hn_ref[c * t:(c + 1) * t, :], preferred_element_type=F32).astype(BF16)
                          for c in range(n_chunks)], axis=0)
    proj_ref[...] = jnp.dot(hn, w_ref[...], preferred_element_type=F32)

    steps = t // SUBLANES
    time_of = lambda i: (i & (SUBLANES - 1)) * steps + lax.shift_right_logical(i, SUBLANE_SHIFT)
    causal = (time_of(lax.broadcasted_iota(jnp.int32, (t, t), 0))
              >= time_of(lax.broadcasted_iota(jnp.int32, (t, t), 1)))
    tri = jnp.where(causal, 1.0, 0.0).astype(BF16)
    lane = lax.broadcasted_iota(jnp.int32, (t, LANES), 1)
    lo = lane < SSD_HEAD_DIM
    rlo = lax.broadcasted_iota(jnp.int32, (LANES, gw), 0) < SSD_HEAD_DIM

    dt = _softplus(proj_ref[:, width + conv_ch:] + dtb_ref[...])
    adt = dt * (-LOG2_E * jnp.exp(alog_ref[...]))
    cs = jnp.concatenate([_split_dot(adt[c * t:(c + 1) * t, :], tri, parts=3, left=True)
                          for c in range(n_chunks)], axis=0)
    dt_w = _split_dot(dt, exp_ref[...], parts=2)
    cs_w = _split_dot(cs, exp_ref[...], parts=2)

    for c in range(n_chunks):
        rows = slice(c * t, (c + 1) * t)
        out = _ssd_chunk(proj_ref[rows, 0:width], proj_ref[rows, width:width + conv_ch],
                         cs[rows, :], dt_w[rows, :], cs_w[rows, :], n_heads, causal, lo, rlo, halo_ref, state_ref,
                         cw_ref, cb_ref, dvec_ref, nrm_ref)
        o_ref[rows, :] = jnp.dot(unperm_ref[...], out, preferred_element_type=F32).astype(BF16)


def _ssd_chunk(z, xbc_raw, cs, dt_w, cs_w, n_heads, causal, lo, rlo, halo_ref, state_ref,
               cw_ref, cb_ref, dvec_ref, nrm_ref):
    t = SSD_CHUNK
    width = n_heads * SSD_HEAD_DIM
    gw = SSD_STATE
    heads_per_group = n_heads // SSD_GROUPS
    xbc = _silu(_conv_segments(xbc_raw, halo_ref, cw_ref, cb_ref))
    xs = xbc[:, :width]
    cs_t = cs.T
    cs_last = cs[t - 1:t, :]
    cdec = jnp.exp2(cs_last)
    ecs_w = jnp.exp2(cs_w)
    dte_w = jnp.exp2(cs_w[t - 1:t, :] - cs_w)

    cbs = []
    for g in range(SSD_GROUPS):
        bm = xbc[:, width + g * gw: width + (g + 1) * gw].astype(BF16)
        cm = xbc[:, width + (SSD_GROUPS + g) * gw: width + (SSD_GROUPS + g + 1) * gw].astype(BF16)
        cb = lax.dot_general(cm, bm, (((1,), (1,)), ((), ())), preferred_element_type=F32)
        cbs.append((bm, cm, cb))

    ys = []
    for p in range(n_heads // 2):
        h0, h1 = 2 * p, 2 * p + 1
        bm, cm, cb = cbs[h0 // heads_per_group]
        sl = slice(p * LANES, (p + 1) * LANES)
        xs_p = xs[:, sl]
        xdt = xs_p * dt_w[:, sl]
        ms = []
        for h in (h0, h1):
            seg = cs[:, h:h + 1] - cs_t[h:h + 1, :]
            ms.append((cb * jnp.where(causal, jnp.exp2(seg), 0.0)).astype(BF16))
        m_cat = jnp.concatenate(ms, axis=1)
        xdt_bd = jnp.concatenate([jnp.where(lo, xdt, 0.0), jnp.where(lo, 0.0, xdt)], axis=0).astype(BF16)
        y = jnp.dot(m_cat, xdt_bd, preferred_element_type=F32)
        prev = state_ref[sl, :]
        y_off = lax.dot_general(cm, prev.astype(BF16), (((1,), (1,)), ((), ())), preferred_element_type=F32)
        y = y + y_off * ecs_w[:, sl]
        xw = (xdt * dte_w[:, sl]).astype(BF16)
        st = lax.dot_general(xw, bm, (((0,), (0,)), ((), ())), preferred_element_type=F32)
        dec = jnp.where(rlo, cdec[:, h0:h0 + 1], cdec[:, h1:h1 + 1])
        state_ref[sl, :] = prev * dec + st
        ys.append(y + dvec_ref[:, sl] * xs_p)

    y = jnp.concatenate(ys, axis=1) * _silu(z)
    gsz = width // SSD_GROUPS
    outs = []
    for g in range(SSD_GROUPS):
        yg = y[:, g * gsz:(g + 1) * gsz]
        outs.append(yg * lax.rsqrt(jnp.mean(yg * yg, axis=-1, keepdims=True) + NORM_EPS))
    return (jnp.concatenate(outs, axis=1) * nrm_ref[...]).astype(BF16)


SSD_TS = 512


def _ssd(hn3, w_ssd, cw, cb, dt_bias, a_log, d_vec, nrm, n_heads):
    b, l, d = hn3.shape
    width = n_heads * SSD_HEAD_DIM
    conv_ch = width + 2 * SSD_GROUPS * SSD_STATE
    expand = jnp.asarray(np.arange(LANES)[:, None] == np.arange(width)[None, :] // SSD_HEAD_DIM,
                         dtype=BF16)
    consts = (cw, cb, dt_bias, a_log, d_vec, nrm, expand, *_segment_perm(SSD_CHUNK))
    return pl.pallas_call(
        functools.partial(_ssd_body, n_heads),
        grid=(b, l // SSD_TS),
        in_specs=[pl.BlockSpec((None, SSD_TS, d), lambda bi_, ci: (bi_, ci, 0))]
        + [w_ssd[1]] + [_resident(a.shape) for a in consts],
        out_specs=pl.BlockSpec((None, SSD_TS, width), lambda bi_, ci: (bi_, ci, 0)),
        out_shape=jax.ShapeDtypeStruct((b, l, width), BF16),
        scratch_shapes=[
            pltpu.VMEM((SSD_TS, w_ssd[0].shape[-1]), F32),
            pltpu.VMEM((CONV_WIDTH - 1, SUBLANES, conv_ch), F32),
            pltpu.VMEM((width, SSD_STATE), F32),
        ],
        compiler_params=_params("parallel", "arbitrary"),
        name="ssd",
    )(hn3, w_ssd[0], *consts)


QKV_TM = 512
Q_SCALE = ATT_HEAD_DIM ** -0.5 * LOG2_E
PERM_ROWS = 256


def _split_dot(x, m, parts=2, left=False):
    acc, rem = None, x
    for k in range(parts):
        piece = rem.astype(BF16)
        term = jnp.dot(m, piece, preferred_element_type=F32) if left else jnp.dot(piece, m, preferred_element_type=F32)
        acc = term if acc is None else acc + term
        if k + 1 < parts:
            rem = rem - piece.astype(F32)
    return acc


def _slab_map(fn, x):
    return jnp.concatenate([fn(x[:, c * GATE_SLAB:(c + 1) * GATE_SLAB]) for c in range(x.shape[1] // GATE_SLAB)],
                           axis=1)


def _head_norm_rope(y, gain, cos, sin, gsum, rot):
    ss = _slab_map(lambda v: _split_dot(v, gsum, parts=1), y * y)
    yn = y * lax.rsqrt(ss * (1.0 / ATT_HEAD_DIM) + NORM_EPS) * gain
    partner = _slab_map(lambda v: _split_dot(v, rot), yn)
    tile = lambda tbl: jnp.concatenate([tbl] * (y.shape[1] // LANES), axis=1)
    return yn * tile(cos) + partner * tile(sin)


def _qkv_body(x_ref, g_ref, w_ref, qg_ref, kg_ref, cos_ref, sin_ref, gsum_ref, rot_ref, p1_ref, p4_ref, p16_ref,
              *refs):
    outs, h_ref = refs[:9], refs[9]
    j = pl.program_id(2)

    @pl.when(j == 0)
    def _():
        h_ref[...] = _rmsnorm(x_ref[...], g_ref[...]).astype(BF16)

    sub = PERM_ROWS

    def section(o1, o4, o16, gain_ref=None, grouped=False):
        y = jnp.dot(h_ref[...], w_ref[...], preferred_element_type=F32)
        if gain_ref is not None:
            y = _head_norm_rope(y, gain_ref[...], cos_ref[...], sin_ref[...], gsum_ref[...], rot_ref[...])
        y_all = y.astype(BF16)
        for t in range(x_ref.shape[0] // sub):
            rows = slice(t * sub, (t + 1) * sub)
            yb = y_all[rows, :]
            if grouped:
                o1[rows, :] = jnp.dot(p1_ref[...], yb, preferred_element_type=F32).astype(BF16)
            else:
                o1[rows, :] = yb
            for perm_ref, o in ((p4_ref, o4), (p16_ref, o16)):
                dil = o.shape[0]
                n = sub // dil
                yp = jnp.dot(perm_ref[...], yb, preferred_element_type=F32).astype(BF16)
                for r in range(dil):
                    o[r, t * n:(t + 1) * n, :] = yp[r * n:(r + 1) * n, :]

    @pl.when(j == 0)
    def _():
        section(*outs[0:3], gain_ref=qg_ref, grouped=True)

    @pl.when(j == 1)
    def _():
        section(*outs[3:6], gain_ref=kg_ref)

    @pl.when(j == 2)
    def _():
        section(*outs[6:9])


def _qkv(x3, g, w_qkv, layer, qg, kg, cos, sin):
    b, l, d = x3.shape
    tm = QKV_TM
    out_specs, out_shape = [], []
    for _ in range(3):
        out_specs.append(pl.BlockSpec((None, tm, d), lambda bi_, i, j: (bi_, i, 0)))
        out_shape.append(jax.ShapeDtypeStruct((b, l, d), BF16))
        for dil in DILATIONS[1:]:
            out_specs.append(pl.BlockSpec((None, dil, tm // dil, d), lambda bi_, i, j: (bi_, 0, i, 0)))
            out_shape.append(jax.ShapeDtypeStruct((b, dil, l // dil, d), BF16))
    consts = _qkv_constants(tm)
    tbl = lambda: pl.BlockSpec((tm, LANES), lambda bi_, i, j: (i, 0))
    return pl.pallas_call(
        _qkv_body,
        grid=(b, l // tm, 3),
        in_specs=[
            pl.BlockSpec((None, tm, d), lambda bi_, i, j: (bi_, i, 0)),
            _resident((1, d)),
            pl.BlockSpec((None, d, d), lambda bi_, i, j: (layer, 0, j)),
            _resident((1, d)), _resident((1, d)), tbl(), tbl(),
        ] + [_resident(c.shape) for c in consts],
        out_specs=out_specs,
        out_shape=out_shape,
        scratch_shapes=[pltpu.VMEM((tm, d), BF16)],
        compiler_params=_params("parallel", "parallel", "arbitrary"),
        name="qkv",
    )(x3, g, w_qkv, qg, kg, cos, sin, *consts)


def _qkv_constants(tm):
    lane = np.arange(GATE_SLAB)
    gsum = (lane[:, None] // ATT_HEAD_DIM == lane[None, :] // ATT_HEAD_DIM)
    half = ROPE_DIM // 2
    pos = lane % ATT_HEAD_DIM
    partner = np.where(pos < half, lane + half, np.where(pos < ROPE_DIM, lane - half, -1))
    rot = lane[:, None] == partner[None, :]
    out_row = np.arange(PERM_ROWS)
    group = SPAN // ACC_DIL
    within = out_row % SPAN
    src = out_row - within + (within % group) * ACC_DIL + within // group
    perms = [src[:, None] == np.arange(PERM_ROWS)[None, :]]
    for dil in DILATIONS[1:]:
        sub = PERM_ROWS
        rows = sub // dil
        assert rows % BF16_ROWS == 0 and tm % sub == 0
        out_row = np.arange(sub)
        src = (out_row % rows) * dil + out_row // rows
        perms.append(src[:, None] == np.arange(sub)[None, :])
    return tuple(jnp.asarray(m, dtype=BF16) for m in (gsum, rot, *perms))


def _rope_tables(l):
    half = ROPE_DIM // 2
    pos = jnp.arange(l, dtype=F32)
    inv = ROPE_THETA ** (-2.0 * jnp.arange(half, dtype=F32) / ROPE_DIM)
    ang = pos[:, None] * inv[None, :]
    cos, sin = jnp.cos(ang), jnp.sin(ang)
    pad = ATT_HEAD_DIM - ROPE_DIM
    cos_h = jnp.concatenate([cos, cos, jnp.ones((l, pad), F32)], axis=1)
    sin_h = jnp.concatenate([-sin, sin, jnp.zeros((l, pad), F32)], axis=1)
    rep = LANES // ATT_HEAD_DIM
    return tuple(jnp.concatenate([tb] * rep, axis=1) for tb in (cos_h, sin_h))


def _attn_body(q1, k1c, k1p, v1c, v1p, q4, k4c, k4p, v4c, v4p, q16, k16c, k16p, v16c, v16p,
               o_ref, s_scr, p_scr, mx_scr, bias_scr, tok_scr, acc_ref, m_ref, l_ref):
    first = pl.program_id(2) == 0
    n_pairs = acc_ref.shape[0]
    n_blocks = SUPER // SPAN
    trans_b = (((1,), (1,)), ((), ()))

    row = lax.broadcasted_iota(jnp.int32, (SPAN, 2 * SPAN), 0)
    kj = lax.broadcasted_iota(jnp.int32, (SPAN, 2 * SPAN), 1)
    group = SPAN // ACC_DIL
    for base, qi in ((0, row), (2, (row % group) * ACC_DIL + row // group)):
        band = (kj >= qi) & (kj <= qi + SPAN)
        bias_scr[base] = jnp.where(band, 0.0, -jnp.inf)
        bias_scr[base + 1] = jnp.where(band & (kj >= SPAN), 0.0, -jnp.inf)

    lo = lax.broadcasted_iota(jnp.int32, (SPAN, LANES), 1) < ATT_HEAD_DIM
    lo2 = lax.broadcasted_iota(jnp.int32, (2 * SPAN, LANES), 1) < ATT_HEAD_DIM
    ones_lo = jnp.where(lo2, 1.0, 0.0).astype(BF16)
    ones_hi = jnp.where(lo2, 0.0, 1.0).astype(BF16)
    lanes = lambda p: slice(p * LANES, (p + 1) * LANES)

    def run_pattern(dil, q_get, cur_prev_k, cur_prev_v):
        per_res = SUPER // (SPAN * dil)
        overwrite = dil == DILATIONS[0]

        def split(f):
            if per_res == 1:
                return f, 0
            if dil == 1:
                return 0, f
            if isinstance(f, int):
                return f % dil, f // dil
            return f & (dil - 1), lax.shift_right_logical(f, dil.bit_length() - 1)

        def slab(refs, r, nb):
            cur, prev = refs
            if dil > 1:
                cur, prev = cur.at[r], prev.at[r]
            if isinstance(nb, int) and nb == 0:
                return jnp.concatenate([prev[...], cur[0:SPAN, :]], axis=0)
            start = (nb - 1) * SPAN
            return cur[pl.ds(start if isinstance(nb, int) else pl.multiple_of(start, SPAN), 2 * SPAN), :]

        def stage_scores(f, slot):
            r, nb = split(f)
            if isinstance(nb, int):
                flag = jnp.where(first, 1, 0) if nb == 0 else 0
            else:
                flag = jnp.where(first & (nb == 0), 1, 0)
            bias = bias_scr[flag + (2 if overwrite else 0)]
            bias2 = jnp.concatenate([bias, bias], axis=0)
            q = q_get(r, nb)
            ks = slab(cur_prev_k, r, nb)
            for p in range(n_pairs):
                qp = q[:, lanes(p)]
                zero = jnp.zeros_like(qp)
                q2 = jnp.concatenate([jnp.where(lo, qp, zero), jnp.where(lo, zero, qp)], axis=0)
                s_scr[slot, p] = lax.dot_general(q2, ks[:, lanes(p)], trans_b, preferred_element_type=F32) + bias2

        def stage_softmax(src, dst):
            for p in range(n_pairs):
                s = s_scr[src, p]
                mx = jnp.max(s, axis=1, keepdims=True)
                pr = jnp.exp2(s - mx).astype(BF16)
                p_scr[dst, p, :, 0:2 * SPAN] = pr[:SPAN]
                p_scr[dst, p, :, 2 * SPAN:] = pr[SPAN:]
                mx_scr[dst, p] = jnp.where(lo, mx[:SPAN], mx[SPAN:])

        def stage_values(f, slot):
            r, nb = split(f)
            vs = slab(cur_prev_v, r, nb)
            if overwrite:
                grp = SPAN // ACC_DIL
                start = nb * grp if isinstance(nb, int) else pl.multiple_of(nb * grp, grp)
                pieces = [(slice(c * grp, (c + 1) * grp), c, pl.ds(start, grp)) for c in range(ACC_DIL)]
            elif dil == ACC_DIL:
                pieces = [(slice(0, SPAN), r, q_rows(nb))]
            else:
                sub = dil // ACC_DIL
                pieces = [(slice(0, SPAN), r & (ACC_DIL - 1),
                           pl.ds(lax.shift_right_logical(r, ACC_SHIFT) if not isinstance(r, int) else r // ACC_DIL,
                                 SPAN, stride=sub))]
            for p in range(n_pairs):
                vp = vs[:, lanes(p)]
                zero = jnp.zeros_like(vp)
                v_bd = jnp.concatenate([jnp.concatenate([jnp.where(lo2, vp, zero), ones_lo], axis=1),
                                        jnp.concatenate([jnp.where(lo2, zero, vp), ones_hi], axis=1)], axis=0)
                pv = jnp.dot(p_scr[slot, p], v_bd, preferred_element_type=F32)
                o_blk, l_blk = pv[:, :LANES], pv[:, LANES:]
                m_blk = mx_scr[slot, p]
                for src, cls, rows in pieces:
                    m_new, l_new, o_new = m_blk[src], l_blk[src], o_blk[src]
                    if overwrite:
                        m_ref[p, cls, rows, :], l_ref[p, cls, rows, :], acc_ref[p, cls, rows, :] = m_new, l_new, o_new
                        continue
                    m_old = m_ref[p, cls, rows, :]
                    m = jnp.maximum(m_old, m_new)
                    e_old = jnp.exp2(m_old - m)
                    e_new = jnp.exp2(m_new - m)
                    l_ref[p, cls, rows, :] = e_old * l_ref[p, cls, rows, :] + e_new * l_new
                    acc_ref[p, cls, rows, :] = e_old * acc_ref[p, cls, rows, :] + e_new * o_new
                    m_ref[p, cls, rows, :] = m

        n_rounds = n_blocks // 2

        def round_(k, odd, scores=True, softmax=True, values=True):
            wr = 2 if odd else 0
            rd = 2 - wr
            if values:
                stage_values(2 * k - 4, rd)
                stage_values(2 * k - 3, rd + 1)
            if softmax:
                stage_softmax(rd, wr)
                stage_softmax(rd + 1, wr + 1)
            if scores:
                stage_scores(2 * k, wr)
                stage_scores(2 * k + 1, wr + 1)

        round_(0, False, softmax=False, values=False)
        round_(1, True, values=False)
        round_(2, False)
        round_(3, True)

        def body(j, c):
            round_(2 * j, False)
            round_(2 * j + 1, True)
            return c

        for j in range(2, n_rounds // 2):
            body(j, 0)
        round_(n_rounds, False, scores=False)
        round_(n_rounds + 1, True, scores=False, softmax=False)

    def q_rows(nb):
        return pl.ds(nb * SPAN if isinstance(nb, int) else pl.multiple_of(nb * SPAN, SPAN), SPAN)

    run_pattern(DILATIONS[0], lambda r, nb: q1[q_rows(nb), :], (k1c, k1p), (v1c, v1p))
    run_pattern(DILATIONS[1], lambda r, nb: q4[r, q_rows(nb), :], (k4c, k4p), (v4c, v4p))
    run_pattern(DILATIONS[2], lambda r, nb: q16[r], (k16c, k16p), (v16c, v16p))

    for p in range(n_pairs):
        for cls in range(ACC_DIL):
            tok_scr[p, pl.ds(cls, SUPER // ACC_DIL, stride=ACC_DIL), :] = acc_ref[p, cls] / l_ref[p, cls]
        o_ref[:, lanes(p)] = tok_scr[p].astype(BF16)


def _attention(qkv_outs):
    q1, q4, q16, k1, k4, k16, v1, v4, v16 = qkv_outs
    b, l, d = q1.shape
    w = ATT_LANES
    n_pairs = w // LANES
    d4, d16 = DILATIONS[1], DILATIONS[2]
    cur1 = pl.BlockSpec((None, SUPER, w), lambda bi_, g, s: (bi_, s, g))
    prev1 = pl.BlockSpec((None, SPAN, w), lambda bi_, g, s: (bi_, jnp.maximum(s * (SUPER // SPAN) - 1, 0), g))
    cur4 = pl.BlockSpec((None, d4, SUPER // d4, w), lambda bi_, g, s: (bi_, 0, s, g))
    prev4 = pl.BlockSpec((None, d4, SPAN, w),
                         lambda bi_, g, s: (bi_, 0, jnp.maximum(s * (SUPER // (d4 * SPAN)) - 1, 0), g))
    cur16 = pl.BlockSpec((None, d16, SPAN, w), lambda bi_, g, s: (bi_, 0, s, g))
    prev16 = pl.BlockSpec((None, d16, SPAN, w), lambda bi_, g, s: (bi_, 0, jnp.maximum(s - 1, 0), g))
    return pl.pallas_call(
        _attn_body,
        grid=(b, d // w, l // SUPER),
        in_specs=[cur1, cur1, prev1, cur1, prev1,
                  cur4, cur4, prev4, cur4, prev4,
                  cur16, cur16, prev16, cur16, prev16],
        out_specs=pl.BlockSpec((None, SUPER, w), lambda bi_, g, s: (bi_, s, g)),
        out_shape=jax.ShapeDtypeStruct((b, l, d), BF16),
        scratch_shapes=[
            pltpu.VMEM((4, n_pairs, 2 * SPAN, 2 * SPAN), F32),
            pltpu.VMEM((4, n_pairs, SPAN, 4 * SPAN), BF16),
            pltpu.VMEM((4, n_pairs, SPAN, LANES), F32),
            pltpu.VMEM((4, SPAN, 2 * SPAN), F32),
            pltpu.VMEM((n_pairs, SUPER, LANES), F32),
        ] + [pltpu.VMEM((n_pairs, ACC_DIL, SUPER // ACC_DIL, LANES), F32)] * 3,
        compiler_params=_params("parallel", "parallel", "arbitrary"),
        name="dilated_attention",
    )(q1, k1, k1, v1, v1, q4, k4, k4, v4, v4, q16, k16, k16, v16, v16)


def _recurrent_layer(x, b, l, i, norm_g, w_lru, w_ssd, w_out, lru_conv_w, lru_conv_b, lru_w_r, lru_b_r, lru_w_i,
                     lru_b_i, lru_lambda, ssd_conv_w, ssd_conv_b, ssd_dt_bias, ssd_a_log, ssd_d, ssd_norm):
    d = x.shape[1]
    width = lru_lambda.shape[0]
    n_heads = ssd_a_log.shape[0]
    row = lambda v: v.reshape(1, -1)
    pad_lanes = lambda v: jnp.pad(v.reshape(1, -1), ((0, 0), (0, LANES - v.shape[0])))
    x3 = x.reshape(b, l, d)
    out_a, hn3 = _lru(x3, row(norm_g), _layer_block(w_lru, i, (d, 2 * width)), lru_conv_w, row(lru_conv_b),
                      _block_diag_slabs(lru_w_r), _block_diag_slabs(lru_w_i), row(lru_b_r), row(lru_b_i),
                      row(lru_lambda))
    out_b = _ssd(hn3, _layer_block(w_ssd, i), ssd_conv_w, row(ssd_conv_b), pad_lanes(ssd_dt_bias),
                 pad_lanes(ssd_a_log), row(jnp.repeat(ssd_d, SSD_HEAD_DIM)), row(ssd_norm), n_heads)
    return ([out_a.reshape(b * l, width), out_b.reshape(b * l, -1)],
            [_layer_block(w_out, i, (width, d), (0, 0)), _layer_block(w_out, i, (width, d), (1, 0))])


def _attention_layer(x, b, l, i, norm_g, w_qkv, w_out, q_norm, k_norm):
    d = x.shape[1]
    heads = d // ATT_HEAD_DIM
    row = lambda v: v.reshape(1, -1)
    outs = _qkv(x.reshape(b, l, d), row(norm_g), w_qkv, i,
                row(jnp.tile(q_norm, heads) * Q_SCALE), row(jnp.tile(k_norm, heads)), *_rope_tables(l))
    o = _attention(outs)
    return [o.reshape(b * l, d)], [_layer_block(w_out, i)]


def kernel(x, rec_norm, rec_w_in, lru_conv_w, lru_conv_b, lru_w_r, lru_b_r, lru_w_i, lru_b_i, lru_lambda,
           ssd_conv_w, ssd_conv_b, ssd_dt_bias, ssd_a_log, ssd_d, ssd_norm, rec_w_out, att_norm, att_w_qkv,
           att_q_norm, att_k_norm, att_w_out, ffn_norm, ffn_w_gate_up, ffn_w_down):
    b, l, d = x.shape
    depth = ffn_norm.shape[0]
    h = x.reshape(b * l, d)
    lru_cols = 2 * lru_lambda.shape[1]
    w_lru = rec_w_in.astype(BF16)
    dt_pad = LANES - ssd_a_log.shape[1]
    w_ssd = jnp.pad(rec_w_in[:, :, lru_cols:], ((0, 0), (0, 0), (0, dt_pad))).astype(BF16)
    rec_wo, att_wqkv, att_wo = rec_w_out.astype(BF16), att_w_qkv.astype(BF16), att_w_out.astype(BF16)
    ffn_wgu, ffn_wd = ffn_w_gate_up.astype(BF16), ffn_w_down.astype(BF16)
    for layer in range(depth):
        i = layer // 2
        if layer % 2 == 0:
            mixes, wos = _recurrent_layer(h, b, l, i, rec_norm[i], w_lru, w_ssd, rec_wo, lru_conv_w[i],
                                          lru_conv_b[i], lru_w_r[i], lru_b_r[i], lru_w_i[i], lru_b_i[i],
                                          lru_lambda[i], ssd_conv_w[i], ssd_conv_b[i], ssd_dt_bias[i],
                                          ssd_a_log[i], ssd_d[i], ssd_norm[i])
        else:
            mixes, wos = _attention_layer(h, b, l, i, att_norm[i], att_wqkv, att_wo, att_q_norm[i], att_k_norm[i])
        h = _mix_ffn(h, mixes, wos, ffn_norm[layer].reshape(1, d), _layer_block(ffn_wgu, layer),
                     _layer_block(ffn_wd, layer))
    return h.reshape(b, l, d)
```

```python
import functools
import math

import jax
import jax.numpy as jnp
import numpy as np
from jax import lax
from jax.experimental import pallas as pl
from jax.experimental.pallas import tpu as pltpu

F32 = jnp.float32
BF16 = jnp.bfloat16

NORM_EPS = 1e-6
LOG2_E = math.log2(math.e)
CONV_WIDTH = 4
LRU_C = 8.0
GATE_SLAB = 256
SSD_HEAD_DIM = 64
SSD_GROUPS = 2
SSD_STATE = 128
SSD_CHUNK = 128
ATT_HEAD_DIM = 64
ROPE_DIM = 16
ROPE_THETA = 500000.0
SPAN = 128
DILATIONS = (1, 4, 16)
SUPER = SPAN * DILATIONS[-1]
ATT_LANES = 256
ACC_DIL = DILATIONS[1]
ACC_SHIFT = ACC_DIL.bit_length() - 1
LANES = 128
SUBLANES = 8
SUBLANE_SHIFT = SUBLANES.bit_length() - 1
BF16_ROWS = 16
VMEM_LIMIT_BYTES = 56 * 1024 * 1024


def _params(*semantics):
    return pltpu.CompilerParams(dimension_semantics=semantics, vmem_limit_bytes=VMEM_LIMIT_BYTES)


def _rmsnorm(x, g):
    ms = jnp.mean(x * x, axis=-1, keepdims=True)
    return x * lax.rsqrt(ms + NORM_EPS) * g


def _sigmoid(x):
    return 1.0 / (1.0 + jnp.exp2(x * (-LOG2_E)))


def _silu(x):
    return x * _sigmoid(x)


def _softplus(x):
    return jnp.maximum(x, 0.0) + jnp.log1p(jnp.exp(-jnp.abs(x)))


def _gelu_tanh(x):
    c = math.sqrt(2.0 / math.pi)
    half_x = 0.5 * x
    return half_x + half_x * jnp.tanh(x * (c + (c * 0.044715) * (x * x)))


FFN_TM = 1024
FFN_TH = 256


def _ffn_body(n_mix, x_ref, *refs):
    mix_refs, wo_refs = refs[:n_mix], refs[n_mix:2 * n_mix]
    g_ref, wgu_ref, wd_ref, o_ref, h_ref, act_ref = refs[2 * n_mix:]
    x1 = x_ref[...]
    for mix_ref, wo_ref in zip(mix_refs, wo_refs):
        x1 = x1 + jnp.dot(mix_ref[...], wo_ref[...], preferred_element_type=F32)
    o_ref[...] = x1
    h_ref[...] = _rmsnorm(x1, g_ref[...]).astype(BF16)
    hid = act_ref.shape[1]
    for c in range(hid // FFN_TH):
        cols = slice(c * FFN_TH, (c + 1) * FFN_TH)
        up_cols = slice(hid + c * FFN_TH, hid + (c + 1) * FFN_TH)
        gate = jnp.dot(h_ref[...], wgu_ref[:, cols], preferred_element_type=F32)
        up = jnp.dot(h_ref[...], wgu_ref[:, up_cols], preferred_element_type=F32)
        act_ref[:, cols] = (_silu(gate) * up).astype(BF16)
    o_ref[...] += jnp.dot(act_ref[...], wd_ref[...], preferred_element_type=F32)


def _resident(shape):
    return pl.BlockSpec(shape, lambda *_: (0,) * len(shape), pipeline_mode=pl.Buffered(1))


def _layer_block(stack, layer, block=None, index=None):
    block = tuple(stack.shape[1:]) if block is None else tuple(block)
    index = (layer,) + (tuple(index) if index is not None else (0,) * len(block))
    return stack, pl.BlockSpec((None,) + block, lambda *_: index, pipeline_mode=pl.Buffered(1))


def _mix_ffn(x, mixes, wos, g, wgu, wd):
    t, d = x.shape
    hid = wd[0].shape[1]
    weights = (*wos, (g, _resident(g.shape)), wgu, wd)
    rows = lambda a: pl.BlockSpec((FFN_TM, a.shape[1]), lambda i: (i, 0))
    return pl.pallas_call(
        functools.partial(_ffn_body, len(mixes)),
        grid=(t // FFN_TM,),
        in_specs=[rows(x)] + [rows(a) for a in mixes] + [spec for _, spec in weights],
        out_specs=rows(x),
        out_shape=jax.ShapeDtypeStruct((t, d), F32),
        scratch_shapes=[pltpu.VMEM((FFN_TM, d), BF16), pltpu.VMEM((FFN_TM, hid), BF16)],
        compiler_params=_params("parallel"),
        name="mix_ffn",
    )(x, *mixes, *[a for a, _ in weights])


def _segment_perm(rows):
    steps = rows // SUBLANES
    rho = np.arange(rows)
    time = (rho % SUBLANES) * steps + rho // SUBLANES
    perm = time[:, None] == np.arange(rows)[None, :]
    return jnp.asarray(perm, dtype=BF16), jnp.asarray(perm.T, dtype=BF16)


def _conv_segments(x, halo_ref, w_ref, b_ref):
    rows, width = x.shape
    sub0 = lax.broadcasted_iota(jnp.int32, (SUBLANES, width), 0) == 0
    halos = []
    for k in range(1, CONV_WIDTH):
        tail = pltpu.roll(x[rows - k * SUBLANES:rows - (k - 1) * SUBLANES, :], 1, 0)
        halos.append(jnp.where(sub0, halo_ref[k - 1], tail))
        halo_ref[k - 1] = tail
    y = b_ref[...] + w_ref[CONV_WIDTH - 1:CONV_WIDTH, :] * x
    for k in range(1, CONV_WIDTH):
        back = jnp.concatenate(halos[k - 1::-1] + [x[:rows - k * SUBLANES, :]], axis=0)
        y = y + w_ref[CONV_WIDTH - 1 - k:CONV_WIDTH - k, :] * back
    return y


LRU_TL = 512
LRU_SUB = 256


def _lru_body(x_ref, g_ref, w_ref, cw_ref, cb_ref, wr_ref, wi_ref, br_ref, bi_ref, lam_ref, perm_ref, unperm_ref,
              o_ref, hn_ref, halo_ref, carry_ref):
    first = pl.program_id(1) == 0
    w = o_ref.shape[1]
    sub = perm_ref.shape[0]
    steps = sub // SUBLANES

    @pl.when(first)
    def _():
        carry_ref[...] = jnp.zeros_like(carry_ref)
        halo_ref[...] = jnp.zeros_like(halo_ref)

    for t in range(o_ref.shape[0] // sub):
        rows = slice(t * sub, (t + 1) * sub)
        hn = _rmsnorm(x_ref[rows, :], g_ref[...]).astype(BF16)
        hn_ref[rows, :] = hn
        hn = jnp.dot(perm_ref[...], hn, preferred_element_type=F32).astype(BF16)
        proj = jnp.dot(hn, w_ref[...], preferred_element_type=F32)
        gate = proj[:, w:]

        xc = _conv_segments(proj[:, :w], halo_ref, cw_ref, cb_ref)
        xcb = xc.astype(BF16)
        pre_r, pre_i = [], []
        for s in range(w // GATE_SLAB):
            slab = xcb[:, s * GATE_SLAB:(s + 1) * GATE_SLAB]
            pre_r.append(jnp.dot(slab, wr_ref[s], preferred_element_type=F32))
            pre_i.append(jnp.dot(slab, wi_ref[s], preferred_element_type=F32))
        r = _sigmoid(jnp.concatenate(pre_r, axis=1) + br_ref[...])
        i = _sigmoid(jnp.concatenate(pre_i, axis=1) + bi_ref[...])
        log2_a = (-LRU_C * LOG2_E * r) * _softplus(-lam_ref[...])
        a = jnp.exp2(log2_a)
        z = 1.0 - a * a
        u = jnp.where(z > 0.0, z * lax.rsqrt(z), 0.0) * (i * xc)

        h_j = jnp.zeros((SUBLANES, w), F32)
        p_j = jnp.ones((SUBLANES, w), F32)
        hs, ps = [], []
        for j in range(steps):
            a_j = a[j * SUBLANES:(j + 1) * SUBLANES, :]
            h_j = a_j * h_j + u[j * SUBLANES:(j + 1) * SUBLANES, :]
            p_j = a_j * p_j
            hs.append(h_j)
            ps.append(p_j)
        c = carry_ref[0:1, :]
        entering = []
        for seg in range(SUBLANES):
            entering.append(c)
            c = p_j[seg:seg + 1, :] * c + h_j[seg:seg + 1, :]
        carry_ref[0:1, :] = c
        enter = jnp.concatenate(entering, axis=0)
        h = jnp.concatenate([hs[j] + ps[j] * enter for j in range(steps)], axis=0)
        out = (h * _gelu_tanh(gate)).astype(BF16)
        o_ref[rows, :] = jnp.dot(unperm_ref[...], out, preferred_element_type=F32).astype(BF16)


def _lru(x3, g, w_lru, cw, cb, wr_bd, wi_bd, br, bi, lam):
    b, l, d = x3.shape
    width = lam.shape[1]
    perms = _segment_perm(LRU_SUB)
    return pl.pallas_call(
        _lru_body,
        grid=(b, l // LRU_TL),
        in_specs=[pl.BlockSpec((None, LRU_TL, d), lambda bi_, li: (bi_, li, 0))]
        + [_resident(g.shape), w_lru[1]]
        + [_resident(a.shape) for a in (cw, cb, wr_bd, wi_bd, br, bi, lam, *perms)],
        out_specs=[pl.BlockSpec((None, LRU_TL, width), lambda bi_, li: (bi_, li, 0)),
                   pl.BlockSpec((None, LRU_TL, d), lambda bi_, li: (bi_, li, 0))],
        out_shape=[jax.ShapeDtypeStruct((b, l, width), BF16),
                   jax.ShapeDtypeStruct((b, l, d), BF16)],
        scratch_shapes=[
            pltpu.VMEM((CONV_WIDTH - 1, SUBLANES, width), F32),
            pltpu.VMEM((SUBLANES, width), F32),
        ],
        compiler_params=_params("parallel", "arbitrary"),
        name="lru",
    )(x3, g, w_lru[0], cw, cb, wr_bd, wi_bd, br, bi, lam, *perms)


def _block_diag_slabs(w):
    nb, bs, _ = w.shape
    per = GATE_SLAB // bs
    w = w.reshape(nb // per, per, bs, bs)
    eye = jnp.eye(per, dtype=w.dtype)
    bd = jnp.einsum("spij,pq->spiqj", w, eye).reshape(nb // per, GATE_SLAB, GATE_SLAB)
    return bd.astype(BF16)


def _ssd_body(n_heads, hn_ref, w_ref, cw_ref, cb_ref, dtb_ref, alog_ref, dvec_ref, nrm_ref,
              exp_ref, perm_ref, unperm_ref, o_ref, proj_ref, halo_ref, state_ref):
    first = pl.program_id(1) == 0
    t = SSD_CHUNK
    width = n_heads * SSD_HEAD_DIM
    gw = SSD_STATE
    conv_ch = width + 2 * SSD_GROUPS * gw
    n_chunks = hn_ref.shape[0] // t

    @pl.when(first)
    def _():
        state_ref[...] = jnp.zeros_like(state_ref)
        halo_ref[...] = jnp.zeros_like(halo_ref)

    hn = jnp.concatenate([jnp.dot(perm_ref[...], hn_ref[c * t:(c + 1) * t, :], preferred_element_type=F32).astype(BF16)
                          for c in range(n_chunks)], axis=0)
    proj_ref[...] = jnp.dot(hn, w_ref[...], preferred_element_type=F32)

    steps = t // SUBLANES
    time_of = lambda i: (i & (SUBLANES - 1)) * steps + lax.shift_right_logical(i, SUBLANE_SHIFT)
    causal = (time_of(lax.broadcasted_iota(jnp.int32, (t, t), 0))
              >= time_of(lax.broadcasted_iota(jnp.int32, (t, t), 1)))
    tri = jnp.where(causal, 1.0, 0.0).astype(BF16)
    lane = lax.broadcasted_iota(jnp.int32, (t, LANES), 1)
    lo = lane < SSD_HEAD_DIM
    rlo = lax.broadcasted_iota(jnp.int32, (LANES, gw), 0) < SSD_HEAD_DIM

    dt = _softplus(proj_ref[:, width + conv_ch:] + dtb_ref[...])
    adt = dt * (-LOG2_E * jnp.exp(alog_ref[...]))
    cs = jnp.concatenate([_split_dot(adt[c * t:(c + 1) * t, :], tri, parts=3, left=True)
                          for c in range(n_chunks)], axis=0)
    dt_w = _split_dot(dt, exp_ref[...], parts=2)
    cs_w = _split_dot(cs, exp_ref[...], parts=2)

    for c in range(n_chunks):
        rows = slice(c * t, (c + 1) * t)
        out = _ssd_chunk(proj_ref[rows, 0:width], proj_ref[rows, width:width + conv_ch],
                         cs[rows, :], dt_w[rows, :], cs_w[rows, :], n_heads, causal, lo, rlo, halo_ref, state_ref,
                         cw_ref, cb_ref, dvec_ref, nrm_ref)
        o_ref[rows, :] = jnp.dot(unperm_ref[...], out, preferred_element_type=F32).astype(BF16)


def _ssd_chunk(z, xbc_raw, cs, dt_w, cs_w, n_heads, causal, lo, rlo, halo_ref, state_ref,
               cw_ref, cb_ref, dvec_ref, nrm_ref):
    t = SSD_CHUNK
    width = n_heads * SSD_HEAD_DIM
    gw = SSD_STATE
    heads_per_group = n_heads // SSD_GROUPS
    xbc = _silu(_conv_segments(xbc_raw, halo_ref, cw_ref, cb_ref))
    xs = xbc[:, :width]
    cs_t = cs.T
    cs_last = cs[t - 1:t, :]
    cdec = jnp.exp2(cs_last)
    ecs_w = jnp.exp2(cs_w)
    dte_w = jnp.exp2(cs_w[t - 1:t, :] - cs_w)

    cbs = []
    for g in range(SSD_GROUPS):
        bm = xbc[:, width + g * gw: width + (g + 1) * gw].astype(BF16)
        cm = xbc[:, width + (SSD_GROUPS + g) * gw: width + (SSD_GROUPS + g + 1) * gw].astype(BF16)
        cb = lax.dot_general(cm, bm, (((1,), (1,)), ((), ())), preferred_element_type=F32)
        cbs.append((bm, cm, cb))

    ys = []
    for p in range(n_heads // 2):
        h0, h1 = 2 * p, 2 * p + 1
        bm, cm, cb = cbs[h0 // heads_per_group]
        sl = slice(p * LANES, (p + 1) * LANES)
        xs_p = xs[:, sl]
        xdt = xs_p * dt_w[:, sl]
        ms = []
        for h in (h0, h1):
            seg = cs[:, h:h + 1] - cs_t[h:h + 1, :]
            ms.append((cb * jnp.where(causal, jnp.exp2(seg), 0.0)).astype(BF16))
        m_cat = jnp.concatenate(ms, axis=1)
        xdt_bd = jnp.concatenate([jnp.where(lo, xdt, 0.0), jnp.where(lo, 0.0, xdt)], axis=0).astype(BF16)
        y = jnp.dot(m_cat, xdt_bd, preferred_element_type=F32)
        prev = state_ref[sl, :]
        y_off = lax.dot_general(cm, prev.astype(BF16), (((1,), (1,)), ((), ())), preferred_element_type=F32)
        y = y + y_off * ecs_w[:, sl]
        xw = (xdt * dte_w[:, sl]).astype(BF16)
        st = lax.dot_general(xw, bm, (((0,), (0,)), ((), ())), preferred_element_type=F32)
        dec = jnp.where(rlo, cdec[:, h0:h0 + 1], cdec[:, h1:h1 + 1])
        state_ref[sl, :] = prev * dec + st
        ys.append(y + dvec_ref[:, sl] * xs_p)

    y = jnp.concatenate(ys, axis=1) * _silu(z)
    gsz = width // SSD_GROUPS
    outs = []
    for g in range(SSD_GROUPS):
        yg = y[:, g * gsz:(g + 1) * gsz]
        outs.append(yg * lax.rsqrt(jnp.mean(yg * yg, axis=-1, keepdims=True) + NORM_EPS))
    return (jnp.concatenate(outs, axis=1) * nrm_ref[...]).astype(BF16)


SSD_TS = 512


def _ssd(hn3, w_ssd, cw, cb, dt_bias, a_log, d_vec, nrm, n_heads):
    b, l, d = hn3.shape
    width = n_heads * SSD_HEAD_DIM
    conv_ch = width + 2 * SSD_GROUPS * SSD_STATE
    expand = jnp.asarray(np.arange(LANES)[:, None] == np.arange(width)[None, :] // SSD_HEAD_DIM,
                         dtype=BF16)
    consts = (cw, cb, dt_bias, a_log, d_vec, nrm, expand, *_segment_perm(SSD_CHUNK))
    return pl.pallas_call(
        functools.partial(_ssd_body, n_heads),
        grid=(b, l // SSD_TS),
        in_specs=[pl.BlockSpec((None, SSD_TS, d), lambda bi_, ci: (bi_, ci, 0))]
        + [w_ssd[1]] + [_resident(a.shape) for a in consts],
        out_specs=pl.BlockSpec((None, SSD_TS, width), lambda bi_, ci: (bi_, ci, 0)),
        out_shape=jax.ShapeDtypeStruct((b, l, width), BF16),
        scratch_shapes=[
            pltpu.VMEM((SSD_TS, w_ssd[0].shape[-1]), F32),
            pltpu.VMEM((CONV_WIDTH - 1, SUBLANES, conv_ch), F32),
            pltpu.VMEM((width, SSD_STATE), F32),
        ],
        compiler_params=_params("parallel", "arbitrary"),
        name="ssd",
    )(hn3, w_ssd[0], *consts)


QKV_TM = 512
Q_SCALE = ATT_HEAD_DIM ** -0.5 * LOG2_E
PERM_ROWS = 256


def _split_dot(x, m, parts=2, left=False):
    acc, rem = None, x
    for k in range(parts):
        piece = rem.astype(BF16)
        term = jnp.dot(m, piece, preferred_element_type=F32) if left else jnp.dot(piece, m, preferred_element_type=F32)
        acc = term if acc is None else acc + term
        if k + 1 < parts:
            rem = rem - piece.astype(F32)
    return acc


def _slab_map(fn, x):
    return jnp.concatenate([fn(x[:, c * GATE_SLAB:(c + 1) * GATE_SLAB]) for c in range(x.shape[1] // GATE_SLAB)],
                           axis=1)


def _head_norm_rope(y, gain, cos, sin, gsum, rot):
    ss = _slab_map(lambda v: _split_dot(v, gsum, parts=1), y * y)
    yn = y * lax.rsqrt(ss * (1.0 / ATT_HEAD_DIM) + NORM_EPS) * gain
    partner = _slab_map(lambda v: _split_dot(v, rot), yn)
    tile = lambda tbl: jnp.concatenate([tbl] * (y.shape[1] // LANES), axis=1)
    return yn * tile(cos) + partner * tile(sin)


def _qkv_body(x_ref, g_ref, w_ref, qg_ref, kg_ref, cos_ref, sin_ref, gsum_ref, rot_ref, p1_ref, p4_ref, p16_ref,
              *refs):
    outs, h_ref = refs[:9], refs[9]
    d = x_ref.shape[1]
    h_ref[...] = _rmsnorm(x_ref[...], g_ref[...]).astype(BF16)

    sub = PERM_ROWS

    def section(j, o1, o4, o16, gain_ref=None, grouped=False):
        y = jnp.dot(h_ref[...], w_ref[:, j * d:(j + 1) * d], preferred_element_type=F32)
        if gain_ref is not None:
            y = _head_norm_rope(y, gain_ref[...], cos_ref[...], sin_ref[...], gsum_ref[...], rot_ref[...])
        y_all = y.astype(BF16)
        for t in range(x_ref.shape[0] // sub):
            rows = slice(t * sub, (t + 1) * sub)
            yb = y_all[rows, :]
            if grouped:
                o1[rows, :] = jnp.dot(p1_ref[...], yb, preferred_element_type=F32).astype(BF16)
            else:
                o1[rows, :] = yb
            for perm_ref, o in ((p4_ref, o4), (p16_ref, o16)):
                dil = o.shape[0]
                n = sub // dil
                yp = jnp.dot(perm_ref[...], yb, preferred_element_type=F32).astype(BF16)
                for r in range(dil):
                    o[r, t * n:(t + 1) * n, :] = yp[r * n:(r + 1) * n, :]

    section(0, *outs[0:3], gain_ref=qg_ref, grouped=True)
    section(1, *outs[3:6], gain_ref=kg_ref)
    section(2, *outs[6:9])


def _qkv(x3, g, w_qkv, layer, qg, kg, cos, sin):
    b, l, d = x3.shape
    tm = QKV_TM
    out_specs, out_shape = [], []
    for _ in range(3):
        out_specs.append(pl.BlockSpec((None, tm, d), lambda bi_, i: (bi_, i, 0)))
        out_shape.append(jax.ShapeDtypeStruct((b, l, d), BF16))
        for dil in DILATIONS[1:]:
            out_specs.append(pl.BlockSpec((None, dil, tm // dil, d), lambda bi_, i: (bi_, 0, i, 0)))
            out_shape.append(jax.ShapeDtypeStruct((b, dil, l // dil, d), BF16))
    consts = _qkv_constants(tm)
    tbl = lambda: pl.BlockSpec((tm, LANES), lambda bi_, i: (i, 0))
    w_arr, w_spec = _layer_block(w_qkv, layer)
    return pl.pallas_call(
        _qkv_body,
        grid=(b, l // tm),
        in_specs=[
            pl.BlockSpec((None, tm, d), lambda bi_, i: (bi_, i, 0)),
            _resident((1, d)),
            w_spec,
            _resident((1, d)), _resident((1, d)), tbl(), tbl(),
        ] + [_resident(c.shape) for c in consts],
        out_specs=out_specs,
        out_shape=out_shape,
        scratch_shapes=[pltpu.VMEM((tm, d), BF16)],
        compiler_params=_params("parallel", "parallel"),
        name="qkv",
    )(x3, g, w_arr, qg, kg, cos, sin, *consts)


def _qkv_constants(tm):
    lane = np.arange(GATE_SLAB)
    gsum = (lane[:, None] // ATT_HEAD_DIM == lane[None, :] // ATT_HEAD_DIM)
    half = ROPE_DIM // 2
    pos = lane % ATT_HEAD_DIM
    partner = np.where(pos < half, lane + half, np.where(pos < ROPE_DIM, lane - half, -1))
    rot = lane[:, None] == partner[None, :]
    out_row = np.arange(PERM_ROWS)
    group = SPAN // ACC_DIL
    within = out_row % SPAN
    src = out_row - within + (within % group) * ACC_DIL + within // group
    perms = [src[:, None] == np.arange(PERM_ROWS)[None, :]]
    for dil in DILATIONS[1:]:
        sub = PERM_ROWS
        rows = sub // dil
        assert rows % BF16_ROWS == 0 and tm % sub == 0
        out_row = np.arange(sub)
        src = (out_row % rows) * dil + out_row // rows
        perms.append(src[:, None] == np.arange(sub)[None, :])
    return tuple(jnp.asarray(m, dtype=BF16) for m in (gsum, rot, *perms))


def _rope_tables(l):
    half = ROPE_DIM // 2
    pos = jnp.arange(l, dtype=F32)
    inv = ROPE_THETA ** (-2.0 * jnp.arange(half, dtype=F32) / ROPE_DIM)
    ang = pos[:, None] * inv[None, :]
    cos, sin = jnp.cos(ang), jnp.sin(ang)
    pad = ATT_HEAD_DIM - ROPE_DIM
    cos_h = jnp.concatenate([cos, cos, jnp.ones((l, pad), F32)], axis=1)
    sin_h = jnp.concatenate([-sin, sin, jnp.zeros((l, pad), F32)], axis=1)
    rep = LANES // ATT_HEAD_DIM
    return tuple(jnp.concatenate([tb] * rep, axis=1) for tb in (cos_h, sin_h))


def _attn_body(q1, k1c, k1p, v1c, v1p, q4, k4c, k4p, v4c, v4p, q16, k16c, k16p, v16c, v16p,
               o_ref, s_scr, p_scr, mx_scr, bias_scr, tok_scr, acc_ref, m_ref, l_ref):
    first = pl.program_id(2) == 0
    n_pairs = acc_ref.shape[0]
    n_blocks = SUPER // SPAN
    trans_b = (((1,), (1,)), ((), ()))

    row = lax.broadcasted_iota(jnp.int32, (SPAN, 2 * SPAN), 0)
    kj = lax.broadcasted_iota(jnp.int32, (SPAN, 2 * SPAN), 1)
    group = SPAN // ACC_DIL
    for base, qi in ((0, row), (2, (row % group) * ACC_DIL + row // group)):
        band = (kj >= qi) & (kj <= qi + SPAN)
        bias_scr[base] = jnp.where(band, 0.0, -jnp.inf)
        bias_scr[base + 1] = jnp.where(band & (kj >= SPAN), 0.0, -jnp.inf)

    lo = lax.broadcasted_iota(jnp.int32, (SPAN, LANES), 1) < ATT_HEAD_DIM
    lo2 = lax.broadcasted_iota(jnp.int32, (2 * SPAN, LANES), 1) < ATT_HEAD_DIM
    ones_lo = jnp.where(lo2, 1.0, 0.0).astype(BF16)
    ones_hi = jnp.where(lo2, 0.0, 1.0).astype(BF16)
    lanes = lambda p: slice(p * LANES, (p + 1) * LANES)

    def run_pattern(dil, q_get, cur_prev_k, cur_prev_v):
        per_res = SUPER // (SPAN * dil)
        overwrite = dil == DILATIONS[0]

        def split(f):
            if per_res == 1:
                return f, 0
            if dil == 1:
                return 0, f
            if isinstance(f, int):
                return f % dil, f // dil
            return f & (dil - 1), lax.shift_right_logical(f, dil.bit_length() - 1)

        def slab(refs, r, nb):
            cur, prev = refs
            if dil > 1:
                cur, prev = cur.at[r], prev.at[r]
            if isinstance(nb, int) and nb == 0:
                return jnp.concatenate([prev[...], cur[0:SPAN, :]], axis=0)
            start = (nb - 1) * SPAN
            return cur[pl.ds(start if isinstance(nb, int) else pl.multiple_of(start, SPAN), 2 * SPAN), :]

        def stage_scores(f, slot):
            r, nb = split(f)
            if isinstance(nb, int):
                flag = jnp.where(first, 1, 0) if nb == 0 else 0
            else:
                flag = jnp.where(first & (nb == 0), 1, 0)
            bias = bias_scr[flag + (2 if overwrite else 0)]
            bias2 = jnp.concatenate([bias, bias], axis=0)
            q = q_get(r, nb)
            ks = slab(cur_prev_k, r, nb)
            for p in range(n_pairs):
                qp = q[:, lanes(p)]
                zero = jnp.zeros_like(qp)
                q2 = jnp.concatenate([jnp.where(lo, qp, zero), jnp.where(lo, zero, qp)], axis=0)
                s_scr[slot, p] = lax.dot_general(q2, ks[:, lanes(p)], trans_b, preferred_element_type=F32) + bias2

        def stage_softmax(src, dst):
            for p in range(n_pairs):
                s = s_scr[src, p]
                mx = jnp.max(s, axis=1, keepdims=True)
                pr = jnp.exp2(s - mx).astype(BF16)
                p_scr[dst, p, :, 0:2 * SPAN] = pr[:SPAN]
                p_scr[dst, p, :, 2 * SPAN:] = pr[SPAN:]
                mx_scr[dst, p] = jnp.where(lo, mx[:SPAN], mx[SPAN:])

        def stage_values(f, slot):
            r, nb = split(f)
            vs = slab(cur_prev_v, r, nb)
            if overwrite:
                grp = SPAN // ACC_DIL
                start = nb * grp if isinstance(nb, int) else pl.multiple_of(nb * grp, grp)
                pieces = [(slice(c * grp, (c + 1) * grp), c, pl.ds(start, grp)) for c in range(ACC_DIL)]
            elif dil == ACC_DIL:
                pieces = [(slice(0, SPAN), r, q_rows(nb))]
            else:
                sub = dil // ACC_DIL
                pieces = [(slice(0, SPAN), r & (ACC_DIL - 1),
                           pl.ds(lax.shift_right_logical(r, ACC_SHIFT) if not isinstance(r, int) else r // ACC_DIL,
                                 SPAN, stride=sub))]
            for p in range(n_pairs):
                vp = vs[:, lanes(p)]
                zero = jnp.zeros_like(vp)
                v_bd = jnp.concatenate([jnp.concatenate([jnp.where(lo2, vp, zero), ones_lo], axis=1),
                                        jnp.concatenate([jnp.where(lo2, zero, vp), ones_hi], axis=1)], axis=0)
                pv = jnp.dot(p_scr[slot, p], v_bd, preferred_element_type=F32)
                o_blk, l_blk = pv[:, :LANES], pv[:, LANES:]
                m_blk = mx_scr[slot, p]
                for src, cls, rows in pieces:
                    m_new, l_new, o_new = m_blk[src], l_blk[src], o_blk[src]
                    if overwrite:
                        m_ref[p, cls, rows, :], l_ref[p, cls, rows, :], acc_ref[p, cls, rows, :] = m_new, l_new, o_new
                        continue
                    m_old = m_ref[p, cls, rows, :]
                    m = jnp.maximum(m_old, m_new)
                    e_old = jnp.exp2(m_old - m)
                    e_new = jnp.exp2(m_new - m)
                    l_ref[p, cls, rows, :] = e_old * l_ref[p, cls, rows, :] + e_new * l_new
                    acc_ref[p, cls, rows, :] = e_old * acc_ref[p, cls, rows, :] + e_new * o_new
                    m_ref[p, cls, rows, :] = m

        n_rounds = n_blocks // 2

        def round_(k, odd, scores=True, softmax=True, values=True):
            wr = 2 if odd else 0
            rd = 2 - wr
            if values:
                stage_values(2 * k - 4, rd)
                stage_values(2 * k - 3, rd + 1)
            if softmax:
                stage_softmax(rd, wr)
                stage_softmax(rd + 1, wr + 1)
            if scores:
                stage_scores(2 * k, wr)
                stage_scores(2 * k + 1, wr + 1)

        round_(0, False, softmax=False, values=False)
        round_(1, True, values=False)
        round_(2, False)
        round_(3, True)

        def body(j, c):
            round_(2 * j, False)
            round_(2 * j + 1, True)
            return c

        for j in range(2, n_rounds // 2):
            body(j, 0)
        round_(n_rounds, False, scores=False)
        round_(n_rounds + 1, True, scores=False, softmax=False)

    def q_rows(nb):
        return pl.ds(nb * SPAN if isinstance(nb, int) else pl.multiple_of(nb * SPAN, SPAN), SPAN)

    run_pattern(DILATIONS[0], lambda r, nb: q1[q_rows(nb), :], (k1c, k1p), (v1c, v1p))
    run_pattern(DILATIONS[1], lambda r, nb: q4[r, q_rows(nb), :], (k4c, k4p), (v4c, v4p))
    run_pattern(DILATIONS[2], lambda r, nb: q16[r], (k16c, k16p), (v16c, v16p))

    for p in range(n_pairs):
        for cls in range(ACC_DIL):
            tok_scr[p, pl.ds(cls, SUPER // ACC_DIL, stride=ACC_DIL), :] = acc_ref[p, cls] / l_ref[p, cls]
        o_ref[:, lanes(p)] = tok_scr[p].astype(BF16)


def _attention(qkv_outs):
    q1, q4, q16, k1, k4, k16, v1, v4, v16 = qkv_outs
    b, l, d = q1.shape
    w = ATT_LANES
    n_pairs = w // LANES
    d4, d16 = DILATIONS[1], DILATIONS[2]
    cur1 = pl.BlockSpec((None, SUPER, w), lambda bi_, g, s: (bi_, s, g))
    prev1 = pl.BlockSpec((None, SPAN, w), lambda bi_, g, s: (bi_, jnp.maximum(s * (SUPER // SPAN) - 1, 0), g))
    cur4 = pl.BlockSpec((None, d4, SUPER // d4, w), lambda bi_, g, s: (bi_, 0, s, g))
    prev4 = pl.BlockSpec((None, d4, SPAN, w),
                         lambda bi_, g, s: (bi_, 0, jnp.maximum(s * (SUPER // (d4 * SPAN)) - 1, 0), g))
    cur16 = pl.BlockSpec((None, d16, SPAN, w), lambda bi_, g, s: (bi_, 0, s, g))
    prev16 = pl.BlockSpec((None, d16, SPAN, w), lambda bi_, g, s: (bi_, 0, jnp.maximum(s - 1, 0), g))
    return pl.pallas_call(
        _attn_body,
        grid=(b, d // w, l // SUPER),
        in_specs=[cur1, cur1, prev1, cur1, prev1,
                  cur4, cur4, prev4, cur4, prev4,
                  cur16, cur16, prev16, cur16, prev16],
        out_specs=pl.BlockSpec((None, SUPER, w), lambda bi_, g, s: (bi_, s, g)),
        out_shape=jax.ShapeDtypeStruct((b, l, d), BF16),
        scratch_shapes=[
            pltpu.VMEM((4, n_pairs, 2 * SPAN, 2 * SPAN), F32),
            pltpu.VMEM((4, n_pairs, SPAN, 4 * SPAN), BF16),
            pltpu.VMEM((4, n_pairs, SPAN, LANES), F32),
            pltpu.VMEM((4, SPAN, 2 * SPAN), F32),
            pltpu.VMEM((n_pairs, SUPER, LANES), F32),
        ] + [pltpu.VMEM((n_pairs, ACC_DIL, SUPER // ACC_DIL, LANES), F32)] * 3,
        compiler_params=_params("parallel", "parallel", "arbitrary"),
        name="dilated_attention",
    )(q1, k1, k1, v1, v1, q4, k4, k4, v4, v4, q16, k16, k16, v16, v16)


def _recurrent_layer(x, b, l, i, norm_g, w_lru, w_ssd, w_out, lru_conv_w, lru_conv_b, lru_w_r, lru_b_r, lru_w_i,
                     lru_b_i, lru_lambda, ssd_conv_w, ssd_conv_b, ssd_dt_bias, ssd_a_log, ssd_d, ssd_norm):
    d = x.shape[1]
    width = lru_lambda.shape[0]
    n_heads = ssd_a_log.shape[0]
    row = lambda v: v.reshape(1, -1)
    pad_lanes = lambda v: jnp.pad(v.reshape(1, -1), ((0, 0), (0, LANES - v.shape[0])))
    x3 = x.reshape(b, l, d)
    out_a, hn3 = _lru(x3, row(norm_g), _layer_block(w_lru, i, (d, 2 * width)), lru_conv_w, row(lru_conv_b),
                      _block_diag_slabs(lru_w_r), _block_diag_slabs(lru_w_i), row(lru_b_r), row(lru_b_i),
                      row(lru_lambda))
    out_b = _ssd(hn3, _layer_block(w_ssd, i), ssd_conv_w, row(ssd_conv_b), pad_lanes(ssd_dt_bias),
                 pad_lanes(ssd_a_log), row(jnp.repeat(ssd_d, SSD_HEAD_DIM)), row(ssd_norm), n_heads)
    return ([out_a.reshape(b * l, width), out_b.reshape(b * l, -1)],
            [_layer_block(w_out, i, (width, d), (0, 0)), _layer_block(w_out, i, (width, d), (1, 0))])


def _attention_layer(x, b, l, i, norm_g, w_qkv, w_out, q_norm, k_norm):
    d = x.shape[1]
    heads = d // ATT_HEAD_DIM
    row = lambda v: v.reshape(1, -1)
    outs = _qkv(x.reshape(b, l, d), row(norm_g), w_qkv, i,
                row(jnp.tile(q_norm, heads) * Q_SCALE), row(jnp.tile(k_norm, heads)), *_rope_tables(l))
    o = _attention(outs)
    return [o.reshape(b * l, d)], [_layer_block(w_out, i)]


def kernel(x, rec_norm, rec_w_in, lru_conv_w, lru_conv_b, lru_w_r, lru_b_r, lru_w_i, lru_b_i, lru_lambda,
           ssd_conv_w, ssd_conv_b, ssd_dt_bias, ssd_a_log, ssd_d, ssd_norm, rec_w_out, att_norm, att_w_qkv,
           att_q_norm, att_k_norm, att_w_out, ffn_norm, ffn_w_gate_up, ffn_w_down):
    b, l, d = x.shape
    depth = ffn_norm.shape[0]
    h = x.reshape(b * l, d)
    lru_cols = 2 * lru_lambda.shape[1]
    w_lru = rec_w_in.astype(BF16)
    dt_pad = LANES - ssd_a_log.shape[1]
    w_ssd = jnp.pad(rec_w_in[:, :, lru_cols:], ((0, 0), (0, 0), (0, dt_pad))).astype(BF16)
    rec_wo, att_wqkv, att_wo = rec_w_out.astype(BF16), att_w_qkv.astype(BF16), att_w_out.astype(BF16)
    ffn_wgu, ffn_wd = ffn_w_gate_up.astype(BF16), ffn_w_down.astype(BF16)
    for layer in range(depth):
        i = layer // 2
        if layer % 2 == 0:
            mixes, wos = _recurrent_layer(h, b, l, i, rec_norm[i], w_lru, w_ssd, rec_wo, lru_conv_w[i],
                                          lru_conv_b[i], lru_w_r[i], lru_b_r[i], lru_w_i[i], lru_b_i[i],
                                          lru_lambda[i], ssd_conv_w[i], ssd_conv_b[i], ssd_dt_bias[i],
                                          ssd_a_log[i], ssd_d[i], ssd_norm[i])
        else:
            mixes, wos = _attention_layer(h, b, l, i, att_norm[i], att_wqkv, att_wo, att_q_norm[i], att_k_norm[i])
        h = _mix_ffn(h, mixes, wos, ffn_norm[layer].reshape(1, d), _layer_block(ffn_wgu, layer),
                     _layer_block(ffn_wd, layer))
    return h.reshape(b, l, d)
```

```python
import functools
import math

import jax
import jax.numpy as jnp
import numpy as np
from jax import lax
from jax.experimental import pallas as pl
from jax.experimental.pallas import tpu as pltpu

F32 = jnp.float32
BF16 = jnp.bfloat16

NORM_EPS = 1e-6
LOG2_E = math.log2(math.e)
CONV_WIDTH = 4
LRU_C = 8.0
GATE_SLAB = 256
SSD_HEAD_DIM = 64
SSD_GROUPS = 2
SSD_STATE = 128
SSD_CHUNK = 128
ATT_HEAD_DIM = 64
ROPE_DIM = 16
ROPE_THETA = 500000.0
SPAN = 128
DILATIONS = (1, 4, 16)
SUPER = SPAN * DILATIONS[-1]
ATT_LANES = 256
ACC_DIL = DILATIONS[1]
ACC_SHIFT = ACC_DIL.bit_length() - 1
LANES = 128
SUBLANES = 8
SUBLANE_SHIFT = SUBLANES.bit_length() - 1
BF16_ROWS = 16
VMEM_LIMIT_BYTES = 56 * 1024 * 1024


def _params(*semantics):
    return pltpu.CompilerParams(dimension_semantics=semantics, vmem_limit_bytes=VMEM_LIMIT_BYTES)


def _rmsnorm(x, g):
    ms = jnp.mean(x * x, axis=-1, keepdims=True)
    return x * lax.rsqrt(ms + NORM_EPS) * g


def _sigmoid(x):
    return 1.0 / (1.0 + jnp.exp2(x * (-LOG2_E)))


def _silu(x):
    return x * _sigmoid(x)


def _softplus(x):
    return jnp.maximum(x, 0.0) + jnp.log1p(jnp.exp(-jnp.abs(x)))


def _gelu_tanh(x):
    c = math.sqrt(2.0 / math.pi)
    half_x = 0.5 * x
    return half_x + half_x * jnp.tanh(x * (c + (c * 0.044715) * (x * x)))


FFN_TM = 1024
FFN_TH = 256


def _ffn_body(n_mix, x_ref, *refs):
    mix_refs, wo_refs = refs[:n_mix], refs[n_mix:2 * n_mix]
    g_ref, wgu_ref, wd_ref, o_ref, h_ref, act_ref = refs[2 * n_mix:]
    x1 = x_ref[...]
    for mix_ref, wo_ref in zip(mix_refs, wo_refs):
        x1 = x1 + jnp.dot(mix_ref[...], wo_ref[...], preferred_element_type=F32)
    o_ref[...] = x1
    h_ref[...] = _rmsnorm(x1, g_ref[...]).astype(BF16)
    hid = act_ref.shape[1]
    for c in range(hid // FFN_TH):
        cols = slice(c * FFN_TH, (c + 1) * FFN_TH)
        up_cols = slice(hid + c * FFN_TH, hid + (c + 1) * FFN_TH)
        gate = jnp.dot(h_ref[...], wgu_ref[:, cols], preferred_element_type=F32)
        up = jnp.dot(h_ref[...], wgu_ref[:, up_cols], preferred_element_type=F32)
        act_ref[:, cols] = (_silu(gate) * up).astype(BF16)
    o_ref[...] += jnp.dot(act_ref[...], wd_ref[...], preferred_element_type=F32)


def _resident(shape):
    return pl.BlockSpec(shape, lambda *_: (0,) * len(shape), pipeline_mode=pl.Buffered(1))


def _layer_block(stack, layer, block=None, index=None):
    block = tuple(stack.shape[1:]) if block is None else tuple(block)
    index = (layer,) + (tuple(index) if index is not None else (0,) * len(block))
    return stack, pl.BlockSpec((None,) + block, lambda *_: index, pipeline_mode=pl.Buffered(1))


def _mix_ffn(x, mixes, wos, g, wgu, wd):
    t, d = x.shape
    hid = wd[0].shape[1]
    weights = (*wos, (g, _resident(g.shape)), wgu, wd)
    rows = lambda a: pl.BlockSpec((FFN_TM, a.shape[1]), lambda i: (i, 0))
    return pl.pallas_call(
        functools.partial(_ffn_body, len(mixes)),
        grid=(t // FFN_TM,),
        in_specs=[rows(x)] + [rows(a) for a in mixes] + [spec for _, spec in weights],
        out_specs=rows(x),
        out_shape=jax.ShapeDtypeStruct((t, d), F32),
        scratch_shapes=[pltpu.VMEM((FFN_TM, d), BF16), pltpu.VMEM((FFN_TM, hid), BF16)],
        compiler_params=_params("parallel"),
        name="mix_ffn",
    )(x, *mixes, *[a for a, _ in weights])


def _segment_perm(rows):
    steps = rows // SUBLANES
    rho = np.arange(rows)
    time = (rho % SUBLANES) * steps + rho // SUBLANES
    perm = time[:, None] == np.arange(rows)[None, :]
    return jnp.asarray(perm, dtype=BF16), jnp.asarray(perm.T, dtype=BF16)


def _conv_segments(x, halo_ref, w_ref, b_ref):
    rows, width = x.shape
    sub0 = lax.broadcasted_iota(jnp.int32, (SUBLANES, width), 0) == 0
    halos = []
    for k in range(1, CONV_WIDTH):
        tail = pltpu.roll(x[rows - k * SUBLANES:rows - (k - 1) * SUBLANES, :], 1, 0)
        halos.append(jnp.where(sub0, halo_ref[k - 1], tail))
        halo_ref[k - 1] = tail
    y = b_ref[...] + w_ref[CONV_WIDTH - 1:CONV_WIDTH, :] * x
    for k in range(1, CONV_WIDTH):
        back = jnp.concatenate(halos[k - 1::-1] + [x[:rows - k * SUBLANES, :]], axis=0)
        y = y + w_ref[CONV_WIDTH - 1 - k:CONV_WIDTH - k, :] * back
    return y


LRU_TL = 1024
LRU_SUB = 256


def _lru_body(x_ref, g_ref, w_ref, cw_ref, cb_ref, wr_ref, wi_ref, br_ref, bi_ref, lam_ref, perm_ref, unperm_ref,
              o_ref, hn_ref, halo_ref, carry_ref):
    first = pl.program_id(1) == 0
    w = o_ref.shape[1]
    sub = perm_ref.shape[0]
    steps = sub // SUBLANES

    @pl.when(first)
    def _():
        carry_ref[...] = jnp.zeros_like(carry_ref)
        halo_ref[...] = jnp.zeros_like(halo_ref)

    for t in range(o_ref.shape[0] // sub):
        rows = slice(t * sub, (t + 1) * sub)
        hn = _rmsnorm(x_ref[rows, :], g_ref[...]).astype(BF16)
        hn_ref[rows, :] = hn
        hn = jnp.dot(perm_ref[...], hn, preferred_element_type=F32).astype(BF16)
        proj = jnp.dot(hn, w_ref[...], preferred_element_type=F32)
        gate = proj[:, w:]

        xc = _conv_segments(proj[:, :w], halo_ref, cw_ref, cb_ref)
        xcb = xc.astype(BF16)
        pre_r, pre_i = [], []
        for s in range(w // GATE_SLAB):
            slab = xcb[:, s * GATE_SLAB:(s + 1) * GATE_SLAB]
            pre_r.append(jnp.dot(slab, wr_ref[s], preferred_element_type=F32))
            pre_i.append(jnp.dot(slab, wi_ref[s], preferred_element_type=F32))
        r = _sigmoid(jnp.concatenate(pre_r, axis=1) + br_ref[...])
        i = _sigmoid(jnp.concatenate(pre_i, axis=1) + bi_ref[...])
        log2_a = (-LRU_C * LOG2_E * r) * _softplus(-lam_ref[...])
        a = jnp.exp2(log2_a)
        z = 1.0 - a * a
        u = jnp.where(z > 0.0, z * lax.rsqrt(z), 0.0) * (i * xc)

        h_j = jnp.zeros((SUBLANES, w), F32)
        p_j = jnp.ones((SUBLANES, w), F32)
        hs, ps = [], []
        for j in range(steps):
            a_j = a[j * SUBLANES:(j + 1) * SUBLANES, :]
            h_j = a_j * h_j + u[j * SUBLANES:(j + 1) * SUBLANES, :]
            p_j = a_j * p_j
            hs.append(h_j)
            ps.append(p_j)
        c = carry_ref[0:1, :]
        entering = []
        for seg in range(SUBLANES):
            entering.append(c)
            c = p_j[seg:seg + 1, :] * c + h_j[seg:seg + 1, :]
        carry_ref[0:1, :] = c
        enter = jnp.concatenate(entering, axis=0)
        h = jnp.concatenate([hs[j] + ps[j] * enter for j in range(steps)], axis=0)
        out = (h * _gelu_tanh(gate)).astype(BF16)
        o_ref[rows, :] = jnp.dot(unperm_ref[...], out, preferred_element_type=F32).astype(BF16)


def _lru(x3, g, w_lru, cw, cb, wr_bd, wi_bd, br, bi, lam):
    b, l, d = x3.shape
    width = lam.shape[1]
    perms = _segment_perm(LRU_SUB)
    return pl.pallas_call(
        _lru_body,
        grid=(b, l // LRU_TL),
        in_specs=[pl.BlockSpec((None, LRU_TL, d), lambda bi_, li: (bi_, li, 0))]
        + [_resident(g.shape), w_lru[1]]
        + [_resident(a.shape) for a in (cw, cb, wr_bd, wi_bd, br, bi, lam, *perms)],
        out_specs=[pl.BlockSpec((None, LRU_TL, width), lambda bi_, li: (bi_, li, 0)),
                   pl.BlockSpec((None, LRU_TL, d), lambda bi_, li: (bi_, li, 0))],
        out_shape=[jax.ShapeDtypeStruct((b, l, width), BF16),
                   jax.ShapeDtypeStruct((b, l, d), BF16)],
        scratch_shapes=[
            pltpu.VMEM((CONV_WIDTH - 1, SUBLANES, width), F32),
            pltpu.VMEM((SUBLANES, width), F32),
        ],
        compiler_params=_params("parallel", "arbitrary"),
        name="lru",
    )(x3, g, w_lru[0], cw, cb, wr_bd, wi_bd, br, bi, lam, *perms)


def _block_diag_slabs(w):
    nb, bs, _ = w.shape
    per = GATE_SLAB // bs
    w = w.reshape(nb // per, per, bs, bs)
    eye = jnp.eye(per, dtype=w.dtype)
    bd = jnp.einsum("spij,pq->spiqj", w, eye).reshape(nb // per, GATE_SLAB, GATE_SLAB)
    return bd.astype(BF16)


def _ssd_body(n_heads, hn_ref, w_ref, cw_ref, cb_ref, dtb_ref, alog_ref, dvec_ref, nrm_ref,
              exp_ref, perm_ref, unperm_ref, o_ref, proj_ref, halo_ref, state_ref):
    first = pl.program_id(1) == 0
    t = SSD_CHUNK
    width = n_heads * SSD_HEAD_DIM
    gw = SSD_STATE
    conv_ch = width + 2 * SSD_GROUPS * gw
    n_chunks = hn_ref.shape[0] // t

    @pl.when(first)
    def _():
        state_ref[...] = jnp.zeros_like(state_ref)
        halo_ref[...] = jnp.zeros_like(halo_ref)

    hn = jnp.concatenate([jnp.dot(perm_ref[...], hn_ref[c * t:(c + 1) * t, :], preferred_element_type=F32).astype(BF16)
                          for c in range(n_chunks)], axis=0)
    proj_ref[...] = jnp.dot(hn, w_ref[...], preferred_element_type=F32)

    steps = t // SUBLANES
    time_of = lambda i: (i & (SUBLANES - 1)) * steps + lax.shift_right_logical(i, SUBLANE_SHIFT)
    causal = (time_of(lax.broadcasted_iota(jnp.int32, (t, t), 0))
              >= time_of(lax.broadcasted_iota(jnp.int32, (t, t), 1)))
    tri = jnp.where(causal, 1.0, 0.0).astype(BF16)
    lane = lax.broadcasted_iota(jnp.int32, (t, LANES), 1)
    lo = lane < SSD_HEAD_DIM
    rlo = lax.broadcasted_iota(jnp.int32, (LANES, gw), 0) < SSD_HEAD_DIM

    dt = _softplus(proj_ref[:, width + conv_ch:] + dtb_ref[...])
    adt = dt * (-LOG2_E * jnp.exp(alog_ref[...]))
    cs = jnp.concatenate([_split_dot(adt[c * t:(c + 1) * t, :], tri, parts=3, left=True)
                          for c in range(n_chunks)], axis=0)
    dt_w = _split_dot(dt, exp_ref[...], parts=2)
    cs_w = _split_dot(cs, exp_ref[...], parts=2)

    for c in range(n_chunks):
        rows = slice(c * t, (c + 1) * t)
        out = _ssd_chunk(proj_ref[rows, 0:width], proj_ref[rows, width:width + conv_ch],
                         cs[rows, :], dt_w[rows, :], cs_w[rows, :], n_heads, causal, lo, rlo, halo_ref, state_ref,
                         cw_ref, cb_ref, dvec_ref, nrm_ref)
        o_ref[rows, :] = jnp.dot(unperm_ref[...], out, preferred_element_type=F32).astype(BF16)


def _ssd_chunk(z, xbc_raw, cs, dt_w, cs_w, n_heads, causal, lo, rlo, halo_ref, state_ref,
               cw_ref, cb_ref, dvec_ref, nrm_ref):
    t = SSD_CHUNK
    width = n_heads * SSD_HEAD_DIM
    gw = SSD_STATE
    heads_per_group = n_heads // SSD_GROUPS
    xbc = _silu(_conv_segments(xbc_raw, halo_ref, cw_ref, cb_ref))
    xs = xbc[:, :width]
    cs_t = cs.T
    cs_last = cs[t - 1:t, :]
    cdec = jnp.exp2(cs_last)
    ecs_w = jnp.exp2(cs_w)
    dte_w = jnp.exp2(cs_w[t - 1:t, :] - cs_w)

    cbs = []
    for g in range(SSD_GROUPS):
        bm = xbc[:, width + g * gw: width + (g + 1) * gw].astype(BF16)
        cm = xbc[:, width + (SSD_GROUPS + g) * gw: width + (SSD_GROUPS + g + 1) * gw].astype(BF16)
        cb = lax.dot_general(cm, bm, (((1,), (1,)), ((), ())), preferred_element_type=F32)
        cbs.append((bm, cm, cb))

    ys = []
    for p in range(n_heads // 2):
        h0, h1 = 2 * p, 2 * p + 1
        bm, cm, cb = cbs[h0 // heads_per_group]
        sl = slice(p * LANES, (p + 1) * LANES)
        xs_p = xs[:, sl]
        xdt = xs_p * dt_w[:, sl]
        ms = []
        for h in (h0, h1):
            seg = cs[:, h:h + 1] - cs_t[h:h + 1, :]
            ms.append((cb * jnp.where(causal, jnp.exp2(seg), 0.0)).astype(BF16))
        m_cat = jnp.concatenate(ms, axis=1)
        xdt_bd = jnp.concatenate([jnp.where(lo, xdt, 0.0), jnp.where(lo, 0.0, xdt)], axis=0).astype(BF16)
        y = jnp.dot(m_cat, xdt_bd, preferred_element_type=F32)
        prev = state_ref[sl, :]
        y_off = lax.dot_general(cm, prev.astype(BF16), (((1,), (1,)), ((), ())), preferred_element_type=F32)
        y = y + y_off * ecs_w[:, sl]
        xw = (xdt * dte_w[:, sl]).astype(BF16)
        st = lax.dot_general(xw, bm, (((0,), (0,)), ((), ())), preferred_element_type=F32)
        dec = jnp.where(rlo, cdec[:, h0:h0 + 1], cdec[:, h1:h1 + 1])
        state_ref[sl, :] = prev * dec + st
        ys.append(y + dvec_ref[:, sl] * xs_p)

    y = jnp.concatenate(ys, axis=1) * _silu(z)
    gsz = width // SSD_GROUPS
    outs = []
    for g in range(SSD_GROUPS):
        yg = y[:, g * gsz:(g + 1) * gsz]
        outs.append(yg * lax.rsqrt(jnp.mean(yg * yg, axis=-1, keepdims=True) + NORM_EPS))
    return (jnp.concatenate(outs, axis=1) * nrm_ref[...]).astype(BF16)


SSD_TS = 1024


def _ssd(hn3, w_ssd, cw, cb, dt_bias, a_log, d_vec, nrm, n_heads):
    b, l, d = hn3.shape
    width = n_heads * SSD_HEAD_DIM
    conv_ch = width + 2 * SSD_GROUPS * SSD_STATE
    expand = jnp.asarray(np.arange(LANES)[:, None] == np.arange(width)[None, :] // SSD_HEAD_DIM,
                         dtype=BF16)
    consts = (cw, cb, dt_bias, a_log, d_vec, nrm, expand, *_segment_perm(SSD_CHUNK))
    return pl.pallas_call(
        functools.partial(_ssd_body, n_heads),
        grid=(b, l // SSD_TS),
        in_specs=[pl.BlockSpec((None, SSD_TS, d), lambda bi_, ci: (bi_, ci, 0))]
        + [w_ssd[1]] + [_resident(a.shape) for a in consts],
        out_specs=pl.BlockSpec((None, SSD_TS, width), lambda bi_, ci: (bi_, ci, 0)),
        out_shape=jax.ShapeDtypeStruct((b, l, width), BF16),
        scratch_shapes=[
            pltpu.VMEM((SSD_TS, w_ssd[0].shape[-1]), F32),
            pltpu.VMEM((CONV_WIDTH - 1, SUBLANES, conv_ch), F32),
            pltpu.VMEM((width, SSD_STATE), F32),
        ],
        compiler_params=_params("parallel", "arbitrary"),
        name="ssd",
    )(hn3, w_ssd[0], *consts)


QKV_TM = 512
Q_SCALE = ATT_HEAD_DIM ** -0.5 * LOG2_E
PERM_ROWS = 256


def _split_dot(x, m, parts=2, left=False):
    acc, rem = None, x
    for k in range(parts):
        piece = rem.astype(BF16)
        term = jnp.dot(m, piece, preferred_element_type=F32) if left else jnp.dot(piece, m, preferred_element_type=F32)
        acc = term if acc is None else acc + term
        if k + 1 < parts:
            rem = rem - piece.astype(F32)
    return acc


def _slab_map(fn, x):
    return jnp.concatenate([fn(x[:, c * GATE_SLAB:(c + 1) * GATE_SLAB]) for c in range(x.shape[1] // GATE_SLAB)],
                           axis=1)


def _head_norm_rope(y, gain, cos, sin, gsum, rot):
    ss = _slab_map(lambda v: _split_dot(v, gsum, parts=1), y * y)
    yn = y * lax.rsqrt(ss * (1.0 / ATT_HEAD_DIM) + NORM_EPS) * gain
    partner = _slab_map(lambda v: _split_dot(v, rot), yn)
    tile = lambda tbl: jnp.concatenate([tbl] * (y.shape[1] // LANES), axis=1)
    return yn * tile(cos) + partner * tile(sin)


def _qkv_body(x_ref, g_ref, w_ref, qg_ref, kg_ref, cos_ref, sin_ref, gsum_ref, rot_ref, p1_ref, p4_ref, p16_ref,
              *refs):
    outs, h_ref = refs[:9], refs[9]
    d = x_ref.shape[1]
    h_ref[...] = _rmsnorm(x_ref[...], g_ref[...]).astype(BF16)

    sub = PERM_ROWS

    def section(j, o1, o4, o16, gain_ref=None, grouped=False):
        y = jnp.dot(h_ref[...], w_ref[:, j * d:(j + 1) * d], preferred_element_type=F32)
        if gain_ref is not None:
            y = _head_norm_rope(y, gain_ref[...], cos_ref[...], sin_ref[...], gsum_ref[...], rot_ref[...])
        y_all = y.astype(BF16)
        for t in range(x_ref.shape[0] // sub):
            rows = slice(t * sub, (t + 1) * sub)
            yb = y_all[rows, :]
            if grouped:
                o1[rows, :] = jnp.dot(p1_ref[...], yb, preferred_element_type=F32).astype(BF16)
            else:
                o1[rows, :] = yb
            for perm_ref, o in ((p4_ref, o4), (p16_ref, o16)):
                dil = o.shape[0]
                n = sub // dil
                yp = jnp.dot(perm_ref[...], yb, preferred_element_type=F32).astype(BF16)
                for r in range(dil):
                    o[r, t * n:(t + 1) * n, :] = yp[r * n:(r + 1) * n, :]

    section(0, *outs[0:3], gain_ref=qg_ref, grouped=True)
    section(1, *outs[3:6], gain_ref=kg_ref)
    section(2, *outs[6:9])


def _qkv(x3, g, w_qkv, layer, qg, kg, cos, sin):
    b, l, d = x3.shape
    tm = QKV_TM
    out_specs, out_shape = [], []
    for _ in range(3):
        out_specs.append(pl.BlockSpec((None, tm, d), lambda bi_, i: (bi_, i, 0)))
        out_shape.append(jax.ShapeDtypeStruct((b, l, d), BF16))
        for dil in DILATIONS[1:]:
            out_specs.append(pl.BlockSpec((None, dil, tm // dil, d), lambda bi_, i: (bi_, 0, i, 0)))
            out_shape.append(jax.ShapeDtypeStruct((b, dil, l // dil, d), BF16))
    consts = _qkv_constants(tm)
    tbl = lambda: pl.BlockSpec((tm, LANES), lambda bi_, i: (i, 0))
    w_arr, w_spec = _layer_block(w_qkv, layer)
    return pl.pallas_call(
        _qkv_body,
        grid=(b, l // tm),
        in_specs=[
            pl.BlockSpec((None, tm, d), lambda bi_, i: (bi_, i, 0)),
            _resident((1, d)),
            w_spec,
            _resident((1, d)), _resident((1, d)), tbl(), tbl(),
        ] + [_resident(c.shape) for c in consts],
        out_specs=out_specs,
        out_shape=out_shape,
        scratch_shapes=[pltpu.VMEM((tm, d), BF16)],
        compiler_params=_params("parallel", "parallel"),
        name="qkv",
    )(x3, g, w_arr, qg, kg, cos, sin, *consts)


def _qkv_constants(tm):
    lane = np.arange(GATE_SLAB)
    gsum = (lane[:, None] // ATT_HEAD_DIM == lane[None, :] // ATT_HEAD_DIM)
    half = ROPE_DIM // 2
    pos = lane % ATT_HEAD_DIM
    partner = np.where(pos < half, lane + half, np.where(pos < ROPE_DIM, lane - half, -1))
    rot = lane[:, None] == partner[None, :]
    out_row = np.arange(PERM_ROWS)
    group = SPAN // ACC_DIL
    within = out_row % SPAN
    src = out_row - within + (within % group) * ACC_DIL + within // group
    perms = [src[:, None] == np.arange(PERM_ROWS)[None, :]]
    for dil in DILATIONS[1:]:
        sub = PERM_ROWS
        rows = sub // dil
        assert rows % BF16_ROWS == 0 and tm % sub == 0
        out_row = np.arange(sub)
        src = (out_row % rows) * dil + out_row // rows
        perms.append(src[:, None] == np.arange(sub)[None, :])
    return tuple(jnp.asarray(m, dtype=BF16) for m in (gsum, rot, *perms))


def _rope_tables(l):
    half = ROPE_DIM // 2
    pos = jnp.arange(l, dtype=F32)
    inv = ROPE_THETA ** (-2.0 * jnp.arange(half, dtype=F32) / ROPE_DIM)
    ang = pos[:, None] * inv[None, :]
    cos, sin = jnp.cos(ang), jnp.sin(ang)
    pad = ATT_HEAD_DIM - ROPE_DIM
    cos_h = jnp.concatenate([cos, cos, jnp.ones((l, pad), F32)], axis=1)
    sin_h = jnp.concatenate([-sin, sin, jnp.zeros((l, pad), F32)], axis=1)
    rep = LANES // ATT_HEAD_DIM
    return tuple(jnp.concatenate([tb] * rep, axis=1) for tb in (cos_h, sin_h))


def _attn_body(q1, k1c, k1p, v1c, v1p, q4, k4c, k4p, v4c, v4p, q16, k16c, k16p, v16c, v16p,
               o_ref, s_scr, p_scr, mx_scr, bias_scr, tok_scr, acc_ref, m_ref, l_ref):
    first = pl.program_id(2) == 0
    n_pairs = acc_ref.shape[0]
    n_blocks = SUPER // SPAN
    trans_b = (((1,), (1,)), ((), ()))

    row = lax.broadcasted_iota(jnp.int32, (SPAN, 2 * SPAN), 0)
    kj = lax.broadcasted_iota(jnp.int32, (SPAN, 2 * SPAN), 1)
    group = SPAN // ACC_DIL
    for base, qi in ((0, row), (2, (row % group) * ACC_DIL + row // group)):
        band = (kj >= qi) & (kj <= qi + SPAN)
        bias_scr[base] = jnp.where(band, 0.0, -jnp.inf)
        bias_scr[base + 1] = jnp.where(band & (kj >= SPAN), 0.0, -jnp.inf)

    lo = lax.broadcasted_iota(jnp.int32, (SPAN, LANES), 1) < ATT_HEAD_DIM
    lo2 = lax.broadcasted_iota(jnp.int32, (2 * SPAN, LANES), 1) < ATT_HEAD_DIM
    ones_lo = jnp.where(lo2, 1.0, 0.0).astype(BF16)
    ones_hi = jnp.where(lo2, 0.0, 1.0).astype(BF16)
    lanes = lambda p: slice(p * LANES, (p + 1) * LANES)

    def run_pattern(dil, q_get, cur_prev_k, cur_prev_v):
        per_res = SUPER // (SPAN * dil)
        overwrite = dil == DILATIONS[0]

        def split(f):
            if per_res == 1:
                return f, 0
            if dil == 1:
                return 0, f
            if isinstance(f, int):
                return f % dil, f // dil
            return f & (dil - 1), lax.shift_right_logical(f, dil.bit_length() - 1)

        def slab(refs, r, nb):
            cur, prev = refs
            if dil > 1:
                cur, prev = cur.at[r], prev.at[r]
            if isinstance(nb, int) and nb == 0:
                return jnp.concatenate([prev[...], cur[0:SPAN, :]], axis=0)
            start = (nb - 1) * SPAN
            return cur[pl.ds(start if isinstance(nb, int) else pl.multiple_of(start, SPAN), 2 * SPAN), :]

        def stage_scores(f, slot):
            r, nb = split(f)
            if isinstance(nb, int):
                flag = jnp.where(first, 1, 0) if nb == 0 else 0
            else:
                flag = jnp.where(first & (nb == 0), 1, 0)
            bias = bias_scr[flag + (2 if overwrite else 0)]
            bias2 = jnp.concatenate([bias, bias], axis=0)
            q = q_get(r, nb)
            ks = slab(cur_prev_k, r, nb)
            for p in range(n_pairs):
                qp = q[:, lanes(p)]
                zero = jnp.zeros_like(qp)
                q2 = jnp.concatenate([jnp.where(lo, qp, zero), jnp.where(lo, zero, qp)], axis=0)
                s_scr[slot, p] = lax.dot_general(q2, ks[:, lanes(p)], trans_b, preferred_element_type=F32) + bias2

        def stage_softmax(src, dst):
            for p in range(n_pairs):
                s = s_scr[src, p]
                mx = jnp.max(s, axis=1, keepdims=True)
                pr = jnp.exp2(s - mx).astype(BF16)
                p_scr[dst, p, :, 0:2 * SPAN] = pr[:SPAN]
                p_scr[dst, p, :, 2 * SPAN:] = pr[SPAN:]
                mx_scr[dst, p] = jnp.where(lo, mx[:SPAN], mx[SPAN:])

        def stage_values(f, slot):
            r, nb = split(f)
            vs = slab(cur_prev_v, r, nb)
            if overwrite:
                grp = SPAN // ACC_DIL
                start = nb * grp if isinstance(nb, int) else pl.multiple_of(nb * grp, grp)
                pieces = [(slice(c * grp, (c + 1) * grp), c, pl.ds(start, grp)) for c in range(ACC_DIL)]
            elif dil == ACC_DIL:
                pieces = [(slice(0, SPAN), r, q_rows(nb))]
            else:
                sub = dil // ACC_DIL
                pieces = [(slice(0, SPAN), r & (ACC_DIL - 1),
                           pl.ds(lax.shift_right_logical(r, ACC_SHIFT) if not isinstance(r, int) else r // ACC_DIL,
                                 SPAN, stride=sub))]
            for p in range(n_pairs):
                vp = vs[:, lanes(p)]
                zero = jnp.zeros_like(vp)
                v_bd = jnp.concatenate([jnp.concatenate([jnp.where(lo2, vp, zero), ones_lo], axis=1),
                                        jnp.concatenate([jnp.where(lo2, zero, vp), ones_hi], axis=1)], axis=0)
                pv = jnp.dot(p_scr[slot, p], v_bd, preferred_element_type=F32)
                o_blk, l_blk = pv[:, :LANES], pv[:, LANES:]
                m_blk = mx_scr[slot, p]
                for src, cls, rows in pieces:
                    m_new, l_new, o_new = m_blk[src], l_blk[src], o_blk[src]
                    if overwrite:
                        m_ref[p, cls, rows, :], l_ref[p, cls, rows, :], acc_ref[p, cls, rows, :] = m_new, l_new, o_new
                        continue
                    m_old = m_ref[p, cls, rows, :]
                    m = jnp.maximum(m_old, m_new)
                    e_old = jnp.exp2(m_old - m)
                    e_new = jnp.exp2(m_new - m)
                    l_ref[p, cls, rows, :] = e_old * l_ref[p, cls, rows, :] + e_new * l_new
                    acc_ref[p, cls, rows, :] = e_old * acc_ref[p, cls, rows, :] + e_new * o_new
                    m_ref[p, cls, rows, :] = m

        n_rounds = n_blocks // 2

        def round_(k, odd, scores=True, softmax=True, values=True):
            wr = 2 if odd else 0
            rd = 2 - wr
            if values:
                stage_values(2 * k - 4, rd)
                stage_values(2 * k - 3, rd + 1)
            if softmax:
                stage_softmax(rd, wr)
                stage_softmax(rd + 1, wr + 1)
            if scores:
                stage_scores(2 * k, wr)
                stage_scores(2 * k + 1, wr + 1)

        round_(0, False, softmax=False, values=False)
        round_(1, True, values=False)
        round_(2, False)
        round_(3, True)

        def body(j, c):
            round_(2 * j, False)
            round_(2 * j + 1, True)
            return c

        for j in range(2, n_rounds // 2):
            body(j, 0)
        round_(n_rounds, False, scores=False)
        round_(n_rounds + 1, True, scores=False, softmax=False)

    def q_rows(nb):
        return pl.ds(nb * SPAN if isinstance(nb, int) else pl.multiple_of(nb * SPAN, SPAN), SPAN)

    run_pattern(DILATIONS[0], lambda r, nb: q1[q_rows(nb), :], (k1c, k1p), (v1c, v1p))
    run_pattern(DILATIONS[1], lambda r, nb: q4[r, q_rows(nb), :], (k4c, k4p), (v4c, v4p))
    run_pattern(DILATIONS[2], lambda r, nb: q16[r], (k16c, k16p), (v16c, v16p))

    for p in range(n_pairs):
        for cls in range(ACC_DIL):
            tok_scr[p, pl.ds(cls, SUPER // ACC_DIL, stride=ACC_DIL), :] = acc_ref[p, cls] / l_ref[p, cls]
        o_ref[:, lanes(p)] = tok_scr[p].astype(BF16)


def _attention(qkv_outs):
    q1, q4, q16, k1, k4, k16, v1, v4, v16 = qkv_outs
    b, l, d = q1.shape
    w = ATT_LANES
    n_pairs = w // LANES
    d4, d16 = DILATIONS[1], DILATIONS[2]
    cur1 = pl.BlockSpec((None, SUPER, w), lambda bi_, g, s: (bi_, s, g))
    prev1 = pl.BlockSpec((None, SPAN, w), lambda bi_, g, s: (bi_, jnp.maximum(s * (SUPER // SPAN) - 1, 0), g))
    cur4 = pl.BlockSpec((None, d4, SUPER // d4, w), lambda bi_, g, s: (bi_, 0, s, g))
    prev4 = pl.BlockSpec((None, d4, SPAN, w),
                         lambda bi_, g, s: (bi_, 0, jnp.maximum(s * (SUPER // (d4 * SPAN)) - 1, 0), g))
    cur16 = pl.BlockSpec((None, d16, SPAN, w), lambda bi_, g, s: (bi_, 0, s, g))
    prev16 = pl.BlockSpec((None, d16, SPAN, w), lambda bi_, g, s: (bi_, 0, jnp.maximum(s - 1, 0), g))
    return pl.pallas_call(
        _attn_body,
        grid=(b, d // w, l // SUPER),
        in_specs=[cur1, cur1, prev1, cur1, prev1,
                  cur4, cur4, prev4, cur4, prev4,
                  cur16, cur16, prev16, cur16, prev16],
        out_specs=pl.BlockSpec((None, SUPER, w), lambda bi_, g, s: (bi_, s, g)),
        out_shape=jax.ShapeDtypeStruct((b, l, d), BF16),
        scratch_shapes=[
            pltpu.VMEM((4, n_pairs, 2 * SPAN, 2 * SPAN), F32),
            pltpu.VMEM((4, n_pairs, SPAN, 4 * SPAN), BF16),
            pltpu.VMEM((4, n_pairs, SPAN, LANES), F32),
            pltpu.VMEM((4, SPAN, 2 * SPAN), F32),
            pltpu.VMEM((n_pairs, SUPER, LANES), F32),
        ] + [pltpu.VMEM((n_pairs, ACC_DIL, SUPER // ACC_DIL, LANES), F32)] * 3,
        compiler_params=_params("parallel", "parallel", "arbitrary"),
        name="dilated_attention",
    )(q1, k1, k1, v1, v1, q4, k4, k4, v4, v4, q16, k16, k16, v16, v16)


def _recurrent_layer(x, b, l, i, norm_g, w_lru, w_ssd, w_out, lru_conv_w, lru_conv_b, lru_w_r, lru_b_r, lru_w_i,
                     lru_b_i, lru_lambda, ssd_conv_w, ssd_conv_b, ssd_dt_bias, ssd_a_log, ssd_d, ssd_norm):
    d = x.shape[1]
    width = lru_lambda.shape[0]
    n_heads = ssd_a_log.shape[0]
    row = lambda v: v.reshape(1, -1)
    pad_lanes = lambda v: jnp.pad(v.reshape(1, -1), ((0, 0), (0, LANES - v.shape[0])))
    x3 = x.reshape(b, l, d)
    out_a, hn3 = _lru(x3, row(norm_g), _layer_block(w_lru, i, (d, 2 * width)), lru_conv_w, row(lru_conv_b),
                      _block_diag_slabs(lru_w_r), _block_diag_slabs(lru_w_i), row(lru_b_r), row(lru_b_i),
                      row(lru_lambda))
    out_b = _ssd(hn3, _layer_block(w_ssd, i), ssd_conv_w, row(ssd_conv_b), pad_lanes(ssd_dt_bias),
                 pad_lanes(ssd_a_log), row(jnp.repeat(ssd_d, SSD_HEAD_DIM)), row(ssd_norm), n_heads)
    return ([out_a.reshape(b * l, width), out_b.reshape(b * l, -1)],
            [_layer_block(w_out, i, (width, d), (0, 0)), _layer_block(w_out, i, (width, d), (1, 0))])


def _attention_layer(x, b, l, i, norm_g, w_qkv, w_out, q_norm, k_norm):
    d = x.shape[1]
    heads = d // ATT_HEAD_DIM
    row = lambda v: v.reshape(1, -1)
    outs = _qkv(x.reshape(b, l, d), row(norm_g), w_qkv, i,
                row(jnp.tile(q_norm, heads) * Q_SCALE), row(jnp.tile(k_norm, heads)), *_rope_tables(l))
    o = _attention(outs)
    return [o.reshape(b * l, d)], [_layer_block(w_out, i)]


def kernel(x, rec_norm, rec_w_in, lru_conv_w, lru_conv_b, lru_w_r, lru_b_r, lru_w_i, lru_b_i, lru_lambda,
           ssd_conv_w, ssd_conv_b, ssd_dt_bias, ssd_a_log, ssd_d, ssd_norm, rec_w_out, att_norm, att_w_qkv,
           att_q_norm, att_k_norm, att_w_out, ffn_norm, ffn_w_gate_up, ffn_w_down):
    b, l, d = x.shape
    depth = ffn_norm.shape[0]
    h = x.reshape(b * l, d)
    lru_cols = 2 * lru_lambda.shape[1]
    w_lru = rec_w_in.astype(BF16)
    dt_pad = LANES - ssd_a_log.shape[1]
    w_ssd = jnp.pad(rec_w_in[:, :, lru_cols:], ((0, 0), (0, 0), (0, dt_pad))).astype(BF16)
    rec_wo, att_wqkv, att_wo = rec_w_out.astype(BF16), att_w_qkv.astype(BF16), att_w_out.astype(BF16)
    ffn_wgu, ffn_wd = ffn_w_gate_up.astype(BF16), ffn_w_down.astype(BF16)
    for layer in range(depth):
        i = layer // 2
        if layer % 2 == 0:
            mixes, wos = _recurrent_layer(h, b, l, i, rec_norm[i], w_lru, w_ssd, rec_wo, lru_conv_w[i],
                                          lru_conv_b[i], lru_w_r[i], lru_b_r[i], lru_w_i[i], lru_b_i[i],
                                          lru_lambda[i], ssd_conv_w[i], ssd_conv_b[i], ssd_dt_bias[i],
                                          ssd_a_log[i], ssd_d[i], ssd_norm[i])
        else:
            mixes, wos = _attention_layer(h, b, l, i, att_norm[i], att_wqkv, att_wo, att_q_norm[i], att_k_norm[i])
        h = _mix_ffn(h, mixes, wos, ffn_norm[layer].reshape(1, d), _layer_block(ffn_wgu, layer),
                     _layer_block(ffn_wd, layer))
    return h.reshape(b, l, d)
```

```python
import functools
import math

import jax
import jax.numpy as jnp
import numpy as np
from jax import lax
from jax.experimental import pallas as pl
from jax.experimental.pallas import tpu as pltpu

F32 = jnp.float32
BF16 = jnp.bfloat16

NORM_EPS = 1e-6
LOG2_E = math.log2(math.e)
CONV_WIDTH = 4
LRU_C = 8.0
GATE_SLAB = 256
SSD_HEAD_DIM = 64
SSD_GROUPS = 2
SSD_STATE = 128
SSD_CHUNK = 128
ATT_HEAD_DIM = 64
ROPE_DIM = 16
ROPE_THETA = 500000.0
SPAN = 128
DILATIONS = (1, 4, 16)
SUPER = SPAN * DILATIONS[-1]
ATT_LANES = 256
ACC_DIL = DILATIONS[1]
ACC_SHIFT = ACC_DIL.bit_length() - 1
LANES = 128
SUBLANES = 8
SUBLANE_SHIFT = SUBLANES.bit_length() - 1
BF16_ROWS = 16
VMEM_LIMIT_BYTES = 56 * 1024 * 1024


def _params(*semantics):
    return pltpu.CompilerParams(dimension_semantics=semantics, vmem_limit_bytes=VMEM_LIMIT_BYTES)


def _rmsnorm(x, g):
    ms = jnp.mean(x * x, axis=-1, keepdims=True)
    return x * lax.rsqrt(ms + NORM_EPS) * g


def _sigmoid(x):
    return 1.0 / (1.0 + jnp.exp2(x * (-LOG2_E)))


def _silu(x):
    return x * _sigmoid(x)


def _softplus(x):
    return jnp.maximum(x, 0.0) + jnp.log1p(jnp.exp(-jnp.abs(x)))


def _gelu_tanh(x):
    c = math.sqrt(2.0 / math.pi)
    half_x = 0.5 * x
    return half_x + half_x * jnp.tanh(x * (c + (c * 0.044715) * (x * x)))


FFN_TM = 1024
FFN_TH = 256


def _ffn_body(n_mix, x_ref, *refs):
    mix_refs, wo_refs = refs[:n_mix], refs[n_mix:2 * n_mix]
    g_ref, wgu_ref, wd_ref, o_ref, h_ref, act_ref = refs[2 * n_mix:]
    x1 = x_ref[...]
    for mix_ref, wo_ref in zip(mix_refs, wo_refs):
        x1 = x1 + jnp.dot(mix_ref[...], wo_ref[...], preferred_element_type=F32)
    o_ref[...] = x1
    h_ref[...] = _rmsnorm(x1, g_ref[...]).astype(BF16)
    hid = act_ref.shape[1]
    for c in range(hid // FFN_TH):
        cols = slice(c * FFN_TH, (c + 1) * FFN_TH)
        up_cols = slice(hid + c * FFN_TH, hid + (c + 1) * FFN_TH)
        gate = jnp.dot(h_ref[...], wgu_ref[:, cols], preferred_element_type=F32)
        up = jnp.dot(h_ref[...], wgu_ref[:, up_cols], preferred_element_type=F32)
        act_ref[:, cols] = (_silu(gate) * up).astype(BF16)
    o_ref[...] += jnp.dot(act_ref[...], wd_ref[...], preferred_element_type=F32)


def _resident(shape):
    return pl.BlockSpec(shape, lambda *_: (0,) * len(shape), pipeline_mode=pl.Buffered(1))


def _layer_block(stack, layer, block=None, index=None):
    block = tuple(stack.shape[1:]) if block is None else tuple(block)
    index = (layer,) + (tuple(index) if index is not None else (0,) * len(block))
    return stack, pl.BlockSpec((None,) + block, lambda *_: index, pipeline_mode=pl.Buffered(1))


def _mix_ffn(x, mixes, wos, g, wgu, wd):
    t, d = x.shape
    hid = wd[0].shape[1]
    weights = (*wos, (g, _resident(g.shape)), wgu, wd)
    rows = lambda a: pl.BlockSpec((FFN_TM, a.shape[1]), lambda i: (i, 0))
    return pl.pallas_call(
        functools.partial(_ffn_body, len(mixes)),
        grid=(t // FFN_TM,),
        in_specs=[rows(x)] + [rows(a) for a in mixes] + [spec for _, spec in weights],
        out_specs=rows(x),
        out_shape=jax.ShapeDtypeStruct((t, d), F32),
        scratch_shapes=[pltpu.VMEM((FFN_TM, d), BF16), pltpu.VMEM((FFN_TM, hid), BF16)],
        compiler_params=_params("parallel"),
        name="mix_ffn",
    )(x, *mixes, *[a for a, _ in weights])


def _segment_perm(rows):
    steps = rows // SUBLANES
    rho = np.arange(rows)
    time = (rho % SUBLANES) * steps + rho // SUBLANES
    perm = time[:, None] == np.arange(rows)[None, :]
    return jnp.asarray(perm, dtype=BF16), jnp.asarray(perm.T, dtype=BF16)


def _conv_segments(x, halo_ref, w_ref, b_ref):
    rows, width = x.shape
    sub0 = lax.broadcasted_iota(jnp.int32, (SUBLANES, width), 0) == 0
    halos = []
    for k in range(1, CONV_WIDTH):
        tail = pltpu.roll(x[rows - k * SUBLANES:rows - (k - 1) * SUBLANES, :], 1, 0)
        halos.append(jnp.where(sub0, halo_ref[k - 1], tail))
        halo_ref[k - 1] = tail
    y = b_ref[...] + w_ref[CONV_WIDTH - 1:CONV_WIDTH, :] * x
    for k in range(1, CONV_WIDTH):
        back = jnp.concatenate(halos[k - 1::-1] + [x[:rows - k * SUBLANES, :]], axis=0)
        y = y + w_ref[CONV_WIDTH - 1 - k:CONV_WIDTH - k, :] * back
    return y


LRU_TL = 1024
LRU_SUB = 256


def _lru_body(x_ref, g_ref, w_ref, cw_ref, cb_ref, wr_ref, wi_ref, br_ref, bi_ref, lam_ref, perm_ref, unperm_ref,
              o_ref, hn_ref, halo_ref, carry_ref):
    first = pl.program_id(1) == 0
    w = o_ref.shape[1]
    sub = perm_ref.shape[0]
    steps = sub // SUBLANES

    @pl.when(first)
    def _():
        carry_ref[...] = jnp.zeros_like(carry_ref)
        halo_ref[...] = jnp.zeros_like(halo_ref)

    for t in range(o_ref.shape[0] // sub):
        rows = slice(t * sub, (t + 1) * sub)
        hn = _rmsnorm(x_ref[rows, :], g_ref[...]).astype(BF16)
        hn_ref[rows, :] = hn
        hn = jnp.dot(perm_ref[...], hn, preferred_element_type=F32).astype(BF16)
        proj = jnp.dot(hn, w_ref[...], preferred_element_type=F32)
        gate = proj[:, w:]

        xc = _conv_segments(proj[:, :w], halo_ref, cw_ref, cb_ref)
        xcb = xc.astype(BF16)
        pre_r, pre_i = [], []
        for s in range(w // GATE_SLAB):
            slab = xcb[:, s * GATE_SLAB:(s + 1) * GATE_SLAB]
            pre_r.append(jnp.dot(slab, wr_ref[s], preferred_element_type=F32))
            pre_i.append(jnp.dot(slab, wi_ref[s], preferred_element_type=F32))
        r = _sigmoid(jnp.concatenate(pre_r, axis=1) + br_ref[...])
        i = _sigmoid(jnp.concatenate(pre_i, axis=1) + bi_ref[...])
        log2_a = (-LRU_C * LOG2_E * r) * _softplus(-lam_ref[...])
        a = jnp.exp2(log2_a)
        z = 1.0 - a * a
        u = jnp.where(z > 0.0, z * lax.rsqrt(z), 0.0) * (i * xc)

        h_j = jnp.zeros((SUBLANES, w), F32)
        p_j = jnp.ones((SUBLANES, w), F32)
        hs, ps = [], []
        for j in range(steps):
            a_j = a[j * SUBLANES:(j + 1) * SUBLANES, :]
            h_j = a_j * h_j + u[j * SUBLANES:(j + 1) * SUBLANES, :]
            p_j = a_j * p_j
            hs.append(h_j)
            ps.append(p_j)
        c = carry_ref[0:1, :]
        entering = []
        for seg in range(SUBLANES):
            entering.append(c)
            c = p_j[seg:seg + 1, :] * c + h_j[seg:seg + 1, :]
        carry_ref[0:1, :] = c
        enter = jnp.concatenate(entering, axis=0)
        h = jnp.concatenate([hs[j] + ps[j] * enter for j in range(steps)], axis=0)
        out = (h * _gelu_tanh(gate)).astype(BF16)
        o_ref[rows, :] = jnp.dot(unperm_ref[...], out, preferred_element_type=F32).astype(BF16)


def _block_diag_slabs(w):
    nb, bs, _ = w.shape
    per = GATE_SLAB // bs
    w = w.reshape(nb // per, per, bs, bs)
    eye = jnp.eye(per, dtype=w.dtype)
    bd = jnp.einsum("spij,pq->spiqj", w, eye).reshape(nb // per, GATE_SLAB, GATE_SLAB)
    return bd.astype(BF16)


def _ssd_body(n_heads, hn_ref, w_ref, cw_ref, cb_ref, dtb_ref, alog_ref, dvec_ref, nrm_ref,
              exp_ref, perm_ref, unperm_ref, o_ref, proj_ref, halo_ref, state_ref):
    first = pl.program_id(1) == 0
    t = SSD_CHUNK
    width = n_heads * SSD_HEAD_DIM
    gw = SSD_STATE
    conv_ch = width + 2 * SSD_GROUPS * gw
    n_chunks = hn_ref.shape[0] // t

    @pl.when(first)
    def _():
        state_ref[...] = jnp.zeros_like(state_ref)
        halo_ref[...] = jnp.zeros_like(halo_ref)

    hn = jnp.concatenate([jnp.dot(perm_ref[...], hn_ref[c * t:(c + 1) * t, :], preferred_element_type=F32).astype(BF16)
                          for c in range(n_chunks)], axis=0)
    proj_ref[...] = jnp.dot(hn, w_ref[...], preferred_element_type=F32)

    steps = t // SUBLANES
    time_of = lambda i: (i & (SUBLANES - 1)) * steps + lax.shift_right_logical(i, SUBLANE_SHIFT)
    causal = (time_of(lax.broadcasted_iota(jnp.int32, (t, t), 0))
              >= time_of(lax.broadcasted_iota(jnp.int32, (t, t), 1)))
    tri = jnp.where(causal, 1.0, 0.0).astype(BF16)
    lane = lax.broadcasted_iota(jnp.int32, (t, LANES), 1)
    lo = lane < SSD_HEAD_DIM
    rlo = lax.broadcasted_iota(jnp.int32, (LANES, gw), 0) < SSD_HEAD_DIM

    dt = _softplus(proj_ref[:, width + conv_ch:] + dtb_ref[...])
    adt = dt * (-LOG2_E * jnp.exp(alog_ref[...]))
    cs = jnp.concatenate([_split_dot(adt[c * t:(c + 1) * t, :], tri, parts=3, left=True)
                          for c in range(n_chunks)], axis=0)
    dt_w = _split_dot(dt, exp_ref[...], parts=2)
    cs_w = _split_dot(cs, exp_ref[...], parts=2)

    for c in range(n_chunks):
        rows = slice(c * t, (c + 1) * t)
        out = _ssd_chunk(proj_ref[rows, 0:width], proj_ref[rows, width:width + conv_ch],
                         cs[rows, :], dt_w[rows, :], cs_w[rows, :], n_heads, causal, lo, rlo, halo_ref, state_ref,
                         cw_ref, cb_ref, dvec_ref, nrm_ref)
        o_ref[rows, :] = jnp.dot(unperm_ref[...], out, preferred_element_type=F32).astype(BF16)


def _ssd_chunk(z, xbc_raw, cs, dt_w, cs_w, n_heads, causal, lo, rlo, halo_ref, state_ref,
               cw_ref, cb_ref, dvec_ref, nrm_ref):
    t = SSD_CHUNK
    width = n_heads * SSD_HEAD_DIM
    gw = SSD_STATE
    heads_per_group = n_heads // SSD_GROUPS
    xbc = _silu(_conv_segments(xbc_raw, halo_ref, cw_ref, cb_ref))
    xs = xbc[:, :width]
    cs_t = cs.T
    cs_last = cs[t - 1:t, :]
    cdec = jnp.exp2(cs_last)
    ecs_w = jnp.exp2(cs_w)
    dte_w = jnp.exp2(cs_w[t - 1:t, :] - cs_w)

    cbs = []
    for g in range(SSD_GROUPS):
        bm = xbc[:, width + g * gw: width + (g + 1) * gw].astype(BF16)
        cm = xbc[:, width + (SSD_GROUPS + g) * gw: width + (SSD_GROUPS + g + 1) * gw].astype(BF16)
        cb = lax.dot_general(cm, bm, (((1,), (1,)), ((), ())), preferred_element_type=F32)
        cbs.append((bm, cm, cb))

    ys = []
    for p in range(n_heads // 2):
        h0, h1 = 2 * p, 2 * p + 1
        bm, cm, cb = cbs[h0 // heads_per_group]
        sl = slice(p * LANES, (p + 1) * LANES)
        xs_p = xs[:, sl]
        xdt = xs_p * dt_w[:, sl]
        ms = []
        for h in (h0, h1):
            seg = cs[:, h:h + 1] - cs_t[h:h + 1, :]
            ms.append((cb * jnp.where(causal, jnp.exp2(seg), 0.0)).astype(BF16))
        m_cat = jnp.concatenate(ms, axis=1)
        xdt_bd = jnp.concatenate([jnp.where(lo, xdt, 0.0), jnp.where(lo, 0.0, xdt)], axis=0).astype(BF16)
        y = jnp.dot(m_cat, xdt_bd, preferred_element_type=F32)
        prev = state_ref[sl, :]
        y_off = lax.dot_general(cm, prev.astype(BF16), (((1,), (1,)), ((), ())), preferred_element_type=F32)
        y = y + y_off * ecs_w[:, sl]
        xw = (xdt * dte_w[:, sl]).astype(BF16)
        st = lax.dot_general(xw, bm, (((0,), (0,)), ((), ())), preferred_element_type=F32)
        dec = jnp.where(rlo, cdec[:, h0:h0 + 1], cdec[:, h1:h1 + 1])
        state_ref[sl, :] = prev * dec + st
        ys.append(y + dvec_ref[:, sl] * xs_p)

    y = jnp.concatenate(ys, axis=1) * _silu(z)
    gsz = width // SSD_GROUPS
    outs = []
    for g in range(SSD_GROUPS):
        yg = y[:, g * gsz:(g + 1) * gsz]
        outs.append(yg * lax.rsqrt(jnp.mean(yg * yg, axis=-1, keepdims=True) + NORM_EPS))
    return (jnp.concatenate(outs, axis=1) * nrm_ref[...]).astype(BF16)


N_LRU_INPUTS = 12
N_SSD_INPUTS = 10


def _mixers_body(n_heads, *refs):
    lru_in = refs[:N_LRU_INPUTS]
    ssd_in = refs[N_LRU_INPUTS:N_LRU_INPUTS + N_SSD_INPUTS]
    o_lru, o_ssd, hn_scr, lru_halo, lru_carry, proj_scr, ssd_halo, ssd_state = refs[N_LRU_INPUTS + N_SSD_INPUTS:]
    _lru_body(*lru_in, o_lru, hn_scr, lru_halo, lru_carry)
    _ssd_body(n_heads, hn_scr, *ssd_in, o_ssd, proj_scr, ssd_halo, ssd_state)


def _mixers(x3, g, w_lru, lru_consts, w_ssd, ssd_consts, n_heads):
    b, l, d = x3.shape
    lru_width = lru_consts[-1].shape[1]
    width = n_heads * SSD_HEAD_DIM
    conv_ch = width + 2 * SSD_GROUPS * SSD_STATE
    expand = jnp.asarray(np.arange(LANES)[:, None] == np.arange(width)[None, :] // SSD_HEAD_DIM,
                         dtype=BF16)
    lru_consts = (*lru_consts, *_segment_perm(LRU_SUB))
    ssd_consts = (*ssd_consts, expand, *_segment_perm(SSD_CHUNK))
    assert 3 + len(lru_consts) == N_LRU_INPUTS and 1 + len(ssd_consts) == N_SSD_INPUTS
    rows = lambda cols: pl.BlockSpec((None, LRU_TL, cols), lambda bi_, li: (bi_, li, 0))
    return pl.pallas_call(
        functools.partial(_mixers_body, n_heads),
        grid=(b, l // LRU_TL),
        in_specs=[rows(d), _resident(g.shape), w_lru[1]] + [_resident(a.shape) for a in lru_consts]
        + [w_ssd[1]] + [_resident(a.shape) for a in ssd_consts],
        out_specs=[rows(lru_width), rows(width)],
        out_shape=[jax.ShapeDtypeStruct((b, l, lru_width), BF16), jax.ShapeDtypeStruct((b, l, width), BF16)],
        scratch_shapes=[
            pltpu.VMEM((LRU_TL, d), BF16),
            pltpu.VMEM((CONV_WIDTH - 1, SUBLANES, lru_width), F32),
            pltpu.VMEM((SUBLANES, lru_width), F32),
            pltpu.VMEM((LRU_TL, w_ssd[0].shape[-1]), F32),
            pltpu.VMEM((CONV_WIDTH - 1, SUBLANES, conv_ch), F32),
            pltpu.VMEM((width, SSD_STATE), F32),
        ],
        compiler_params=_params("parallel", "arbitrary"),
        name="lru_ssd",
    )(x3, g, w_lru[0], *lru_consts, w_ssd[0], *ssd_consts)


QKV_TM = 512
Q_SCALE = ATT_HEAD_DIM ** -0.5 * LOG2_E
PERM_ROWS = 256


def _split_dot(x, m, parts=2, left=False):
    acc, rem = None, x
    for k in range(parts):
        piece = rem.astype(BF16)
        term = jnp.dot(m, piece, preferred_element_type=F32) if left else jnp.dot(piece, m, preferred_element_type=F32)
        acc = term if acc is None else acc + term
        if k + 1 < parts:
            rem = rem - piece.astype(F32)
    return acc


def _slab_map(fn, x):
    return jnp.concatenate([fn(x[:, c * GATE_SLAB:(c + 1) * GATE_SLAB]) for c in range(x.shape[1] // GATE_SLAB)],
                           axis=1)


def _head_norm_rope(y, gain, cos, sin, gsum, rot):
    ss = _slab_map(lambda v: _split_dot(v, gsum, parts=1), y * y)
    yn = y * lax.rsqrt(ss * (1.0 / ATT_HEAD_DIM) + NORM_EPS) * gain
    partner = _slab_map(lambda v: _split_dot(v, rot), yn)
    tile = lambda tbl: jnp.concatenate([tbl] * (y.shape[1] // LANES), axis=1)
    return yn * tile(cos) + partner * tile(sin)


def _qkv_body(x_ref, g_ref, w_ref, qg_ref, kg_ref, cos_ref, sin_ref, gsum_ref, rot_ref, p1_ref, p4_ref, p16_ref,
              *refs):
    outs, h_ref = refs[:9], refs[9]
    d = x_ref.shape[1]
    h_ref[...] = _rmsnorm(x_ref[...], g_ref[...]).astype(BF16)

    sub = PERM_ROWS

    def section(j, o1, o4, o16, gain_ref=None, grouped=False):
        y = jnp.dot(h_ref[...], w_ref[:, j * d:(j + 1) * d], preferred_element_type=F32)
        if gain_ref is not None:
            y = _head_norm_rope(y, gain_ref[...], cos_ref[...], sin_ref[...], gsum_ref[...], rot_ref[...])
        y_all = y.astype(BF16)
        for t in range(x_ref.shape[0] // sub):
            rows = slice(t * sub, (t + 1) * sub)
            yb = y_all[rows, :]
            if grouped:
                o1[rows, :] = jnp.dot(p1_ref[...], yb, preferred_element_type=F32).astype(BF16)
            else:
                o1[rows, :] = yb
            for perm_ref, o in ((p4_ref, o4), (p16_ref, o16)):
                dil = o.shape[0]
                n = sub // dil
                yp = jnp.dot(perm_ref[...], yb, preferred_element_type=F32).astype(BF16)
                for r in range(dil):
                    o[r, t * n:(t + 1) * n, :] = yp[r * n:(r + 1) * n, :]

    section(0, *outs[0:3], gain_ref=qg_ref, grouped=True)
    section(1, *outs[3:6], gain_ref=kg_ref)
    section(2, *outs[6:9])


def _qkv(x3, g, w_qkv, layer, qg, kg, cos, sin):
    b, l, d = x3.shape
    tm = QKV_TM
    out_specs, out_shape = [], []
    for _ in range(3):
        out_specs.append(pl.BlockSpec((None, tm, d), lambda bi_, i: (bi_, i, 0)))
        out_shape.append(jax.ShapeDtypeStruct((b, l, d), BF16))
        for dil in DILATIONS[1:]:
            out_specs.append(pl.BlockSpec((None, dil, tm // dil, d), lambda bi_, i: (bi_, 0, i, 0)))
            out_shape.append(jax.ShapeDtypeStruct((b, dil, l // dil, d), BF16))
    consts = _qkv_constants(tm)
    tbl = lambda: pl.BlockSpec((tm, LANES), lambda bi_, i: (i, 0))
    w_arr, w_spec = _layer_block(w_qkv, layer)
    return pl.pallas_call(
        _qkv_body,
        grid=(b, l // tm),
        in_specs=[
            pl.BlockSpec((None, tm, d), lambda bi_, i: (bi_, i, 0)),
            _resident((1, d)),
            w_spec,
            _resident((1, d)), _resident((1, d)), tbl(), tbl(),
        ] + [_resident(c.shape) for c in consts],
        out_specs=out_specs,
        out_shape=out_shape,
        scratch_shapes=[pltpu.VMEM((tm, d), BF16)],
        compiler_params=_params("parallel", "parallel"),
        name="qkv",
    )(x3, g, w_arr, qg, kg, cos, sin, *consts)


def _qkv_constants(tm):
    lane = np.arange(GATE_SLAB)
    gsum = (lane[:, None] // ATT_HEAD_DIM == lane[None, :] // ATT_HEAD_DIM)
    half = ROPE_DIM // 2
    pos = lane % ATT_HEAD_DIM
    partner = np.where(pos < half, lane + half, np.where(pos < ROPE_DIM, lane - half, -1))
    rot = lane[:, None] == partner[None, :]
    out_row = np.arange(PERM_ROWS)
    group = SPAN // ACC_DIL
    within = out_row % SPAN
    src = out_row - within + (within % group) * ACC_DIL + within // group
    perms = [src[:, None] == np.arange(PERM_ROWS)[None, :]]
    for dil in DILATIONS[1:]:
        sub = PERM_ROWS
        rows = sub // dil
        assert rows % BF16_ROWS == 0 and tm % sub == 0
        out_row = np.arange(sub)
        src = (out_row % rows) * dil + out_row // rows
        perms.append(src[:, None] == np.arange(sub)[None, :])
    return tuple(jnp.asarray(m, dtype=BF16) for m in (gsum, rot, *perms))


def _rope_tables(l):
    half = ROPE_DIM // 2
    pos = jnp.arange(l, dtype=F32)
    inv = ROPE_THETA ** (-2.0 * jnp.arange(half, dtype=F32) / ROPE_DIM)
    ang = pos[:, None] * inv[None, :]
    cos, sin = jnp.cos(ang), jnp.sin(ang)
    pad = ATT_HEAD_DIM - ROPE_DIM
    cos_h = jnp.concatenate([cos, cos, jnp.ones((l, pad), F32)], axis=1)
    sin_h = jnp.concatenate([-sin, sin, jnp.zeros((l, pad), F32)], axis=1)
    rep = LANES // ATT_HEAD_DIM
    return tuple(jnp.concatenate([tb] * rep, axis=1) for tb in (cos_h, sin_h))


def _attn_body(q1, k1c, k1p, v1c, v1p, q4, k4c, k4p, v4c, v4p, q16, k16c, k16p, v16c, v16p,
               o_ref, s_scr, p_scr, mx_scr, bias_scr, tok_scr, acc_ref, m_ref, l_ref):
    first = pl.program_id(2) == 0
    n_pairs = acc_ref.shape[0]
    n_blocks = SUPER // SPAN
    trans_b = (((1,), (1,)), ((), ()))

    row = lax.broadcasted_iota(jnp.int32, (SPAN, 2 * SPAN), 0)
    kj = lax.broadcasted_iota(jnp.int32, (SPAN, 2 * SPAN), 1)
    group = SPAN // ACC_DIL
    for base, qi in ((0, row), (2, (row % group) * ACC_DIL + row // group)):
        band = (kj >= qi) & (kj <= qi + SPAN)
        bias_scr[base] = jnp.where(band, 0.0, -jnp.inf)
        bias_scr[base + 1] = jnp.where(band & (kj >= SPAN), 0.0, -jnp.inf)

    lo = lax.broadcasted_iota(jnp.int32, (SPAN, LANES), 1) < ATT_HEAD_DIM
    lo2 = lax.broadcasted_iota(jnp.int32, (2 * SPAN, LANES), 1) < ATT_HEAD_DIM
    ones_lo = jnp.where(lo2, 1.0, 0.0).astype(BF16)
    ones_hi = jnp.where(lo2, 0.0, 1.0).astype(BF16)
    lanes = lambda p: slice(p * LANES, (p + 1) * LANES)

    def run_pattern(dil, q_get, cur_prev_k, cur_prev_v):
        per_res = SUPER // (SPAN * dil)
        overwrite = dil == DILATIONS[0]

        def split(f):
            if per_res == 1:
                return f, 0
            if dil == 1:
                return 0, f
            if isinstance(f, int):
                return f % dil, f // dil
            return f & (dil - 1), lax.shift_right_logical(f, dil.bit_length() - 1)

        def slab(refs, r, nb):
            cur, prev = refs
            if dil > 1:
                cur, prev = cur.at[r], prev.at[r]
            if isinstance(nb, int) and nb == 0:
                return jnp.concatenate([prev[...], cur[0:SPAN, :]], axis=0)
            start = (nb - 1) * SPAN
            return cur[pl.ds(start if isinstance(nb, int) else pl.multiple_of(start, SPAN), 2 * SPAN), :]

        def stage_scores(f, slot):
            r, nb = split(f)
            if isinstance(nb, int):
                flag = jnp.where(first, 1, 0) if nb == 0 else 0
            else:
                flag = jnp.where(first & (nb == 0), 1, 0)
            bias = bias_scr[flag + (2 if overwrite else 0)]
            bias2 = jnp.concatenate([bias, bias], axis=0)
            q = q_get(r, nb)
            ks = slab(cur_prev_k, r, nb)
            for p in range(n_pairs):
                qp = q[:, lanes(p)]
                zero = jnp.zeros_like(qp)
                q2 = jnp.concatenate([jnp.where(lo, qp, zero), jnp.where(lo, zero, qp)], axis=0)
                s_scr[slot, p] = lax.dot_general(q2, ks[:, lanes(p)], trans_b, preferred_element_type=F32) + bias2

        def stage_softmax(src, dst):
            for p in range(n_pairs):
                s = s_scr[src, p]
                mx = jnp.max(s, axis=1, keepdims=True)
                pr = jnp.exp2(s - mx).astype(BF16)
                p_scr[dst, p, :, 0:2 * SPAN] = pr[:SPAN]
                p_scr[dst, p, :, 2 * SPAN:] = pr[SPAN:]
                mx_scr[dst, p] = jnp.where(lo, mx[:SPAN], mx[SPAN:])

        def stage_values(f, slot):
            r, nb = split(f)
            vs = slab(cur_prev_v, r, nb)
            if overwrite:
                grp = SPAN // ACC_DIL
                start = nb * grp if isinstance(nb, int) else pl.multiple_of(nb * grp, grp)
                pieces = [(slice(c * grp, (c + 1) * grp), c, pl.ds(start, grp)) for c in range(ACC_DIL)]
            elif dil == ACC_DIL:
                pieces = [(slice(0, SPAN), r, q_rows(nb))]
            else:
                sub = dil // ACC_DIL
                pieces = [(slice(0, SPAN), r & (ACC_DIL - 1),
                           pl.ds(lax.shift_right_logical(r, ACC_SHIFT) if not isinstance(r, int) else r // ACC_DIL,
                                 SPAN, stride=sub))]
            for p in range(n_pairs):
                vp = vs[:, lanes(p)]
                zero = jnp.zeros_like(vp)
                v_bd = jnp.concatenate([jnp.concatenate([jnp.where(lo2, vp, zero), ones_lo], axis=1),
                                        jnp.concatenate([jnp.where(lo2, zero, vp), ones_hi], axis=1)], axis=0)
                pv = jnp.dot(p_scr[slot, p], v_bd, preferred_element_type=F32)
                o_blk, l_blk = pv[:, :LANES], pv[:, LANES:]
                m_blk = mx_scr[slot, p]
                for src, cls, rows in pieces:
                    m_new, l_new, o_new = m_blk[src], l_blk[src], o_blk[src]
                    if overwrite:
                        m_ref[p, cls, rows, :], l_ref[p, cls, rows, :], acc_ref[p, cls, rows, :] = m_new, l_new, o_new
                        continue
                    m_old = m_ref[p, cls, rows, :]
                    m = jnp.maximum(m_old, m_new)
                    e_old = jnp.exp2(m_old - m)
                    e_new = jnp.exp2(m_new - m)
                    l_ref[p, cls, rows, :] = e_old * l_ref[p, cls, rows, :] + e_new * l_new
                    acc_ref[p, cls, rows, :] = e_old * acc_ref[p, cls, rows, :] + e_new * o_new
                    m_ref[p, cls, rows, :] = m

        n_rounds = n_blocks // 2

        def round_(k, odd, scores=True, softmax=True, values=True):
            wr = 2 if odd else 0
            rd = 2 - wr
            if values:
                stage_values(2 * k - 4, rd)
                stage_values(2 * k - 3, rd + 1)
            if softmax:
                stage_softmax(rd, wr)
                stage_softmax(rd + 1, wr + 1)
            if scores:
                stage_scores(2 * k, wr)
                stage_scores(2 * k + 1, wr + 1)

        round_(0, False, softmax=False, values=False)
        round_(1, True, values=False)
        round_(2, False)
        round_(3, True)

        def body(j, c):
            round_(2 * j, False)
            round_(2 * j + 1, True)
            return c

        for j in range(2, n_rounds // 2):
            body(j, 0)
        round_(n_rounds, False, scores=False)
        round_(n_rounds + 1, True, scores=False, softmax=False)

    def q_rows(nb):
        return pl.ds(nb * SPAN if isinstance(nb, int) else pl.multiple_of(nb * SPAN, SPAN), SPAN)

    run_pattern(DILATIONS[0], lambda r, nb: q1[q_rows(nb), :], (k1c, k1p), (v1c, v1p))
    run_pattern(DILATIONS[1], lambda r, nb: q4[r, q_rows(nb), :], (k4c, k4p), (v4c, v4p))
    run_pattern(DILATIONS[2], lambda r, nb: q16[r], (k16c, k16p), (v16c, v16p))

    for p in range(n_pairs):
        for cls in range(ACC_DIL):
            tok_scr[p, pl.ds(cls, SUPER // ACC_DIL, stride=ACC_DIL), :] = acc_ref[p, cls] / l_ref[p, cls]
        o_ref[:, lanes(p)] = tok_scr[p].astype(BF16)


def _attention(qkv_outs):
    q1, q4, q16, k1, k4, k16, v1, v4, v16 = qkv_outs
    b, l, d = q1.shape
    w = ATT_LANES
    n_pairs = w // LANES
    d4, d16 = DILATIONS[1], DILATIONS[2]
    cur1 = pl.BlockSpec((None, SUPER, w), lambda bi_, g, s: (bi_, s, g))
    prev1 = pl.BlockSpec((None, SPAN, w), lambda bi_, g, s: (bi_, jnp.maximum(s * (SUPER // SPAN) - 1, 0), g))
    cur4 = pl.BlockSpec((None, d4, SUPER // d4, w), lambda bi_, g, s: (bi_, 0, s, g))
    prev4 = pl.BlockSpec((None, d4, SPAN, w),
                         lambda bi_, g, s: (bi_, 0, jnp.maximum(s * (SUPER // (d4 * SPAN)) - 1, 0), g))
    cur16 = pl.BlockSpec((None, d16, SPAN, w), lambda bi_, g, s: (bi_, 0, s, g))
    prev16 = pl.BlockSpec((None, d16, SPAN, w), lambda bi_, g, s: (bi_, 0, jnp.maximum(s - 1, 0), g))
    return pl.pallas_call(
        _attn_body,
        grid=(b, d // w, l // SUPER),
        in_specs=[cur1, cur1, prev1, cur1, prev1,
                  cur4, cur4, prev4, cur4, prev4,
                  cur16, cur16, prev16, cur16, prev16],
        out_specs=pl.BlockSpec((None, SUPER, w), lambda bi_, g, s: (bi_, s, g)),
        out_shape=jax.ShapeDtypeStruct((b, l, d), BF16),
        scratch_shapes=[
            pltpu.VMEM((4, n_pairs, 2 * SPAN, 2 * SPAN), F32),
            pltpu.VMEM((4, n_pairs, SPAN, 4 * SPAN), BF16),
            pltpu.VMEM((4, n_pairs, SPAN, LANES), F32),
            pltpu.VMEM((4, SPAN, 2 * SPAN), F32),
            pltpu.VMEM((n_pairs, SUPER, LANES), F32),
        ] + [pltpu.VMEM((n_pairs, ACC_DIL, SUPER // ACC_DIL, LANES), F32)] * 3,
        compiler_params=_params("parallel", "parallel", "arbitrary"),
        name="dilated_attention",
    )(q1, k1, k1, v1, v1, q4, k4, k4, v4, v4, q16, k16, k16, v16, v16)


def _recurrent_layer(x, b, l, i, norm_g, w_lru, w_ssd, w_out, lru_conv_w, lru_conv_b, lru_w_r, lru_b_r, lru_w_i,
                     lru_b_i, lru_lambda, ssd_conv_w, ssd_conv_b, ssd_dt_bias, ssd_a_log, ssd_d, ssd_norm):
    d = x.shape[1]
    width = lru_lambda.shape[0]
    n_heads = ssd_a_log.shape[0]
    row = lambda v: v.reshape(1, -1)
    pad_lanes = lambda v: jnp.pad(v.reshape(1, -1), ((0, 0), (0, LANES - v.shape[0])))
    x3 = x.reshape(b, l, d)
    out_a, out_b = _mixers(
        x3, row(norm_g), _layer_block(w_lru, i, (d, 2 * width)),
        (lru_conv_w, row(lru_conv_b), _block_diag_slabs(lru_w_r), _block_diag_slabs(lru_w_i), row(lru_b_r),
         row(lru_b_i), row(lru_lambda)),
        _layer_block(w_ssd, i),
        (ssd_conv_w, row(ssd_conv_b), pad_lanes(ssd_dt_bias), pad_lanes(ssd_a_log),
         row(jnp.repeat(ssd_d, SSD_HEAD_DIM)), row(ssd_norm)), n_heads)
    return ([out_a.reshape(b * l, width), out_b.reshape(b * l, -1)],
            [_layer_block(w_out, i, (width, d), (0, 0)), _layer_block(w_out, i, (width, d), (1, 0))])


def _attention_layer(x, b, l, i, norm_g, w_qkv, w_out, q_norm, k_norm):
    d = x.shape[1]
    heads = d // ATT_HEAD_DIM
    row = lambda v: v.reshape(1, -1)
    outs = _qkv(x.reshape(b, l, d), row(norm_g), w_qkv, i,
                row(jnp.tile(q_norm, heads) * Q_SCALE), row(jnp.tile(k_norm, heads)), *_rope_tables(l))
    o = _attention(outs)
    return [o.reshape(b * l, d)], [_layer_block(w_out, i)]


def kernel(x, rec_norm, rec_w_in, lru_conv_w, lru_conv_b, lru_w_r, lru_b_r, lru_w_i, lru_b_i, lru_lambda,
           ssd_conv_w, ssd_conv_b, ssd_dt_bias, ssd_a_log, ssd_d, ssd_norm, rec_w_out, att_norm, att_w_qkv,
           att_q_norm, att_k_norm, att_w_out, ffn_norm, ffn_w_gate_up, ffn_w_down):
    b, l, d = x.shape
    depth = ffn_norm.shape[0]
    h = x.reshape(b * l, d)
    lru_cols = 2 * lru_lambda.shape[1]
    w_lru = rec_w_in.astype(BF16)
    dt_pad = LANES - ssd_a_log.shape[1]
    w_ssd = jnp.pad(rec_w_in[:, :, lru_cols:], ((0, 0), (0, 0), (0, dt_pad))).astype(BF16)
    rec_wo, att_wqkv, att_wo = rec_w_out.astype(BF16), att_w_qkv.astype(BF16), att_w_out.astype(BF16)
    ffn_wgu, ffn_wd = ffn_w_gate_up.astype(BF16), ffn_w_down.astype(BF16)
    for layer in range(depth):
        i = layer // 2
        if layer % 2 == 0:
            mixes, wos = _recurrent_layer(h, b, l, i, rec_norm[i], w_lru, w_ssd, rec_wo, lru_conv_w[i],
                                          lru_conv_b[i], lru_w_r[i], lru_b_r[i], lru_w_i[i], lru_b_i[i],
                                          lru_lambda[i], ssd_conv_w[i], ssd_conv_b[i], ssd_dt_bias[i],
                                          ssd_a_log[i], ssd_d[i], ssd_norm[i])
        else:
            mixes, wos = _attention_layer(h, b, l, i, att_norm[i], att_wqkv, att_wo, att_q_norm[i], att_k_norm[i])
        h = _mix_ffn(h, mixes, wos, ffn_norm[layer].reshape(1, d), _layer_block(ffn_wgu, layer),
                     _layer_block(ffn_wd, layer))
    return h.reshape(b, l, d)
```

```python
import functools
import math

import jax
import jax.numpy as jnp
import numpy as np
from jax import lax
from jax.experimental import pallas as pl
from jax.experimental.pallas import tpu as pltpu

F32 = jnp.float32
BF16 = jnp.bfloat16

NORM_EPS = 1e-6
LOG2_E = math.log2(math.e)
CONV_WIDTH = 4
LRU_C = 8.0
GATE_SLAB = 256
SSD_HEAD_DIM = 64
SSD_GROUPS = 2
SSD_STATE = 128
SSD_CHUNK = 128
ATT_HEAD_DIM = 64
ROPE_DIM = 16
ROPE_THETA = 500000.0
SPAN = 128
DILATIONS = (1, 4, 16)
SUPER = SPAN * DILATIONS[-1]
ATT_LANES = 256
ACC_DIL = DILATIONS[1]
ACC_SHIFT = ACC_DIL.bit_length() - 1
LANES = 128
SUBLANES = 8
SUBLANE_SHIFT = SUBLANES.bit_length() - 1
BF16_ROWS = 16
VMEM_LIMIT_BYTES = 56 * 1024 * 1024


def _params(*semantics):
    return pltpu.CompilerParams(dimension_semantics=semantics, vmem_limit_bytes=VMEM_LIMIT_BYTES)


def _rmsnorm(x, g):
    ms = jnp.mean(x * x, axis=-1, keepdims=True)
    return x * lax.rsqrt(ms + NORM_EPS) * g


def _sigmoid(x):
    return 1.0 / (1.0 + jnp.exp2(x * (-LOG2_E)))


def _silu(x):
    return x * _sigmoid(x)


def _softplus(x):
    return jnp.maximum(x, 0.0) + jnp.log1p(jnp.exp(-jnp.abs(x)))


def _gelu_tanh(x):
    c = math.sqrt(2.0 / math.pi)
    half_x = 0.5 * x
    return half_x + half_x * jnp.tanh(x * (c + (c * 0.044715) * (x * x)))


FFN_TM = 1024
FFN_TH = 256


def _ffn_body(n_mix, x_ref, *refs):
    mix_refs, wo_refs = refs[:n_mix], refs[n_mix:2 * n_mix]
    g_ref, wgu_ref, wd_ref, o_ref, h_ref, act_ref = refs[2 * n_mix:]
    x1 = x_ref[...]
    for mix_ref, wo_ref in zip(mix_refs, wo_refs):
        x1 = x1 + jnp.dot(mix_ref[...], wo_ref[...], preferred_element_type=F32)
    o_ref[...] = x1
    h_ref[...] = _rmsnorm(x1, g_ref[...]).astype(BF16)
    hid = act_ref.shape[1]
    for c in range(hid // FFN_TH):
        cols = slice(c * FFN_TH, (c + 1) * FFN_TH)
        up_cols = slice(hid + c * FFN_TH, hid + (c + 1) * FFN_TH)
        gate = jnp.dot(h_ref[...], wgu_ref[:, cols], preferred_element_type=F32)
        up = jnp.dot(h_ref[...], wgu_ref[:, up_cols], preferred_element_type=F32)
        act_ref[:, cols] = (_silu(gate) * up).astype(BF16)
    o_ref[...] += jnp.dot(act_ref[...], wd_ref[...], preferred_element_type=F32)


def _resident(shape):
    return pl.BlockSpec(shape, lambda *_: (0,) * len(shape), pipeline_mode=pl.Buffered(1))


def _layer_block(stack, layer, block=None, index=None):
    block = tuple(stack.shape[1:]) if block is None else tuple(block)
    index = (layer,) + (tuple(index) if index is not None else (0,) * len(block))
    return stack, pl.BlockSpec((None,) + block, lambda *_: index, pipeline_mode=pl.Buffered(1))


def _mix_ffn(x, mixes, wos, g, wgu, wd):
    t, d = x.shape
    hid = wd[0].shape[1]
    weights = (*wos, (g, _resident(g.shape)), wgu, wd)
    rows = lambda a: pl.BlockSpec((FFN_TM, a.shape[1]), lambda i: (i, 0))
    return pl.pallas_call(
        functools.partial(_ffn_body, len(mixes)),
        grid=(t // FFN_TM,),
        in_specs=[rows(x)] + [rows(a) for a in mixes] + [spec for _, spec in weights],
        out_specs=rows(x),
        out_shape=jax.ShapeDtypeStruct((t, d), F32),
        scratch_shapes=[pltpu.VMEM((FFN_TM, d), BF16), pltpu.VMEM((FFN_TM, hid), BF16)],
        compiler_params=_params("parallel"),
        name="mix_ffn",
    )(x, *mixes, *[a for a, _ in weights])


def _segment_perm(rows):
    steps = rows // SUBLANES
    rho = np.arange(rows)
    time = (rho % SUBLANES) * steps + rho // SUBLANES
    perm = time[:, None] == np.arange(rows)[None, :]
    return jnp.asarray(perm, dtype=BF16), jnp.asarray(perm.T, dtype=BF16)


def _conv_segments(x, halo_ref, w_ref, b_ref):
    rows, width = x.shape
    sub0 = lax.broadcasted_iota(jnp.int32, (SUBLANES, width), 0) == 0
    halos = []
    for k in range(1, CONV_WIDTH):
        tail = pltpu.roll(x[rows - k * SUBLANES:rows - (k - 1) * SUBLANES, :], 1, 0)
        halos.append(jnp.where(sub0, halo_ref[k - 1], tail))
        halo_ref[k - 1] = tail
    y = b_ref[...] + w_ref[CONV_WIDTH - 1:CONV_WIDTH, :] * x
    for k in range(1, CONV_WIDTH):
        back = jnp.concatenate(halos[k - 1::-1] + [x[:rows - k * SUBLANES, :]], axis=0)
        y = y + w_ref[CONV_WIDTH - 1 - k:CONV_WIDTH - k, :] * back
    return y


LRU_TL = 1024
LRU_SUB = 256


def _lru_body(x_ref, g_ref, w_ref, cw_ref, cb_ref, wr_ref, wi_ref, br_ref, bi_ref, lam_ref, perm_ref, unperm_ref,
              o_ref, hn_ref, halo_ref, carry_ref):
    first = pl.program_id(1) == 0
    w = o_ref.shape[1]
    sub = perm_ref.shape[0]
    steps = sub // SUBLANES

    @pl.when(first)
    def _():
        carry_ref[...] = jnp.zeros_like(carry_ref)
        halo_ref[...] = jnp.zeros_like(halo_ref)

    for t in range(o_ref.shape[0] // sub):
        rows = slice(t * sub, (t + 1) * sub)
        hn = _rmsnorm(x_ref[rows, :], g_ref[...]).astype(BF16)
        hn_ref[rows, :] = hn
        hn = jnp.dot(perm_ref[...], hn, preferred_element_type=F32).astype(BF16)
        proj = jnp.dot(hn, w_ref[...], preferred_element_type=F32)
        gate = proj[:, w:]

        xc = _conv_segments(proj[:, :w], halo_ref, cw_ref, cb_ref)
        xcb = xc.astype(BF16)
        pre_r, pre_i = [], []
        for s in range(w // GATE_SLAB):
            slab = xcb[:, s * GATE_SLAB:(s + 1) * GATE_SLAB]
            pre_r.append(jnp.dot(slab, wr_ref[s], preferred_element_type=F32))
            pre_i.append(jnp.dot(slab, wi_ref[s], preferred_element_type=F32))
        r = _sigmoid(jnp.concatenate(pre_r, axis=1) + br_ref[...])
        i = _sigmoid(jnp.concatenate(pre_i, axis=1) + bi_ref[...])
        log2_a = (-LRU_C * LOG2_E * r) * _softplus(-lam_ref[...])
        a = jnp.exp2(log2_a)
        z = 1.0 - a * a
        u = jnp.where(z > 0.0, z * lax.rsqrt(z), 0.0) * (i * xc)

        h_j = jnp.zeros((SUBLANES, w), F32)
        p_j = jnp.ones((SUBLANES, w), F32)
        hs, ps = [], []
        for j in range(steps):
            a_j = a[j * SUBLANES:(j + 1) * SUBLANES, :]
            h_j = a_j * h_j + u[j * SUBLANES:(j + 1) * SUBLANES, :]
            p_j = a_j * p_j
            hs.append(h_j)
            ps.append(p_j)
        c = carry_ref[0:1, :]
        entering = []
        for seg in range(SUBLANES):
            entering.append(c)
            c = p_j[seg:seg + 1, :] * c + h_j[seg:seg + 1, :]
        carry_ref[0:1, :] = c
        enter = jnp.concatenate(entering, axis=0)
        h = jnp.concatenate([hs[j] + ps[j] * enter for j in range(steps)], axis=0)
        out = (h * _gelu_tanh(gate)).astype(BF16)
        o_ref[rows, :] = jnp.dot(unperm_ref[...], out, preferred_element_type=F32).astype(BF16)


def _lru(x3, g, w_lru, cw, cb, wr_bd, wi_bd, br, bi, lam):
    b, l, d = x3.shape
    width = lam.shape[1]
    perms = _segment_perm(LRU_SUB)
    return pl.pallas_call(
        _lru_body,
        grid=(b, l // LRU_TL),
        in_specs=[pl.BlockSpec((None, LRU_TL, d), lambda bi_, li: (bi_, li, 0))]
        + [_resident(g.shape), w_lru[1]]
        + [_resident(a.shape) for a in (cw, cb, wr_bd, wi_bd, br, bi, lam, *perms)],
        out_specs=[pl.BlockSpec((None, LRU_TL, width), lambda bi_, li: (bi_, li, 0)),
                   pl.BlockSpec((None, LRU_TL, d), lambda bi_, li: (bi_, li, 0))],
        out_shape=[jax.ShapeDtypeStruct((b, l, width), BF16),
                   jax.ShapeDtypeStruct((b, l, d), BF16)],
        scratch_shapes=[
            pltpu.VMEM((CONV_WIDTH - 1, SUBLANES, width), F32),
            pltpu.VMEM((SUBLANES, width), F32),
        ],
        compiler_params=_params("parallel", "arbitrary"),
        name="lru",
    )(x3, g, w_lru[0], cw, cb, wr_bd, wi_bd, br, bi, lam, *perms)


def _block_diag_slabs(w):
    nb, bs, _ = w.shape
    per = GATE_SLAB // bs
    w = w.reshape(nb // per, per, bs, bs)
    eye = jnp.eye(per, dtype=w.dtype)
    bd = jnp.einsum("spij,pq->spiqj", w, eye).reshape(nb // per, GATE_SLAB, GATE_SLAB)
    return bd.astype(BF16)


def _ssd_body(n_heads, hn_ref, w_ref, cw_ref, cb_ref, dtb_ref, alog_ref, dvec_ref, nrm_ref,
              exp_ref, perm_ref, unperm_ref, o_ref, proj_ref, halo_ref, state_ref):
    first = pl.program_id(1) == 0
    t = SSD_CHUNK
    width = n_heads * SSD_HEAD_DIM
    gw = SSD_STATE
    conv_ch = width + 2 * SSD_GROUPS * gw
    n_chunks = hn_ref.shape[0] // t

    @pl.when(first)
    def _():
        state_ref[...] = jnp.zeros_like(state_ref)
        halo_ref[...] = jnp.zeros_like(halo_ref)

    hn = jnp.concatenate([jnp.dot(perm_ref[...], hn_ref[c * t:(c + 1) * t, :], preferred_element_type=F32).astype(BF16)
                          for c in range(n_chunks)], axis=0)
    proj_ref[...] = jnp.dot(hn, w_ref[...], preferred_element_type=F32)

    steps = t // SUBLANES
    time_of = lambda i: (i & (SUBLANES - 1)) * steps + lax.shift_right_logical(i, SUBLANE_SHIFT)
    causal = (time_of(lax.broadcasted_iota(jnp.int32, (t, t), 0))
              >= time_of(lax.broadcasted_iota(jnp.int32, (t, t), 1)))
    tri = jnp.where(causal, 1.0, 0.0).astype(BF16)
    lane = lax.broadcasted_iota(jnp.int32, (t, LANES), 1)
    lo = lane < SSD_HEAD_DIM
    rlo = lax.broadcasted_iota(jnp.int32, (LANES, gw), 0) < SSD_HEAD_DIM

    dt = _softplus(proj_ref[:, width + conv_ch:] + dtb_ref[...])
    adt = dt * (-LOG2_E * jnp.exp(alog_ref[...]))
    cs = jnp.concatenate([_split_dot(adt[c * t:(c + 1) * t, :], tri, parts=3, left=True)
                          for c in range(n_chunks)], axis=0)
    dt_w = _split_dot(dt, exp_ref[...], parts=2)
    cs_w = _split_dot(cs, exp_ref[...], parts=2)

    for c in range(n_chunks):
        rows = slice(c * t, (c + 1) * t)
        out = _ssd_chunk(proj_ref[rows, 0:width], proj_ref[rows, width:width + conv_ch],
                         cs[rows, :], dt_w[rows, :], cs_w[rows, :], n_heads, causal, lo, rlo, halo_ref, state_ref,
                         cw_ref, cb_ref, dvec_ref, nrm_ref)
        o_ref[rows, :] = jnp.dot(unperm_ref[...], out, preferred_element_type=F32).astype(BF16)


def _ssd_chunk(z, xbc_raw, cs, dt_w, cs_w, n_heads, causal, lo, rlo, halo_ref, state_ref,
               cw_ref, cb_ref, dvec_ref, nrm_ref):
    t = SSD_CHUNK
    width = n_heads * SSD_HEAD_DIM
    gw = SSD_STATE
    heads_per_group = n_heads // SSD_GROUPS
    xbc = _silu(_conv_segments(xbc_raw, halo_ref, cw_ref, cb_ref))
    xs = xbc[:, :width]
    cs_t = cs.T
    cs_last = cs[t - 1:t, :]
    cdec = jnp.exp2(cs_last)
    ecs_w = jnp.exp2(cs_w)
    dte_w = jnp.exp2(cs_w[t - 1:t, :] - cs_w)

    cbs = []
    for g in range(SSD_GROUPS):
        bm = xbc[:, width + g * gw: width + (g + 1) * gw].astype(BF16)
        cm = xbc[:, width + (SSD_GROUPS + g) * gw: width + (SSD_GROUPS + g + 1) * gw].astype(BF16)
        cb = lax.dot_general(cm, bm, (((1,), (1,)), ((), ())), preferred_element_type=F32)
        cbs.append((bm, cm, cb))

    ys = []
    for p in range(n_heads // 2):
        h0, h1 = 2 * p, 2 * p + 1
        bm, cm, cb = cbs[h0 // heads_per_group]
        sl = slice(p * LANES, (p + 1) * LANES)
        xs_p = xs[:, sl]
        xdt = xs_p * dt_w[:, sl]
        ms = []
        for h in (h0, h1):
            seg = cs[:, h:h + 1] - cs_t[h:h + 1, :]
            ms.append((cb * jnp.where(causal, jnp.exp2(seg), 0.0)).astype(BF16))
        m_cat = jnp.concatenate(ms, axis=1)
        xdt_bd = jnp.concatenate([jnp.where(lo, xdt, 0.0), jnp.where(lo, 0.0, xdt)], axis=0).astype(BF16)
        y = jnp.dot(m_cat, xdt_bd, preferred_element_type=F32)
        prev = state_ref[sl, :]
        y_off = lax.dot_general(cm, prev.astype(BF16), (((1,), (1,)), ((), ())), preferred_element_type=F32)
        y = y + y_off * ecs_w[:, sl]
        xw = (xdt * dte_w[:, sl]).astype(BF16)
        st = lax.dot_general(xw, bm, (((0,), (0,)), ((), ())), preferred_element_type=F32)
        dec = jnp.where(rlo, cdec[:, h0:h0 + 1], cdec[:, h1:h1 + 1])
        state_ref[sl, :] = prev * dec + st
        ys.append(y + dvec_ref[:, sl] * xs_p)

    y = jnp.concatenate(ys, axis=1) * _silu(z)
    gsz = width // SSD_GROUPS
    outs = []
    for g in range(SSD_GROUPS):
        yg = y[:, g * gsz:(g + 1) * gsz]
        outs.append(yg * lax.rsqrt(jnp.mean(yg * yg, axis=-1, keepdims=True) + NORM_EPS))
    return (jnp.concatenate(outs, axis=1) * nrm_ref[...]).astype(BF16)


SSD_TS = 1024


def _ssd(hn3, w_ssd, cw, cb, dt_bias, a_log, d_vec, nrm, n_heads):
    b, l, d = hn3.shape
    width = n_heads * SSD_HEAD_DIM
    conv_ch = width + 2 * SSD_GROUPS * SSD_STATE
    expand = jnp.asarray(np.arange(LANES)[:, None] == np.arange(width)[None, :] // SSD_HEAD_DIM,
                         dtype=BF16)
    consts = (cw, cb, dt_bias, a_log, d_vec, nrm, expand, *_segment_perm(SSD_CHUNK))
    return pl.pallas_call(
        functools.partial(_ssd_body, n_heads),
        grid=(b, l // SSD_TS),
        in_specs=[pl.BlockSpec((None, SSD_TS, d), lambda bi_, ci: (bi_, ci, 0))]
        + [w_ssd[1]] + [_resident(a.shape) for a in consts],
        out_specs=pl.BlockSpec((None, SSD_TS, width), lambda bi_, ci: (bi_, ci, 0)),
        out_shape=jax.ShapeDtypeStruct((b, l, width), BF16),
        scratch_shapes=[
            pltpu.VMEM((SSD_TS, w_ssd[0].shape[-1]), F32),
            pltpu.VMEM((CONV_WIDTH - 1, SUBLANES, conv_ch), F32),
            pltpu.VMEM((width, SSD_STATE), F32),
        ],
        compiler_params=_params("parallel", "arbitrary"),
        name="ssd",
    )(hn3, w_ssd[0], *consts)


QKV_TM = 512
Q_SCALE = ATT_HEAD_DIM ** -0.5 * LOG2_E
PERM_ROWS = 256


def _split_dot(x, m, parts=2, left=False):
    acc, rem = None, x
    for k in range(parts):
        piece = rem.astype(BF16)
        term = jnp.dot(m, piece, preferred_element_type=F32) if left else jnp.dot(piece, m, preferred_element_type=F32)
        acc = term if acc is None else acc + term
        if k + 1 < parts:
            rem = rem - piece.astype(F32)
    return acc


def _slab_map(fn, x):
    return jnp.concatenate([fn(x[:, c * GATE_SLAB:(c + 1) * GATE_SLAB]) for c in range(x.shape[1] // GATE_SLAB)],
                           axis=1)


def _head_norm_rope(y, gain, cos, sin, gsum, rot):
    ss = _slab_map(lambda v: _split_dot(v, gsum, parts=1), y * y)
    yn = y * lax.rsqrt(ss * (1.0 / ATT_HEAD_DIM) + NORM_EPS) * gain
    partner = _slab_map(lambda v: _split_dot(v, rot), yn)
    tile = lambda tbl: jnp.concatenate([tbl] * (y.shape[1] // LANES), axis=1)
    return yn * tile(cos) + partner * tile(sin)


def _qkv_body(x_ref, g_ref, w_ref, qg_ref, kg_ref, cos_ref, sin_ref, gsum_ref, rot_ref, p1_ref, p4_ref, p16_ref,
              *refs):
    outs, h_ref = refs[:9], refs[9]
    d = x_ref.shape[1]
    h_ref[...] = _rmsnorm(x_ref[...], g_ref[...]).astype(BF16)

    sub = PERM_ROWS

    def section(j, o1, o4, o16, gain_ref=None, grouped=False):
        y = jnp.dot(h_ref[...], w_ref[:, j * d:(j + 1) * d], preferred_element_type=F32)
        if gain_ref is not None:
            y = _head_norm_rope(y, gain_ref[...], cos_ref[...], sin_ref[...], gsum_ref[...], rot_ref[...])
        y_all = y.astype(BF16)
        for t in range(x_ref.shape[0] // sub):
            rows = slice(t * sub, (t + 1) * sub)
            yb = y_all[rows, :]
            if grouped:
                o1[rows, :] = jnp.dot(p1_ref[...], yb, preferred_element_type=F32).astype(BF16)
            else:
                o1[rows, :] = yb
            for perm_ref, o in ((p4_ref, o4), (p16_ref, o16)):
                dil = o.shape[0]
                n = sub // dil
                yp = jnp.dot(perm_ref[...], yb, preferred_element_type=F32).astype(BF16)
                for r in range(dil):
                    o[r, t * n:(t + 1) * n, :] = yp[r * n:(r + 1) * n, :]

    section(0, *outs[0:3], gain_ref=qg_ref, grouped=True)
    section(1, *outs[3:6], gain_ref=kg_ref)
    section(2, *outs[6:9])


def _qkv(x3, g, w_qkv, layer, qg, kg, cos, sin):
    b, l, d = x3.shape
    tm = QKV_TM
    out_specs, out_shape = [], []
    for _ in range(3):
        out_specs.append(pl.BlockSpec((None, tm, d), lambda bi_, i: (bi_, i, 0)))
        out_shape.append(jax.ShapeDtypeStruct((b, l, d), BF16))
        for dil in DILATIONS[1:]:
            out_specs.append(pl.BlockSpec((None, dil, tm // dil, d), lambda bi_, i: (bi_, 0, i, 0)))
            out_shape.append(jax.ShapeDtypeStruct((b, dil, l // dil, d), BF16))
    consts = _qkv_constants(tm)
    tbl = lambda: pl.BlockSpec((tm, LANES), lambda bi_, i: (i, 0))
    w_arr, w_spec = _layer_block(w_qkv, layer)
    return pl.pallas_call(
        _qkv_body,
        grid=(b, l // tm),
        in_specs=[
            pl.BlockSpec((None, tm, d), lambda bi_, i: (bi_, i, 0)),
            _resident((1, d)),
            w_spec,
            _resident((1, d)), _resident((1, d)), tbl(), tbl(),
        ] + [_resident(c.shape) for c in consts],
        out_specs=out_specs,
        out_shape=out_shape,
        scratch_shapes=[pltpu.VMEM((tm, d), BF16)],
        compiler_params=_params("parallel", "parallel"),
        name="qkv",
    )(x3, g, w_arr, qg, kg, cos, sin, *consts)


def _qkv_constants(tm):
    lane = np.arange(GATE_SLAB)
    gsum = (lane[:, None] // ATT_HEAD_DIM == lane[None, :] // ATT_HEAD_DIM)
    half = ROPE_DIM // 2
    pos = lane % ATT_HEAD_DIM
    partner = np.where(pos < half, lane + half, np.where(pos < ROPE_DIM, lane - half, -1))
    rot = lane[:, None] == partner[None, :]
    out_row = np.arange(PERM_ROWS)
    group = SPAN // ACC_DIL
    within = out_row % SPAN
    src = out_row - within + (within % group) * ACC_DIL + within // group
    perms = [src[:, None] == np.arange(PERM_ROWS)[None, :]]
    for dil in DILATIONS[1:]:
        sub = PERM_ROWS
        rows = sub // dil
        assert rows % BF16_ROWS == 0 and tm % sub == 0
        out_row = np.arange(sub)
        src = (out_row % rows) * dil + out_row // rows
        perms.append(src[:, None] == np.arange(sub)[None, :])
    return tuple(jnp.asarray(m, dtype=BF16) for m in (gsum, rot, *perms))


def _rope_tables(l):
    half = ROPE_DIM // 2
    pos = jnp.arange(l, dtype=F32)
    inv = ROPE_THETA ** (-2.0 * jnp.arange(half, dtype=F32) / ROPE_DIM)
    ang = pos[:, None] * inv[None, :]
    cos, sin = jnp.cos(ang), jnp.sin(ang)
    pad = ATT_HEAD_DIM - ROPE_DIM
    cos_h = jnp.concatenate([cos, cos, jnp.ones((l, pad), F32)], axis=1)
    sin_h = jnp.concatenate([-sin, sin, jnp.zeros((l, pad), F32)], axis=1)
    rep = LANES // ATT_HEAD_DIM
    return tuple(jnp.concatenate([tb] * rep, axis=1) for tb in (cos_h, sin_h))


def _attn_body(q1, k1c, k1p, v1c, v1p, q4, k4c, k4p, v4c, v4p, q16, k16c, k16p, v16c, v16p,
               o_ref, s_scr, p_scr, mx_scr, bias_scr, tok_scr, acc_ref, m_ref, l_ref):
    first = pl.program_id(2) == 0
    n_pairs = acc_ref.shape[0]
    n_blocks = SUPER // SPAN
    trans_b = (((1,), (1,)), ((), ()))

    @pl.when((pl.program_id(0) == 0) & (pl.program_id(1) == 0) & first)
    def _():
        row = lax.broadcasted_iota(jnp.int32, (SPAN, 2 * SPAN), 0)
        kj = lax.broadcasted_iota(jnp.int32, (SPAN, 2 * SPAN), 1)
        group = SPAN // ACC_DIL
        for base, qi in ((0, row), (2, (row % group) * ACC_DIL + row // group)):
            band = (kj >= qi) & (kj <= qi + SPAN)
            bias_scr[base] = jnp.where(band, 0.0, -jnp.inf)
            bias_scr[base + 1] = jnp.where(band & (kj >= SPAN), 0.0, -jnp.inf)

    lo = lax.broadcasted_iota(jnp.int32, (SPAN, LANES), 1) < ATT_HEAD_DIM
    lo2 = lax.broadcasted_iota(jnp.int32, (2 * SPAN, LANES), 1) < ATT_HEAD_DIM
    ones_lo = jnp.where(lo2, 1.0, 0.0).astype(BF16)
    ones_hi = jnp.where(lo2, 0.0, 1.0).astype(BF16)
    lanes = lambda p: slice(p * LANES, (p + 1) * LANES)

    def run_pattern(dil, q_get, cur_prev_k, cur_prev_v):
        per_res = SUPER // (SPAN * dil)
        overwrite = dil == DILATIONS[0]

        def split(f):
            if per_res == 1:
                return f, 0
            if dil == 1:
                return 0, f
            if isinstance(f, int):
                return f % dil, f // dil
            return f & (dil - 1), lax.shift_right_logical(f, dil.bit_length() - 1)

        def slab(refs, r, nb):
            cur, prev = refs
            if dil > 1:
                cur, prev = cur.at[r], prev.at[r]
            if isinstance(nb, int) and nb == 0:
                return jnp.concatenate([prev[...], cur[0:SPAN, :]], axis=0)
            start = (nb - 1) * SPAN
            return cur[pl.ds(start if isinstance(nb, int) else pl.multiple_of(start, SPAN), 2 * SPAN), :]

        def stage_scores(f, slot):
            r, nb = split(f)
            if isinstance(nb, int):
                flag = jnp.where(first, 1, 0) if nb == 0 else 0
            else:
                flag = jnp.where(first & (nb == 0), 1, 0)
            bias = bias_scr[flag + (2 if overwrite else 0)]
            bias2 = jnp.concatenate([bias, bias], axis=0)
            q = q_get(r, nb)
            ks = slab(cur_prev_k, r, nb)
            for p in range(n_pairs):
                qp = q[:, lanes(p)]
                zero = jnp.zeros_like(qp)
                q2 = jnp.concatenate([jnp.where(lo, qp, zero), jnp.where(lo, zero, qp)], axis=0)
                s_scr[slot, p] = lax.dot_general(q2, ks[:, lanes(p)], trans_b, preferred_element_type=F32) + bias2

        def stage_softmax(src, dst):
            for p in range(n_pairs):
                s = s_scr[src, p]
                mx = jnp.max(s, axis=1, keepdims=True)
                pr = jnp.exp2(s - mx).astype(BF16)
                p_scr[dst, p, :, 0:2 * SPAN] = pr[:SPAN]
                p_scr[dst, p, :, 2 * SPAN:] = pr[SPAN:]
                mx_scr[dst, p] = jnp.where(lo, mx[:SPAN], mx[SPAN:])

        def stage_values(f, slot):
            r, nb = split(f)
            vs = slab(cur_prev_v, r, nb)
            if overwrite:
                grp = SPAN // ACC_DIL
                start = nb * grp if isinstance(nb, int) else pl.multiple_of(nb * grp, grp)
                pieces = [(slice(c * grp, (c + 1) * grp), c, pl.ds(start, grp)) for c in range(ACC_DIL)]
            elif dil == ACC_DIL:
                pieces = [(slice(0, SPAN), r, q_rows(nb))]
            else:
                sub = dil // ACC_DIL
                pieces = [(slice(0, SPAN), r & (ACC_DIL - 1),
                           pl.ds(lax.shift_right_logical(r, ACC_SHIFT) if not isinstance(r, int) else r // ACC_DIL,
                                 SPAN, stride=sub))]
            for p in range(n_pairs):
                vp = vs[:, lanes(p)]
                zero = jnp.zeros_like(vp)
                v_bd = jnp.concatenate([jnp.concatenate([jnp.where(lo2, vp, zero), ones_lo], axis=1),
                                        jnp.concatenate([jnp.where(lo2, zero, vp), ones_hi], axis=1)], axis=0)
                pv = jnp.dot(p_scr[slot, p], v_bd, preferred_element_type=F32)
                o_blk, l_blk = pv[:, :LANES], pv[:, LANES:]
                m_blk = mx_scr[slot, p]
                for src, cls, rows in pieces:
                    m_new, l_new, o_new = m_blk[src], l_blk[src], o_blk[src]
                    if overwrite:
                        m_ref[p, cls, rows, :], l_ref[p, cls, rows, :], acc_ref[p, cls, rows, :] = m_new, l_new, o_new
                        continue
                    m_old = m_ref[p, cls, rows, :]
                    m = jnp.maximum(m_old, m_new)
                    e_old = jnp.exp2(m_old - m)
                    e_new = jnp.exp2(m_new - m)
                    l_ref[p, cls, rows, :] = e_old * l_ref[p, cls, rows, :] + e_new * l_new
                    acc_ref[p, cls, rows, :] = e_old * acc_ref[p, cls, rows, :] + e_new * o_new
                    m_ref[p, cls, rows, :] = m

        n_rounds = n_blocks // 2

        def round_(k, odd, scores=True, softmax=True, values=True):
            wr = 2 if odd else 0
            rd = 2 - wr
            if values:
                stage_values(2 * k - 4, rd)
                stage_values(2 * k - 3, rd + 1)
            if softmax:
                stage_softmax(rd, wr)
                stage_softmax(rd + 1, wr + 1)
            if scores:
                stage_scores(2 * k, wr)
                stage_scores(2 * k + 1, wr + 1)

        round_(0, False, softmax=False, values=False)
        round_(1, True, values=False)
        round_(2, False)
        round_(3, True)

        def body(j, c):
            round_(2 * j, False)
            round_(2 * j + 1, True)
            return c

        for j in range(2, n_rounds // 2):
            body(j, 0)
        round_(n_rounds, False, scores=False)
        round_(n_rounds + 1, True, scores=False, softmax=False)

    def q_rows(nb):
        return pl.ds(nb * SPAN if isinstance(nb, int) else pl.multiple_of(nb * SPAN, SPAN), SPAN)

    run_pattern(DILATIONS[0], lambda r, nb: q1[q_rows(nb), :], (k1c, k1p), (v1c, v1p))
    run_pattern(DILATIONS[1], lambda r, nb: q4[r, q_rows(nb), :], (k4c, k4p), (v4c, v4p))
    run_pattern(DILATIONS[2], lambda r, nb: q16[r], (k16c, k16p), (v16c, v16p))

    for p in range(n_pairs):
        for cls in range(ACC_DIL):
            tok_scr[p, pl.ds(cls, SUPER // ACC_DIL, stride=ACC_DIL), :] = acc_ref[p, cls] / l_ref[p, cls]
        o_ref[:, lanes(p)] = tok_scr[p].astype(BF16)


def _attention(qkv_outs):
    q1, q4, q16, k1, k4, k16, v1, v4, v16 = qkv_outs
    b, l, d = q1.shape
    w = ATT_LANES
    n_pairs = w // LANES
    d4, d16 = DILATIONS[1], DILATIONS[2]
    cur1 = pl.BlockSpec((None, SUPER, w), lambda bi_, g, s: (bi_, s, g))
    prev1 = pl.BlockSpec((None, SPAN, w), lambda bi_, g, s: (bi_, jnp.maximum(s * (SUPER // SPAN) - 1, 0), g))
    cur4 = pl.BlockSpec((None, d4, SUPER // d4, w), lambda bi_, g, s: (bi_, 0, s, g))
    prev4 = pl.BlockSpec((None, d4, SPAN, w),
                         lambda bi_, g, s: (bi_, 0, jnp.maximum(s * (SUPER // (d4 * SPAN)) - 1, 0), g))
    cur16 = pl.BlockSpec((None, d16, SPAN, w), lambda bi_, g, s: (bi_, 0, s, g))
    prev16 = pl.BlockSpec((None, d16, SPAN, w), lambda bi_, g, s: (bi_, 0, jnp.maximum(s - 1, 0), g))
    return pl.pallas_call(
        _attn_body,
        grid=(b, d // w, l // SUPER),
        in_specs=[cur1, cur1, prev1, cur1, prev1,
                  cur4, cur4, prev4, cur4, prev4,
                  cur16, cur16, prev16, cur16, prev16],
        out_specs=pl.BlockSpec((None, SUPER, w), lambda bi_, g, s: (bi_, s, g)),
        out_shape=jax.ShapeDtypeStruct((b, l, d), BF16),
        scratch_shapes=[
            pltpu.VMEM((4, n_pairs, 2 * SPAN, 2 * SPAN), F32),
            pltpu.VMEM((4, n_pairs, SPAN, 4 * SPAN), BF16),
            pltpu.VMEM((4, n_pairs, SPAN, LANES), F32),
            pltpu.VMEM((4, SPAN, 2 * SPAN), F32),
            pltpu.VMEM((n_pairs, SUPER, LANES), F32),
        ] + [pltpu.VMEM((n_pairs, ACC_DIL, SUPER // ACC_DIL, LANES), F32)] * 3,
        compiler_params=_params("arbitrary", "arbitrary", "arbitrary"),
        name="dilated_attention",
    )(q1, k1, k1, v1, v1, q4, k4, k4, v4, v4, q16, k16, k16, v16, v16)


def _recurrent_layer(x, b, l, i, norm_g, w_lru, w_ssd, w_out, lru_conv_w, lru_conv_b, lru_w_r, lru_b_r, lru_w_i,
                     lru_b_i, lru_lambda, ssd_conv_w, ssd_conv_b, ssd_dt_bias, ssd_a_log, ssd_d, ssd_norm):
    d = x.shape[1]
    width = lru_lambda.shape[0]
    n_heads = ssd_a_log.shape[0]
    row = lambda v: v.reshape(1, -1)
    pad_lanes = lambda v: jnp.pad(v.reshape(1, -1), ((0, 0), (0, LANES - v.shape[0])))
    x3 = x.reshape(b, l, d)
    out_a, hn3 = _lru(x3, row(norm_g), _layer_block(w_lru, i, (d, 2 * width)), lru_conv_w, row(lru_conv_b),
                      _block_diag_slabs(lru_w_r), _block_diag_slabs(lru_w_i), row(lru_b_r), row(lru_b_i),
                      row(lru_lambda))
    out_b = _ssd(hn3, _layer_block(w_ssd, i), ssd_conv_w, row(ssd_conv_b), pad_lanes(ssd_dt_bias),
                 pad_lanes(ssd_a_log), row(jnp.repeat(ssd_d, SSD_HEAD_DIM)), row(ssd_norm), n_heads)
    return ([out_a.reshape(b * l, width), out_b.reshape(b * l, -1)],
            [_layer_block(w_out, i, (width, d), (0, 0)), _layer_block(w_out, i, (width, d), (1, 0))])


def _attention_layer(x, b, l, i, norm_g, w_qkv, w_out, q_norm, k_norm):
    d = x.shape[1]
    heads = d // ATT_HEAD_DIM
    row = lambda v: v.reshape(1, -1)
    outs = _qkv(x.reshape(b, l, d), row(norm_g), w_qkv, i,
                row(jnp.tile(q_norm, heads) * Q_SCALE), row(jnp.tile(k_norm, heads)), *_rope_tables(l))
    o = _attention(outs)
    return [o.reshape(b * l, d)], [_layer_block(w_out, i)]


def kernel(x, rec_norm, rec_w_in, lru_conv_w, lru_conv_b, lru_w_r, lru_b_r, lru_w_i, lru_b_i, lru_lambda,
           ssd_conv_w, ssd_conv_b, ssd_dt_bias, ssd_a_log, ssd_d, ssd_norm, rec_w_out, att_norm, att_w_qkv,
           att_q_norm, att_k_norm, att_w_out, ffn_norm, ffn_w_gate_up, ffn_w_down):
    b, l, d = x.shape
    depth = ffn_norm.shape[0]
    h = x.reshape(b * l, d)
    lru_cols = 2 * lru_lambda.shape[1]
    w_lru = rec_w_in.astype(BF16)
    dt_pad = LANES - ssd_a_log.shape[1]
    w_ssd = jnp.pad(rec_w_in[:, :, lru_cols:], ((0, 0), (0, 0), (0, dt_pad))).astype(BF16)
    rec_wo, att_wqkv, att_wo = rec_w_out.astype(BF16), att_w_qkv.astype(BF16), att_w_out.astype(BF16)
    ffn_wgu, ffn_wd = ffn_w_gate_up.astype(BF16), ffn_w_down.astype(BF16)
    for layer in range(depth):
        i = layer // 2
        if layer % 2 == 0:
            mixes, wos = _recurrent_layer(h, b, l, i, rec_norm[i], w_lru, w_ssd, rec_wo, lru_conv_w[i],
                                          lru_conv_b[i], lru_w_r[i], lru_b_r[i], lru_w_i[i], lru_b_i[i],
                                          lru_lambda[i], ssd_conv_w[i], ssd_conv_b[i], ssd_dt_bias[i],
                                          ssd_a_log[i], ssd_d[i], ssd_norm[i])
        else:
            mixes, wos = _attention_layer(h, b, l, i, att_norm[i], att_wqkv, att_wo, att_q_norm[i], att_k_norm[i])
        h = _mix_ffn(h, mixes, wos, ffn_norm[layer].reshape(1, d), _layer_block(ffn_wgu, layer),
                     _layer_block(ffn_wd, layer))
    return h.reshape(b, l, d)
```
